```python
import math
import jax, jax.numpy as jnp
from jax import lax
import numpy as np

D_MODEL = 1024
BATCH = 2
SEQ = 8192
DEPTH = 2
DEC_BATCH = 32
DEC_SEQ = 4
PAST_LEN = 8192
PAGE_SIZE = 128

HEAD_DIM = 64
N_HEADS = 12
MIX_W = N_HEADS * HEAD_DIM
N_MEM_HEADS = 4
MEM_W = N_MEM_HEADS * HEAD_DIM
N_MEM = 256
N_KV_B = 4
KV_GROUP = N_HEADS // N_KV_B
CONV_W = 4
DN_CONV_CH = 3 * MIX_W
DN_CHUNK = 64
MOBA_BLOCK = 256
MOBA_TOPK = 3
MOBA_Q_CHUNK = 32
ROT_DIM = HEAD_DIM // 4
ROPE_THETA = 500000.0
D_FF = 2816
N_A = DEPTH // 2
N_B = DEPTH - N_A
IN_A = 4 * MIX_W + 2 * N_HEADS + MEM_W
IN_B = MIX_W + MEM_W
OUT_W = MIX_W + MEM_W
EPS = 1e-6
F32 = jnp.float32

kernel_name = 'yoco_gdn_moba_macaron_memory_step'


def rmsnorm(x, g):
    xf = x.astype(F32)
    y = xf * lax.rsqrt(jnp.mean(xf * xf, axis=-1, keepdims=True) + EPS)
    return (y * g.astype(F32)).astype(x.dtype)


def l2norm(t):
    tf = t.astype(F32)
    return tf * lax.rsqrt(jnp.sum(tf * tf, axis=-1, keepdims=True) + EPS)


def swiglu(x, g, w1, w3, w2):
    h = rmsnorm(x, g)
    return (jax.nn.silu(h @ w1) * (h @ w3)) @ w2


def partial_rope(x, pos):
    half = ROT_DIM // 2
    inv = ROPE_THETA ** (-jnp.arange(half, dtype=F32) * 2.0 / ROT_DIM)
    ang = pos.astype(F32)[:, None] * inv[None, :]
    cos = jnp.cos(ang)[:, None, :]
    sin = jnp.sin(ang)[:, None, :]
    xf = x.astype(F32)
    x1 = xf[..., :half]
    x2 = xf[..., half:ROT_DIM]
    out = jnp.concatenate([x1 * cos - x2 * sin, x2 * cos + x1 * sin, xf[..., ROT_DIM:]], axis=-1)
    return out.astype(x.dtype)


def short_conv(u, buf, w):
    L = u.shape[1]
    up = jnp.concatenate([buf.astype(u.dtype), u], axis=1)
    y = sum(up[:, j:j + L] * w[j] for j in range(CONV_W))
    return jax.nn.silu(y), up[:, up.shape[1] - (CONV_W - 1):]


def gated_delta_rule(q, k, v, g, beta, s0):
    B, L, H, d = q.shape
    C = math.gcd(L, DN_CHUNK)
    N = L // C

    def ch4(t):
        return t.reshape(B, N, C, H, d).transpose(0, 3, 1, 2, 4)

    def ch3(t):
        return t.reshape(B, N, C, H).transpose(0, 3, 1, 2)

    qc = ch4(q.astype(F32)) * (d ** -0.5)
    kc = ch4(k.astype(F32))
    vc = ch4(v.astype(F32))
    gc = lax.cumsum(ch3(g.astype(F32)), axis=3)
    bc = ch3(beta.astype(F32))
    kb = kc * bc[..., None]
    vb = vc * bc[..., None]
    tril = jnp.tril(jnp.ones((C, C), dtype=bool))
    strict = jnp.tril(jnp.ones((C, C), dtype=bool), -1)
    diff = gc[..., :, None] - gc[..., None, :]
    decay = jnp.exp(jnp.where(tril, diff, -jnp.inf))
    m = jnp.einsum('bhncd,bhnsd->bhncs', kb, kc) * jnp.where(strict, decay, 0.0)
    tmat = jnp.eye(C, dtype=F32) + m
    rhs = jnp.concatenate([vb, kb * jnp.exp(gc)[..., None]], axis=-1)
    sol = lax.linalg.triangular_solve(tmat, rhs, left_side=True, lower=True, unit_diagonal=True)
    u = sol[..., :d]
    w = sol[..., d:]
    a_intra = jnp.einsum('bhncd,bhnsd->bhncs', qc, kc) * decay

    def step(S, xs):
        qi, ki, ui, wi, gi, ai = xs
        v_new = ui - jnp.einsum('bhcd,bhde->bhce', wi, S)
        o = (jnp.einsum('bhcd,bhde->bhce', qi * jnp.exp(gi)[..., None], S)
             + jnp.einsum('bhcs,bhse->bhce', ai, v_new))
        glast = gi[..., -1]
        S = (S * jnp.exp(glast)[..., None, None]
             + jnp.einsum('bhcd,bhce->bhde', ki * jnp.exp(glast[..., None] - gi)[..., None], v_new))
        return S, o

    xs = tuple(jnp.moveaxis(t, 2, 0) for t in (qc, kc, u, w, gc, a_intra))
    s_fin, o = lax.scan(step, s0.astype(F32), xs)
    o = o.transpose(1, 0, 3, 2, 4).reshape(B, L, H, d)
    return o, s_fin


def mem_kv(mem, g, w):
    B = mem.shape[0]
    k, v = jnp.split(rmsnorm(mem, g) @ w, 2, axis=-1)
    return (k.reshape(B, N_MEM, N_MEM_HEADS, HEAD_DIM), v.reshape(B, N_MEM, N_MEM_HEADS, HEAD_DIM))


def mem_attend(q, mk, mv):
    s = jnp.einsum('blhd,bmhd->bhlm', q, mk.astype(q.dtype)).astype(F32) * (HEAD_DIM ** -0.5)
    p = jax.nn.softmax(s, axis=-1).astype(q.dtype)
    return jnp.einsum('bhlm,bmhd->blhd', p, mv.astype(q.dtype))


def moba_attend(q, q_pos, K, V):
    B, Lq = q.shape[0], q.shape[1]
    T = K.shape[1]
    NB = -(-T // MOBA_BLOCK)
    pad = NB * MOBA_BLOCK - T
    Kb = jnp.pad(K, ((0, 0), (0, pad), (0, 0), (0, 0))).reshape(B, NB, MOBA_BLOCK, N_KV_B, HEAD_DIM)
    Vb = jnp.pad(V, ((0, 0), (0, pad), (0, 0), (0, 0))).reshape(B, NB, MOBA_BLOCK, N_KV_B, HEAD_DIM)
    head_kv = jnp.arange(N_HEADS) // KV_GROUP
    kbar = jnp.mean(Kb.astype(F32), axis=2)[:, :, head_kv]
    gate = jnp.einsum('blhd,bnhd->blhn', q.astype(F32), kbar)
    qblk = q_pos // MOBA_BLOCK
    past = jnp.arange(NB)[None, :] < qblk[:, None]
    gate = jnp.where(past[None, :, None, :], gate, -jnp.inf)
    k_sel = min(MOBA_TOPK, NB)
    _, top_idx = lax.top_k(gate, k_sel)
    sel_valid = top_idx < qblk[None, :, None, None]
    own = jnp.broadcast_to(qblk[None, :, None, None], (B, Lq, N_HEADS, 1)).astype(top_idx.dtype)
    idx = jnp.concatenate([top_idx, own], axis=-1)
    valid = jnp.concatenate([sel_valid, jnp.ones((B, Lq, N_HEADS, 1), dtype=bool)], axis=-1)
    n_slots = k_sel + 1
    Kt = jnp.moveaxis(Kb, 3, 1)
    Vt = jnp.moveaxis(Vb, 3, 1)
    bi = jnp.arange(B)[:, None, None, None]
    hk = head_kv[None, None, :, None]
    qc = math.gcd(Lq, MOBA_Q_CHUNK)
    nq = Lq // qc

    def attend(args):
        q_c, pos_c, idx_c, valid_c = args
        kg = Kt[bi, hk, idx_c].astype(q_c.dtype)
        vg = Vt[bi, hk, idx_c].astype(q_c.dtype)
        s = jnp.einsum('bqhd,bqhsjd->bqhsj', q_c, kg).astype(F32) * (HEAD_DIM ** -0.5)
        kpos = idx_c[..., None] * MOBA_BLOCK + jnp.arange(MOBA_BLOCK)
        mask = valid_c[..., None] & (kpos <= pos_c[None, :, None, None, None])
        s = jnp.where(mask, s, -jnp.inf).reshape(B, qc, N_HEADS, n_slots * MOBA_BLOCK)
        p = jax.nn.softmax(s, axis=-1).reshape(B, qc, N_HEADS, n_slots, MOBA_BLOCK).astype(q_c.dtype)
        return jnp.einsum('bqhsj,bqhsjd->bqhd', p, vg)

    def to_chunks(t):
        return jnp.moveaxis(t.reshape(B, nq, qc, *t.shape[2:]), 1, 0)

    out = lax.map(attend, (to_chunks(q), q_pos.reshape(nq, qc), to_chunks(idx), to_chunks(valid)))
    return jnp.moveaxis(out, 0, 1).reshape(B, Lq, N_HEADS, HEAD_DIM)


def deltanet_branch(p, a, h, conv_buf, s0):
    B, L, _ = h.shape
    proj = h @ p['w_in_a'][a]
    qkv, gate, a_in, b_in, q_mem = jnp.split(
        proj, [3 * MIX_W, 4 * MIX_W, 4 * MIX_W + N_HEADS, 4 * MIX_W + 2 * N_HEADS], axis=-1)
    qkv, new_buf = short_conv(qkv, conv_buf, p['dn_conv_w'][a])
    q, k, v = [t.reshape(B, L, N_HEADS, HEAD_DIM) for t in jnp.split(qkv, 3, axis=-1)]
    g = -jnp.exp(p['dn_a_log'][a].astype(F32)) * jax.nn.softplus(a_in.astype(F32) + p['dn_dt_bias'][a].astype(F32))
    beta = jax.nn.sigmoid(b_in.astype(F32))
    o, s_fin = gated_delta_rule(l2norm(q), l2norm(k), v, g, beta, s0)
    o = rmsnorm(o.astype(h.dtype), p['dn_out_norm'][a]) * jax.nn.silu(gate.reshape(B, L, N_HEADS, HEAD_DIM))
    return o, q_mem, new_buf, s_fin.astype(s0.dtype)


def run_trunk(p, x, pos, mem_k, mem_v, conv0, s0, past_k, past_v):
    B, L, _ = x.shape
    new_conv, new_s = [], []
    K = V = k_new = v_new = None
    for l in range(DEPTH):
        if l == N_A:
            k_new, v_new = jnp.split(rmsnorm(x, p['kv_norm']) @ p['w_kv'], 2, axis=-1)
            k_new = partial_rope(k_new.reshape(B, L, N_KV_B, HEAD_DIM), pos)
            v_new = v_new.reshape(B, L, N_KV_B, HEAD_DIM)
            if past_k is None:
                K, V = k_new, v_new
            else:
                K = jnp.concatenate([past_k.astype(k_new.dtype), k_new], axis=1)
                V = jnp.concatenate([past_v.astype(v_new.dtype), v_new], axis=1)
        x = x + 0.5 * swiglu(x, p['ffn1_norm'][l], p['ffn1_w1'][l], p['ffn1_w3'][l], p['ffn1_w2'][l])
        h = rmsnorm(x, p['mix_norm'][l])
        if l < N_A:
            o, q_mem, cbuf, s_fin = deltanet_branch(p, l, h, conv0[l], s0[l])
            new_conv.append(cbuf)
            new_s.append(s_fin)
        else:
            q, q_mem = jnp.split(h @ p['w_in_b'][l - N_A], [MIX_W], axis=-1)
            q = partial_rope(q.reshape(B, L, N_HEADS, HEAD_DIM), pos)
            o = moba_attend(q, pos, K, V)
        mo = mem_attend(q_mem.reshape(B, L, N_MEM_HEADS, HEAD_DIM), mem_k[l], mem_v[l])
        mix = jnp.concatenate([o.reshape(B, L, MIX_W).astype(x.dtype), mo.reshape(B, L, MEM_W).astype(x.dtype)], axis=-1)
        x = x + mix @ p['w_out'][l]
        x = x + 0.5 * swiglu(x, p['ffn2_norm'][l], p['ffn2_w1'][l], p['ffn2_w3'][l], p['ffn2_w2'][l])
    return rmsnorm(x, p['final_norm']), jnp.stack(new_conv), jnp.stack(new_s), k_new, v_new


def setup_inputs(seed: int = 0) -> dict:
    key = jax.random.key(seed)
    cnt = [0]

    def sub():
        cnt[0] += 1
        return jax.random.fold_in(key, cnt[0])

    def nrm(shape, scale):
        return jax.random.normal(sub(), shape, F32) * scale

    def gain(shape):
        return 1.0 + nrm(shape, 0.02)

    n_pages = PAST_LEN // PAGE_SIZE
    n_pool = (DEC_BATCH * n_pages * 5) // 4
    page_table = jax.random.permutation(sub(), n_pool)[:DEC_BATCH * n_pages].reshape(DEC_BATCH, n_pages).astype(jnp.int32)
    dt = jnp.exp(jax.random.uniform(sub(), (N_A, N_HEADS), F32, math.log(1e-3), math.log(1e-1)))
    return {
        'x_prompt': nrm((BATCH, SEQ, D_MODEL), 1.0),
        'x_sample': nrm((DEC_BATCH, DEC_SEQ, D_MODEL), 1.0),
        'mem_prompt': nrm((BATCH, N_MEM, D_MODEL), 1.0),
        'cache_mem_k': nrm((DEPTH, DEC_BATCH, N_MEM, N_MEM_HEADS, HEAD_DIM), 1.0),
        'cache_mem_v': nrm((DEPTH, DEC_BATCH, N_MEM, N_MEM_HEADS, HEAD_DIM), 1.0),
        'state_dn_conv': nrm((N_A, DEC_BATCH, CONV_W - 1, DN_CONV_CH), 1.0),
        'state_dn_S': nrm((N_A, DEC_BATCH, N_HEADS, HEAD_DIM, HEAD_DIM), 0.1),
        'cache_kv_k': nrm((n_pool, PAGE_SIZE, N_KV_B, HEAD_DIM), 1.0),
        'cache_kv_v': nrm((n_pool, PAGE_SIZE, N_KV_B, HEAD_DIM), 1.0),
        'page_table': page_table,
        'ffn1_norm': gain((DEPTH, D_MODEL)),
        'ffn1_w1': nrm((DEPTH, D_MODEL, D_FF), D_MODEL ** -0.5),
        'ffn1_w3': nrm((DEPTH, D_MODEL, D_FF), D_MODEL ** -0.5),
        'ffn1_w2': nrm((DEPTH, D_FF, D_MODEL), D_FF ** -0.5),
        'mix_norm': gain((DEPTH, D_MODEL)),
        'w_in_a': nrm((N_A, D_MODEL, IN_A), D_MODEL ** -0.5),
        'w_in_b': nrm((N_B, D_MODEL, IN_B), D_MODEL ** -0.5),
        'w_out': nrm((DEPTH, OUT_W, D_MODEL), OUT_W ** -0.5),
        'dn_conv_w': nrm((N_A, CONV_W, DN_CONV_CH), CONV_W ** -0.5),
        'dn_a_log': jnp.log(jax.random.uniform(sub(), (N_A, N_HEADS), F32, 1.0, 16.0)),
        'dn_dt_bias': dt + jnp.log(-jnp.expm1(-dt)),
        'dn_out_norm': gain((N_A, HEAD_DIM)),
        'kv_norm': gain((D_MODEL,)),
        'w_kv': nrm((D_MODEL, 2 * N_KV_B * HEAD_DIM), D_MODEL ** -0.5),
        'mem_norm': gain((DEPTH, D_MODEL)),
        'w_mem_kv': nrm((DEPTH, D_MODEL, 2 * MEM_W), D_MODEL ** -0.5),
        'ffn2_norm': gain((DEPTH, D_MODEL)),
        'ffn2_w1': nrm((DEPTH, D_MODEL, D_FF), D_MODEL ** -0.5),
        'ffn2_w3': nrm((DEPTH, D_MODEL, D_FF), D_MODEL ** -0.5),
        'ffn2_w2': nrm((DEPTH, D_FF, D_MODEL), D_FF ** -0.5),
        'final_norm': gain((D_MODEL,)),
    }


def reference(x_prompt, x_sample, mem_prompt, cache_mem_k, cache_mem_v, state_dn_conv, state_dn_S,
              cache_kv_k, cache_kv_v, page_table,
              ffn1_norm, ffn1_w1, ffn1_w3, ffn1_w2, mix_norm, w_in_a, w_in_b, w_out,
              dn_conv_w, dn_a_log, dn_dt_bias, dn_out_norm, kv_norm, w_kv, mem_norm, w_mem_kv,
              ffn2_norm, ffn2_w1, ffn2_w3, ffn2_w2, final_norm):
    p = dict(ffn1_norm=ffn1_norm, ffn1_w1=ffn1_w1, ffn1_w3=ffn1_w3, ffn1_w2=ffn1_w2,
             mix_norm=mix_norm, w_in_a=w_in_a, w_in_b=w_in_b, w_out=w_out,
             dn_conv_w=dn_conv_w, dn_a_log=dn_a_log, dn_dt_bias=dn_dt_bias, dn_out_norm=dn_out_norm,
             kv_norm=kv_norm, w_kv=w_kv,
             ffn2_norm=ffn2_norm, ffn2_w1=ffn2_w1, ffn2_w3=ffn2_w3, ffn2_w2=ffn2_w2,
             final_norm=final_norm)
    Bp, Lp, _ = x_prompt.shape
    mkv = [mem_kv(mem_prompt, mem_norm[l], w_mem_kv[l]) for l in range(DEPTH)]
    mem_k_p = jnp.stack([t[0] for t in mkv])
    mem_v_p = jnp.stack([t[1] for t in mkv])
    conv0 = jnp.zeros((N_A, Bp, CONV_W - 1, DN_CONV_CH), x_prompt.dtype)
    s0 = jnp.zeros((N_A, Bp, N_HEADS, HEAD_DIM, HEAD_DIM), x_prompt.dtype)
    pos_p = jnp.arange(Lp, dtype=jnp.int32)
    y_prompt, conv_p, s_p, k_p, v_p = run_trunk(p, x_prompt, pos_p, mem_k_p, mem_v_p, conv0, s0, None, None)
    Bs, Ls, _ = x_sample.shape
    n_pages = page_table.shape[1]
    past_len = n_pages * cache_kv_k.shape[1]
    past_k = cache_kv_k[page_table].reshape(Bs, past_len, N_KV_B, HEAD_DIM)
    past_v = cache_kv_v[page_table].reshape(Bs, past_len, N_KV_B, HEAD_DIM)
    pos_s = past_len + jnp.arange(Ls, dtype=jnp.int32)
    y_sample, conv_s, s_s, k_s, v_s = run_trunk(p, x_sample, pos_s, cache_mem_k, cache_mem_v,
                                                 state_dn_conv, state_dn_S, past_k, past_v)
    return (y_prompt, y_sample, conv_p, s_p, k_p, v_p, mem_k_p, mem_v_p, conv_s, s_s, k_s, v_s)
```

```python
import functools
import math

import jax
import jax.numpy as jnp
from jax import lax
from jax.experimental import pallas as pl
from jax.experimental.pallas import tpu as pltpu

F32 = jnp.float32
BF16 = jnp.bfloat16
HI = lax.Precision.HIGHEST

HEAD_DIM = 64
N_HEADS = 12
MIX_W = N_HEADS * HEAD_DIM
N_MEM_HEADS = 4
MEM_W = N_MEM_HEADS * HEAD_DIM
N_KV_B = 4
KV_GROUP = N_HEADS // N_KV_B
GROUP_W = KV_GROUP * HEAD_DIM
CONV_W = 4
DN_CHUNK = 64
MOBA_BLOCK = 256
MOBA_TOPK = 3
ROT_DIM = HEAD_DIM // 4
ROPE_THETA = 500000.0
EPS = 1e-6
QK_SCALE = HEAD_DIM ** -0.5

LANES = 128
SUBLANES = 8
VMEM_BYTES_V7X = 64 * 2 ** 20
NEG_INF = float("-inf")

NT = (((1,), (1,)), ((), ()))
TN = (((0,), (0,)), ((), ()))


def _params(semantics, est_bytes):
    limit = min(max(int(est_bytes * 1.5), 16 * 2 ** 20), VMEM_BYTES_V7X - 8 * 2 ** 20)
    return pltpu.CompilerParams(dimension_semantics=semantics, vmem_limit_bytes=limit)


def _row_tile(m, pref):
    t = min(m, pref)
    while m % t or t % SUBLANES:
        t -= 1
    return t


def _rms(x, g):
    return x * lax.rsqrt(jnp.mean(x * x, axis=-1, keepdims=True) + EPS) * g


def _sigmoid(x):
    return 1.0 / (1.0 + jnp.exp(-x))


def _softplus(x):
    return jnp.maximum(x, 0.0) + jnp.log(1.0 + jnp.exp(-jnp.abs(x)))


def _bdot(a, b):
    return jnp.dot(a.astype(BF16), b.astype(BF16), preferred_element_type=F32)


def _bdot_nt(a, b):
    return lax.dot_general(a.astype(BF16), b.astype(BF16), NT, preferred_element_type=F32)


def _ffn_kernel(*refs, n_f, final):
    if final:
        x_ref, g_ref, w1_ref, w3_ref, w2_ref, fg_ref, o_ref, h_scr, acc_scr = refs
    else:
        x_ref, g_ref, w1_ref, w3_ref, w2_ref, o_ref, h_scr, acc_scr = refs
    f = pl.program_id(1)

    @pl.when(f == 0)
    def _():
        h_scr[...] = _rms(x_ref[...], g_ref[...]).astype(BF16)
        acc_scr[...] = jnp.zeros_like(acc_scr)

    h = h_scr[...]
    a = jnp.dot(h, w1_ref[...], preferred_element_type=F32)
    b = jnp.dot(h, w3_ref[...], preferred_element_type=F32)
    u = (a * _sigmoid(a)) * b
    acc_scr[...] += jnp.dot(u.astype(BF16), w2_ref[...], preferred_element_type=F32)

    @pl.when(f == n_f - 1)
    def _():
        y = x_ref[...] + 0.5 * acc_scr[...]
        if final:
            y = _rms(y, fg_ref[...])
        o_ref[...] = y


def _ffn(x, g, w1, w3, w2, final_gain=None):
    m, d = x.shape
    d_ff = w1.shape[1]
    tm = _row_tile(m, 512)
    n_f = 2 if d_ff % (2 * LANES) == 0 else 1
    tf = d_ff // n_f
    final = final_gain is not None
    in_specs = [
        pl.BlockSpec((tm, d), lambda i, f: (i, 0)),
        pl.BlockSpec((1, d), lambda i, f: (0, 0)),
        pl.BlockSpec((d, tf), lambda i, f: (0, f)),
        pl.BlockSpec((d, tf), lambda i, f: (0, f)),
        pl.BlockSpec((tf, d), lambda i, f: (f, 0)),
    ]
    args = [x, g.reshape(1, d), w1, w3, w2]
    if final:
        in_specs.append(pl.BlockSpec((1, d), lambda i, f: (0, 0)))
        args.append(final_gain.reshape(1, d))
    est = 4 * tm * d * 4 + 6 * d * tf * 2 + tm * d * 6 + 3 * tm * tf * 4
    return pl.pallas_call(
        functools.partial(_ffn_kernel, n_f=n_f, final=final),
        grid=(m // tm, n_f),
        in_specs=in_specs,
        out_specs=pl.BlockSpec((tm, d), lambda i, f: (i, 0)),
        out_shape=jax.ShapeDtypeStruct((m, d), F32),
        scratch_shapes=[pltpu.VMEM((tm, d), BF16), pltpu.VMEM((tm, d), F32)],
        compiler_params=_params(("parallel", "arbitrary"), est),
        name="ffn_final" if final else "ffn",
    )(*args)


def _norm_proj_kernel(x_ref, g_ref, w_ref, *out_refs, splits):
    h = _rms(x_ref[...], g_ref[...]).astype(BF16)
    for o_ref, (a, b) in zip(out_refs, splits):
        o_ref[...] = jnp.dot(h, w_ref[:, a:b], preferred_element_type=F32)


def _norm_proj(x, g, w, widths, name):
    m, d = x.shape
    n = w.shape[1]
    assert sum(widths) == n and all(c % LANES == 0 for c in widths)
    tm = _row_tile(m, 256)
    splits, a = [], 0
    for c in widths:
        splits.append((a, a + c))
        a += c
    est = 2 * tm * d * 4 + 2 * d * n * 2 + 3 * tm * n * 4
    return pl.pallas_call(
        functools.partial(_norm_proj_kernel, splits=tuple(splits)),
        grid=(m // tm,),
        in_specs=[
            pl.BlockSpec((tm, d), lambda i: (i, 0)),
            pl.BlockSpec((1, d), lambda i: (0, 0)),
            pl.BlockSpec((d, n), lambda i: (0, 0)),
        ],
        out_specs=[pl.BlockSpec((tm, c), lambda i: (i, 0)) for c in widths],
        out_shape=[jax.ShapeDtypeStruct((m, c), F32) for c in widths],
        compiler_params=_params(("parallel",), est),
        name=name,
    )(x, g.reshape(1, d), w)


def _rope_table_kernel(inv_ref, c_ref, s1_ref, s2_ref, *, period, base, tl):
    i = pl.program_id(0)
    row = lax.broadcasted_iota(jnp.int32, (tl, LANES), 0) + i * tl
    pos = base + lax.rem(row, period)
    ang = pos.astype(F32) * inv_ref[...]
    in_head = lax.broadcasted_iota(jnp.int32, (tl, LANES), 1) % HEAD_DIM
    half = ROT_DIM // 2
    c = jnp.cos(ang)
    s = jnp.sin(ang)
    c_ref[...] = jnp.where(in_head < ROT_DIM, c, 1.0)
    s1_ref[...] = jnp.where(in_head < half, -s, 0.0)
    s2_ref[...] = jnp.where((in_head >= half) & (in_head < ROT_DIM), s, 0.0)


def _rope_tables(n_rows, period, base):
    half = ROT_DIM // 2
    inv = ROPE_THETA ** (-jnp.arange(half, dtype=F32) * 2.0 / ROT_DIM)
    in_head = jnp.arange(LANES) % HEAD_DIM
    inv_lane = jnp.where(in_head < ROT_DIM, inv[in_head % half], 0.0).astype(F32).reshape(1, LANES)
    tl = _row_tile(n_rows, 1024)
    shp = jax.ShapeDtypeStruct((n_rows, LANES), F32)
    return pl.pallas_call(
        functools.partial(_rope_table_kernel, period=period, base=base, tl=tl),
        grid=(n_rows // tl,),
        in_specs=[pl.BlockSpec((1, LANES), lambda i: (0, 0))],
        out_specs=[pl.BlockSpec((tl, LANES), lambda i: (i, 0))] * 3,
        out_shape=[shp] * 3,
        compiler_params=_params(("parallel",), 16 * tl * LANES * 4),
        name="rope_tables",
    )(inv_lane)


def _apply_rope(t, c, s1, s2):
    w = t.shape[1]
    reps = w // LANES
    half = ROT_DIM // 2
    tile = lambda a: jnp.concatenate([a] * reps, axis=1) if reps > 1 else a
    up = pltpu.roll(t, w - half, 1)
    down = pltpu.roll(t, half, 1)
    return t * tile(c) + up * tile(s1) + down * tile(s2)


def _kv_proj_kernel(x_ref, g_ref, w_ref, c_ref, s1_ref, s2_ref, *out_refs, for_prompt):
    h = _rms(x_ref[...], g_ref[...]).astype(BF16)
    kv_w = N_KV_B * HEAD_DIM
    k = jnp.dot(h, w_ref[:, :kv_w], preferred_element_type=F32)
    v = jnp.dot(h, w_ref[:, kv_w:], preferred_element_type=F32)
    k = _apply_rope(k, c_ref[...], s1_ref[...], s2_ref[...])
    out_refs[0][...] = k
    out_refs[1][...] = v
    if for_prompt:
        kh_ref, vh_ref, kbar_ref = out_refs[2:]
        for hk in range(N_KV_B):
            sl = slice(hk * HEAD_DIM, (hk + 1) * HEAD_DIM)
            kh_ref[hk] = k[:, sl].astype(BF16)
            vh_ref[hk] = v[:, sl].astype(BF16)
        kbar_ref[0] = jnp.sum(k, axis=0, keepdims=True) * (1.0 / MOBA_BLOCK)


def _kv_proj(x, g, w, tables, for_prompt):
    m, d = x.shape
    kv_w = N_KV_B * HEAD_DIM
    tm = MOBA_BLOCK if for_prompt else _row_tile(m, 256)
    n_tab = tables[0].shape[0] // tm
    row = lambda i: (i, 0)
    tab = lambda i: (i % n_tab, 0)
    out_specs = [pl.BlockSpec((tm, kv_w), row), pl.BlockSpec((tm, kv_w), row)]
    out_shape = [jax.ShapeDtypeStruct((m, kv_w), F32)] * 2
    if for_prompt:
        out_specs += [pl.BlockSpec((N_KV_B, tm, HEAD_DIM), lambda i: (0, i, 0))] * 2
        out_shape += [jax.ShapeDtypeStruct((N_KV_B, m, HEAD_DIM), BF16)] * 2
        out_specs.append(pl.BlockSpec((1, 1, kv_w), lambda i: (i, 0, 0)))
        out_shape.append(jax.ShapeDtypeStruct((m // tm, 1, kv_w), F32))
    est = 2 * tm * d * 4 + 2 * d * 2 * kv_w * 2 + 12 * tm * kv_w * 4
    return pl.pallas_call(
        functools.partial(_kv_proj_kernel, for_prompt=for_prompt),
        grid=(m // tm,),
        in_specs=[
            pl.BlockSpec((tm, d), row),
            pl.BlockSpec((1, d), lambda i: (0, 0)),
            pl.BlockSpec((d, 2 * kv_w), lambda i: (0, 0)),
            pl.BlockSpec((tm, LANES), tab),
            pl.BlockSpec((tm, LANES), tab),
            pl.BlockSpec((tm, LANES), tab),
        ],
        out_specs=out_specs,
        out_shape=out_shape,
        compiler_params=_params(("parallel",), est),
        name="kv_proj",
    )(x, g.reshape(1, d), w, *tables)


def _q_proj_kernel(x_ref, g_ref, w_ref, c_ref, s1_ref, s2_ref, qh_ref, qm_ref):
    h = _rms(x_ref[...], g_ref[...]).astype(BF16)
    q = jnp.dot(h, w_ref[:, :MIX_W], preferred_element_type=F32)
    qm_ref[...] = jnp.dot(h, w_ref[:, MIX_W:], preferred_element_type=F32)
    q = _apply_rope(q, c_ref[...], s1_ref[...], s2_ref[...]) * QK_SCALE
    for hk in range(N_KV_B):
        qh_ref[hk] = q[:, hk * GROUP_W:(hk + 1) * GROUP_W]


def _q_proj(x, g, w, tables):
    m, d = x.shape
    n = w.shape[1]
    tm = _row_tile(m, 256)
    n_tab = tables[0].shape[0] // tm
    row = lambda i: (i, 0)
    tab = lambda i: (i % n_tab, 0)
    est = 2 * tm * d * 4 + 2 * d * n * 2 + 16 * tm * MIX_W * 4
    return pl.pallas_call(
        _q_proj_kernel,
        grid=(m // tm,),
        in_specs=[
            pl.BlockSpec((tm, d), row),
            pl.BlockSpec((1, d), lambda i: (0, 0)),
            pl.BlockSpec((d, n), lambda i: (0, 0)),
            pl.BlockSpec((tm, LANES), tab),
            pl.BlockSpec((tm, LANES), tab),
            pl.BlockSpec((tm, LANES), tab),
        ],
        out_specs=[
            pl.BlockSpec((N_KV_B, tm, GROUP_W), lambda i: (0, i, 0)),
            pl.BlockSpec((tm, MEM_W), row),
        ],
        out_shape=[
            jax.ShapeDtypeStruct((N_KV_B, m, GROUP_W), F32),
            jax.ShapeDtypeStruct((m, MEM_W), F32),
        ],
        compiler_params=_params(("parallel",), est),
        name="q_proj",
    )(x, g.reshape(1, d), w, *tables)


def _conv_kernel(u_ref, prev_ref, buf_ref, w_ref, y_ref, nb_ref, pad_scr, *, tl, n_t):
    i = pl.program_id(1)
    halo = CONV_W - 1
    lo = SUBLANES - halo
    pad_scr[SUBLANES:SUBLANES + tl, :] = u_ref[0]

    @pl.when(i == 0)
    def _():
        pad_scr[lo:SUBLANES, :] = buf_ref[0]

    @pl.when(i > 0)
    def _():
        p = prev_ref[0]
        pad_scr[lo:SUBLANES, :] = p[p.shape[0] - halo:, :]

    y = pad_scr[lo:lo + tl, :] * w_ref[0:1, :]
    for j in range(1, CONV_W):
        y = y + pad_scr[lo + j:lo + j + tl, :] * w_ref[j:j + 1, :]
    y_ref[0] = y * _sigmoid(y)

    @pl.when(i == n_t - 1)
    def _():
        nb_ref[0] = pad_scr[lo + tl:SUBLANES + tl, :]


def _conv(u, buf, w):
    b, l, c = u.shape
    tl = _row_tile(l, 512) if l % SUBLANES == 0 else l
    n_t = l // tl
    pr = min(SUBLANES, l)
    per = tl // pr
    est = 6 * tl * c * 4
    return pl.pallas_call(
        functools.partial(_conv_kernel, tl=tl, n_t=n_t),
        grid=(b, n_t),
        in_specs=[
            pl.BlockSpec((1, tl, c), lambda bi, i: (bi, i, 0)),
            pl.BlockSpec((1, pr, c), lambda bi, i: (bi, jnp.maximum(i * per - 1, 0), 0)),
            pl.BlockSpec((1, CONV_W - 1, c), lambda bi, i: (bi, 0, 0)),
            pl.BlockSpec((CONV_W, c), lambda bi, i: (0, 0)),
        ],
        out_specs=[
            pl.BlockSpec((1, tl, c), lambda bi, i: (bi, i, 0)),
            pl.BlockSpec((1, CONV_W - 1, c), lambda bi, i: (bi, 0, 0)),
        ],
        out_shape=[
            jax.ShapeDtypeStruct((b, l, c), F32),
            jax.ShapeDtypeStruct((b, CONV_W - 1, c), F32),
        ],
        scratch_shapes=[pltpu.VMEM((tl + SUBLANES, c), F32)],
        compiler_params=_params(("parallel", "arbitrary"), est),
        name="short_conv",
    )(u, u, buf, w)


def _unit_lower_inverse(m_strict, eye):
    c = m_strict.shape[0]
    x = eye - m_strict
    p = jnp.dot(m_strict, m_strict, precision=HI, preferred_element_type=F32)
    power = 2
    while 2 * power < c:
        xp = jnp.dot(jnp.concatenate([x, p], axis=0), p, precision=HI, preferred_element_type=F32)
        x = x + xp[:c]
        p = xp[c:]
        power *= 2
    return x + jnp.dot(x, p, precision=HI, preferred_element_type=F32)


def _gdn_kernel(q_ref, k_ref, v_ref, gate_ref, ab_ref, alog_ref, dt_ref, gain_ref, s0_ref,
                o_ref, sfin_ref, s_scr, g_scr, b_scr, *, t_rows, n_t, l_valid):
    t = pl.program_id(1)
    c = DN_CHUNK
    d = HEAD_DIM

    @pl.when(t == 0)
    def _():
        s_scr[...] = s0_ref[0]

    ab = ab_ref[0]
    row = lax.broadcasted_iota(jnp.int32, ab.shape, 0) + t * t_rows
    valid = row < l_valid
    g_scr[...] = jnp.where(valid, -jnp.exp(alog_ref[...]) * _softplus(ab + dt_ref[...]), 0.0)
    b_scr[...] = jnp.where(valid, _sigmoid(ab), 0.0)

    ri = lax.broadcasted_iota(jnp.int32, (c, c), 0)
    ci = lax.broadcasted_iota(jnp.int32, (c, c), 1)
    lower = ri >= ci
    strict = ri > ci
    eye = jnp.where(ri == ci, 1.0, 0.0).astype(F32)
    tril = jnp.where(lower, 1.0, 0.0).astype(F32)
    triu = jnp.where(ri <= ci, 1.0, 0.0).astype(F32)
    er = lax.broadcasted_iota(jnp.int32, (LANES, MIX_W), 0)
    ec = lax.broadcasted_iota(jnp.int32, (LANES, MIX_W), 1) // d
    sel_g = jnp.where(er == ec, 1.0, 0.0).astype(F32)
    sel_b = jnp.where(er == ec + N_HEADS, 1.0, 0.0).astype(F32)
    gain = gain_ref[...]

    def chunk(ci_, carry):
        r0 = pl.multiple_of(ci_ * c, c)
        rows = pl.ds(r0, c)
        g_c = g_scr[rows, :]
        gc = jnp.dot(tril, g_c, precision=HI, preferred_element_type=F32)
        gcb = jnp.dot(gc, sel_g, precision=HI, preferred_element_type=F32)
        gct = lax.dot_general(g_c, triu, TN, precision=HI, preferred_element_type=F32)
        bb = jnp.dot(b_scr[rows, :], sel_b, precision=HI, preferred_element_type=F32)
        glast = gcb[c - 1:c, :]
        eg = jnp.exp(gcb)
        ekd = jnp.exp(glast - gcb)
        egl = jnp.exp(glast)
        q_c = q_ref[0, rows, :]
        k_c = k_ref[0, rows, :]
        v_c = v_ref[0, rows, :]
        gate_c = gate_ref[0, rows, :]
        outs = []
        for h in range(N_HEADS):
            sl = slice(h * d, (h + 1) * d)
            q = q_c[:, sl]
            k = k_c[:, sl]
            v = v_c[:, sl]
            qn = q * lax.rsqrt(jnp.sum(q * q, axis=-1, keepdims=True) + EPS) * QK_SCALE
            kn = k * lax.rsqrt(jnp.sum(k * k, axis=-1, keepdims=True) + EPS)
            beta = bb[:, sl]
            kb = kn * beta
            vb = v * beta
            diff = gcb[:, sl] - gct[h:h + 1, :]
            dec = jnp.exp(jnp.where(lower, diff, NEG_INF))
            m_strict = jnp.where(strict, _bdot_nt(kb, kn) * dec, 0.0)
            a_intra = _bdot_nt(qn, kn) * dec
            tinv = _unit_lower_inverse(m_strict, eye)
            rhs = jnp.concatenate([vb, kb * eg[:, sl]], axis=1)
            sol = jnp.dot(tinv, rhs, precision=HI, preferred_element_type=F32)
            u = sol[:, :d]
            w = sol[:, d:]
            s_old = s_scr[h]
            v_new = u - _bdot(w, s_old)
            o = _bdot(qn * eg[:, sl], s_old) + _bdot(a_intra, v_new)
            kd = kn * ekd[:, sl]
            s_scr[h] = s_old * egl[:, sl] + lax.dot_general(
                kd.astype(BF16), v_new.astype(BF16), TN, preferred_element_type=F32)
            o = o * lax.rsqrt(jnp.mean(o * o, axis=-1, keepdims=True) + EPS) * gain[:, sl]
            gt = gate_c[:, sl]
            outs.append(o * (gt * _sigmoid(gt)))
        o_ref[0, rows, :] = jnp.concatenate(outs, axis=1)
        return carry

    lax.fori_loop(0, t_rows // c, chunk, 0)

    @pl.when(t == n_t - 1)
    def _():
        sfin_ref[0] = s_scr[...]


def _gdn(qkv, gate, ab, a_log, dt_bias, out_gain, s0, l_valid):
    b, lp, _ = qkv.shape
    assert lp % DN_CHUNK == 0
    t_rows = DN_CHUNK * math.gcd(lp // DN_CHUNK, 8)
    n_t = lp // t_rows
    pad = lambda a: jnp.pad(a.astype(F32), (0, LANES - N_HEADS)).reshape(1, LANES)
    blk = lambda j: pl.BlockSpec((1, t_rows, MIX_W), lambda bi, t, j=j: (bi, t, j))
    state = pl.BlockSpec((1, N_HEADS, HEAD_DIM, HEAD_DIM), lambda bi, t: (bi, 0, 0, 0))
    vec = pl.BlockSpec((1, LANES), lambda bi, t: (0, 0))
    est = 10 * t_rows * MIX_W * 4 + 40 * DN_CHUNK * MIX_W * 4 + 6 * N_HEADS * HEAD_DIM * LANES * 4
    return pl.pallas_call(
        functools.partial(_gdn_kernel, t_rows=t_rows, n_t=n_t, l_valid=l_valid),
        grid=(b, n_t),
        in_specs=[
            blk(0), blk(1), blk(2),
            pl.BlockSpec((1, t_rows, MIX_W), lambda bi, t: (bi, t, 0)),
            pl.BlockSpec((1, t_rows, LANES), lambda bi, t: (bi, t, 0)),
            vec, vec,
            pl.BlockSpec((1, MIX_W), lambda bi, t: (0, 0)),
            state,
        ],
        out_specs=[pl.BlockSpec((1, t_rows, MIX_W), lambda bi, t: (bi, t, 0)), state],
        out_shape=[
            jax.ShapeDtypeStruct((b, lp, MIX_W), F32),
            jax.ShapeDtypeStruct((b, N_HEADS, HEAD_DIM, HEAD_DIM), F32),
        ],
        scratch_shapes=[
            pltpu.VMEM((N_HEADS, HEAD_DIM, HEAD_DIM), F32),
            pltpu.VMEM((t_rows, LANES), F32),
            pltpu.VMEM((t_rows, LANES), F32),
        ],
        compiler_params=_params(("parallel", "arbitrary"), est),
        name="gated_delta_rule",
    )(qkv, qkv, qkv, gate, ab, pad(a_log), pad(dt_bias),
      jnp.tile(out_gain.astype(F32), N_HEADS).reshape(1, MIX_W), s0)


def _mem_attn_kernel(q_ref, k_ref, v_ref, o_ref):
    q = q_ref[0]
    k = k_ref[0]
    v = v_ref[0]
    outs = []
    for h in range(N_MEM_HEADS):
        sl = slice(h * HEAD_DIM, (h + 1) * HEAD_DIM)
        s = _bdot_nt(q[:, sl], k[:, sl]) * QK_SCALE
        p = jnp.exp(s - jnp.max(s, axis=-1, keepdims=True))
        outs.append(_bdot(p, v[:, sl]) / jnp.sum(p, axis=-1, keepdims=True))
    o_ref[0] = jnp.concatenate(outs, axis=1)


def _mem_attn(q, mk, mv):
    b, l, w = q.shape
    n_mem = mk.shape[1]
    tl = _row_tile(l, 512)
    est = 4 * tl * w * 4 + 4 * n_mem * w * 4 + 8 * tl * n_mem * 4
    return pl.pallas_call(
        _mem_attn_kernel,
        grid=(b, l // tl),
        in_specs=[
            pl.BlockSpec((1, tl, w), lambda bi, i: (bi, i, 0)),
            pl.BlockSpec((1, n_mem, w), lambda bi, i: (bi, 0, 0)),
            pl.BlockSpec((1, n_mem, w), lambda bi, i: (bi, 0, 0)),
        ],
        out_specs=pl.BlockSpec((1, tl, w), lambda bi, i: (bi, i, 0)),
        out_shape=jax.ShapeDtypeStruct((b, l, w), F32),
        compiler_params=_params(("parallel", "parallel"), est),
        name="mem_attn",
    )(q, mk, mv)


def _out_proj_kernel(x_ref, o_ref, mo_ref, wo_ref, wm_ref, y_ref, *, grouped):
    y = x_ref[...] + _bdot(mo_ref[...], wm_ref[...])
    if grouped:
        for hk in range(N_KV_B):
            y = y + _bdot(o_ref[hk], wo_ref[hk])
    else:
        y = y + _bdot(o_ref[...], wo_ref[...])
    y_ref[...] = y


def _out_proj(x, o, mo, w_out, grouped):
    m, d = x.shape
    tm = _row_tile(m, 512)
    row = lambda i: (i, 0)
    if grouped:
        wo = w_out[:MIX_W].reshape(N_KV_B, GROUP_W, d)
        o_spec = pl.BlockSpec((N_KV_B, tm, GROUP_W), lambda i: (0, i, 0))
        wo_spec = pl.BlockSpec((N_KV_B, GROUP_W, d), lambda i: (0, 0, 0))
    else:
        wo = w_out[:MIX_W]
        o_spec = pl.BlockSpec((tm, MIX_W), row)
        wo_spec = pl.BlockSpec((MIX_W, d), lambda i: (0, 0))
    est = 6 * tm * d * 4 + 4 * tm * d * 4 + 4 * d * d * 2
    return pl.pallas_call(
        functools.partial(_out_proj_kernel, grouped=grouped),
        grid=(m // tm,),
        in_specs=[
            pl.BlockSpec((tm, d), row),
            o_spec,
            pl.BlockSpec((tm, MEM_W), row),
            wo_spec,
            pl.BlockSpec((MEM_W, d), lambda i: (0, 0)),
        ],
        out_specs=pl.BlockSpec((tm, d), row),
        out_shape=jax.ShapeDtypeStruct((m, d), F32),
        compiler_params=_params(("parallel",), est),
        name="out_proj",
    )(x, o, mo, wo, w_out[MIX_W:])


def _select_blocks(gate, n_past, k_sel):
    lane = lax.broadcasted_iota(jnp.int32, gate.shape, 1)
    lane_f = lane.astype(F32)
    gate = jnp.where(lane < n_past, gate, NEG_INF)
    sel = jnp.zeros(gate.shape, F32)
    for _ in range(k_sel):
        mx = jnp.max(gate, axis=1, keepdims=True)
        first = jnp.min(jnp.where(gate == mx, lane_f, float(gate.shape[1])), axis=1, keepdims=True)
        pick = (lane_f == first) & (mx > NEG_INF)
        sel = jnp.where(pick, 1.0, sel)
        gate = jnp.where(pick, NEG_INF, gate)
    return sel


def _softmax_update(s, v, m_scr, l_scr, acc_scr):
    m_old = m_scr[...]
    m_new = jnp.maximum(m_old, jnp.max(s, axis=1, keepdims=True))
    alpha = jnp.exp(m_old - m_new)
    p = jnp.exp(s - m_new)
    l_scr[...] = alpha * l_scr[...] + jnp.sum(p, axis=1, keepdims=True)
    acc_scr[...] = alpha * acc_scr[...] + _bdot(p, v)
    m_scr[...] = m_new


def _moba_prompt_kernel(q_ref, k_ref, v_ref, kbar_ref, o_ref, m_scr, l_scr, acc_scr, sel_scr,
                        *, k_sel):
    i = pl.program_id(2)
    blk = MOBA_BLOCK
    d = HEAD_DIM
    q = q_ref[0]
    q3 = jnp.concatenate([q[:, g * d:(g + 1) * d] for g in range(KV_GROUP)], axis=0)
    gate = lax.dot_general(q3, kbar_ref[0, 0], NT, precision=HI, preferred_element_type=F32)
    sel_scr[...] = _select_blocks(gate, i, k_sel)
    q3b = q3.astype(BF16)

    own = pl.ds(pl.multiple_of(i * blk, blk), blk)
    s = lax.dot_general(q3b, k_ref[0, own, :], NT, preferred_element_type=F32)
    qpos = lax.broadcasted_iota(jnp.int32, s.shape, 0) % blk
    kpos = lax.broadcasted_iota(jnp.int32, s.shape, 1)
    s = jnp.where(kpos <= qpos, s, NEG_INF)
    m0 = jnp.max(s, axis=1, keepdims=True)
    p = jnp.exp(s - m0)
    m_scr[...] = m0
    l_scr[...] = jnp.sum(p, axis=1, keepdims=True)
    acc_scr[...] = jnp.dot(p.astype(BF16), v_ref[0, own, :], preferred_element_type=F32)

    def past(j, carry):
        rows = pl.ds(pl.multiple_of(j * blk, blk), blk)
        sel = sel_scr[...]
        lane = lax.broadcasted_iota(jnp.int32, sel.shape, 1)
        chosen = jnp.max(jnp.where(lane == j, sel, 0.0), axis=1, keepdims=True)
        sj = lax.dot_general(q3b, k_ref[0, rows, :], NT, preferred_element_type=F32)
        sj = jnp.where(chosen > 0.0, sj, NEG_INF)
        _softmax_update(sj, v_ref[0, rows, :], m_scr, l_scr, acc_scr)
        return carry

    lax.fori_loop(0, i, past, 0)
    out = acc_scr[...] / l_scr[...]
    o_ref[0] = jnp.concatenate([out[g * blk:(g + 1) * blk] for g in range(KV_GROUP)], axis=1)


def _moba_prompt(qh, kh, vh, kbar, b, l):
    nb = l // MOBA_BLOCK
    rows = KV_GROUP * MOBA_BLOCK
    k_sel = min(MOBA_TOPK, nb)
    est = 8 * l * LANES * 2 + 4 * MOBA_BLOCK * GROUP_W * 4 + 12 * rows * MOBA_BLOCK * 4
    return pl.pallas_call(
        functools.partial(_moba_prompt_kernel, k_sel=k_sel),
        grid=(N_KV_B, b, nb),
        in_specs=[
            pl.BlockSpec((1, MOBA_BLOCK, GROUP_W), lambda hk, bi, i: (hk, bi * nb + i, 0)),
            pl.BlockSpec((1, l, HEAD_DIM), lambda hk, bi, i: (hk, bi, 0)),
            pl.BlockSpec((1, l, HEAD_DIM), lambda hk, bi, i: (hk, bi, 0)),
            pl.BlockSpec((1, 1, nb, HEAD_DIM), lambda hk, bi, i: (hk, bi, 0, 0)),
        ],
        out_specs=pl.BlockSpec((1, MOBA_BLOCK, GROUP_W), lambda hk, bi, i: (hk, bi * nb + i, 0)),
        out_shape=jax.ShapeDtypeStruct((N_KV_B, b * l, GROUP_W), F32),
        scratch_shapes=[
            pltpu.VMEM((rows, 1), F32),
            pltpu.VMEM((rows, 1), F32),
            pltpu.VMEM((rows, HEAD_DIM), F32),
            pltpu.VMEM((rows, nb), F32),
        ],
        compiler_params=_params(("parallel", "parallel", "arbitrary"), est),
        name="moba_prompt",
    )(qh, kh, vh, kbar)


def _page_means_kernel(pt_ref, k0_ref, k1_ref, o_ref):
    j = pl.program_id(1)
    s = jnp.sum(k0_ref[0], axis=0, keepdims=True) + jnp.sum(k1_ref[0], axis=0, keepdims=True)
    o_ref[0, pl.ds(j, 1), :] = s * (1.0 / MOBA_BLOCK)


def _page_means(page_table, cache_k):
    bs, n_pages = page_table.shape
    page, w = cache_k.shape[1:]
    assert 2 * page == MOBA_BLOCK
    nb = n_pages // 2
    spec = lambda off: pl.BlockSpec((1, page, w), lambda b, j, pt: (pt[b, 2 * j + off], 0, 0))
    return pl.pallas_call(
        _page_means_kernel,
        grid_spec=pltpu.PrefetchScalarGridSpec(
            num_scalar_prefetch=1,
            grid=(bs, nb),
            in_specs=[spec(0), spec(1)],
            out_specs=pl.BlockSpec((1, nb, w), lambda b, j, pt: (b, 0, 0)),
        ),
        out_shape=jax.ShapeDtypeStruct((bs, nb, w), F32),
        compiler_params=_params(("parallel", "arbitrary"), 8 * page * w * 4),
        name="page_means",
    )(page_table, cache_k, cache_k)


def _moba_sample_kernel(pt_ref, q_ref, kbar_ref, k0_ref, k1_ref, v0_ref, v1_ref, kn_ref, vn_ref,
                        o_ref, m_scr, l_scr, acc_scr, sel_scr, *, n_blocks, l_new, k_sel):
    j = pl.program_id(1)
    q = q_ref[0]
    qb = q.astype(BF16)

    @pl.when(j == 0)
    def _():
        gate = lax.dot_general(q, kbar_ref[0], NT, precision=HI, preferred_element_type=F32)
        sel_scr[...] = _select_blocks(gate, n_blocks, k_sel)
        s = _bdot_nt(qb, kn_ref[0])
        t_q = lax.broadcasted_iota(jnp.int32, s.shape, 0) % l_new
        t_k = lax.broadcasted_iota(jnp.int32, s.shape, 1)
        s = jnp.where(t_k <= t_q, s, NEG_INF)
        m0 = jnp.max(s, axis=1, keepdims=True)
        p = jnp.exp(s - m0)
        m_scr[...] = m0
        l_scr[...] = jnp.sum(p, axis=1, keepdims=True)
        acc_scr[...] = _bdot(p, vn_ref[0])

    sel = sel_scr[...]
    lane = lax.broadcasted_iota(jnp.int32, sel.shape, 1)
    chosen = jnp.max(jnp.where(lane == j, sel, 0.0), axis=1, keepdims=True)
    for k_ref, v_ref in ((k0_ref, v0_ref), (k1_ref, v1_ref)):
        s = _bdot_nt(qb, k_ref[0])
        s = jnp.where(chosen > 0.0, s, NEG_INF)
        _softmax_update(s, v_ref[0], m_scr, l_scr, acc_scr)

    @pl.when(j == n_blocks - 1)
    def _():
        o_ref[0] = acc_scr[...] / l_scr[...]


def _moba_sample(q_bd, kbar, page_table, cache_k, cache_v, k_new, v_new, l_new):
    bs, rows, w = q_bd.shape
    n_pages = page_table.shape[1]
    page = cache_k.shape[1]
    nb = n_pages // 2
    k_sel = min(MOBA_TOPK, nb + 1)
    lp = k_new.shape[1]
    pg = lambda off: pl.BlockSpec((1, page, w), lambda b, j, pt: (pt[b, 2 * j + off], 0, 0))
    per_b = lambda r: pl.BlockSpec((1, r, w), lambda b, j, pt: (b, 0, 0))
    return pl.pallas_call(
        functools.partial(_moba_sample_kernel, n_blocks=nb, l_new=l_new, k_sel=k_sel),
        grid_spec=pltpu.PrefetchScalarGridSpec(
            num_scalar_prefetch=1,
            grid=(bs, nb),
            in_specs=[per_b(rows), per_b(nb), pg(0), pg(1), pg(0), pg(1), per_b(lp), per_b(lp)],
            out_specs=per_b(rows),
            scratch_shapes=[
                pltpu.VMEM((rows, 1), F32),
                pltpu.VMEM((rows, 1), F32),
                pltpu.VMEM((rows, w), F32),
                pltpu.VMEM((rows, nb), F32),
            ],
        ),
        out_shape=jax.ShapeDtypeStruct((bs, rows, w), F32),
        compiler_params=_params(("parallel", "arbitrary"), 16 * page * w * 4),
        name="moba_sample",
    )(page_table, q_bd, kbar, cache_k, cache_k, cache_v, cache_v, k_new, v_new)


def _pack_w_in_a(w):
    d = w.shape[0]
    qkv_gate = w[:, :4 * MIX_W]
    ab = w[:, 4 * MIX_W:4 * MIX_W + 2 * N_HEADS]
    q_mem = w[:, 4 * MIX_W + 2 * N_HEADS:]
    ab = jnp.pad(ab, ((0, 0), (0, LANES - 2 * N_HEADS)))
    return jnp.concatenate([qkv_gate, q_mem, ab], axis=1).astype(BF16)


def _pad_rows(a, rows):
    return jnp.pad(a, ((0, 0), (0, rows - a.shape[1]), (0, 0)))


def _trunk(p, x3, pos_base, mem_k, mem_v, conv0, s0, paged):
    b, l, d = x3.shape
    m = b * l
    x = x3.reshape(m, d)
    sample = paged is not None
    ffn = lambda x, tag, i, fg=None: _ffn(x, p[tag + "_norm"][i], p[tag + "_w1"][i], p[tag + "_w3"][i],
                                          p[tag + "_w2"][i], fg)
    lq = -(-l // SUBLANES) * SUBLANES

    def memory(q_mem, layer):
        q3 = _pad_rows(q_mem.reshape(b, l, MEM_W), lq)
        mo = _mem_attn(q3, mem_k[layer], mem_v[layer])
        return mo[:, :l].reshape(m, MEM_W)

    x = ffn(x, "ffn1", 0)
    qkv, gate, q_mem, ab = _norm_proj(x, p["mix_norm"][0], p["w_in_a"],
                                      (3 * MIX_W, MIX_W, MEM_W, LANES), "in_proj_a")
    qkv, new_conv = _conv(qkv.reshape(b, l, 3 * MIX_W), conv0, p["dn_conv_w"])
    lp = -(-l // DN_CHUNK) * DN_CHUNK
    o, s_fin = _gdn(_pad_rows(qkv, lp), _pad_rows(gate.reshape(b, l, MIX_W), lp),
                    _pad_rows(ab.reshape(b, l, LANES), lp),
                    p["dn_a_log"], p["dn_dt_bias"], p["dn_out_norm"], s0, l)
    o = o[:, :l].reshape(m, MIX_W)
    x = _out_proj(x, o, memory(q_mem, 0), p["w_out"][0], grouped=False)
    x = ffn(x, "ffn2", 0)

    if sample:
        tables = _rope_tables(m, l, pos_base)
    else:
        tables = _rope_tables(l, l, pos_base)
    kv = _kv_proj(x, p["kv_norm"], p["w_kv"], tables, for_prompt=not sample)
    k_new, v_new = kv[0], kv[1]
    x = ffn(x, "ffn1", 1)
    qh, q_mem = _q_proj(x, p["mix_norm"][1], p["w_in_b"], tables)
    if sample:
        page_table, cache_k, cache_v = paged
        kbar = _page_means(page_table, cache_k)
        eye = jnp.eye(N_KV_B, dtype=F32)
        q5 = qh.reshape(N_KV_B, b, l, KV_GROUP, HEAD_DIM).transpose(1, 0, 3, 2, 4)
        q_bd = (q5[:, :, :, :, None, :] * eye[None, :, None, None, :, None]).reshape(
            b, N_HEADS * l, N_KV_B * HEAD_DIM)
        o_bd = _moba_sample(q_bd, kbar, page_table, cache_k, cache_v,
                            _pad_rows(k_new.reshape(b, l, -1), lq),
                            _pad_rows(v_new.reshape(b, l, -1), lq), l)
        o6 = o_bd.reshape(b, N_KV_B, KV_GROUP, l, N_KV_B, HEAD_DIM)
        o = (o6 * eye[None, :, None, None, :, None]).sum(axis=4)
        o = o.transpose(0, 3, 1, 2, 4).reshape(m, MIX_W)
        x = _out_proj(x, o, memory(q_mem, 1), p["w_out"][1], grouped=False)
    else:
        kh, vh, kbar = kv[2], kv[3], kv[4]
        nb = l // MOBA_BLOCK
        kbar = kbar.reshape(b, nb, N_KV_B, HEAD_DIM).transpose(2, 0, 1, 3)
        o4 = _moba_prompt(qh, kh, vh, kbar, b, l)
        x = _out_proj(x, o4, memory(q_mem, 1), p["w_out"][1], grouped=True)
    y = ffn(x, "ffn2", 1, p["final_norm"])
    return (y.reshape(b, l, d), new_conv[None], s_fin[None],
            k_new.reshape(b, l, N_KV_B, HEAD_DIM), v_new.reshape(b, l, N_KV_B, HEAD_DIM))


def kernel(x_prompt, x_sample, mem_prompt, cache_mem_k, cache_mem_v, state_dn_conv, state_dn_S,
           cache_kv_k, cache_kv_v, page_table,
           ffn1_norm, ffn1_w1, ffn1_w3, ffn1_w2, mix_norm, w_in_a, w_in_b, w_out,
           dn_conv_w, dn_a_log, dn_dt_bias, dn_out_norm, kv_norm, w_kv, mem_norm, w_mem_kv,
           ffn2_norm, ffn2_w1, ffn2_w3, ffn2_w2, final_norm):
    assert w_in_a.shape[0] == 1 and w_in_b.shape[0] == 1 and ffn1_w1.shape[0] == 2
    bf = lambda a: a.astype(BF16)
    p = dict(ffn1_norm=ffn1_norm, ffn1_w1=bf(ffn1_w1), ffn1_w3=bf(ffn1_w3), ffn1_w2=bf(ffn1_w2),
             ffn2_norm=ffn2_norm, ffn2_w1=bf(ffn2_w1), ffn2_w3=bf(ffn2_w3), ffn2_w2=bf(ffn2_w2),
             mix_norm=mix_norm, w_in_a=_pack_w_in_a(w_in_a[0]), w_in_b=bf(w_in_b[0]), w_out=bf(w_out),
             dn_conv_w=dn_conv_w[0], dn_a_log=dn_a_log[0], dn_dt_bias=dn_dt_bias[0],
             dn_out_norm=dn_out_norm[0], kv_norm=kv_norm, w_kv=bf(w_kv), final_norm=final_norm)

    bp, lp, _ = x_prompt.shape
    assert lp % MOBA_BLOCK == 0
    n_mem = mem_prompt.shape[1]
    mem_flat = mem_prompt.reshape(bp * n_mem, -1)
    mk, mv = [], []
    for layer in range(mem_norm.shape[0]):
        k_l, v_l = _norm_proj(mem_flat, mem_norm[layer], bf(w_mem_kv[layer]), (MEM_W, MEM_W), "mem_kv")
        mk.append(k_l.reshape(bp, n_mem, MEM_W))
        mv.append(v_l.reshape(bp, n_mem, MEM_W))
    conv0 = jnp.zeros((bp, CONV_W - 1, 3 * MIX_W), F32)
    s0 = jnp.zeros((bp, N_HEADS, HEAD_DIM, HEAD_DIM), F32)
    y_p, conv_p, s_p, k_p, v_p = _trunk(p, x_prompt, 0, mk, mv, conv0, s0, None)
    mem_k_p = jnp.stack(mk).reshape(-1, bp, n_mem, N_MEM_HEADS, HEAD_DIM)
    mem_v_p = jnp.stack(mv).reshape(-1, bp, n_mem, N_MEM_HEADS, HEAD_DIM)

    bs = x_sample.shape[0]
    n_pool, page = cache_kv_k.shape[:2]
    past_len = page_table.shape[1] * page
    assert past_len % MOBA_BLOCK == 0 and x_sample.shape[1] <= MOBA_BLOCK
    cmk = cache_mem_k.reshape(cache_mem_k.shape[0], bs, n_mem, MEM_W)
    cmv = cache_mem_v.reshape(cache_mem_v.shape[0], bs, n_mem, MEM_W)
    paged = (page_table, cache_kv_k.reshape(n_pool, page, -1), cache_kv_v.reshape(n_pool, page, -1))
    y_s, conv_s, s_s, k_s, v_s = _trunk(p, x_sample, past_len, cmk, cmv, state_dn_conv[0],
                                        state_dn_S[0], paged)
    return (y_p, y_s, conv_p, s_p, k_p, v_p, mem_k_p, mem_v_p, conv_s, s_s, k_s, v_s)
```

```python
import functools
import math

import jax
import jax.numpy as jnp
from jax import lax
from jax.experimental import pallas as pl
from jax.experimental.pallas import tpu as pltpu

F32 = jnp.float32
BF16 = jnp.bfloat16
HI = lax.Precision.HIGHEST

HEAD_DIM = 64
N_HEADS = 12
MIX_W = N_HEADS * HEAD_DIM
N_MEM_HEADS = 4
MEM_W = N_MEM_HEADS * HEAD_DIM
N_KV_B = 4
KV_GROUP = N_HEADS // N_KV_B
GROUP_W = KV_GROUP * HEAD_DIM
CONV_W = 4
DN_CHUNK = 64
MOBA_BLOCK = 256
MOBA_TOPK = 3
ROT_DIM = HEAD_DIM // 4
ROPE_THETA = 500000.0
EPS = 1e-6
QK_SCALE = HEAD_DIM ** -0.5

LANES = 128
SUBLANES = 8
VMEM_BYTES_V7X = 64 * 2 ** 20
NEG_INF = float("-inf")

NT = (((1,), (1,)), ((), ()))
TN = (((0,), (0,)), ((), ()))


def _params(semantics, est_bytes):
    limit = min(max(int(est_bytes * 1.5), 16 * 2 ** 20), VMEM_BYTES_V7X - 8 * 2 ** 20)
    return pltpu.CompilerParams(dimension_semantics=semantics, vmem_limit_bytes=limit)


def _row_tile(m, pref):
    t = min(m, pref)
    while m % t or t % SUBLANES:
        t -= 1
    return t


def _rms(x, g):
    return x * lax.rsqrt(jnp.mean(x * x, axis=-1, keepdims=True) + EPS) * g


def _sigmoid(x):
    return 1.0 / (1.0 + jnp.exp(-x))


def _softplus(x):
    return jnp.maximum(x, 0.0) + jnp.log(1.0 + jnp.exp(-jnp.abs(x)))


def _bdot(a, b):
    return jnp.dot(a.astype(BF16), b.astype(BF16), preferred_element_type=F32)


def _bdot_nt(a, b):
    return lax.dot_general(a.astype(BF16), b.astype(BF16), NT, preferred_element_type=F32)


def _ffn_kernel(*refs, n_f, final):
    if final:
        x_ref, g_ref, w1_ref, w3_ref, w2_ref, fg_ref, o_ref, h_scr, acc_scr = refs
    else:
        x_ref, g_ref, w1_ref, w3_ref, w2_ref, o_ref, h_scr, acc_scr = refs
    f = pl.program_id(1)

    @pl.when(f == 0)
    def _():
        h_scr[...] = _rms(x_ref[...], g_ref[...]).astype(BF16)
        acc_scr[...] = jnp.zeros_like(acc_scr)

    h = h_scr[...]
    a = jnp.dot(h, w1_ref[...], preferred_element_type=F32)
    b = jnp.dot(h, w3_ref[...], preferred_element_type=F32)
    u = (a * _sigmoid(a)) * b
    acc_scr[...] += jnp.dot(u.astype(BF16), w2_ref[...], preferred_element_type=F32)

    @pl.when(f == n_f - 1)
    def _():
        y = x_ref[...] + 0.5 * acc_scr[...]
        if final:
            y = _rms(y, fg_ref[...])
        o_ref[...] = y


def _ffn(x, g, w1, w3, w2, final_gain=None):
    m, d = x.shape
    d_ff = w1.shape[1]
    tm = _row_tile(m, 512)
    n_f = 2 if d_ff % (2 * LANES) == 0 else 1
    tf = d_ff // n_f
    final = final_gain is not None
    in_specs = [
        pl.BlockSpec((tm, d), lambda i, f: (i, 0)),
        pl.BlockSpec((1, d), lambda i, f: (0, 0)),
        pl.BlockSpec((d, tf), lambda i, f: (0, f)),
        pl.BlockSpec((d, tf), lambda i, f: (0, f)),
        pl.BlockSpec((tf, d), lambda i, f: (f, 0)),
    ]
    args = [x, g.reshape(1, d), w1, w3, w2]
    if final:
        in_specs.append(pl.BlockSpec((1, d), lambda i, f: (0, 0)))
        args.append(final_gain.reshape(1, d))
    est = 4 * tm * d * 4 + 6 * d * tf * 2 + tm * d * 6 + 3 * tm * tf * 4
    return pl.pallas_call(
        functools.partial(_ffn_kernel, n_f=n_f, final=final),
        grid=(m // tm, n_f),
        in_specs=in_specs,
        out_specs=pl.BlockSpec((tm, d), lambda i, f: (i, 0)),
        out_shape=jax.ShapeDtypeStruct((m, d), F32),
        scratch_shapes=[pltpu.VMEM((tm, d), BF16), pltpu.VMEM((tm, d), F32)],
        compiler_params=_params(("parallel", "arbitrary"), est),
        name="ffn_final" if final else "ffn",
    )(*args)


def _norm_proj_kernel(x_ref, g_ref, w_ref, *out_refs, splits):
    h = _rms(x_ref[...], g_ref[...]).astype(BF16)
    for o_ref, (a, b) in zip(out_refs, splits):
        o_ref[...] = jnp.dot(h, w_ref[:, a:b], preferred_element_type=F32)


def _norm_proj(x, g, w, widths, name):
    m, d = x.shape
    n = w.shape[1]
    assert sum(widths) == n and all(c % LANES == 0 for c in widths)
    tm = _row_tile(m, 256)
    splits, a = [], 0
    for c in widths:
        splits.append((a, a + c))
        a += c
    est = 2 * tm * d * 4 + 2 * d * n * 2 + 3 * tm * n * 4
    return pl.pallas_call(
        functools.partial(_norm_proj_kernel, splits=tuple(splits)),
        grid=(m // tm,),
        in_specs=[
            pl.BlockSpec((tm, d), lambda i: (i, 0)),
            pl.BlockSpec((1, d), lambda i: (0, 0)),
            pl.BlockSpec((d, n), lambda i: (0, 0)),
        ],
        out_specs=[pl.BlockSpec((tm, c), lambda i: (i, 0)) for c in widths],
        out_shape=[jax.ShapeDtypeStruct((m, c), F32) for c in widths],
        compiler_params=_params(("parallel",), est),
        name=name,
    )(x, g.reshape(1, d), w)


def _rope_table_kernel(inv_ref, c_ref, s1_ref, s2_ref, *, period, base, tl):
    i = pl.program_id(0)
    row = lax.broadcasted_iota(jnp.int32, (tl, LANES), 0) + i * tl
    pos = base + lax.rem(row, period)
    ang = pos.astype(F32) * inv_ref[...]
    in_head = lax.broadcasted_iota(jnp.int32, (tl, LANES), 1) % HEAD_DIM
    half = ROT_DIM // 2
    c = jnp.cos(ang)
    s = jnp.sin(ang)
    c_ref[...] = jnp.where(in_head < ROT_DIM, c, 1.0)
    s1_ref[...] = jnp.where(in_head < half, -s, 0.0)
    s2_ref[...] = jnp.where((in_head >= half) & (in_head < ROT_DIM), s, 0.0)


def _rope_tables(n_rows, period, base):
    half = ROT_DIM // 2
    inv = ROPE_THETA ** (-jnp.arange(half, dtype=F32) * 2.0 / ROT_DIM)
    in_head = jnp.arange(LANES) % HEAD_DIM
    inv_lane = jnp.where(in_head < ROT_DIM, inv[in_head % half], 0.0).astype(F32).reshape(1, LANES)
    tl = _row_tile(n_rows, 1024)
    shp = jax.ShapeDtypeStruct((n_rows, LANES), F32)
    return pl.pallas_call(
        functools.partial(_rope_table_kernel, period=period, base=base, tl=tl),
        grid=(n_rows // tl,),
        in_specs=[pl.BlockSpec((1, LANES), lambda i: (0, 0))],
        out_specs=[pl.BlockSpec((tl, LANES), lambda i: (i, 0))] * 3,
        out_shape=[shp] * 3,
        compiler_params=_params(("parallel",), 16 * tl * LANES * 4),
        name="rope_tables",
    )(inv_lane)


def _apply_rope(t, c, s1, s2):
    w = t.shape[1]
    reps = w // LANES
    half = ROT_DIM // 2
    tile = lambda a: jnp.concatenate([a] * reps, axis=1) if reps > 1 else a
    up = pltpu.roll(t, w - half, 1)
    down = pltpu.roll(t, half, 1)
    return t * tile(c) + up * tile(s1) + down * tile(s2)


def _kv_proj_kernel(x_ref, g_ref, w_ref, c_ref, s1_ref, s2_ref, *out_refs, for_prompt):
    h = _rms(x_ref[...], g_ref[...]).astype(BF16)
    kv_w = N_KV_B * HEAD_DIM
    k = jnp.dot(h, w_ref[:, :kv_w], preferred_element_type=F32)
    v = jnp.dot(h, w_ref[:, kv_w:], preferred_element_type=F32)
    k = _apply_rope(k, c_ref[...], s1_ref[...], s2_ref[...])
    out_refs[0][...] = k
    out_refs[1][...] = v
    if for_prompt:
        kh_ref, vt_ref, kbar_ref = out_refs[2:]
        for hk in range(N_KV_B):
            kh_ref[hk] = k[:, hk * HEAD_DIM:(hk + 1) * HEAD_DIM].astype(BF16)
        vt_ref[0] = v.T.astype(BF16)
        kbar_ref[0] = jnp.sum(k, axis=0, keepdims=True) * (1.0 / MOBA_BLOCK)


def _kv_proj(x, g, w, tables, for_prompt):
    m, d = x.shape
    kv_w = N_KV_B * HEAD_DIM
    tm = MOBA_BLOCK if for_prompt else _row_tile(m, 256)
    n_tab = tables[0].shape[0] // tm
    row = lambda i: (i, 0)
    tab = lambda i: (i % n_tab, 0)
    out_specs = [pl.BlockSpec((tm, kv_w), row), pl.BlockSpec((tm, kv_w), row)]
    out_shape = [jax.ShapeDtypeStruct((m, kv_w), F32)] * 2
    if for_prompt:
        out_specs.append(pl.BlockSpec((N_KV_B, tm, HEAD_DIM), lambda i: (0, i, 0)))
        out_shape.append(jax.ShapeDtypeStruct((N_KV_B, m, HEAD_DIM), BF16))
        out_specs.append(pl.BlockSpec((1, kv_w, tm), lambda i: (i, 0, 0)))
        out_shape.append(jax.ShapeDtypeStruct((m // tm, kv_w, tm), BF16))
        out_specs.append(pl.BlockSpec((1, 1, kv_w), lambda i: (i, 0, 0)))
        out_shape.append(jax.ShapeDtypeStruct((m // tm, 1, kv_w), F32))
    est = 2 * tm * d * 4 + 2 * d * 2 * kv_w * 2 + 12 * tm * kv_w * 4
    return pl.pallas_call(
        functools.partial(_kv_proj_kernel, for_prompt=for_prompt),
        grid=(m // tm,),
        in_specs=[
            pl.BlockSpec((tm, d), row),
            pl.BlockSpec((1, d), lambda i: (0, 0)),
            pl.BlockSpec((d, 2 * kv_w), lambda i: (0, 0)),
            pl.BlockSpec((tm, LANES), tab),
            pl.BlockSpec((tm, LANES), tab),
            pl.BlockSpec((tm, LANES), tab),
        ],
        out_specs=out_specs,
        out_shape=out_shape,
        compiler_params=_params(("parallel",), est),
        name="kv_proj",
    )(x, g.reshape(1, d), w, *tables)


def _q_proj_kernel(x_ref, g_ref, w_ref, c_ref, s1_ref, s2_ref, qh_ref, qm_ref):
    h = _rms(x_ref[...], g_ref[...]).astype(BF16)
    q = jnp.dot(h, w_ref[:, :MIX_W], preferred_element_type=F32)
    qm_ref[...] = jnp.dot(h, w_ref[:, MIX_W:], preferred_element_type=F32)
    q = _apply_rope(q, c_ref[...], s1_ref[...], s2_ref[...]) * QK_SCALE
    for hk in range(N_KV_B):
        qh_ref[hk] = q[:, hk * GROUP_W:(hk + 1) * GROUP_W]


def _q_proj(x, g, w, tables):
    m, d = x.shape
    n = w.shape[1]
    tm = _row_tile(m, 256)
    n_tab = tables[0].shape[0] // tm
    row = lambda i: (i, 0)
    tab = lambda i: (i % n_tab, 0)
    est = 2 * tm * d * 4 + 2 * d * n * 2 + 16 * tm * MIX_W * 4
    return pl.pallas_call(
        _q_proj_kernel,
        grid=(m // tm,),
        in_specs=[
            pl.BlockSpec((tm, d), row),
            pl.BlockSpec((1, d), lambda i: (0, 0)),
            pl.BlockSpec((d, n), lambda i: (0, 0)),
            pl.BlockSpec((tm, LANES), tab),
            pl.BlockSpec((tm, LANES), tab),
            pl.BlockSpec((tm, LANES), tab),
        ],
        out_specs=[
            pl.BlockSpec((N_KV_B, tm, GROUP_W), lambda i: (0, i, 0)),
            pl.BlockSpec((tm, MEM_W), row),
        ],
        out_shape=[
            jax.ShapeDtypeStruct((N_KV_B, m, GROUP_W), F32),
            jax.ShapeDtypeStruct((m, MEM_W), F32),
        ],
        compiler_params=_params(("parallel",), est),
        name="q_proj",
    )(x, g.reshape(1, d), w, *tables)


def _conv_kernel(u_ref, prev_ref, buf_ref, w_ref, y_ref, nb_ref, pad_scr, *, tl, n_t):
    i = pl.program_id(1)
    halo = CONV_W - 1
    lo = SUBLANES - halo
    pad_scr[SUBLANES:SUBLANES + tl, :] = u_ref[0]

    @pl.when(i == 0)
    def _():
        pad_scr[lo:SUBLANES, :] = buf_ref[0]

    @pl.when(i > 0)
    def _():
        p = prev_ref[0]
        pad_scr[lo:SUBLANES, :] = p[p.shape[0] - halo:, :]

    y = pad_scr[lo:lo + tl, :] * w_ref[0:1, :]
    for j in range(1, CONV_W):
        y = y + pad_scr[lo + j:lo + j + tl, :] * w_ref[j:j + 1, :]
    y_ref[0] = y * _sigmoid(y)

    @pl.when(i == n_t - 1)
    def _():
        nb_ref[0] = pad_scr[lo + tl:SUBLANES + tl, :]


def _conv(u, buf, w):
    b, l, c = u.shape
    tl = _row_tile(l, 512) if l % SUBLANES == 0 else l
    n_t = l // tl
    pr = min(SUBLANES, l)
    per = tl // pr
    est = 6 * tl * c * 4
    return pl.pallas_call(
        functools.partial(_conv_kernel, tl=tl, n_t=n_t),
        grid=(b, n_t),
        in_specs=[
            pl.BlockSpec((1, tl, c), lambda bi, i: (bi, i, 0)),
            pl.BlockSpec((1, pr, c), lambda bi, i: (bi, jnp.maximum(i * per - 1, 0), 0)),
            pl.BlockSpec((1, CONV_W - 1, c), lambda bi, i: (bi, 0, 0)),
            pl.BlockSpec((CONV_W, c), lambda bi, i: (0, 0)),
        ],
        out_specs=[
            pl.BlockSpec((1, tl, c), lambda bi, i: (bi, i, 0)),
            pl.BlockSpec((1, CONV_W - 1, c), lambda bi, i: (bi, 0, 0)),
        ],
        out_shape=[
            jax.ShapeDtypeStruct((b, l, c), F32),
            jax.ShapeDtypeStruct((b, CONV_W - 1, c), F32),
        ],
        scratch_shapes=[pltpu.VMEM((tl + SUBLANES, c), F32)],
        compiler_params=_params(("parallel", "arbitrary"), est),
        name="short_conv",
    )(u, u, buf, w)


def _unit_lower_inverse(m_strict, eye):
    c = m_strict.shape[0]
    x = eye - m_strict
    p = jnp.dot(m_strict, m_strict, precision=HI, preferred_element_type=F32)
    power = 2
    while 2 * power < c:
        xp = jnp.dot(jnp.concatenate([x, p], axis=0), p, precision=HI, preferred_element_type=F32)
        x = x + xp[:c]
        p = xp[c:]
        power *= 2
    return x + jnp.dot(x, p, precision=HI, preferred_element_type=F32)


def _gdn_kernel(q_ref, k_ref, v_ref, gate_ref, ab_ref, alog_ref, dt_ref, gain_ref, s0_ref,
                o_ref, sfin_ref, s_scr, g_scr, b_scr, *, t_rows, n_t, l_valid):
    t = pl.program_id(1)
    c = DN_CHUNK
    d = HEAD_DIM

    @pl.when(t == 0)
    def _():
        s_scr[...] = s0_ref[0]

    ab = ab_ref[0]
    row = lax.broadcasted_iota(jnp.int32, ab.shape, 0) + t * t_rows
    valid = row < l_valid
    g_scr[...] = jnp.where(valid, -jnp.exp(alog_ref[...]) * _softplus(ab + dt_ref[...]), 0.0)
    b_scr[...] = jnp.where(valid, _sigmoid(ab), 0.0)

    ri = lax.broadcasted_iota(jnp.int32, (c, c), 0)
    ci = lax.broadcasted_iota(jnp.int32, (c, c), 1)
    lower = ri >= ci
    strict = ri > ci
    eye = jnp.where(ri == ci, 1.0, 0.0).astype(F32)
    tril = jnp.where(lower, 1.0, 0.0).astype(F32)
    triu = jnp.where(ri <= ci, 1.0, 0.0).astype(F32)
    er = lax.broadcasted_iota(jnp.int32, (LANES, MIX_W), 0)
    ec = lax.broadcasted_iota(jnp.int32, (LANES, MIX_W), 1) // d
    sel_g = jnp.where(er == ec, 1.0, 0.0).astype(F32)
    sel_b = jnp.where(er == ec + N_HEADS, 1.0, 0.0).astype(F32)
    gain = gain_ref[...]

    def chunk(ci_, carry):
        r0 = pl.multiple_of(ci_ * c, c)
        rows = pl.ds(r0, c)
        g_c = g_scr[rows, :]
        gc = jnp.dot(tril, g_c, precision=HI, preferred_element_type=F32)
        gcb = jnp.dot(gc, sel_g, precision=HI, preferred_element_type=F32)
        gct = lax.dot_general(g_c, triu, TN, precision=HI, preferred_element_type=F32)
        bb = jnp.dot(b_scr[rows, :], sel_b, precision=HI, preferred_element_type=F32)
        glast = gcb[c - 1:c, :]
        eg = jnp.exp(gcb)
        ekd = jnp.exp(glast - gcb)
        egl = jnp.exp(glast)
        q_c = q_ref[0, rows, :]
        k_c = k_ref[0, rows, :]
        v_c = v_ref[0, rows, :]
        gate_c = gate_ref[0, rows, :]
        outs = []
        for h in range(N_HEADS):
            sl = slice(h * d, (h + 1) * d)
            q = q_c[:, sl]
            k = k_c[:, sl]
            v = v_c[:, sl]
            qn = q * lax.rsqrt(jnp.sum(q * q, axis=-1, keepdims=True) + EPS) * QK_SCALE
            kn = k * lax.rsqrt(jnp.sum(k * k, axis=-1, keepdims=True) + EPS)
            beta = bb[:, sl]
            kb = kn * beta
            vb = v * beta
            diff = gcb[:, sl] - gct[h:h + 1, :]
            dec = jnp.exp(jnp.where(lower, diff, NEG_INF))
            m_strict = jnp.where(strict, _bdot_nt(kb, kn) * dec, 0.0)
            a_intra = _bdot_nt(qn, kn) * dec
            tinv = _unit_lower_inverse(m_strict, eye)
            rhs = jnp.concatenate([vb, kb * eg[:, sl]], axis=1)
            sol = jnp.dot(tinv, rhs, precision=HI, preferred_element_type=F32)
            u = sol[:, :d]
            w = sol[:, d:]
            s_old = s_scr[h]
            v_new = u - _bdot(w, s_old)
            o = _bdot(qn * eg[:, sl], s_old) + _bdot(a_intra, v_new)
            kd = kn * ekd[:, sl]
            s_scr[h] = s_old * egl[:, sl] + lax.dot_general(
                kd.astype(BF16), v_new.astype(BF16), TN, preferred_element_type=F32)
            o = o * lax.rsqrt(jnp.mean(o * o, axis=-1, keepdims=True) + EPS) * gain[:, sl]
            gt = gate_c[:, sl]
            outs.append(o * (gt * _sigmoid(gt)))
        o_ref[0, rows, :] = jnp.concatenate(outs, axis=1)
        return carry

    lax.fori_loop(0, t_rows // c, chunk, 0)

    @pl.when(t == n_t - 1)
    def _():
        sfin_ref[0] = s_scr[...]


def _gdn(qkv, gate, ab, a_log, dt_bias, out_gain, s0, l_valid):
    b, lp, _ = qkv.shape
    assert lp % DN_CHUNK == 0
    t_rows = DN_CHUNK * math.gcd(lp // DN_CHUNK, 8)
    n_t = lp // t_rows
    pad = lambda a: jnp.pad(a.astype(F32), (0, LANES - N_HEADS)).reshape(1, LANES)
    blk = lambda j: pl.BlockSpec((1, t_rows, MIX_W), lambda bi, t, j=j: (bi, t, j))
    state = pl.BlockSpec((1, N_HEADS, HEAD_DIM, HEAD_DIM), lambda bi, t: (bi, 0, 0, 0))
    vec = pl.BlockSpec((1, LANES), lambda bi, t: (0, 0))
    est = 10 * t_rows * MIX_W * 4 + 40 * DN_CHUNK * MIX_W * 4 + 6 * N_HEADS * HEAD_DIM * LANES * 4
    return pl.pallas_call(
        functools.partial(_gdn_kernel, t_rows=t_rows, n_t=n_t, l_valid=l_valid),
        grid=(b, n_t),
        in_specs=[
            blk(0), blk(1), blk(2),
            pl.BlockSpec((1, t_rows, MIX_W), lambda bi, t: (bi, t, 0)),
            pl.BlockSpec((1, t_rows, LANES), lambda bi, t: (bi, t, 0)),
            vec, vec,
            pl.BlockSpec((1, MIX_W), lambda bi, t: (0, 0)),
            state,
        ],
        out_specs=[pl.BlockSpec((1, t_rows, MIX_W), lambda bi, t: (bi, t, 0)), state],
        out_shape=[
            jax.ShapeDtypeStruct((b, lp, MIX_W), F32),
            jax.ShapeDtypeStruct((b, N_HEADS, HEAD_DIM, HEAD_DIM), F32),
        ],
        scratch_shapes=[
            pltpu.VMEM((N_HEADS, HEAD_DIM, HEAD_DIM), F32),
            pltpu.VMEM((t_rows, LANES), F32),
            pltpu.VMEM((t_rows, LANES), F32),
        ],
        compiler_params=_params(("parallel", "arbitrary"), est),
        name="gated_delta_rule",
    )(qkv, qkv, qkv, gate, ab, pad(a_log), pad(dt_bias),
      jnp.tile(out_gain.astype(F32), N_HEADS).reshape(1, MIX_W), s0)


def _mem_attn_kernel(q_ref, k_ref, v_ref, o_ref):
    q = q_ref[0]
    k = k_ref[0]
    v = v_ref[0]
    outs = []
    for h in range(N_MEM_HEADS):
        sl = slice(h * HEAD_DIM, (h + 1) * HEAD_DIM)
        s = _bdot_nt(q[:, sl], k[:, sl]) * QK_SCALE
        p = jnp.exp(s - jnp.max(s, axis=-1, keepdims=True))
        outs.append(_bdot(p, v[:, sl]) / jnp.sum(p, axis=-1, keepdims=True))
    o_ref[0] = jnp.concatenate(outs, axis=1)


def _mem_attn(q, mk, mv):
    b, l, w = q.shape
    n_mem = mk.shape[1]
    tl = _row_tile(l, 512)
    est = 4 * tl * w * 4 + 4 * n_mem * w * 4 + 8 * tl * n_mem * 4
    return pl.pallas_call(
        _mem_attn_kernel,
        grid=(b, l // tl),
        in_specs=[
            pl.BlockSpec((1, tl, w), lambda bi, i: (bi, i, 0)),
            pl.BlockSpec((1, n_mem, w), lambda bi, i: (bi, 0, 0)),
            pl.BlockSpec((1, n_mem, w), lambda bi, i: (bi, 0, 0)),
        ],
        out_specs=pl.BlockSpec((1, tl, w), lambda bi, i: (bi, i, 0)),
        out_shape=jax.ShapeDtypeStruct((b, l, w), F32),
        compiler_params=_params(("parallel", "parallel"), est),
        name="mem_attn",
    )(q, mk, mv)


def _out_proj_kernel(x_ref, o_ref, mo_ref, wo_ref, wm_ref, y_ref, *, grouped):
    y = x_ref[...] + _bdot(mo_ref[...], wm_ref[...])
    if grouped:
        for hk in range(N_KV_B):
            y = y + _bdot(o_ref[hk], wo_ref[hk])
    else:
        y = y + _bdot(o_ref[...], wo_ref[...])
    y_ref[...] = y


def _out_proj(x, o, mo, w_out, grouped):
    m, d = x.shape
    tm = _row_tile(m, 512)
    row = lambda i: (i, 0)
    if grouped:
        wo = w_out[:MIX_W].reshape(N_KV_B, GROUP_W, d)
        o_spec = pl.BlockSpec((N_KV_B, tm, GROUP_W), lambda i: (0, i, 0))
        wo_spec = pl.BlockSpec((N_KV_B, GROUP_W, d), lambda i: (0, 0, 0))
    else:
        wo = w_out[:MIX_W]
        o_spec = pl.BlockSpec((tm, MIX_W), row)
        wo_spec = pl.BlockSpec((MIX_W, d), lambda i: (0, 0))
    est = 6 * tm * d * 4 + 4 * tm * d * 4 + 4 * d * d * 2
    return pl.pallas_call(
        functools.partial(_out_proj_kernel, grouped=grouped),
        grid=(m // tm,),
        in_specs=[
            pl.BlockSpec((tm, d), row),
            o_spec,
            pl.BlockSpec((tm, MEM_W), row),
            wo_spec,
            pl.BlockSpec((MEM_W, d), lambda i: (0, 0)),
        ],
        out_specs=pl.BlockSpec((tm, d), row),
        out_shape=jax.ShapeDtypeStruct((m, d), F32),
        compiler_params=_params(("parallel",), est),
        name="out_proj",
    )(x, o, mo, wo, w_out[MIX_W:])


def _select_blocks(gate, n_past, k_sel):
    lane = lax.broadcasted_iota(jnp.int32, gate.shape, 1)
    lane_f = lane.astype(F32)
    gate = jnp.where(lane < n_past, gate, NEG_INF)
    sel = jnp.zeros(gate.shape, F32)
    for _ in range(k_sel):
        mx = jnp.max(gate, axis=1, keepdims=True)
        first = jnp.min(jnp.where(gate == mx, lane_f, float(gate.shape[1])), axis=1, keepdims=True)
        pick = (lane_f == first) & (mx > NEG_INF)
        sel = jnp.where(pick, 1.0, sel)
        gate = jnp.where(pick, NEG_INF, gate)
    return sel


def _softmax_update(s, v, m_scr, l_scr, acc_scr):
    m_old = m_scr[...]
    m_new = jnp.maximum(m_old, jnp.max(s, axis=1, keepdims=True))
    alpha = jnp.exp(m_old - m_new)
    p = jnp.exp(s - m_new)
    l_scr[...] = alpha * l_scr[...] + jnp.sum(p, axis=1, keepdims=True)
    acc_scr[...] = alpha * acc_scr[...] + _bdot(p, v)
    m_scr[...] = m_new


def _select_blocks_t(gate, n_past, k_sel):
    blk = lax.broadcasted_iota(jnp.int32, gate.shape, 0)
    blk_f = blk.astype(F32)
    gate = jnp.where(blk < n_past, gate, NEG_INF)
    sel = jnp.zeros(gate.shape, F32)
    for _ in range(k_sel):
        mx = jnp.max(gate, axis=0, keepdims=True)
        first = jnp.min(jnp.where(gate == mx, blk_f, float(gate.shape[0])), axis=0, keepdims=True)
        pick = (blk_f == first) & (mx > NEG_INF)
        sel = jnp.where(pick, 1.0, sel)
        gate = jnp.where(pick, NEG_INF, gate)
    return sel


def _moba_prompt_kernel(q_ref, k_ref, vt_ref, kbar_ref, o_ref, q_scr, m_scr, l_scr, acc_scr, sel_scr,
                        *, k_sel):
    i = pl.program_id(2)
    blk = MOBA_BLOCK
    d = HEAD_DIM
    n_chunks = KV_GROUP * blk // LANES
    q = q_ref[0]
    q3 = jnp.concatenate([q[:, g * d:(g + 1) * d] for g in range(KV_GROUP)], axis=0)
    gate_t = lax.dot_general(kbar_ref[0, 0], q3, NT, precision=HI, preferred_element_type=F32)
    sel_scr[...] = _select_blocks_t(gate_t, i, k_sel)
    q_scr[...] = q3.astype(BF16)

    def block_scores(k_blk, c):
        return lax.dot_general(k_blk, q_scr[c * LANES:(c + 1) * LANES, :], NT,
                               preferred_element_type=F32)

    k_own = k_ref[0, pl.ds(pl.multiple_of(i * blk, blk), blk), :]
    vt_own = vt_ref[i, :, :]
    scores = [block_scores(k_own, c) for c in range(n_chunks)]
    probs = []
    for c in range(n_chunks):
        cs = slice(c * LANES, (c + 1) * LANES)
        kpos = lax.broadcasted_iota(jnp.int32, (blk, LANES), 0)
        qpos = (lax.broadcasted_iota(jnp.int32, (blk, LANES), 1) + c * LANES) % blk
        s = jnp.where(kpos <= qpos, scores[c], NEG_INF)
        m0 = jnp.max(s, axis=0, keepdims=True)
        p = jnp.exp(s - m0)
        m_scr[:, cs] = m0
        l_scr[:, cs] = jnp.sum(p, axis=0, keepdims=True)
        probs.append(p.astype(BF16))
    for c in range(n_chunks):
        acc_scr[:, c * LANES:(c + 1) * LANES] = jnp.dot(vt_own, probs[c], preferred_element_type=F32)

    def past(j, carry):
        k_j = k_ref[0, pl.ds(pl.multiple_of(j * blk, blk), blk), :]
        vt_j = vt_ref[j, :, :]
        chosen = sel_scr[pl.ds(j, 1), :]
        scores = [block_scores(k_j, c) for c in range(n_chunks)]
        probs, alphas = [], []
        for c in range(n_chunks):
            cs = slice(c * LANES, (c + 1) * LANES)
            s = jnp.where(chosen[:, cs] > 0.0, scores[c], NEG_INF)
            m_old = m_scr[:, cs]
            m_new = jnp.maximum(m_old, jnp.max(s, axis=0, keepdims=True))
            alpha = jnp.exp(m_old - m_new)
            p = jnp.exp(s - m_new)
            l_scr[:, cs] = alpha * l_scr[:, cs] + jnp.sum(p, axis=0, keepdims=True)
            m_scr[:, cs] = m_new
            probs.append(p.astype(BF16))
            alphas.append(alpha)
        for c in range(n_chunks):
            cs = slice(c * LANES, (c + 1) * LANES)
            acc_scr[:, cs] = alphas[c] * acc_scr[:, cs] + jnp.dot(vt_j, probs[c],
                                                                  preferred_element_type=F32)
        return carry

    lax.fori_loop(0, i, past, 0)
    o_t = (acc_scr[...] / l_scr[...]).astype(BF16)
    eye = (lax.broadcasted_iota(jnp.int32, (blk, blk), 0)
           == lax.broadcasted_iota(jnp.int32, (blk, blk), 1)).astype(BF16)
    o_ref[0] = jnp.concatenate(
        [lax.dot_general(eye, o_t[:, g * blk:(g + 1) * blk], NT, preferred_element_type=F32)
         for g in range(KV_GROUP)], axis=1)


def _moba_prompt(qh, kh, vt, kbar, b, l):
    nb = l // MOBA_BLOCK
    rows = KV_GROUP * MOBA_BLOCK
    k_sel = min(MOBA_TOPK, nb)
    est = 4 * l * LANES * 2 + 4 * l * HEAD_DIM * 2 + 8 * MOBA_BLOCK * GROUP_W * 4 + 24 * rows * LANES * 4
    return pl.pallas_call(
        functools.partial(_moba_prompt_kernel, k_sel=k_sel),
        grid=(N_KV_B, b, nb),
        in_specs=[
            pl.BlockSpec((1, MOBA_BLOCK, GROUP_W), lambda hk, bi, i: (hk, bi * nb + i, 0)),
            pl.BlockSpec((1, l, HEAD_DIM), lambda hk, bi, i: (hk, bi, 0)),
            pl.BlockSpec((nb, HEAD_DIM, MOBA_BLOCK), lambda hk, bi, i: (bi, hk, 0)),
            pl.BlockSpec((1, 1, nb, HEAD_DIM), lambda hk, bi, i: (hk, bi, 0, 0)),
        ],
        out_specs=pl.BlockSpec((1, MOBA_BLOCK, GROUP_W), lambda hk, bi, i: (hk, bi * nb + i, 0)),
        out_shape=jax.ShapeDtypeStruct((N_KV_B, b * l, GROUP_W), F32),
        scratch_shapes=[
            pltpu.VMEM((rows, HEAD_DIM), BF16),
            pltpu.VMEM((1, rows), F32),
            pltpu.VMEM((1, rows), F32),
            pltpu.VMEM((HEAD_DIM, rows), F32),
            pltpu.VMEM((nb, rows), F32),
        ],
        compiler_params=_params(("parallel", "parallel", "arbitrary"), est),
        name="moba_prompt",
    )(qh, kh, vt, kbar)


def _page_means_kernel(pt_ref, k0_ref, k1_ref, o_ref):
    j = pl.program_id(1)
    s = jnp.sum(k0_ref[0], axis=0, keepdims=True) + jnp.sum(k1_ref[0], axis=0, keepdims=True)
    o_ref[0, pl.ds(j, 1), :] = s * (1.0 / MOBA_BLOCK)


def _page_means(page_table, cache_k):
    bs, n_pages = page_table.shape
    page, w = cache_k.shape[1:]
    assert 2 * page == MOBA_BLOCK
    nb = n_pages // 2
    spec = lambda off: pl.BlockSpec((1, page, w), lambda b, j, pt: (pt[b, 2 * j + off], 0, 0))
    return pl.pallas_call(
        _page_means_kernel,
        grid_spec=pltpu.PrefetchScalarGridSpec(
            num_scalar_prefetch=1,
            grid=(bs, nb),
            in_specs=[spec(0), spec(1)],
            out_specs=pl.BlockSpec((1, nb, w), lambda b, j, pt: (b, 0, 0)),
        ),
        out_shape=jax.ShapeDtypeStruct((bs, nb, w), F32),
        compiler_params=_params(("parallel", "arbitrary"), 8 * page * w * 4),
        name="page_means",
    )(page_table, cache_k, cache_k)


def _moba_sample_kernel(pt_ref, q_ref, kbar_ref, k0_ref, k1_ref, v0_ref, v1_ref, kn_ref, vn_ref,
                        o_ref, m_scr, l_scr, acc_scr, sel_scr, *, n_blocks, l_new, k_sel):
    j = pl.program_id(1)
    q = q_ref[0]
    qb = q.astype(BF16)

    @pl.when(j == 0)
    def _():
        gate = lax.dot_general(q, kbar_ref[0], NT, precision=HI, preferred_element_type=F32)
        sel_scr[...] = _select_blocks(gate, n_blocks, k_sel)
        s = _bdot_nt(qb, kn_ref[0])
        t_q = lax.broadcasted_iota(jnp.int32, s.shape, 0) % l_new
        t_k = lax.broadcasted_iota(jnp.int32, s.shape, 1)
        s = jnp.where(t_k <= t_q, s, NEG_INF)
        m0 = jnp.max(s, axis=1, keepdims=True)
        p = jnp.exp(s - m0)
        m_scr[...] = m0
        l_scr[...] = jnp.sum(p, axis=1, keepdims=True)
        acc_scr[...] = _bdot(p, vn_ref[0])

    sel = sel_scr[...]
    lane = lax.broadcasted_iota(jnp.int32, sel.shape, 1)
    chosen = jnp.max(jnp.where(lane == j, sel, 0.0), axis=1, keepdims=True)
    for k_ref, v_ref in ((k0_ref, v0_ref), (k1_ref, v1_ref)):
        s = _bdot_nt(qb, k_ref[0])
        s = jnp.where(chosen > 0.0, s, NEG_INF)
        _softmax_update(s, v_ref[0], m_scr, l_scr, acc_scr)

    @pl.when(j == n_blocks - 1)
    def _():
        o_ref[0] = acc_scr[...] / l_scr[...]


def _moba_sample(q_bd, kbar, page_table, cache_k, cache_v, k_new, v_new, l_new):
    bs, rows, w = q_bd.shape
    n_pages = page_table.shape[1]
    page = cache_k.shape[1]
    nb = n_pages // 2
    k_sel = min(MOBA_TOPK, nb + 1)
    lp = k_new.shape[1]
    pg = lambda off: pl.BlockSpec((1, page, w), lambda b, j, pt: (pt[b, 2 * j + off], 0, 0))
    per_b = lambda r: pl.BlockSpec((1, r, w), lambda b, j, pt: (b, 0, 0))
    return pl.pallas_call(
        functools.partial(_moba_sample_kernel, n_blocks=nb, l_new=l_new, k_sel=k_sel),
        grid_spec=pltpu.PrefetchScalarGridSpec(
            num_scalar_prefetch=1,
            grid=(bs, nb),
            in_specs=[per_b(rows), per_b(nb), pg(0), pg(1), pg(0), pg(1), per_b(lp), per_b(lp)],
            out_specs=per_b(rows),
            scratch_shapes=[
                pltpu.VMEM((rows, 1), F32),
                pltpu.VMEM((rows, 1), F32),
                pltpu.VMEM((rows, w), F32),
                pltpu.VMEM((rows, nb), F32),
            ],
        ),
        out_shape=jax.ShapeDtypeStruct((bs, rows, w), F32),
        compiler_params=_params(("parallel", "arbitrary"), 16 * page * w * 4),
        name="moba_sample",
    )(page_table, q_bd, kbar, cache_k, cache_k, cache_v, cache_v, k_new, v_new)


def _pack_w_in_a(w):
    d = w.shape[0]
    qkv_gate = w[:, :4 * MIX_W]
    ab = w[:, 4 * MIX_W:4 * MIX_W + 2 * N_HEADS]
    q_mem = w[:, 4 * MIX_W + 2 * N_HEADS:]
    ab = jnp.pad(ab, ((0, 0), (0, LANES - 2 * N_HEADS)))
    return jnp.concatenate([qkv_gate, q_mem, ab], axis=1).astype(BF16)


def _pad_rows(a, rows):
    return jnp.pad(a, ((0, 0), (0, rows - a.shape[1]), (0, 0)))


def _trunk(p, x3, pos_base, mem_k, mem_v, conv0, s0, paged):
    b, l, d = x3.shape
    m = b * l
    x = x3.reshape(m, d)
    sample = paged is not None
    ffn = lambda x, tag, i, fg=None: _ffn(x, p[tag + "_norm"][i], p[tag + "_w1"][i], p[tag + "_w3"][i],
                                          p[tag + "_w2"][i], fg)
    lq = -(-l // SUBLANES) * SUBLANES

    def memory(q_mem, layer):
        q3 = _pad_rows(q_mem.reshape(b, l, MEM_W), lq)
        mo = _mem_attn(q3, mem_k[layer], mem_v[layer])
        return mo[:, :l].reshape(m, MEM_W)

    x = ffn(x, "ffn1", 0)
    qkv, gate, q_mem, ab = _norm_proj(x, p["mix_norm"][0], p["w_in_a"],
                                      (3 * MIX_W, MIX_W, MEM_W, LANES), "in_proj_a")
    qkv, new_conv = _conv(qkv.reshape(b, l, 3 * MIX_W), conv0, p["dn_conv_w"])
    lp = -(-l // DN_CHUNK) * DN_CHUNK
    o, s_fin = _gdn(_pad_rows(qkv, lp), _pad_rows(gate.reshape(b, l, MIX_W), lp),
                    _pad_rows(ab.reshape(b, l, LANES), lp),
                    p["dn_a_log"], p["dn_dt_bias"], p["dn_out_norm"], s0, l)
    o = o[:, :l].reshape(m, MIX_W)
    x = _out_proj(x, o, memory(q_mem, 0), p["w_out"][0], grouped=False)
    x = ffn(x, "ffn2", 0)

    if sample:
        tables = _rope_tables(m, l, pos_base)
    else:
        tables = _rope_tables(l, l, pos_base)
    kv = _kv_proj(x, p["kv_norm"], p["w_kv"], tables, for_prompt=not sample)
    k_new, v_new = kv[0], kv[1]
    x = ffn(x, "ffn1", 1)
    qh, q_mem = _q_proj(x, p["mix_norm"][1], p["w_in_b"], tables)
    if sample:
        page_table, cache_k, cache_v = paged
        kbar = _page_means(page_table, cache_k)
        eye = jnp.eye(N_KV_B, dtype=F32)
        q5 = qh.reshape(N_KV_B, b, l, KV_GROUP, HEAD_DIM).transpose(1, 0, 3, 2, 4)
        q_bd = (q5[:, :, :, :, None, :] * eye[None, :, None, None, :, None]).reshape(
            b, N_HEADS * l, N_KV_B * HEAD_DIM)
        o_bd = _moba_sample(q_bd, kbar, page_table, cache_k, cache_v,
                            _pad_rows(k_new.reshape(b, l, -1), lq),
                            _pad_rows(v_new.reshape(b, l, -1), lq), l)
        o6 = o_bd.reshape(b, N_KV_B, KV_GROUP, l, N_KV_B, HEAD_DIM)
        o = (o6 * eye[None, :, None, None, :, None]).sum(axis=4)
        o = o.transpose(0, 3, 1, 2, 4).reshape(m, MIX_W)
        x = _out_proj(x, o, memory(q_mem, 1), p["w_out"][1], grouped=False)
    else:
        kh, vt, kbar = kv[2], kv[3], kv[4]
        nb = l // MOBA_BLOCK
        kbar = kbar.reshape(b, nb, N_KV_B, HEAD_DIM).transpose(2, 0, 1, 3)
        o4 = _moba_prompt(qh, kh, vt, kbar, b, l)
        x = _out_proj(x, o4, memory(q_mem, 1), p["w_out"][1], grouped=True)
    y = ffn(x, "ffn2", 1, p["final_norm"])
    return (y.reshape(b, l, d), new_conv[None], s_fin[None],
            k_new.reshape(b, l, N_KV_B, HEAD_DIM), v_new.reshape(b, l, N_KV_B, HEAD_DIM))


def kernel(x_prompt, x_sample, mem_prompt, cache_mem_k, cache_mem_v, state_dn_conv, state_dn_S,
           cache_kv_k, cache_kv_v, page_table,
           ffn1_norm, ffn1_w1, ffn1_w3, ffn1_w2, mix_norm, w_in_a, w_in_b, w_out,
           dn_conv_w, dn_a_log, dn_dt_bias, dn_out_norm, kv_norm, w_kv, mem_norm, w_mem_kv,
           ffn2_norm, ffn2_w1, ffn2_w3, ffn2_w2, final_norm):
    assert w_in_a.shape[0] == 1 and w_in_b.shape[0] == 1 and ffn1_w1.shape[0] == 2
    bf = lambda a: a.astype(BF16)
    p = dict(ffn1_norm=ffn1_norm, ffn1_w1=bf(ffn1_w1), ffn1_w3=bf(ffn1_w3), ffn1_w2=bf(ffn1_w2),
             ffn2_norm=ffn2_norm, ffn2_w1=bf(ffn2_w1), ffn2_w3=bf(ffn2_w3), ffn2_w2=bf(ffn2_w2),
             mix_norm=mix_norm, w_in_a=_pack_w_in_a(w_in_a[0]), w_in_b=bf(w_in_b[0]), w_out=bf(w_out),
             dn_conv_w=dn_conv_w[0], dn_a_log=dn_a_log[0], dn_dt_bias=dn_dt_bias[0],
             dn_out_norm=dn_out_norm[0], kv_norm=kv_norm, w_kv=bf(w_kv), final_norm=final_norm)

    bp, lp, _ = x_prompt.shape
    assert lp % MOBA_BLOCK == 0
    n_mem = mem_prompt.shape[1]
    mem_flat = mem_prompt.reshape(bp * n_mem, -1)
    mk, mv = [], []
    for layer in range(mem_norm.shape[0]):
        k_l, v_l = _norm_proj(mem_flat, mem_norm[layer], bf(w_mem_kv[layer]), (MEM_W, MEM_W), "mem_kv")
        mk.append(k_l.reshape(bp, n_mem, MEM_W))
        mv.append(v_l.reshape(bp, n_mem, MEM_W))
    conv0 = jnp.zeros((bp, CONV_W - 1, 3 * MIX_W), F32)
    s0 = jnp.zeros((bp, N_HEADS, HEAD_DIM, HEAD_DIM), F32)
    y_p, conv_p, s_p, k_p, v_p = _trunk(p, x_prompt, 0, mk, mv, conv0, s0, None)
    mem_k_p = jnp.stack(mk).reshape(-1, bp, n_mem, N_MEM_HEADS, HEAD_DIM)
    mem_v_p = jnp.stack(mv).reshape(-1, bp, n_mem, N_MEM_HEADS, HEAD_DIM)

    bs = x_sample.shape[0]
    n_pool, page = cache_kv_k.shape[:2]
    past_len = page_table.shape[1] * page
    assert past_len % MOBA_BLOCK == 0 and x_sample.shape[1] <= MOBA_BLOCK
    cmk = cache_mem_k.reshape(cache_mem_k.shape[0], bs, n_mem, MEM_W)
    cmv = cache_mem_v.reshape(cache_mem_v.shape[0], bs, n_mem, MEM_W)
    paged = (page_table, cache_kv_k.reshape(n_pool, page, -1), cache_kv_v.reshape(n_pool, page, -1))
    y_s, conv_s, s_s, k_s, v_s = _trunk(p, x_sample, past_len, cmk, cmv, state_dn_conv[0],
                                        state_dn_S[0], paged)
    return (y_p, y_s, conv_p, s_p, k_p, v_p, mem_k_p, mem_v_p, conv_s, s_s, k_s, v_s)
```

```python
import functools
import math

import jax
import jax.numpy as jnp
from jax import lax
from jax.experimental import pallas as pl
from jax.experimental.pallas import tpu as pltpu

F32 = jnp.float32
BF16 = jnp.bfloat16
HI = lax.Precision.HIGHEST

HEAD_DIM = 64
N_HEADS = 12
MIX_W = N_HEADS * HEAD_DIM
N_MEM_HEADS = 4
MEM_W = N_MEM_HEADS * HEAD_DIM
N_KV_B = 4
KV_GROUP = N_HEADS // N_KV_B
GROUP_W = KV_GROUP * HEAD_DIM
CONV_W = 4
DN_CHUNK = 64
MOBA_BLOCK = 256
MOBA_TOPK = 3
ROT_DIM = HEAD_DIM // 4
ROPE_THETA = 500000.0
EPS = 1e-6
QK_SCALE = HEAD_DIM ** -0.5

LANES = 128
SUBLANES = 8
VMEM_BYTES_V7X = 64 * 2 ** 20
NEG_INF = float("-inf")

NT = (((1,), (1,)), ((), ()))
TN = (((0,), (0,)), ((), ()))


def _params(semantics, est_bytes):
    limit = min(max(int(est_bytes * 1.5), 16 * 2 ** 20), VMEM_BYTES_V7X - 8 * 2 ** 20)
    return pltpu.CompilerParams(dimension_semantics=semantics, vmem_limit_bytes=limit)


def _row_tile(m, pref):
    t = min(m, pref)
    while m % t or t % SUBLANES:
        t -= 1
    return t


def _rms(x, g):
    return x * lax.rsqrt(jnp.mean(x * x, axis=-1, keepdims=True) + EPS) * g


def _sigmoid(x):
    return 1.0 / (1.0 + jnp.exp(-x))


def _softplus(x):
    return jnp.maximum(x, 0.0) + jnp.log(1.0 + jnp.exp(-jnp.abs(x)))


def _bdot(a, b):
    return jnp.dot(a.astype(BF16), b.astype(BF16), preferred_element_type=F32)


def _bdot_nt(a, b):
    return lax.dot_general(a.astype(BF16), b.astype(BF16), NT, preferred_element_type=F32)


def _ffn_kernel(*refs, n_f, final):
    if final:
        x_ref, g_ref, w1_ref, w3_ref, w2_ref, fg_ref, o_ref, h_scr, acc_scr = refs
    else:
        x_ref, g_ref, w1_ref, w3_ref, w2_ref, o_ref, h_scr, acc_scr = refs
    f = pl.program_id(1)

    @pl.when(f == 0)
    def _():
        h_scr[...] = _rms(x_ref[...], g_ref[...]).astype(BF16)
        acc_scr[...] = jnp.zeros_like(acc_scr)

    h = h_scr[...]
    a = jnp.dot(h, w1_ref[...], preferred_element_type=F32)
    b = jnp.dot(h, w3_ref[...], preferred_element_type=F32)
    u = (a * _sigmoid(a)) * b
    acc_scr[...] += jnp.dot(u.astype(BF16), w2_ref[...], preferred_element_type=F32)

    @pl.when(f == n_f - 1)
    def _():
        y = x_ref[...] + 0.5 * acc_scr[...]
        if final:
            y = _rms(y, fg_ref[...])
        o_ref[...] = y


def _ffn(x, g, w1, w3, w2, final_gain=None):
    m, d = x.shape
    d_ff = w1.shape[1]
    tm = _row_tile(m, 512)
    n_f = 2 if d_ff % (2 * LANES) == 0 else 1
    tf = d_ff // n_f
    final = final_gain is not None
    in_specs = [
        pl.BlockSpec((tm, d), lambda i, f: (i, 0)),
        pl.BlockSpec((1, d), lambda i, f: (0, 0)),
        pl.BlockSpec((d, tf), lambda i, f: (0, f)),
        pl.BlockSpec((d, tf), lambda i, f: (0, f)),
        pl.BlockSpec((tf, d), lambda i, f: (f, 0)),
    ]
    args = [x, g.reshape(1, d), w1, w3, w2]
    if final:
        in_specs.append(pl.BlockSpec((1, d), lambda i, f: (0, 0)))
        args.append(final_gain.reshape(1, d))
    est = 4 * tm * d * 4 + 6 * d * tf * 2 + tm * d * 6 + 3 * tm * tf * 4
    return pl.pallas_call(
        functools.partial(_ffn_kernel, n_f=n_f, final=final),
        grid=(m // tm, n_f),
        in_specs=in_specs,
        out_specs=pl.BlockSpec((tm, d), lambda i, f: (i, 0)),
        out_shape=jax.ShapeDtypeStruct((m, d), F32),
        scratch_shapes=[pltpu.VMEM((tm, d), BF16), pltpu.VMEM((tm, d), F32)],
        compiler_params=_params(("parallel", "arbitrary"), est),
        name="ffn_final" if final else "ffn",
    )(*args)


def _norm_proj_kernel(x_ref, g_ref, w_ref, *out_refs, splits):
    h = _rms(x_ref[...], g_ref[...]).astype(BF16)
    for o_ref, (a, b) in zip(out_refs, splits):
        o_ref[...] = jnp.dot(h, w_ref[:, a:b], preferred_element_type=F32)


def _norm_proj(x, g, w, widths, name):
    m, d = x.shape
    n = w.shape[1]
    assert sum(widths) == n and all(c % LANES == 0 for c in widths)
    tm = _row_tile(m, 256)
    splits, a = [], 0
    for c in widths:
        splits.append((a, a + c))
        a += c
    est = 2 * tm * d * 4 + 2 * d * n * 2 + 3 * tm * n * 4
    return pl.pallas_call(
        functools.partial(_norm_proj_kernel, splits=tuple(splits)),
        grid=(m // tm,),
        in_specs=[
            pl.BlockSpec((tm, d), lambda i: (i, 0)),
            pl.BlockSpec((1, d), lambda i: (0, 0)),
            pl.BlockSpec((d, n), lambda i: (0, 0)),
        ],
        out_specs=[pl.BlockSpec((tm, c), lambda i: (i, 0)) for c in widths],
        out_shape=[jax.ShapeDtypeStruct((m, c), F32) for c in widths],
        compiler_params=_params(("parallel",), est),
        name=name,
    )(x, g.reshape(1, d), w)


def _rope_table_kernel(inv_ref, c_ref, s1_ref, s2_ref, *, period, base, tl):
    i = pl.program_id(0)
    row = lax.broadcasted_iota(jnp.int32, (tl, LANES), 0) + i * tl
    pos = base + lax.rem(row, period)
    ang = pos.astype(F32) * inv_ref[...]
    in_head = lax.broadcasted_iota(jnp.int32, (tl, LANES), 1) % HEAD_DIM
    half = ROT_DIM // 2
    c = jnp.cos(ang)
    s = jnp.sin(ang)
    c_ref[...] = jnp.where(in_head < ROT_DIM, c, 1.0)
    s1_ref[...] = jnp.where(in_head < half, -s, 0.0)
    s2_ref[...] = jnp.where((in_head >= half) & (in_head < ROT_DIM), s, 0.0)


def _rope_tables(n_rows, period, base):
    half = ROT_DIM // 2
    inv = ROPE_THETA ** (-jnp.arange(half, dtype=F32) * 2.0 / ROT_DIM)
    in_head = jnp.arange(LANES) % HEAD_DIM
    inv_lane = jnp.where(in_head < ROT_DIM, inv[in_head % half], 0.0).astype(F32).reshape(1, LANES)
    tl = _row_tile(n_rows, 1024)
    shp = jax.ShapeDtypeStruct((n_rows, LANES), F32)
    return pl.pallas_call(
        functools.partial(_rope_table_kernel, period=period, base=base, tl=tl),
        grid=(n_rows // tl,),
        in_specs=[pl.BlockSpec((1, LANES), lambda i: (0, 0))],
        out_specs=[pl.BlockSpec((tl, LANES), lambda i: (i, 0))] * 3,
        out_shape=[shp] * 3,
        compiler_params=_params(("parallel",), 16 * tl * LANES * 4),
        name="rope_tables",
    )(inv_lane)


def _apply_rope(t, c, s1, s2):
    w = t.shape[1]
    reps = w // LANES
    half = ROT_DIM // 2
    tile = lambda a: jnp.concatenate([a] * reps, axis=1) if reps > 1 else a
    up = pltpu.roll(t, w - half, 1)
    down = pltpu.roll(t, half, 1)
    return t * tile(c) + up * tile(s1) + down * tile(s2)


def _kv_proj_kernel(x_ref, g_ref, w_ref, c_ref, s1_ref, s2_ref, *out_refs, for_prompt):
    h = _rms(x_ref[...], g_ref[...]).astype(BF16)
    kv_w = N_KV_B * HEAD_DIM
    k = jnp.dot(h, w_ref[:, :kv_w], preferred_element_type=F32)
    v = jnp.dot(h, w_ref[:, kv_w:], preferred_element_type=F32)
    k = _apply_rope(k, c_ref[...], s1_ref[...], s2_ref[...])
    out_refs[0][...] = k
    out_refs[1][...] = v
    if for_prompt:
        kh_ref, vt_ref, kbar_ref = out_refs[2:]
        for hk in range(N_KV_B):
            kh_ref[hk] = k[:, hk * HEAD_DIM:(hk + 1) * HEAD_DIM].astype(BF16)
        vt_ref[0] = v.T.astype(BF16)
        kbar_ref[0] = jnp.sum(k, axis=0, keepdims=True) * (1.0 / MOBA_BLOCK)


def _kv_proj(x, g, w, tables, for_prompt):
    m, d = x.shape
    kv_w = N_KV_B * HEAD_DIM
    tm = MOBA_BLOCK if for_prompt else _row_tile(m, 256)
    n_tab = tables[0].shape[0] // tm
    row = lambda i: (i, 0)
    tab = lambda i: (i % n_tab, 0)
    out_specs = [pl.BlockSpec((tm, kv_w), row), pl.BlockSpec((tm, kv_w), row)]
    out_shape = [jax.ShapeDtypeStruct((m, kv_w), F32)] * 2
    if for_prompt:
        out_specs.append(pl.BlockSpec((N_KV_B, tm, HEAD_DIM), lambda i: (0, i, 0)))
        out_shape.append(jax.ShapeDtypeStruct((N_KV_B, m, HEAD_DIM), BF16))
        out_specs.append(pl.BlockSpec((1, kv_w, tm), lambda i: (i, 0, 0)))
        out_shape.append(jax.ShapeDtypeStruct((m // tm, kv_w, tm), BF16))
        out_specs.append(pl.BlockSpec((1, 1, kv_w), lambda i: (i, 0, 0)))
        out_shape.append(jax.ShapeDtypeStruct((m // tm, 1, kv_w), F32))
    est = 2 * tm * d * 4 + 2 * d * 2 * kv_w * 2 + 12 * tm * kv_w * 4
    return pl.pallas_call(
        functools.partial(_kv_proj_kernel, for_prompt=for_prompt),
        grid=(m // tm,),
        in_specs=[
            pl.BlockSpec((tm, d), row),
            pl.BlockSpec((1, d), lambda i: (0, 0)),
            pl.BlockSpec((d, 2 * kv_w), lambda i: (0, 0)),
            pl.BlockSpec((tm, LANES), tab),
            pl.BlockSpec((tm, LANES), tab),
            pl.BlockSpec((tm, LANES), tab),
        ],
        out_specs=out_specs,
        out_shape=out_shape,
        compiler_params=_params(("parallel",), est),
        name="kv_proj",
    )(x, g.reshape(1, d), w, *tables)


def _q_proj_kernel(x_ref, g_ref, w_ref, c_ref, s1_ref, s2_ref, qh_ref, qm_ref):
    h = _rms(x_ref[...], g_ref[...]).astype(BF16)
    q = jnp.dot(h, w_ref[:, :MIX_W], preferred_element_type=F32)
    qm_ref[...] = jnp.dot(h, w_ref[:, MIX_W:], preferred_element_type=F32)
    q = _apply_rope(q, c_ref[...], s1_ref[...], s2_ref[...]) * QK_SCALE
    for hk in range(N_KV_B):
        qh_ref[hk] = q[:, hk * GROUP_W:(hk + 1) * GROUP_W]


def _q_proj(x, g, w, tables):
    m, d = x.shape
    n = w.shape[1]
    tm = _row_tile(m, 256)
    n_tab = tables[0].shape[0] // tm
    row = lambda i: (i, 0)
    tab = lambda i: (i % n_tab, 0)
    est = 2 * tm * d * 4 + 2 * d * n * 2 + 16 * tm * MIX_W * 4
    return pl.pallas_call(
        _q_proj_kernel,
        grid=(m // tm,),
        in_specs=[
            pl.BlockSpec((tm, d), row),
            pl.BlockSpec((1, d), lambda i: (0, 0)),
            pl.BlockSpec((d, n), lambda i: (0, 0)),
            pl.BlockSpec((tm, LANES), tab),
            pl.BlockSpec((tm, LANES), tab),
            pl.BlockSpec((tm, LANES), tab),
        ],
        out_specs=[
            pl.BlockSpec((N_KV_B, tm, GROUP_W), lambda i: (0, i, 0)),
            pl.BlockSpec((tm, MEM_W), row),
        ],
        out_shape=[
            jax.ShapeDtypeStruct((N_KV_B, m, GROUP_W), F32),
            jax.ShapeDtypeStruct((m, MEM_W), F32),
        ],
        compiler_params=_params(("parallel",), est),
        name="q_proj",
    )(x, g.reshape(1, d), w, *tables)


def _conv_kernel(u_ref, prev_ref, buf_ref, w_ref, y_ref, nb_ref, pad_scr, *, tl, n_t):
    i = pl.program_id(1)
    halo = CONV_W - 1
    lo = SUBLANES - halo
    pad_scr[SUBLANES:SUBLANES + tl, :] = u_ref[0]

    @pl.when(i == 0)
    def _():
        pad_scr[lo:SUBLANES, :] = buf_ref[0]

    @pl.when(i > 0)
    def _():
        p = prev_ref[0]
        pad_scr[lo:SUBLANES, :] = p[p.shape[0] - halo:, :]

    y = pad_scr[lo:lo + tl, :] * w_ref[0:1, :]
    for j in range(1, CONV_W):
        y = y + pad_scr[lo + j:lo + j + tl, :] * w_ref[j:j + 1, :]
    y_ref[0] = y * _sigmoid(y)

    @pl.when(i == n_t - 1)
    def _():
        nb_ref[0] = pad_scr[lo + tl:SUBLANES + tl, :]


def _conv(u, buf, w):
    b, l, c = u.shape
    tl = _row_tile(l, 512) if l % SUBLANES == 0 else l
    n_t = l // tl
    pr = min(SUBLANES, l)
    per = tl // pr
    est = 6 * tl * c * 4
    return pl.pallas_call(
        functools.partial(_conv_kernel, tl=tl, n_t=n_t),
        grid=(b, n_t),
        in_specs=[
            pl.BlockSpec((1, tl, c), lambda bi, i: (bi, i, 0)),
            pl.BlockSpec((1, pr, c), lambda bi, i: (bi, jnp.maximum(i * per - 1, 0), 0)),
            pl.BlockSpec((1, CONV_W - 1, c), lambda bi, i: (bi, 0, 0)),
            pl.BlockSpec((CONV_W, c), lambda bi, i: (0, 0)),
        ],
        out_specs=[
            pl.BlockSpec((1, tl, c), lambda bi, i: (bi, i, 0)),
            pl.BlockSpec((1, CONV_W - 1, c), lambda bi, i: (bi, 0, 0)),
        ],
        out_shape=[
            jax.ShapeDtypeStruct((b, l, c), F32),
            jax.ShapeDtypeStruct((b, CONV_W - 1, c), F32),
        ],
        scratch_shapes=[pltpu.VMEM((tl + SUBLANES, c), F32)],
        compiler_params=_params(("parallel", "arbitrary"), est),
        name="short_conv",
    )(u, u, buf, w)


HEADS_PER_SLAB = LANES * 2 // HEAD_DIM
SLAB_W = HEADS_PER_SLAB * HEAD_DIM
N_SLABS = N_HEADS // HEADS_PER_SLAB


def _split2(x):
    hi = x.astype(BF16)
    return hi, (x - hi.astype(F32)).astype(BF16)


def _split3(x):
    hi = x.astype(BF16)
    r = x - hi.astype(F32)
    mid = r.astype(BF16)
    return hi, mid, (r - mid.astype(F32)).astype(BF16)


def _dot_right01(x, sel):
    hi, mid, lo = _split3(x)
    d = lambda a: jnp.dot(a, sel, preferred_element_type=F32)
    return d(hi) + (d(mid) + d(lo))


def _dot_left01(sel, x):
    hi, mid, lo = _split3(x)
    d = lambda a: jnp.dot(sel, a, preferred_element_type=F32)
    return d(hi) + (d(mid) + d(lo))


def _dot3(a, b_hi, b_lo):
    a_hi, a_lo = _split2(a)
    d = lambda x, y: jnp.dot(x, y, preferred_element_type=F32)
    return d(a_hi, b_hi) + (d(a_hi, b_lo) + d(a_lo, b_hi))


def _block_diag(x, mask):
    return jnp.concatenate([x] * HEADS_PER_SLAB, axis=0) * mask


def _block_diag_pieces(x, mask):
    hi, lo = _split2(x)
    return _block_diag(hi, mask), _block_diag(lo, mask)


def _unit_lower_inverses(ms, eye_t, mask):
    c = ms[0].shape[0]
    xs = [eye_t - m for m in ms]
    ps = [_dot3(m, *_block_diag_pieces(m, mask)) for m in ms]
    power = 2
    while 2 * power < c:
        xps = [_dot3(jnp.concatenate([x, p], axis=0), *_block_diag_pieces(p, mask))
               for x, p in zip(xs, ps)]
        xs = [x + xp[:c] for x, xp in zip(xs, xps)]
        ps = [xp[c:] for xp in xps]
        power *= 2
    return [x + _dot3(x, *_block_diag_pieces(p, mask)) for x, p in zip(xs, ps)]


def _gdn_kernel(q_ref, k_ref, v_ref, gate_ref, ab_ref, alog_ref, dt_ref, gain_ref, s0_ref,
                o_ref, sfin_ref,
                s_scr, qn_scr, kn_scr, gcb_scr, bb_scr, u_scr, w_scr, a_scr, qg_scr, kd_scr, egl_scr,
                *, t_rows, n_t, l_valid):
    t = pl.program_id(1)
    c = DN_CHUNK
    d = HEAD_DIM
    n_chunks = t_rows // c

    r_s = lax.broadcasted_iota(jnp.int32, (SLAB_W, SLAB_W), 0)
    c_s = lax.broadcasted_iota(jnp.int32, (SLAB_W, SLAB_W), 1)
    same_head = (r_s // d) == (c_s // d)
    bd_f32 = jnp.where(same_head, 1.0, 0.0).astype(F32)
    bd_mask = bd_f32.astype(BF16)
    r_t = lax.broadcasted_iota(jnp.int32, (c, SLAB_W), 0)
    c_t = lax.broadcasted_iota(jnp.int32, (c, SLAB_W), 1) % d
    lower_t = r_t >= c_t
    strict_t = r_t > c_t
    eye_t = jnp.where(r_t == c_t, 1.0, 0.0).astype(F32)
    r_c = lax.broadcasted_iota(jnp.int32, (c, c), 0)
    c_c = lax.broadcasted_iota(jnp.int32, (c, c), 1)
    tril = jnp.where(r_c >= c_c, 1.0, 0.0).astype(BF16)
    er = lax.broadcasted_iota(jnp.int32, (LANES, MIX_W), 0)
    ec = lax.broadcasted_iota(jnp.int32, (LANES, MIX_W), 1) // d
    sel_g = jnp.where(er == ec, 1.0, 0.0).astype(BF16)
    sel_b = jnp.where(er == ec + N_HEADS, 1.0, 0.0).astype(BF16)

    @pl.when(t == 0)
    def _():
        for s in range(N_SLABS):
            rows = jnp.concatenate([s0_ref[0, s * HEADS_PER_SLAB + h] for h in range(HEADS_PER_SLAB)],
                                   axis=0)
            s_scr[s] = jnp.concatenate([rows] * HEADS_PER_SLAB, axis=1) * bd_f32

    ab = ab_ref[0]
    row = lax.broadcasted_iota(jnp.int32, ab.shape, 0) + t * t_rows
    valid = row < l_valid
    g = jnp.where(valid, -jnp.exp(alog_ref[...]) * _softplus(ab + dt_ref[...]), 0.0)
    beta = jnp.where(valid, _sigmoid(ab), 0.0)
    gc = jnp.concatenate([_dot_left01(tril, g[i * c:(i + 1) * c]) for i in range(n_chunks)], axis=0)
    gcb_scr[...] = _dot_right01(gc, sel_g)
    bb_scr[...] = _dot_right01(beta, sel_b)
    for s in range(N_SLABS):
        cs = slice(s * SLAB_W, (s + 1) * SLAB_W)
        for src, dst, scale in ((q_ref, qn_scr, QK_SCALE), (k_ref, kn_scr, 1.0)):
            x = src[0, :, cs]
            hi, lo = _split2(x * x)
            ssq = (jnp.dot(hi, bd_mask, preferred_element_type=F32)
                   + jnp.dot(lo, bd_mask, preferred_element_type=F32))
            dst[:, cs] = x * (lax.rsqrt(ssq + EPS) * scale)

    per_iter = 2 if n_chunks % 2 == 0 else 1
    slabs = [slice(s * SLAB_W, (s + 1) * SLAB_W) for s in range(N_SLABS)]

    def factors(it, carry):
        items = []
        for u in range(per_iter):
            ci = it * per_iter + u
            rows = pl.ds(pl.multiple_of(ci * c, c), c)
            items += [(ci, rows, cs) for cs in slabs]
        pre = []
        for ci, rows, cs in items:
            gcb = gcb_scr[rows, cs]
            kn = kn_scr[rows, cs]
            qn = qn_scr[rows, cs]
            b_ = bb_scr[rows, cs]
            kb = kn * b_
            g_row = jnp.sum(gcb * eye_t, axis=0, keepdims=True)
            dec = jnp.exp(jnp.where(lower_t, gcb - g_row, NEG_INF))
            both = lax.dot_general(jnp.concatenate([kb, qn], axis=0).astype(BF16),
                                   _block_diag(kn.astype(BF16), bd_mask), NT,
                                   preferred_element_type=F32)
            pre.append((gcb, kn, qn, kb, dec, both))
        ms = []
        for (ci, rows, cs), (gcb, kn, qn, kb, dec, both) in zip(items, pre):
            ms.append(jnp.where(strict_t, both[:c] * dec, 0.0))
            a_scr[rows, cs] = (both[c:] * dec).astype(BF16)
        tinvs = _unit_lower_inverses(ms, eye_t, bd_mask)
        sols = []
        for (ci, rows, cs), (gcb, kn, qn, kb, dec, both), tinv in zip(items, pre, tinvs):
            eg = jnp.exp(gcb)
            v_hi, v_lo = _block_diag_pieces(v_ref[0, rows, cs] * bb_scr[rows, cs], bd_mask)
            k_hi, k_lo = _block_diag_pieces(kb * eg, bd_mask)
            sols.append(_dot3(tinv, jnp.concatenate([v_hi, k_hi], axis=1),
                              jnp.concatenate([v_lo, k_lo], axis=1)))
            glast = gcb[c - 1:c, :]
            qg_scr[rows, cs] = (qn * eg).astype(BF16)
            kd_scr[rows, cs] = (kn * jnp.exp(glast - gcb)).astype(BF16)
            egl_scr[pl.ds(ci, 1), cs] = jnp.exp(glast)
        for (ci, rows, cs), sol in zip(items, sols):
            u_scr[rows, cs] = sol[:, :SLAB_W]
            w_scr[rows, cs] = sol[:, SLAB_W:].astype(BF16)
        return carry

    lax.fori_loop(0, n_chunks // per_iter, factors, 0)

    gain = gain_ref[...]

    def recur(ci, carry):
        rows = pl.ds(pl.multiple_of(ci * c, c), c)
        olds = [s_scr[s] for s in range(N_SLABS)]
        rs = [jnp.dot(jnp.concatenate([w_scr[rows, cs], qg_scr[rows, cs]], axis=0),
                      s_old.astype(BF16), preferred_element_type=F32)
              for cs, s_old in zip(slabs, olds)]
        v_news = [(u_scr[rows, cs] - r[:c]).astype(BF16) for cs, r in zip(slabs, rs)]
        upds = [lax.dot_general(kd_scr[rows, cs], v_new, TN, preferred_element_type=F32)
                for cs, v_new in zip(slabs, v_news)]
        os_ = [r[c:] + jnp.dot(a_scr[rows, cs], _block_diag(v_new, bd_mask),
                               preferred_element_type=F32)
               for cs, r, v_new in zip(slabs, rs, v_news)]
        for s, (cs, s_old, upd) in enumerate(zip(slabs, olds, upds)):
            s_scr[s] = (s_old * egl_scr[pl.ds(ci, 1), cs] + upd) * bd_f32
        ssqs = []
        for o in os_:
            hi, lo = _split2(o * o)
            ssqs.append(jnp.dot(hi, bd_mask, preferred_element_type=F32)
                        + jnp.dot(lo, bd_mask, preferred_element_type=F32))
        for cs, o, ssq in zip(slabs, os_, ssqs):
            gt = gate_ref[0, rows, cs]
            o_ref[0, rows, cs] = (o * lax.rsqrt(ssq * (1.0 / d) + EPS) * gain[:, cs]
                                  * (gt * _sigmoid(gt)))
        return carry

    lax.fori_loop(0, n_chunks, recur, 0)

    @pl.when(t == n_t - 1)
    def _():
        for s in range(N_SLABS):
            full = s_scr[s]
            for h in range(HEADS_PER_SLAB):
                sfin_ref[0, s * HEADS_PER_SLAB + h] = full[h * d:(h + 1) * d, h * d:(h + 1) * d]


def _gdn(qkv, gate, ab, a_log, dt_bias, out_gain, s0, l_valid):
    b, lp, _ = qkv.shape
    assert lp % DN_CHUNK == 0
    t_rows = DN_CHUNK * math.gcd(lp // DN_CHUNK, 8)
    n_t = lp // t_rows
    n_chunks = t_rows // DN_CHUNK
    pad = lambda a: jnp.pad(a.astype(F32), (0, LANES - N_HEADS)).reshape(1, LANES)
    blk = lambda j: pl.BlockSpec((1, t_rows, MIX_W), lambda bi, t, j=j: (bi, t, j))
    state = pl.BlockSpec((1, N_HEADS, HEAD_DIM, HEAD_DIM), lambda bi, t: (bi, 0, 0, 0))
    vec = pl.BlockSpec((1, LANES), lambda bi, t: (0, 0))
    wide_f32 = pltpu.VMEM((t_rows, MIX_W), F32)
    wide_bf16 = pltpu.VMEM((t_rows, MIX_W), BF16)
    est = 10 * t_rows * MIX_W * 4 + 5 * t_rows * MIX_W * 4 + 4 * t_rows * MIX_W * 2 + 64 * SLAB_W * SLAB_W * 4
    return pl.pallas_call(
        functools.partial(_gdn_kernel, t_rows=t_rows, n_t=n_t, l_valid=l_valid),
        grid=(b, n_t),
        in_specs=[
            blk(0), blk(1), blk(2),
            pl.BlockSpec((1, t_rows, MIX_W), lambda bi, t: (bi, t, 0)),
            pl.BlockSpec((1, t_rows, LANES), lambda bi, t: (bi, t, 0)),
            vec, vec,
            pl.BlockSpec((1, MIX_W), lambda bi, t: (0, 0)),
            state,
        ],
        out_specs=[pl.BlockSpec((1, t_rows, MIX_W), lambda bi, t: (bi, t, 0)), state],
        out_shape=[
            jax.ShapeDtypeStruct((b, lp, MIX_W), F32),
            jax.ShapeDtypeStruct((b, N_HEADS, HEAD_DIM, HEAD_DIM), F32),
        ],
        scratch_shapes=[
            pltpu.VMEM((N_SLABS, SLAB_W, SLAB_W), F32),
            wide_f32, wide_f32,
            wide_f32, wide_f32,
            wide_f32,
            wide_bf16, wide_bf16, wide_bf16, wide_bf16,
            pltpu.VMEM((max(n_chunks, SUBLANES), MIX_W), F32),
        ],
        compiler_params=_params(("parallel", "arbitrary"), est),
        name="gated_delta_rule",
    )(qkv, qkv, qkv, gate, ab, pad(a_log), pad(dt_bias),
      jnp.tile(out_gain.astype(F32), N_HEADS).reshape(1, MIX_W), s0)


def _mem_attn_kernel(q_ref, k_ref, v_ref, o_ref):
    q = q_ref[0]
    k = k_ref[0]
    v = v_ref[0]
    outs = []
    for h in range(N_MEM_HEADS):
        sl = slice(h * HEAD_DIM, (h + 1) * HEAD_DIM)
        s = _bdot_nt(q[:, sl], k[:, sl]) * QK_SCALE
        p = jnp.exp(s - jnp.max(s, axis=-1, keepdims=True))
        outs.append(_bdot(p, v[:, sl]) / jnp.sum(p, axis=-1, keepdims=True))
    o_ref[0] = jnp.concatenate(outs, axis=1)


def _mem_attn(q, mk, mv):
    b, l, w = q.shape
    n_mem = mk.shape[1]
    tl = _row_tile(l, 512)
    est = 4 * tl * w * 4 + 4 * n_mem * w * 4 + 8 * tl * n_mem * 4
    return pl.pallas_call(
        _mem_attn_kernel,
        grid=(b, l // tl),
        in_specs=[
            pl.BlockSpec((1, tl, w), lambda bi, i: (bi, i, 0)),
            pl.BlockSpec((1, n_mem, w), lambda bi, i: (bi, 0, 0)),
            pl.BlockSpec((1, n_mem, w), lambda bi, i: (bi, 0, 0)),
        ],
        out_specs=pl.BlockSpec((1, tl, w), lambda bi, i: (bi, i, 0)),
        out_shape=jax.ShapeDtypeStruct((b, l, w), F32),
        compiler_params=_params(("parallel", "parallel"), est),
        name="mem_attn",
    )(q, mk, mv)


def _out_proj_kernel(x_ref, o_ref, mo_ref, wo_ref, wm_ref, y_ref, *, grouped):
    y = x_ref[...] + _bdot(mo_ref[...], wm_ref[...])
    if grouped:
        for hk in range(N_KV_B):
            y = y + _bdot(o_ref[hk], wo_ref[hk])
    else:
        y = y + _bdot(o_ref[...], wo_ref[...])
    y_ref[...] = y


def _out_proj(x, o, mo, w_out, grouped):
    m, d = x.shape
    tm = _row_tile(m, 512)
    row = lambda i: (i, 0)
    if grouped:
        wo = w_out[:MIX_W].reshape(N_KV_B, GROUP_W, d)
        o_spec = pl.BlockSpec((N_KV_B, tm, GROUP_W), lambda i: (0, i, 0))
        wo_spec = pl.BlockSpec((N_KV_B, GROUP_W, d), lambda i: (0, 0, 0))
    else:
        wo = w_out[:MIX_W]
        o_spec = pl.BlockSpec((tm, MIX_W), row)
        wo_spec = pl.BlockSpec((MIX_W, d), lambda i: (0, 0))
    est = 6 * tm * d * 4 + 4 * tm * d * 4 + 4 * d * d * 2
    return pl.pallas_call(
        functools.partial(_out_proj_kernel, grouped=grouped),
        grid=(m // tm,),
        in_specs=[
            pl.BlockSpec((tm, d), row),
            o_spec,
            pl.BlockSpec((tm, MEM_W), row),
            wo_spec,
            pl.BlockSpec((MEM_W, d), lambda i: (0, 0)),
        ],
        out_specs=pl.BlockSpec((tm, d), row),
        out_shape=jax.ShapeDtypeStruct((m, d), F32),
        compiler_params=_params(("parallel",), est),
        name="out_proj",
    )(x, o, mo, wo, w_out[MIX_W:])


def _select_blocks(gate, n_past, k_sel):
    lane = lax.broadcasted_iota(jnp.int32, gate.shape, 1)
    lane_f = lane.astype(F32)
    gate = jnp.where(lane < n_past, gate, NEG_INF)
    sel = jnp.zeros(gate.shape, F32)
    for _ in range(k_sel):
        mx = jnp.max(gate, axis=1, keepdims=True)
        first = jnp.min(jnp.where(gate == mx, lane_f, float(gate.shape[1])), axis=1, keepdims=True)
        pick = (lane_f == first) & (mx > NEG_INF)
        sel = jnp.where(pick, 1.0, sel)
        gate = jnp.where(pick, NEG_INF, gate)
    return sel


def _softmax_update(s, v, m_scr, l_scr, acc_scr):
    m_old = m_scr[...]
    m_new = jnp.maximum(m_old, jnp.max(s, axis=1, keepdims=True))
    alpha = jnp.exp(m_old - m_new)
    p = jnp.exp(s - m_new)
    l_scr[...] = alpha * l_scr[...] + jnp.sum(p, axis=1, keepdims=True)
    acc_scr[...] = alpha * acc_scr[...] + _bdot(p, v)
    m_scr[...] = m_new


def _select_blocks_t(gate, n_past, k_sel):
    blk = lax.broadcasted_iota(jnp.int32, gate.shape, 0)
    blk_f = blk.astype(F32)
    gate = jnp.where(blk < n_past, gate, NEG_INF)
    sel = jnp.zeros(gate.shape, F32)
    for _ in range(k_sel):
        mx = jnp.max(gate, axis=0, keepdims=True)
        first = jnp.min(jnp.where(gate == mx, blk_f, float(gate.shape[0])), axis=0, keepdims=True)
        pick = (blk_f == first) & (mx > NEG_INF)
        sel = jnp.where(pick, 1.0, sel)
        gate = jnp.where(pick, NEG_INF, gate)
    return sel


def _moba_prompt_kernel(q_ref, k_ref, vt_ref, kbar_ref, o_ref, q_scr, m_scr, l_scr, acc_scr, sel_scr,
                        *, k_sel):
    i = pl.program_id(2)
    blk = MOBA_BLOCK
    d = HEAD_DIM
    n_chunks = KV_GROUP * blk // LANES
    q = q_ref[0]
    q3 = jnp.concatenate([q[:, g * d:(g + 1) * d] for g in range(KV_GROUP)], axis=0)
    gate_t = lax.dot_general(kbar_ref[0, 0], q3, NT, precision=HI, preferred_element_type=F32)
    sel_scr[...] = _select_blocks_t(gate_t, i, k_sel)
    q_scr[...] = q3.astype(BF16)

    def block_scores(k_blk, c):
        return lax.dot_general(k_blk, q_scr[c * LANES:(c + 1) * LANES, :], NT,
                               preferred_element_type=F32)

    k_own = k_ref[0, pl.ds(pl.multiple_of(i * blk, blk), blk), :]
    vt_own = vt_ref[i, :, :]
    scores = [block_scores(k_own, c) for c in range(n_chunks)]
    probs = []
    for c in range(n_chunks):
        cs = slice(c * LANES, (c + 1) * LANES)
        kpos = lax.broadcasted_iota(jnp.int32, (blk, LANES), 0)
        qpos = (lax.broadcasted_iota(jnp.int32, (blk, LANES), 1) + c * LANES) % blk
        s = jnp.where(kpos <= qpos, scores[c], NEG_INF)
        m0 = jnp.max(s, axis=0, keepdims=True)
        p = jnp.exp(s - m0)
        m_scr[:, cs] = m0
        l_scr[:, cs] = jnp.sum(p, axis=0, keepdims=True)
        probs.append(p.astype(BF16))
    for c in range(n_chunks):
        acc_scr[:, c * LANES:(c + 1) * LANES] = jnp.dot(vt_own, probs[c], preferred_element_type=F32)

    def past(j, carry):
        k_j = k_ref[0, pl.ds(pl.multiple_of(j * blk, blk), blk), :]
        vt_j = vt_ref[j, :, :]
        chosen = sel_scr[pl.ds(j, 1), :]
        scores = [block_scores(k_j, c) for c in range(n_chunks)]
        probs, alphas = [], []
        for c in range(n_chunks):
            cs = slice(c * LANES, (c + 1) * LANES)
            s = jnp.where(chosen[:, cs] > 0.0, scores[c], NEG_INF)
            m_old = m_scr[:, cs]
            m_new = jnp.maximum(m_old, jnp.max(s, axis=0, keepdims=True))
            alpha = jnp.exp(m_old - m_new)
            p = jnp.exp(s - m_new)
            l_scr[:, cs] = alpha * l_scr[:, cs] + jnp.sum(p, axis=0, keepdims=True)
            m_scr[:, cs] = m_new
            probs.append(p.astype(BF16))
            alphas.append(alpha)
        for c in range(n_chunks):
            cs = slice(c * LANES, (c + 1) * LANES)
            acc_scr[:, cs] = alphas[c] * acc_scr[:, cs] + jnp.dot(vt_j, probs[c],
                                                                  preferred_element_type=F32)
        return carry

    lax.fori_loop(0, i, past, 0)
    o_t = (acc_scr[...] / l_scr[...]).astype(BF16)
    eye = (lax.broadcasted_iota(jnp.int32, (blk, blk), 0)
           == lax.broadcasted_iota(jnp.int32, (blk, blk), 1)).astype(BF16)
    o_ref[0] = jnp.concatenate(
        [lax.dot_general(eye, o_t[:, g * blk:(g + 1) * blk], NT, preferred_element_type=F32)
         for g in range(KV_GROUP)], axis=1)


def _moba_prompt(qh, kh, vt, kbar, b, l):
    nb = l // MOBA_BLOCK
    rows = KV_GROUP * MOBA_BLOCK
    k_sel = min(MOBA_TOPK, nb)
    est = 4 * l * LANES * 2 + 4 * l * HEAD_DIM * 2 + 8 * MOBA_BLOCK * GROUP_W * 4 + 24 * rows * LANES * 4
    return pl.pallas_call(
        functools.partial(_moba_prompt_kernel, k_sel=k_sel),
        grid=(N_KV_B, b, nb),
        in_specs=[
            pl.BlockSpec((1, MOBA_BLOCK, GROUP_W), lambda hk, bi, i: (hk, bi * nb + i, 0)),
            pl.BlockSpec((1, l, HEAD_DIM), lambda hk, bi, i: (hk, bi, 0)),
            pl.BlockSpec((nb, HEAD_DIM, MOBA_BLOCK), lambda hk, bi, i: (bi, hk, 0)),
            pl.BlockSpec((1, 1, nb, HEAD_DIM), lambda hk, bi, i: (hk, bi, 0, 0)),
        ],
        out_specs=pl.BlockSpec((1, MOBA_BLOCK, GROUP_W), lambda hk, bi, i: (hk, bi * nb + i, 0)),
        out_shape=jax.ShapeDtypeStruct((N_KV_B, b * l, GROUP_W), F32),
        scratch_shapes=[
            pltpu.VMEM((rows, HEAD_DIM), BF16),
            pltpu.VMEM((1, rows), F32),
            pltpu.VMEM((1, rows), F32),
            pltpu.VMEM((HEAD_DIM, rows), F32),
            pltpu.VMEM((nb, rows), F32),
        ],
        compiler_params=_params(("parallel", "parallel", "arbitrary"), est),
        name="moba_prompt",
    )(qh, kh, vt, kbar)


def _page_means_kernel(pt_ref, k0_ref, k1_ref, o_ref):
    j = pl.program_id(1)
    s = jnp.sum(k0_ref[0], axis=0, keepdims=True) + jnp.sum(k1_ref[0], axis=0, keepdims=True)
    o_ref[0, pl.ds(j, 1), :] = s * (1.0 / MOBA_BLOCK)


def _page_means(page_table, cache_k):
    bs, n_pages = page_table.shape
    page, w = cache_k.shape[1:]
    assert 2 * page == MOBA_BLOCK
    nb = n_pages // 2
    spec = lambda off: pl.BlockSpec((1, page, w), lambda b, j, pt: (pt[b, 2 * j + off], 0, 0))
    return pl.pallas_call(
        _page_means_kernel,
        grid_spec=pltpu.PrefetchScalarGridSpec(
            num_scalar_prefetch=1,
            grid=(bs, nb),
            in_specs=[spec(0), spec(1)],
            out_specs=pl.BlockSpec((1, nb, w), lambda b, j, pt: (b, 0, 0)),
        ),
        out_shape=jax.ShapeDtypeStruct((bs, nb, w), F32),
        compiler_params=_params(("parallel", "arbitrary"), 8 * page * w * 4),
        name="page_means",
    )(page_table, cache_k, cache_k)


def _moba_sample_kernel(pt_ref, q_ref, kbar_ref, k0_ref, k1_ref, v0_ref, v1_ref, kn_ref, vn_ref,
                        o_ref, m_scr, l_scr, acc_scr, sel_scr, *, n_blocks, l_new, k_sel):
    j = pl.program_id(1)
    q = q_ref[0]
    qb = q.astype(BF16)

    @pl.when(j == 0)
    def _():
        gate = lax.dot_general(q, kbar_ref[0], NT, precision=HI, preferred_element_type=F32)
        sel_scr[...] = _select_blocks(gate, n_blocks, k_sel)
        s = _bdot_nt(qb, kn_ref[0])
        t_q = lax.broadcasted_iota(jnp.int32, s.shape, 0) % l_new
        t_k = lax.broadcasted_iota(jnp.int32, s.shape, 1)
        s = jnp.where(t_k <= t_q, s, NEG_INF)
        m0 = jnp.max(s, axis=1, keepdims=True)
        p = jnp.exp(s - m0)
        m_scr[...] = m0
        l_scr[...] = jnp.sum(p, axis=1, keepdims=True)
        acc_scr[...] = _bdot(p, vn_ref[0])

    sel = sel_scr[...]
    lane = lax.broadcasted_iota(jnp.int32, sel.shape, 1)
    chosen = jnp.max(jnp.where(lane == j, sel, 0.0), axis=1, keepdims=True)
    for k_ref, v_ref in ((k0_ref, v0_ref), (k1_ref, v1_ref)):
        s = _bdot_nt(qb, k_ref[0])
        s = jnp.where(chosen > 0.0, s, NEG_INF)
        _softmax_update(s, v_ref[0], m_scr, l_scr, acc_scr)

    @pl.when(j == n_blocks - 1)
    def _():
        o_ref[0] = acc_scr[...] / l_scr[...]


def _moba_sample(q_bd, kbar, page_table, cache_k, cache_v, k_new, v_new, l_new):
    bs, rows, w = q_bd.shape
    n_pages = page_table.shape[1]
    page = cache_k.shape[1]
    nb = n_pages // 2
    k_sel = min(MOBA_TOPK, nb + 1)
    lp = k_new.shape[1]
    pg = lambda off: pl.BlockSpec((1, page, w), lambda b, j, pt: (pt[b, 2 * j + off], 0, 0))
    per_b = lambda r: pl.BlockSpec((1, r, w), lambda b, j, pt: (b, 0, 0))
    return pl.pallas_call(
        functools.partial(_moba_sample_kernel, n_blocks=nb, l_new=l_new, k_sel=k_sel),
        grid_spec=pltpu.PrefetchScalarGridSpec(
            num_scalar_prefetch=1,
            grid=(bs, nb),
            in_specs=[per_b(rows), per_b(nb), pg(0), pg(1), pg(0), pg(1), per_b(lp), per_b(lp)],
            out_specs=per_b(rows),
            scratch_shapes=[
                pltpu.VMEM((rows, 1), F32),
                pltpu.VMEM((rows, 1), F32),
                pltpu.VMEM((rows, w), F32),
                pltpu.VMEM((rows, nb), F32),
            ],
        ),
        out_shape=jax.ShapeDtypeStruct((bs, rows, w), F32),
        compiler_params=_params(("parallel", "arbitrary"), 16 * page * w * 4),
        name="moba_sample",
    )(page_table, q_bd, kbar, cache_k, cache_k, cache_v, cache_v, k_new, v_new)


def _pack_w_in_a(w):
    d = w.shape[0]
    qkv_gate = w[:, :4 * MIX_W]
    ab = w[:, 4 * MIX_W:4 * MIX_W + 2 * N_HEADS]
    q_mem = w[:, 4 * MIX_W + 2 * N_HEADS:]
    ab = jnp.pad(ab, ((0, 0), (0, LANES - 2 * N_HEADS)))
    return jnp.concatenate([qkv_gate, q_mem, ab], axis=1).astype(BF16)


def _pad_rows(a, rows):
    return jnp.pad(a, ((0, 0), (0, rows - a.shape[1]), (0, 0)))


def _trunk(p, x3, pos_base, mem_k, mem_v, conv0, s0, paged):
    b, l, d = x3.shape
    m = b * l
    x = x3.reshape(m, d)
    sample = paged is not None
    ffn = lambda x, tag, i, fg=None: _ffn(x, p[tag + "_norm"][i], p[tag + "_w1"][i], p[tag + "_w3"][i],
                                          p[tag + "_w2"][i], fg)
    lq = -(-l // SUBLANES) * SUBLANES

    def memory(q_mem, layer):
        q3 = _pad_rows(q_mem.reshape(b, l, MEM_W), lq)
        mo = _mem_attn(q3, mem_k[layer], mem_v[layer])
        return mo[:, :l].reshape(m, MEM_W)

    x = ffn(x, "ffn1", 0)
    qkv, gate, q_mem, ab = _norm_proj(x, p["mix_norm"][0], p["w_in_a"],
                                      (3 * MIX_W, MIX_W, MEM_W, LANES), "in_proj_a")
    qkv, new_conv = _conv(qkv.reshape(b, l, 3 * MIX_W), conv0, p["dn_conv_w"])
    lp = -(-l // DN_CHUNK) * DN_CHUNK
    o, s_fin = _gdn(_pad_rows(qkv, lp), _pad_rows(gate.reshape(b, l, MIX_W), lp),
                    _pad_rows(ab.reshape(b, l, LANES), lp),
                    p["dn_a_log"], p["dn_dt_bias"], p["dn_out_norm"], s0, l)
    o = o[:, :l].reshape(m, MIX_W)
    x = _out_proj(x, o, memory(q_mem, 0), p["w_out"][0], grouped=False)
    x = ffn(x, "ffn2", 0)

    if sample:
        tables = _rope_tables(m, l, pos_base)
    else:
        tables = _rope_tables(l, l, pos_base)
    kv = _kv_proj(x, p["kv_norm"], p["w_kv"], tables, for_prompt=not sample)
    k_new, v_new = kv[0], kv[1]
    x = ffn(x, "ffn1", 1)
    qh, q_mem = _q_proj(x, p["mix_norm"][1], p["w_in_b"], tables)
    if sample:
        page_table, cache_k, cache_v = paged
        kbar = _page_means(page_table, cache_k)
        eye = jnp.eye(N_KV_B, dtype=F32)
        q5 = qh.reshape(N_KV_B, b, l, KV_GROUP, HEAD_DIM).transpose(1, 0, 3, 2, 4)
        q_bd = (q5[:, :, :, :, None, :] * eye[None, :, None, None, :, None]).reshape(
            b, N_HEADS * l, N_KV_B * HEAD_DIM)
        o_bd = _moba_sample(q_bd, kbar, page_table, cache_k, cache_v,
                            _pad_rows(k_new.reshape(b, l, -1), lq),
                            _pad_rows(v_new.reshape(b, l, -1), lq), l)
        o6 = o_bd.reshape(b, N_KV_B, KV_GROUP, l, N_KV_B, HEAD_DIM)
        o = (o6 * eye[None, :, None, None, :, None]).sum(axis=4)
        o = o.transpose(0, 3, 1, 2, 4).reshape(m, MIX_W)
        x = _out_proj(x, o, memory(q_mem, 1), p["w_out"][1], grouped=False)
    else:
        kh, vt, kbar = kv[2], kv[3], kv[4]
        nb = l // MOBA_BLOCK
        kbar = kbar.reshape(b, nb, N_KV_B, HEAD_DIM).transpose(2, 0, 1, 3)
        o4 = _moba_prompt(qh, kh, vt, kbar, b, l)
        x = _out_proj(x, o4, memory(q_mem, 1), p["w_out"][1], grouped=True)
    y = ffn(x, "ffn2", 1, p["final_norm"])
    return (y.reshape(b, l, d), new_conv[None], s_fin[None],
            k_new.reshape(b, l, N_KV_B, HEAD_DIM), v_new.reshape(b, l, N_KV_B, HEAD_DIM))


def kernel(x_prompt, x_sample, mem_prompt, cache_mem_k, cache_mem_v, state_dn_conv, state_dn_S,
           cache_kv_k, cache_kv_v, page_table,
           ffn1_norm, ffn1_w1, ffn1_w3, ffn1_w2, mix_norm, w_in_a, w_in_b, w_out,
           dn_conv_w, dn_a_log, dn_dt_bias, dn_out_norm, kv_norm, w_kv, mem_norm, w_mem_kv,
           ffn2_norm, ffn2_w1, ffn2_w3, ffn2_w2, final_norm):
    assert w_in_a.shape[0] == 1 and w_in_b.shape[0] == 1 and ffn1_w1.shape[0] == 2
    bf = lambda a: a.astype(BF16)
    p = dict(ffn1_norm=ffn1_norm, ffn1_w1=bf(ffn1_w1), ffn1_w3=bf(ffn1_w3), ffn1_w2=bf(ffn1_w2),
             ffn2_norm=ffn2_norm, ffn2_w1=bf(ffn2_w1), ffn2_w3=bf(ffn2_w3), ffn2_w2=bf(ffn2_w2),
             mix_norm=mix_norm, w_in_a=_pack_w_in_a(w_in_a[0]), w_in_b=bf(w_in_b[0]), w_out=bf(w_out),
             dn_conv_w=dn_conv_w[0], dn_a_log=dn_a_log[0], dn_dt_bias=dn_dt_bias[0],
             dn_out_norm=dn_out_norm[0], kv_norm=kv_norm, w_kv=bf(w_kv), final_norm=final_norm)

    bp, lp, _ = x_prompt.shape
    assert lp % MOBA_BLOCK == 0
    n_mem = mem_prompt.shape[1]
    mem_flat = mem_prompt.reshape(bp * n_mem, -1)
    mk, mv = [], []
    for layer in range(mem_norm.shape[0]):
        k_l, v_l = _norm_proj(mem_flat, mem_norm[layer], bf(w_mem_kv[layer]), (MEM_W, MEM_W), "mem_kv")
        mk.append(k_l.reshape(bp, n_mem, MEM_W))
        mv.append(v_l.reshape(bp, n_mem, MEM_W))
    conv0 = jnp.zeros((bp, CONV_W - 1, 3 * MIX_W), F32)
    s0 = jnp.zeros((bp, N_HEADS, HEAD_DIM, HEAD_DIM), F32)
    y_p, conv_p, s_p, k_p, v_p = _trunk(p, x_prompt, 0, mk, mv, conv0, s0, None)
    mem_k_p = jnp.stack(mk).reshape(-1, bp, n_mem, N_MEM_HEADS, HEAD_DIM)
    mem_v_p = jnp.stack(mv).reshape(-1, bp, n_mem, N_MEM_HEADS, HEAD_DIM)

    bs = x_sample.shape[0]
    n_pool, page = cache_kv_k.shape[:2]
    past_len = page_table.shape[1] * page
    assert past_len % MOBA_BLOCK == 0 and x_sample.shape[1] <= MOBA_BLOCK
    cmk = cache_mem_k.reshape(cache_mem_k.shape[0], bs, n_mem, MEM_W)
    cmv = cache_mem_v.reshape(cache_mem_v.shape[0], bs, n_mem, MEM_W)
    paged = (page_table, cache_kv_k.reshape(n_pool, page, -1), cache_kv_v.reshape(n_pool, page, -1))
    y_s, conv_s, s_s, k_s, v_s = _trunk(p, x_sample, past_len, cmk, cmv, state_dn_conv[0],
                                        state_dn_S[0], paged)
    return (y_p, y_s, conv_p, s_p, k_p, v_p, mem_k_p, mem_v_p, conv_s, s_s, k_s, v_s)
```

```python
import functools
import math

import jax
import jax.numpy as jnp
from jax import lax
from jax.experimental import pallas as pl
from jax.experimental.pallas import tpu as pltpu

F32 = jnp.float32
BF16 = jnp.bfloat16
HI = lax.Precision.HIGHEST

HEAD_DIM = 64
N_HEADS = 12
MIX_W = N_HEADS * HEAD_DIM
N_MEM_HEADS = 4
MEM_W = N_MEM_HEADS * HEAD_DIM
N_KV_B = 4
KV_GROUP = N_HEADS // N_KV_B
GROUP_W = KV_GROUP * HEAD_DIM
CONV_W = 4
DN_CHUNK = 64
MOBA_BLOCK = 256
MOBA_TOPK = 3
ROT_DIM = HEAD_DIM // 4
ROPE_THETA = 500000.0
EPS = 1e-6
QK_SCALE = HEAD_DIM ** -0.5

LANES = 128
SUBLANES = 8
VMEM_BYTES_V7X = 64 * 2 ** 20
NEG_INF = float("-inf")

NT = (((1,), (1,)), ((), ()))
TN = (((0,), (0,)), ((), ()))


def _params(semantics, est_bytes):
    limit = min(max(int(est_bytes * 1.5), 16 * 2 ** 20), VMEM_BYTES_V7X - 8 * 2 ** 20)
    return pltpu.CompilerParams(dimension_semantics=semantics, vmem_limit_bytes=limit)


def _row_tile(m, pref):
    t = min(m, pref)
    while m % t or t % SUBLANES:
        t -= 1
    return t


def _rms(x, g):
    return x * lax.rsqrt(jnp.mean(x * x, axis=-1, keepdims=True) + EPS) * g


def _sigmoid(x):
    return 1.0 / (1.0 + jnp.exp(-x))


def _softplus(x):
    return jnp.maximum(x, 0.0) + jnp.log(1.0 + jnp.exp(-jnp.abs(x)))


def _bdot(a, b):
    return jnp.dot(a.astype(BF16), b.astype(BF16), preferred_element_type=F32)


def _bdot_nt(a, b):
    return lax.dot_general(a.astype(BF16), b.astype(BF16), NT, preferred_element_type=F32)


def _ffn_kernel(*refs, n_f, final):
    if final:
        x_ref, g_ref, w1_ref, w3_ref, w2_ref, fg_ref, o_ref, h_scr, acc_scr = refs
    else:
        x_ref, g_ref, w1_ref, w3_ref, w2_ref, o_ref, h_scr, acc_scr = refs
    f = pl.program_id(1)

    @pl.when(f == 0)
    def _():
        h_scr[...] = _rms(x_ref[...], g_ref[...]).astype(BF16)
        acc_scr[...] = jnp.zeros_like(acc_scr)

    h = h_scr[...]
    a = jnp.dot(h, w1_ref[...], preferred_element_type=F32)
    b = jnp.dot(h, w3_ref[...], preferred_element_type=F32)
    u = (a * _sigmoid(a)) * b
    acc_scr[...] += jnp.dot(u.astype(BF16), w2_ref[...], preferred_element_type=F32)

    @pl.when(f == n_f - 1)
    def _():
        y = x_ref[...] + 0.5 * acc_scr[...]
        if final:
            y = _rms(y, fg_ref[...])
        o_ref[...] = y


def _ffn(x, g, w1, w3, w2, final_gain=None):
    m, d = x.shape
    d_ff = w1.shape[1]
    tm = _row_tile(m, 512)
    n_f = 2 if d_ff % (2 * LANES) == 0 else 1
    tf = d_ff // n_f
    final = final_gain is not None
    in_specs = [
        pl.BlockSpec((tm, d), lambda i, f: (i, 0)),
        pl.BlockSpec((1, d), lambda i, f: (0, 0)),
        pl.BlockSpec((d, tf), lambda i, f: (0, f)),
        pl.BlockSpec((d, tf), lambda i, f: (0, f)),
        pl.BlockSpec((tf, d), lambda i, f: (f, 0)),
    ]
    args = [x, g.reshape(1, d), w1, w3, w2]
    if final:
        in_specs.append(pl.BlockSpec((1, d), lambda i, f: (0, 0)))
        args.append(final_gain.reshape(1, d))
    est = 4 * tm * d * 4 + 6 * d * tf * 2 + tm * d * 6 + 3 * tm * tf * 4
    return pl.pallas_call(
        functools.partial(_ffn_kernel, n_f=n_f, final=final),
        grid=(m // tm, n_f),
        in_specs=in_specs,
        out_specs=pl.BlockSpec((tm, d), lambda i, f: (i, 0)),
        out_shape=jax.ShapeDtypeStruct((m, d), F32),
        scratch_shapes=[pltpu.VMEM((tm, d), BF16), pltpu.VMEM((tm, d), F32)],
        compiler_params=_params(("parallel", "arbitrary"), est),
        name="ffn_final" if final else "ffn",
    )(*args)


def _norm_proj_kernel(x_ref, g_ref, w_ref, *out_refs, splits):
    h = _rms(x_ref[...], g_ref[...]).astype(BF16)
    for o_ref, (a, b) in zip(out_refs, splits):
        o_ref[...] = jnp.dot(h, w_ref[:, a:b], preferred_element_type=F32)


def _norm_proj(x, g, w, widths, name):
    m, d = x.shape
    n = w.shape[1]
    assert sum(widths) == n and all(c % LANES == 0 for c in widths)
    tm = _row_tile(m, 256)
    splits, a = [], 0
    for c in widths:
        splits.append((a, a + c))
        a += c
    est = 2 * tm * d * 4 + 2 * d * n * 2 + 3 * tm * n * 4
    return pl.pallas_call(
        functools.partial(_norm_proj_kernel, splits=tuple(splits)),
        grid=(m // tm,),
        in_specs=[
            pl.BlockSpec((tm, d), lambda i: (i, 0)),
            pl.BlockSpec((1, d), lambda i: (0, 0)),
            pl.BlockSpec((d, n), lambda i: (0, 0)),
        ],
        out_specs=[pl.BlockSpec((tm, c), lambda i: (i, 0)) for c in widths],
        out_shape=[jax.ShapeDtypeStruct((m, c), F32) for c in widths],
        compiler_params=_params(("parallel",), est),
        name=name,
    )(x, g.reshape(1, d), w)


def _rope_table_kernel(inv_ref, c_ref, s1_ref, s2_ref, *, period, base, tl):
    i = pl.program_id(0)
    row = lax.broadcasted_iota(jnp.int32, (tl, LANES), 0) + i * tl
    pos = base + lax.rem(row, period)
    ang = pos.astype(F32) * inv_ref[...]
    in_head = lax.broadcasted_iota(jnp.int32, (tl, LANES), 1) % HEAD_DIM
    half = ROT_DIM // 2
    c = jnp.cos(ang)
    s = jnp.sin(ang)
    c_ref[...] = jnp.where(in_head < ROT_DIM, c, 1.0)
    s1_ref[...] = jnp.where(in_head < half, -s, 0.0)
    s2_ref[...] = jnp.where((in_head >= half) & (in_head < ROT_DIM), s, 0.0)


def _rope_tables(n_rows, period, base):
    half = ROT_DIM // 2
    inv = ROPE_THETA ** (-jnp.arange(half, dtype=F32) * 2.0 / ROT_DIM)
    in_head = jnp.arange(LANES) % HEAD_DIM
    inv_lane = jnp.where(in_head < ROT_DIM, inv[in_head % half], 0.0).astype(F32).reshape(1, LANES)
    tl = _row_tile(n_rows, 1024)
    shp = jax.ShapeDtypeStruct((n_rows, LANES), F32)
    return pl.pallas_call(
        functools.partial(_rope_table_kernel, period=period, base=base, tl=tl),
        grid=(n_rows // tl,),
        in_specs=[pl.BlockSpec((1, LANES), lambda i: (0, 0))],
        out_specs=[pl.BlockSpec((tl, LANES), lambda i: (i, 0))] * 3,
        out_shape=[shp] * 3,
        compiler_params=_params(("parallel",), 16 * tl * LANES * 4),
        name="rope_tables",
    )(inv_lane)


def _apply_rope(t, c, s1, s2):
    w = t.shape[1]
    reps = w // LANES
    half = ROT_DIM // 2
    tile = lambda a: jnp.concatenate([a] * reps, axis=1) if reps > 1 else a
    up = pltpu.roll(t, w - half, 1)
    down = pltpu.roll(t, half, 1)
    return t * tile(c) + up * tile(s1) + down * tile(s2)


def _kv_proj_kernel(x_ref, g_ref, w_ref, c_ref, s1_ref, s2_ref, *out_refs, for_prompt):
    h = _rms(x_ref[...], g_ref[...]).astype(BF16)
    kv_w = N_KV_B * HEAD_DIM
    k = jnp.dot(h, w_ref[:, :kv_w], preferred_element_type=F32)
    v = jnp.dot(h, w_ref[:, kv_w:], preferred_element_type=F32)
    k = _apply_rope(k, c_ref[...], s1_ref[...], s2_ref[...])
    if for_prompt:
        kt_ref, vt_ref, kh_ref, vtb_ref, kbar_ref, v_scr = out_refs
        v_scr[...] = v
        v_t = v_scr[...].T
        kt_ref[0] = k.T
        vt_ref[0] = v_t
        vtb_ref[0] = v_t.astype(BF16)
        for hk in range(N_KV_B):
            kh_ref[hk] = k[:, hk * HEAD_DIM:(hk + 1) * HEAD_DIM].astype(BF16)
        kbar_ref[0] = jnp.sum(k, axis=0, keepdims=True) * (1.0 / MOBA_BLOCK)
    else:
        out_refs[0][...] = k
        out_refs[1][...] = v


def _kv_proj(x, g, w, tables, seq_len, for_prompt):
    m, d = x.shape
    kv_w = N_KV_B * HEAD_DIM
    tm = MOBA_BLOCK if for_prompt else _row_tile(m, 256)
    n_tab = tables[0].shape[0] // tm
    row = lambda i: (i, 0)
    tab = lambda i: (i % n_tab, 0)
    if for_prompt:
        nb = seq_len // tm
        t_spec = pl.BlockSpec((1, kv_w, tm), lambda i: (i // nb, 0, i % nb))
        t_shape = jax.ShapeDtypeStruct((m // seq_len, kv_w, seq_len), F32)
        out_specs = [t_spec, t_spec,
                     pl.BlockSpec((N_KV_B, tm, HEAD_DIM), lambda i: (0, i, 0)),
                     pl.BlockSpec((1, kv_w, tm), lambda i: (i, 0, 0)),
                     pl.BlockSpec((1, 1, kv_w), lambda i: (i, 0, 0))]
        out_shape = [t_shape, t_shape,
                     jax.ShapeDtypeStruct((N_KV_B, m, HEAD_DIM), BF16),
                     jax.ShapeDtypeStruct((m // tm, kv_w, tm), BF16),
                     jax.ShapeDtypeStruct((m // tm, 1, kv_w), F32)]
    else:
        out_specs = [pl.BlockSpec((tm, kv_w), row), pl.BlockSpec((tm, kv_w), row)]
        out_shape = [jax.ShapeDtypeStruct((m, kv_w), F32)] * 2
    est = 2 * tm * d * 4 + 2 * d * 2 * kv_w * 2 + 12 * tm * kv_w * 4
    return pl.pallas_call(
        functools.partial(_kv_proj_kernel, for_prompt=for_prompt),
        grid=(m // tm,),
        in_specs=[
            pl.BlockSpec((tm, d), row),
            pl.BlockSpec((1, d), lambda i: (0, 0)),
            pl.BlockSpec((d, 2 * kv_w), lambda i: (0, 0)),
            pl.BlockSpec((tm, LANES), tab),
            pl.BlockSpec((tm, LANES), tab),
            pl.BlockSpec((tm, LANES), tab),
        ],
        out_specs=out_specs,
        out_shape=out_shape,
        scratch_shapes=[pltpu.VMEM((tm, kv_w), F32)] if for_prompt else [],
        compiler_params=_params(("parallel",), est),
        name="kv_proj",
    )(x, g.reshape(1, d), w, *tables)


def _q_proj_kernel(x_ref, g_ref, w_ref, c_ref, s1_ref, s2_ref, qh_ref, qm_ref):
    h = _rms(x_ref[...], g_ref[...]).astype(BF16)
    q = jnp.dot(h, w_ref[:, :MIX_W], preferred_element_type=F32)
    qm_ref[...] = jnp.dot(h, w_ref[:, MIX_W:], preferred_element_type=F32)
    q = _apply_rope(q, c_ref[...], s1_ref[...], s2_ref[...]) * QK_SCALE
    for hk in range(N_KV_B):
        qh_ref[hk] = q[:, hk * GROUP_W:(hk + 1) * GROUP_W]


def _q_proj(x, g, w, tables):
    m, d = x.shape
    n = w.shape[1]
    tm = _row_tile(m, 256)
    n_tab = tables[0].shape[0] // tm
    row = lambda i: (i, 0)
    tab = lambda i: (i % n_tab, 0)
    est = 2 * tm * d * 4 + 2 * d * n * 2 + 16 * tm * MIX_W * 4
    return pl.pallas_call(
        _q_proj_kernel,
        grid=(m // tm,),
        in_specs=[
            pl.BlockSpec((tm, d), row),
            pl.BlockSpec((1, d), lambda i: (0, 0)),
            pl.BlockSpec((d, n), lambda i: (0, 0)),
            pl.BlockSpec((tm, LANES), tab),
            pl.BlockSpec((tm, LANES), tab),
            pl.BlockSpec((tm, LANES), tab),
        ],
        out_specs=[
            pl.BlockSpec((N_KV_B, tm, GROUP_W), lambda i: (0, i, 0)),
            pl.BlockSpec((tm, MEM_W), row),
        ],
        out_shape=[
            jax.ShapeDtypeStruct((N_KV_B, m, GROUP_W), F32),
            jax.ShapeDtypeStruct((m, MEM_W), F32),
        ],
        compiler_params=_params(("parallel",), est),
        name="q_proj",
    )(x, g.reshape(1, d), w, *tables)


def _conv_kernel(u_ref, prev_ref, buf_ref, w_ref, y_ref, nb_ref, pad_scr, *, tl, n_t):
    i = pl.program_id(1)
    halo = CONV_W - 1
    lo = SUBLANES - halo
    pad_scr[SUBLANES:SUBLANES + tl, :] = u_ref[0]

    @pl.when(i == 0)
    def _():
        pad_scr[lo:SUBLANES, :] = buf_ref[0]

    @pl.when(i > 0)
    def _():
        p = prev_ref[0]
        pad_scr[lo:SUBLANES, :] = p[p.shape[0] - halo:, :]

    y = pad_scr[lo:lo + tl, :] * w_ref[0:1, :]
    for j in range(1, CONV_W):
        y = y + pad_scr[lo + j:lo + j + tl, :] * w_ref[j:j + 1, :]
    y_ref[0] = y * _sigmoid(y)

    @pl.when(i == n_t - 1)
    def _():
        nb_ref[0] = pad_scr[lo + tl:SUBLANES + tl, :]


def _conv(u, buf, w):
    b, l, c = u.shape
    tl = _row_tile(l, 512) if l % SUBLANES == 0 else l
    n_t = l // tl
    pr = min(SUBLANES, l)
    per = tl // pr
    est = 6 * tl * c * 4
    return pl.pallas_call(
        functools.partial(_conv_kernel, tl=tl, n_t=n_t),
        grid=(b, n_t),
        in_specs=[
            pl.BlockSpec((1, tl, c), lambda bi, i: (bi, i, 0)),
            pl.BlockSpec((1, pr, c), lambda bi, i: (bi, jnp.maximum(i * per - 1, 0), 0)),
            pl.BlockSpec((1, CONV_W - 1, c), lambda bi, i: (bi, 0, 0)),
            pl.BlockSpec((CONV_W, c), lambda bi, i: (0, 0)),
        ],
        out_specs=[
            pl.BlockSpec((1, tl, c), lambda bi, i: (bi, i, 0)),
            pl.BlockSpec((1, CONV_W - 1, c), lambda bi, i: (bi, 0, 0)),
        ],
        out_shape=[
            jax.ShapeDtypeStruct((b, l, c), F32),
            jax.ShapeDtypeStruct((b, CONV_W - 1, c), F32),
        ],
        scratch_shapes=[pltpu.VMEM((tl + SUBLANES, c), F32)],
        compiler_params=_params(("parallel", "arbitrary"), est),
        name="short_conv",
    )(u, u, buf, w)


HEADS_PER_SLAB = LANES * 2 // HEAD_DIM
SLAB_W = HEADS_PER_SLAB * HEAD_DIM
N_SLABS = N_HEADS // HEADS_PER_SLAB


def _split2(x):
    hi = x.astype(BF16)
    return hi, (x - hi.astype(F32)).astype(BF16)


def _split3(x):
    hi = x.astype(BF16)
    r = x - hi.astype(F32)
    mid = r.astype(BF16)
    return hi, mid, (r - mid.astype(F32)).astype(BF16)


def _dot_right01(x, sel):
    hi, mid, lo = _split3(x)
    d = lambda a: jnp.dot(a, sel, preferred_element_type=F32)
    return d(hi) + (d(mid) + d(lo))


def _dot_left01(sel, x):
    hi, mid, lo = _split3(x)
    d = lambda a: jnp.dot(sel, a, preferred_element_type=F32)
    return d(hi) + (d(mid) + d(lo))


def _dot3(a, b_hi, b_lo):
    a_hi, a_lo = _split2(a)
    d = lambda x, y: jnp.dot(x, y, preferred_element_type=F32)
    return d(a_hi, b_hi) + (d(a_hi, b_lo) + d(a_lo, b_hi))


def _block_diag(x, mask):
    return jnp.concatenate([x] * HEADS_PER_SLAB, axis=0) * mask


def _block_diag_pieces(x, mask):
    hi, lo = _split2(x)
    return _block_diag(hi, mask), _block_diag(lo, mask)


def _unit_lower_inverses(ms, eye_t, mask):
    c = ms[0].shape[0]
    xs = [eye_t - m for m in ms]
    ps = [_dot3(m, *_block_diag_pieces(m, mask)) for m in ms]
    power = 2
    while 2 * power < c:
        xps = [_dot3(jnp.concatenate([x, p], axis=0), *_block_diag_pieces(p, mask))
               for x, p in zip(xs, ps)]
        xs = [x + xp[:c] for x, xp in zip(xs, xps)]
        ps = [xp[c:] for xp in xps]
        power *= 2
    return [x + _dot3(x, *_block_diag_pieces(p, mask)) for x, p in zip(xs, ps)]


def _gdn_kernel(q_ref, k_ref, v_ref, gate_ref, ab_ref, alog_ref, dt_ref, gain_ref, s0_ref,
                o_ref, sfin_ref,
                s_scr, qn_scr, kn_scr, gcb_scr, bb_scr, u_scr, w_scr, a_scr, qg_scr, kd_scr, egl_scr,
                *, t_rows, n_t, l_valid):
    t = pl.program_id(1)
    c = DN_CHUNK
    d = HEAD_DIM
    n_chunks = t_rows // c

    r_s = lax.broadcasted_iota(jnp.int32, (SLAB_W, SLAB_W), 0)
    c_s = lax.broadcasted_iota(jnp.int32, (SLAB_W, SLAB_W), 1)
    same_head = (r_s // d) == (c_s // d)
    bd_f32 = jnp.where(same_head, 1.0, 0.0).astype(F32)
    bd_mask = bd_f32.astype(BF16)
    r_t = lax.broadcasted_iota(jnp.int32, (c, SLAB_W), 0)
    c_t = lax.broadcasted_iota(jnp.int32, (c, SLAB_W), 1) % d
    lower_t = r_t >= c_t
    strict_t = r_t > c_t
    eye_t = jnp.where(r_t == c_t, 1.0, 0.0).astype(F32)
    r_c = lax.broadcasted_iota(jnp.int32, (c, c), 0)
    c_c = lax.broadcasted_iota(jnp.int32, (c, c), 1)
    tril = jnp.where(r_c >= c_c, 1.0, 0.0).astype(BF16)
    er = lax.broadcasted_iota(jnp.int32, (LANES, MIX_W), 0)
    ec = lax.broadcasted_iota(jnp.int32, (LANES, MIX_W), 1) // d
    sel_g = jnp.where(er == ec, 1.0, 0.0).astype(BF16)
    sel_b = jnp.where(er == ec + N_HEADS, 1.0, 0.0).astype(BF16)

    @pl.when(t == 0)
    def _():
        for s in range(N_SLABS):
            rows = jnp.concatenate([s0_ref[0, s * HEADS_PER_SLAB + h] for h in range(HEADS_PER_SLAB)],
                                   axis=0)
            s_scr[s] = jnp.concatenate([rows] * HEADS_PER_SLAB, axis=1) * bd_f32

    ab = ab_ref[0]
    row = lax.broadcasted_iota(jnp.int32, ab.shape, 0) + t * t_rows
    valid = row < l_valid
    g = jnp.where(valid, -jnp.exp(alog_ref[...]) * _softplus(ab + dt_ref[...]), 0.0)
    beta = jnp.where(valid, _sigmoid(ab), 0.0)
    gc = jnp.concatenate([_dot_left01(tril, g[i * c:(i + 1) * c]) for i in range(n_chunks)], axis=0)
    gcb_scr[...] = _dot_right01(gc, sel_g)
    bb_scr[...] = _dot_right01(beta, sel_b)
    for s in range(N_SLABS):
        cs = slice(s * SLAB_W, (s + 1) * SLAB_W)
        for src, dst, scale in ((q_ref, qn_scr, QK_SCALE), (k_ref, kn_scr, 1.0)):
            x = src[0, :, cs]
            hi, lo = _split2(x * x)
            ssq = (jnp.dot(hi, bd_mask, preferred_element_type=F32)
                   + jnp.dot(lo, bd_mask, preferred_element_type=F32))
            dst[:, cs] = x * (lax.rsqrt(ssq + EPS) * scale)

    per_iter = 2 if n_chunks % 2 == 0 else 1
    slabs = [slice(s * SLAB_W, (s + 1) * SLAB_W) for s in range(N_SLABS)]

    def factors(it, carry):
        items = []
        for u in range(per_iter):
            ci = it * per_iter + u
            rows = pl.ds(pl.multiple_of(ci * c, c), c)
            items += [(ci, rows, cs) for cs in slabs]
        pre = []
        for ci, rows, cs in items:
            gcb = gcb_scr[rows, cs]
            kn = kn_scr[rows, cs]
            qn = qn_scr[rows, cs]
            b_ = bb_scr[rows, cs]
            kb = kn * b_
            g_row = jnp.sum(gcb * eye_t, axis=0, keepdims=True)
            dec = jnp.exp(jnp.where(lower_t, gcb - g_row, NEG_INF))
            both = lax.dot_general(jnp.concatenate([kb, qn], axis=0).astype(BF16),
                                   _block_diag(kn.astype(BF16), bd_mask), NT,
                                   preferred_element_type=F32)
            pre.append((gcb, kn, qn, kb, dec, both))
        ms = []
        for (ci, rows, cs), (gcb, kn, qn, kb, dec, both) in zip(items, pre):
            ms.append(jnp.where(strict_t, both[:c] * dec, 0.0))
            a_scr[rows, cs] = (both[c:] * dec).astype(BF16)
        tinvs = _unit_lower_inverses(ms, eye_t, bd_mask)
        sols = []
        for (ci, rows, cs), (gcb, kn, qn, kb, dec, both), tinv in zip(items, pre, tinvs):
            eg = jnp.exp(gcb)
            v_hi, v_lo = _block_diag_pieces(v_ref[0, rows, cs] * bb_scr[rows, cs], bd_mask)
            k_hi, k_lo = _block_diag_pieces(kb * eg, bd_mask)
            sols.append(_dot3(tinv, jnp.concatenate([v_hi, k_hi], axis=1),
                              jnp.concatenate([v_lo, k_lo], axis=1)))
            glast = gcb[c - 1:c, :]
            qg_scr[rows, cs] = (qn * eg).astype(BF16)
            kd_scr[rows, cs] = (kn * jnp.exp(glast - gcb)).astype(BF16)
            egl_scr[pl.ds(ci, 1), cs] = jnp.exp(glast)
        for (ci, rows, cs), sol in zip(items, sols):
            u_scr[rows, cs] = sol[:, :SLAB_W]
            w_scr[rows, cs] = sol[:, SLAB_W:].astype(BF16)
        return carry

    lax.fori_loop(0, n_chunks // per_iter, factors, 0)

    gain = gain_ref[...]

    def recur(ci, carry):
        rows = pl.ds(pl.multiple_of(ci * c, c), c)
        olds = [s_scr[s] for s in range(N_SLABS)]
        rs = [jnp.dot(jnp.concatenate([w_scr[rows, cs], qg_scr[rows, cs]], axis=0),
                      s_old.astype(BF16), preferred_element_type=F32)
              for cs, s_old in zip(slabs, olds)]
        v_news = [(u_scr[rows, cs] - r[:c]).astype(BF16) for cs, r in zip(slabs, rs)]
        upds = [lax.dot_general(kd_scr[rows, cs], v_new, TN, preferred_element_type=F32)
                for cs, v_new in zip(slabs, v_news)]
        os_ = [r[c:] + jnp.dot(a_scr[rows, cs], _block_diag(v_new, bd_mask),
                               preferred_element_type=F32)
               for cs, r, v_new in zip(slabs, rs, v_news)]
        for s, (cs, s_old, upd) in enumerate(zip(slabs, olds, upds)):
            s_scr[s] = (s_old * egl_scr[pl.ds(ci, 1), cs] + upd) * bd_f32
        ssqs = []
        for o in os_:
            hi, lo = _split2(o * o)
            ssqs.append(jnp.dot(hi, bd_mask, preferred_element_type=F32)
                        + jnp.dot(lo, bd_mask, preferred_element_type=F32))
        for cs, o, ssq in zip(slabs, os_, ssqs):
            gt = gate_ref[0, rows, cs]
            o_ref[0, rows, cs] = (o * lax.rsqrt(ssq * (1.0 / d) + EPS) * gain[:, cs]
                                  * (gt * _sigmoid(gt)))
        return carry

    lax.fori_loop(0, n_chunks, recur, 0)

    @pl.when(t == n_t - 1)
    def _():
        for s in range(N_SLABS):
            full = s_scr[s]
            for h in range(HEADS_PER_SLAB):
                sfin_ref[0, s * HEADS_PER_SLAB + h] = full[h * d:(h + 1) * d, h * d:(h + 1) * d]


def _gdn(qkv, gate, ab, a_log, dt_bias, out_gain, s0, l_valid):
    b, lp, _ = qkv.shape
    assert lp % DN_CHUNK == 0
    t_rows = DN_CHUNK * math.gcd(lp // DN_CHUNK, 8)
    n_t = lp // t_rows
    n_chunks = t_rows // DN_CHUNK
    pad = lambda a: jnp.pad(a.astype(F32), (0, LANES - N_HEADS)).reshape(1, LANES)
    blk = lambda j: pl.BlockSpec((1, t_rows, MIX_W), lambda bi, t, j=j: (bi, t, j))
    state = pl.BlockSpec((1, N_HEADS, HEAD_DIM, HEAD_DIM), lambda bi, t: (bi, 0, 0, 0))
    vec = pl.BlockSpec((1, LANES), lambda bi, t: (0, 0))
    wide_f32 = pltpu.VMEM((t_rows, MIX_W), F32)
    wide_bf16 = pltpu.VMEM((t_rows, MIX_W), BF16)
    est = 10 * t_rows * MIX_W * 4 + 5 * t_rows * MIX_W * 4 + 4 * t_rows * MIX_W * 2 + 64 * SLAB_W * SLAB_W * 4
    return pl.pallas_call(
        functools.partial(_gdn_kernel, t_rows=t_rows, n_t=n_t, l_valid=l_valid),
        grid=(b, n_t),
        in_specs=[
            blk(0), blk(1), blk(2),
            pl.BlockSpec((1, t_rows, MIX_W), lambda bi, t: (bi, t, 0)),
            pl.BlockSpec((1, t_rows, LANES), lambda bi, t: (bi, t, 0)),
            vec, vec,
            pl.BlockSpec((1, MIX_W), lambda bi, t: (0, 0)),
            state,
        ],
        out_specs=[pl.BlockSpec((1, t_rows, MIX_W), lambda bi, t: (bi, t, 0)), state],
        out_shape=[
            jax.ShapeDtypeStruct((b, lp, MIX_W), F32),
            jax.ShapeDtypeStruct((b, N_HEADS, HEAD_DIM, HEAD_DIM), F32),
        ],
        scratch_shapes=[
            pltpu.VMEM((N_SLABS, SLAB_W, SLAB_W), F32),
            wide_f32, wide_f32,
            wide_f32, wide_f32,
            wide_f32,
            wide_bf16, wide_bf16, wide_bf16, wide_bf16,
            pltpu.VMEM((max(n_chunks, SUBLANES), MIX_W), F32),
        ],
        compiler_params=_params(("parallel", "arbitrary"), est),
        name="gated_delta_rule",
    )(qkv, qkv, qkv, gate, ab, pad(a_log), pad(dt_bias),
      jnp.tile(out_gain.astype(F32), N_HEADS).reshape(1, MIX_W), s0)


def _mem_kv_kernel(x_ref, g_ref, w_ref, kt_ref, vt_ref, kv_scr):
    h = _rms(x_ref[...], g_ref[...]).astype(BF16)
    kv_scr[...] = jnp.dot(h, w_ref[...], preferred_element_type=F32)
    kt_ref[0] = kv_scr[:, :MEM_W].T
    vt_ref[0] = kv_scr[:, MEM_W:].T


def _mem_kv(mem, g, w):
    b, n_mem, d = mem.shape
    shp = jax.ShapeDtypeStruct((b, MEM_W, n_mem), F32)
    spec = pl.BlockSpec((1, MEM_W, n_mem), lambda i: (i, 0, 0))
    return pl.pallas_call(
        _mem_kv_kernel,
        grid=(b,),
        in_specs=[
            pl.BlockSpec((n_mem, d), lambda i: (i, 0)),
            pl.BlockSpec((1, d), lambda i: (0, 0)),
            pl.BlockSpec((d, 2 * MEM_W), lambda i: (0, 0)),
        ],
        out_specs=[spec, spec],
        out_shape=[shp, shp],
        scratch_shapes=[pltpu.VMEM((n_mem, 2 * MEM_W), F32)],
        compiler_params=_params(("parallel",), 2 * n_mem * d * 4 + 2 * d * 2 * MEM_W * 2 + 8 * n_mem * MEM_W * 4),
        name="mem_kv",
    )(mem.reshape(b * n_mem, d), g.reshape(1, d), w)


def _mem_attn_kernel(q_ref, kt_ref, vt_ref, o_ref):
    q = q_ref[0]
    kt = kt_ref[0]
    vt = vt_ref[0]
    outs = []
    for h in range(N_MEM_HEADS):
        sl = slice(h * HEAD_DIM, (h + 1) * HEAD_DIM)
        s = _bdot(q[:, sl], kt[sl, :]) * QK_SCALE
        p = jnp.exp(s - jnp.max(s, axis=-1, keepdims=True))
        outs.append(_bdot_nt(p, vt[sl, :]) / jnp.sum(p, axis=-1, keepdims=True))
    o_ref[0] = jnp.concatenate(outs, axis=1)


def _mem_attn(q, mk_t, mv_t):
    b, l, w = q.shape
    n_mem = mk_t.shape[2]
    tl = _row_tile(l, 512)
    est = 4 * tl * w * 4 + 4 * n_mem * w * 4 + 8 * tl * n_mem * 4
    return pl.pallas_call(
        _mem_attn_kernel,
        grid=(b, l // tl),
        in_specs=[
            pl.BlockSpec((1, tl, w), lambda bi, i: (bi, i, 0)),
            pl.BlockSpec((1, w, n_mem), lambda bi, i: (bi, 0, 0)),
            pl.BlockSpec((1, w, n_mem), lambda bi, i: (bi, 0, 0)),
        ],
        out_specs=pl.BlockSpec((1, tl, w), lambda bi, i: (bi, i, 0)),
        out_shape=jax.ShapeDtypeStruct((b, l, w), F32),
        compiler_params=_params(("parallel", "parallel"), est),
        name="mem_attn",
    )(q, mk_t, mv_t)


def _out_proj_kernel(x_ref, o_ref, mo_ref, wo_ref, wm_ref, y_ref, *, grouped):
    y = x_ref[...] + _bdot(mo_ref[...], wm_ref[...])
    if grouped:
        for hk in range(N_KV_B):
            y = y + _bdot(o_ref[hk], wo_ref[hk])
    else:
        y = y + _bdot(o_ref[...], wo_ref[...])
    y_ref[...] = y


def _out_proj(x, o, mo, w_out, grouped):
    m, d = x.shape
    tm = _row_tile(m, 512)
    row = lambda i: (i, 0)
    if grouped:
        wo = w_out[:MIX_W].reshape(N_KV_B, GROUP_W, d)
        o_spec = pl.BlockSpec((N_KV_B, tm, GROUP_W), lambda i: (0, i, 0))
        wo_spec = pl.BlockSpec((N_KV_B, GROUP_W, d), lambda i: (0, 0, 0))
    else:
        wo = w_out[:MIX_W]
        o_spec = pl.BlockSpec((tm, MIX_W), row)
        wo_spec = pl.BlockSpec((MIX_W, d), lambda i: (0, 0))
    est = 6 * tm * d * 4 + 4 * tm * d * 4 + 4 * d * d * 2
    return pl.pallas_call(
        functools.partial(_out_proj_kernel, grouped=grouped),
        grid=(m // tm,),
        in_specs=[
            pl.BlockSpec((tm, d), row),
            o_spec,
            pl.BlockSpec((tm, MEM_W), row),
            wo_spec,
            pl.BlockSpec((MEM_W, d), lambda i: (0, 0)),
        ],
        out_specs=pl.BlockSpec((tm, d), row),
        out_shape=jax.ShapeDtypeStruct((m, d), F32),
        compiler_params=_params(("parallel",), est),
        name="out_proj",
    )(x, o, mo, wo, w_out[MIX_W:])


def _select_blocks(gate, n_past, k_sel):
    lane = lax.broadcasted_iota(jnp.int32, gate.shape, 1)
    lane_f = lane.astype(F32)
    gate = jnp.where(lane < n_past, gate, NEG_INF)
    sel = jnp.zeros(gate.shape, F32)
    for _ in range(k_sel):
        mx = jnp.max(gate, axis=1, keepdims=True)
        first = jnp.min(jnp.where(gate == mx, lane_f, float(gate.shape[1])), axis=1, keepdims=True)
        pick = (lane_f == first) & (mx > NEG_INF)
        sel = jnp.where(pick, 1.0, sel)
        gate = jnp.where(pick, NEG_INF, gate)
    return sel


def _select_blocks_t(gate, n_past, k_sel):
    blk = lax.broadcasted_iota(jnp.int32, gate.shape, 0)
    blk_f = blk.astype(F32)
    gate = jnp.where(blk < n_past, gate, NEG_INF)
    sel = jnp.zeros(gate.shape, F32)
    for _ in range(k_sel):
        mx = jnp.max(gate, axis=0, keepdims=True)
        first = jnp.min(jnp.where(gate == mx, blk_f, float(gate.shape[0])), axis=0, keepdims=True)
        pick = (blk_f == first) & (mx > NEG_INF)
        sel = jnp.where(pick, 1.0, sel)
        gate = jnp.where(pick, NEG_INF, gate)
    return sel


def _moba_prompt_kernel(q_ref, k_ref, vt_ref, kbar_ref, o_ref, q_scr, m_scr, l_scr, acc_scr, sel_scr,
                        *, k_sel):
    i = pl.program_id(2)
    blk = MOBA_BLOCK
    d = HEAD_DIM
    n_chunks = KV_GROUP * blk // LANES
    q = q_ref[0]
    q3 = jnp.concatenate([q[:, g * d:(g + 1) * d] for g in range(KV_GROUP)], axis=0)
    gate_t = lax.dot_general(kbar_ref[0, 0], q3, NT, precision=HI, preferred_element_type=F32)
    sel_scr[...] = _select_blocks_t(gate_t, i, k_sel)
    q_scr[...] = q3.astype(BF16)

    def block_scores(k_blk, c):
        return lax.dot_general(k_blk, q_scr[c * LANES:(c + 1) * LANES, :], NT,
                               preferred_element_type=F32)

    k_own = k_ref[0, pl.ds(pl.multiple_of(i * blk, blk), blk), :]
    vt_own = vt_ref[i, :, :]
    scores = [block_scores(k_own, c) for c in range(n_chunks)]
    probs = []
    for c in range(n_chunks):
        cs = slice(c * LANES, (c + 1) * LANES)
        kpos = lax.broadcasted_iota(jnp.int32, (blk, LANES), 0)
        qpos = (lax.broadcasted_iota(jnp.int32, (blk, LANES), 1) + c * LANES) % blk
        s = jnp.where(kpos <= qpos, scores[c], NEG_INF)
        m0 = jnp.max(s, axis=0, keepdims=True)
        p = jnp.exp(s - m0)
        m_scr[:, cs] = m0
        l_scr[:, cs] = jnp.sum(p, axis=0, keepdims=True)
        probs.append(p.astype(BF16))
    for c in range(n_chunks):
        acc_scr[:, c * LANES:(c + 1) * LANES] = jnp.dot(vt_own, probs[c], preferred_element_type=F32)

    def past(j, carry):
        k_j = k_ref[0, pl.ds(pl.multiple_of(j * blk, blk), blk), :]
        vt_j = vt_ref[j, :, :]
        chosen = sel_scr[pl.ds(j, 1), :]
        scores = [block_scores(k_j, c) for c in range(n_chunks)]
        probs, alphas = [], []
        for c in range(n_chunks):
            cs = slice(c * LANES, (c + 1) * LANES)
            s = jnp.where(chosen[:, cs] > 0.0, scores[c], NEG_INF)
            m_old = m_scr[:, cs]
            m_new = jnp.maximum(m_old, jnp.max(s, axis=0, keepdims=True))
            alpha = jnp.exp(m_old - m_new)
            p = jnp.exp(s - m_new)
            l_scr[:, cs] = alpha * l_scr[:, cs] + jnp.sum(p, axis=0, keepdims=True)
            m_scr[:, cs] = m_new
            probs.append(p.astype(BF16))
            alphas.append(alpha)
        for c in range(n_chunks):
            cs = slice(c * LANES, (c + 1) * LANES)
            acc_scr[:, cs] = alphas[c] * acc_scr[:, cs] + jnp.dot(vt_j, probs[c],
                                                                  preferred_element_type=F32)
        return carry

    lax.fori_loop(0, i, past, 0)
    o_t = (acc_scr[...] / l_scr[...]).astype(BF16)
    eye = (lax.broadcasted_iota(jnp.int32, (blk, blk), 0)
           == lax.broadcasted_iota(jnp.int32, (blk, blk), 1)).astype(BF16)
    o_ref[0] = jnp.concatenate(
        [lax.dot_general(eye, o_t[:, g * blk:(g + 1) * blk], NT, preferred_element_type=F32)
         for g in range(KV_GROUP)], axis=1)


def _moba_prompt(qh, kh, vt, kbar, b, l):
    nb = l // MOBA_BLOCK
    rows = KV_GROUP * MOBA_BLOCK
    k_sel = min(MOBA_TOPK, nb)
    est = 4 * l * LANES * 2 + 4 * l * HEAD_DIM * 2 + 8 * MOBA_BLOCK * GROUP_W * 4 + 24 * rows * LANES * 4
    return pl.pallas_call(
        functools.partial(_moba_prompt_kernel, k_sel=k_sel),
        grid=(N_KV_B, b, nb),
        in_specs=[
            pl.BlockSpec((1, MOBA_BLOCK, GROUP_W), lambda hk, bi, i: (hk, bi * nb + i, 0)),
            pl.BlockSpec((1, l, HEAD_DIM), lambda hk, bi, i: (hk, bi, 0)),
            pl.BlockSpec((nb, HEAD_DIM, MOBA_BLOCK), lambda hk, bi, i: (bi, hk, 0)),
            pl.BlockSpec((1, 1, nb, HEAD_DIM), lambda hk, bi, i: (hk, bi, 0, 0)),
        ],
        out_specs=pl.BlockSpec((1, MOBA_BLOCK, GROUP_W), lambda hk, bi, i: (hk, bi * nb + i, 0)),
        out_shape=jax.ShapeDtypeStruct((N_KV_B, b * l, GROUP_W), F32),
        scratch_shapes=[
            pltpu.VMEM((rows, HEAD_DIM), BF16),
            pltpu.VMEM((1, rows), F32),
            pltpu.VMEM((1, rows), F32),
            pltpu.VMEM((HEAD_DIM, rows), F32),
            pltpu.VMEM((nb, rows), F32),
        ],
        compiler_params=_params(("parallel", "parallel", "arbitrary"), est),
        name="moba_prompt",
    )(qh, kh, vt, kbar)


def _moba_sample_kernel(pt_ref, q_ref, kn_ref, vn_ref, kt_hbm, vt_hbm, o_ref,
                        kbuf, vbuf, sem, kbar_scr, s_scr, p_scr,
                        *, n_seq, n_pages, l_new, k_sel):
    b = pl.program_id(0)
    slot = lax.rem(b, 2)
    nb = n_pages // 2
    page = kbuf.shape[-1]
    unroll = math.gcd(nb, 4)

    def page_copy(hbm, buf, sl, which, pg, p):
        return pltpu.make_async_copy(hbm.at[pg], buf.at[sl, p], sem.at[sl, which])

    def fetch(seq, sl):
        def body(p, carry):
            pg = pt_ref[seq, p]
            page_copy(kt_hbm, kbuf, sl, 0, pg, p).start()
            page_copy(vt_hbm, vbuf, sl, 1, pg, p).start()
            return carry
        lax.fori_loop(0, n_pages, body, 0)

    def wait_pages(hbm, buf, which):
        def body(p, carry):
            page_copy(hbm, buf, slot, which, 0, p).wait()
            return carry
        lax.fori_loop(0, n_pages, body, 0)

    @pl.when(b == 0)
    def _():
        fetch(0, 0)

    @pl.when(b + 1 < n_seq)
    def _():
        fetch(b + 1, 1 - slot)

    q = q_ref[0]
    qb = q.astype(BF16)
    rows = q.shape[0]
    wait_pages(kt_hbm, kbuf, 0)

    ones = jnp.ones((SUBLANES, page), BF16)

    def mean_body(j, carry):
        t = kbuf[slot, 2 * j] + kbuf[slot, 2 * j + 1]
        hi, mid, lo = _split3(t)
        d = lambda a: lax.dot_general(ones, a, NT, preferred_element_type=F32)
        kbar_scr[pl.ds(j, 1), :] = (d(hi) + (d(mid) + d(lo)))[0:1] * (1.0 / MOBA_BLOCK)
        return carry

    lax.fori_loop(0, nb, mean_body, 0, unroll=unroll)
    gate = lax.dot_general(q, kbar_scr[...], NT, precision=HI, preferred_element_type=F32)
    sel = _select_blocks(gate, nb, k_sel)
    lane = lax.broadcasted_iota(jnp.int32, sel.shape, 1)

    def score_body(j, carry):
        chosen = jnp.max(jnp.where(lane == j, sel, 0.0), axis=1, keepdims=True)
        s = jnp.concatenate(
            [jnp.dot(qb, kbuf[slot, 2 * j + h].astype(BF16), preferred_element_type=F32)
             for h in range(2)], axis=1)
        s = jnp.where(chosen > 0.0, s, NEG_INF)
        s_scr[j] = s
        return jnp.maximum(carry, s)

    s_max = lax.fori_loop(0, nb, score_body, jnp.full((rows, 2 * page), NEG_INF, F32), unroll=unroll)
    s_new = _bdot_nt(qb, kn_ref[0])
    t_q = lax.broadcasted_iota(jnp.int32, s_new.shape, 0) % l_new
    t_k = lax.broadcasted_iota(jnp.int32, s_new.shape, 1)
    s_new = jnp.where(t_k <= t_q, s_new, NEG_INF)
    m = jnp.maximum(jnp.max(s_max, axis=1, keepdims=True), jnp.max(s_new, axis=1, keepdims=True))
    p_new = jnp.exp(s_new - m)

    def prob_body(j, carry):
        p = jnp.exp(s_scr[j] - m)
        p_scr[j] = p.astype(BF16)
        return carry + p

    p_sum = lax.fori_loop(0, nb, prob_body, jnp.zeros((rows, 2 * page), F32), unroll=unroll)
    denom = jnp.sum(p_sum, axis=1, keepdims=True) + jnp.sum(p_new, axis=1, keepdims=True)
    wait_pages(vt_hbm, vbuf, 1)

    def pv_body(j, acc):
        pj = p_scr[j]
        for h in range(2):
            acc = acc + lax.dot_general(pj[:, h * page:(h + 1) * page],
                                        vbuf[slot, 2 * j + h].astype(BF16), NT,
                                        preferred_element_type=F32)
        return acc

    acc = lax.fori_loop(0, nb, pv_body, _bdot(p_new, vn_ref[0]), unroll=unroll)
    o_ref[0] = acc / denom


def _moba_sample(q_bd, page_table, kt_pool, vt_pool, k_new, v_new, l_new):
    bs, rows, w = q_bd.shape
    n_pages = page_table.shape[1]
    page = kt_pool.shape[2]
    assert 2 * page == MOBA_BLOCK and kt_pool.shape[1] == w
    nb = n_pages // 2
    k_sel = min(MOBA_TOPK, nb + 1)
    lp = k_new.shape[1]
    per_b = lambda r: pl.BlockSpec((1, r, w), lambda b, pt: (b, 0, 0))
    pool_bytes = 2 * n_pages * w * page * 4
    est = 2 * pool_bytes + 8 * nb * rows * 2 * page * 4
    return pl.pallas_call(
        functools.partial(_moba_sample_kernel, n_seq=bs, n_pages=n_pages, l_new=l_new, k_sel=k_sel),
        grid_spec=pltpu.PrefetchScalarGridSpec(
            num_scalar_prefetch=1,
            grid=(bs,),
            in_specs=[per_b(rows), per_b(lp), per_b(lp),
                      pl.BlockSpec(memory_space=pl.ANY), pl.BlockSpec(memory_space=pl.ANY)],
            out_specs=per_b(rows),
            scratch_shapes=[
                pltpu.VMEM((2, n_pages, w, page), F32),
                pltpu.VMEM((2, n_pages, w, page), F32),
                pltpu.SemaphoreType.DMA((2, 2)),
                pltpu.VMEM((nb, w), F32),
                pltpu.VMEM((nb, rows, 2 * page), F32),
                pltpu.VMEM((nb, rows, 2 * page), BF16),
            ],
        ),
        out_shape=jax.ShapeDtypeStruct((bs, rows, w), F32),
        compiler_params=pltpu.CompilerParams(
            dimension_semantics=("arbitrary",),
            vmem_limit_bytes=min(int(est * 1.25), VMEM_BYTES_V7X - 8 * 2 ** 20)),
        name="moba_sample",
    )(page_table, q_bd, k_new, v_new, kt_pool, vt_pool)


def _pack_w_in_a(w):
    d = w.shape[0]
    qkv_gate = w[:, :4 * MIX_W]
    ab = w[:, 4 * MIX_W:4 * MIX_W + 2 * N_HEADS]
    q_mem = w[:, 4 * MIX_W + 2 * N_HEADS:]
    ab = jnp.pad(ab, ((0, 0), (0, LANES - 2 * N_HEADS)))
    return jnp.concatenate([qkv_gate, q_mem, ab], axis=1).astype(BF16)


def _pad_rows(a, rows):
    return jnp.pad(a, ((0, 0), (0, rows - a.shape[1]), (0, 0)))


def _trunk(p, x3, pos_base, mem_k, mem_v, conv0, s0, paged):
    b, l, d = x3.shape
    m = b * l
    x = x3.reshape(m, d)
    sample = paged is not None
    ffn = lambda x, tag, i, fg=None: _ffn(x, p[tag + "_norm"][i], p[tag + "_w1"][i], p[tag + "_w3"][i],
                                          p[tag + "_w2"][i], fg)
    lq = -(-l // SUBLANES) * SUBLANES

    def memory(q_mem, layer):
        q3 = _pad_rows(q_mem.reshape(b, l, MEM_W), lq)
        mo = _mem_attn(q3, mem_k[layer], mem_v[layer])
        return mo[:, :l].reshape(m, MEM_W)

    x = ffn(x, "ffn1", 0)
    qkv, gate, q_mem, ab = _norm_proj(x, p["mix_norm"][0], p["w_in_a"],
                                      (3 * MIX_W, MIX_W, MEM_W, LANES), "in_proj_a")
    qkv, new_conv = _conv(qkv.reshape(b, l, 3 * MIX_W), conv0, p["dn_conv_w"])
    lp = -(-l // DN_CHUNK) * DN_CHUNK
    o, s_fin = _gdn(_pad_rows(qkv, lp), _pad_rows(gate.reshape(b, l, MIX_W), lp),
                    _pad_rows(ab.reshape(b, l, LANES), lp),
                    p["dn_a_log"], p["dn_dt_bias"], p["dn_out_norm"], s0, l)
    o = o[:, :l].reshape(m, MIX_W)
    x = _out_proj(x, o, memory(q_mem, 0), p["w_out"][0], grouped=False)
    x = ffn(x, "ffn2", 0)

    if sample:
        tables = _rope_tables(m, l, pos_base)
    else:
        tables = _rope_tables(l, l, pos_base)
    kv = _kv_proj(x, p["kv_norm"], p["w_kv"], tables, l, for_prompt=not sample)
    x = ffn(x, "ffn1", 1)
    qh, q_mem = _q_proj(x, p["mix_norm"][1], p["w_in_b"], tables)
    if sample:
        page_table, kt_pool, vt_pool = paged
        k_new, v_new = kv
        k_out = k_new.reshape(b, l, N_KV_B, HEAD_DIM)
        v_out = v_new.reshape(b, l, N_KV_B, HEAD_DIM)
        eye = jnp.eye(N_KV_B, dtype=F32)
        q5 = qh.reshape(N_KV_B, b, l, KV_GROUP, HEAD_DIM).transpose(1, 0, 3, 2, 4)
        q_bd = (q5[:, :, :, :, None, :] * eye[None, :, None, None, :, None]).reshape(
            b, N_HEADS * l, N_KV_B * HEAD_DIM)
        o_bd = _moba_sample(q_bd, page_table, kt_pool, vt_pool,
                            _pad_rows(k_new.reshape(b, l, -1), lq),
                            _pad_rows(v_new.reshape(b, l, -1), lq), l)
        o6 = o_bd.reshape(b, N_KV_B, KV_GROUP, l, N_KV_B, HEAD_DIM)
        o = (o6 * eye[None, :, None, None, :, None]).sum(axis=4)
        o = o.transpose(0, 3, 1, 2, 4).reshape(m, MIX_W)
        x = _out_proj(x, o, memory(q_mem, 1), p["w_out"][1], grouped=False)
    else:
        k_t, v_t, kh, vt, kbar = kv
        to_rows = lambda a: a.reshape(b, N_KV_B, HEAD_DIM, l).transpose(0, 3, 1, 2)
        k_out, v_out = to_rows(k_t), to_rows(v_t)
        nb = l // MOBA_BLOCK
        kbar = kbar.reshape(b, nb, N_KV_B, HEAD_DIM).transpose(2, 0, 1, 3)
        o4 = _moba_prompt(qh, kh, vt, kbar, b, l)
        x = _out_proj(x, o4, memory(q_mem, 1), p["w_out"][1], grouped=True)
    y = ffn(x, "ffn2", 1, p["final_norm"])
    return y.reshape(b, l, d), new_conv[None], s_fin[None], k_out, v_out


def kernel(x_prompt, x_sample, mem_prompt, cache_mem_k, cache_mem_v, state_dn_conv, state_dn_S,
           cache_kv_k, cache_kv_v, page_table,
           ffn1_norm, ffn1_w1, ffn1_w3, ffn1_w2, mix_norm, w_in_a, w_in_b, w_out,
           dn_conv_w, dn_a_log, dn_dt_bias, dn_out_norm, kv_norm, w_kv, mem_norm, w_mem_kv,
           ffn2_norm, ffn2_w1, ffn2_w3, ffn2_w2, final_norm):
    assert w_in_a.shape[0] == 1 and w_in_b.shape[0] == 1 and ffn1_w1.shape[0] == 2
    bf = lambda a: a.astype(BF16)
    p = dict(ffn1_norm=ffn1_norm, ffn1_w1=bf(ffn1_w1), ffn1_w3=bf(ffn1_w3), ffn1_w2=bf(ffn1_w2),
             ffn2_norm=ffn2_norm, ffn2_w1=bf(ffn2_w1), ffn2_w3=bf(ffn2_w3), ffn2_w2=bf(ffn2_w2),
             mix_norm=mix_norm, w_in_a=_pack_w_in_a(w_in_a[0]), w_in_b=bf(w_in_b[0]), w_out=bf(w_out),
             dn_conv_w=dn_conv_w[0], dn_a_log=dn_a_log[0], dn_dt_bias=dn_dt_bias[0],
             dn_out_norm=dn_out_norm[0], kv_norm=kv_norm, w_kv=bf(w_kv), final_norm=final_norm)

    bp, lp, _ = x_prompt.shape
    assert lp % MOBA_BLOCK == 0
    n_mem = mem_prompt.shape[1]
    n_layers = mem_norm.shape[0]
    mem_t = [_mem_kv(mem_prompt, mem_norm[layer], bf(w_mem_kv[layer])) for layer in range(n_layers)]
    mk = [t[0] for t in mem_t]
    mv = [t[1] for t in mem_t]
    conv0 = jnp.zeros((bp, CONV_W - 1, 3 * MIX_W), F32)
    s0 = jnp.zeros((bp, N_HEADS, HEAD_DIM, HEAD_DIM), F32)
    y_p, conv_p, s_p, k_p, v_p = _trunk(p, x_prompt, 0, mk, mv, conv0, s0, None)
    token_major = lambda ts: jnp.stack(ts).reshape(n_layers, bp, N_MEM_HEADS, HEAD_DIM, n_mem).transpose(
        0, 1, 4, 2, 3)
    mem_k_p = token_major(mk)
    mem_v_p = token_major(mv)

    bs = x_sample.shape[0]
    n_pool, page = cache_kv_k.shape[:2]
    past_len = page_table.shape[1] * page
    assert past_len % MOBA_BLOCK == 0 and x_sample.shape[1] <= MOBA_BLOCK
    token_minor = lambda c: jnp.transpose(c, (0, 1, 3, 4, 2)).reshape(c.shape[0], bs, MEM_W, n_mem)
    cmk = token_minor(cache_mem_k)
    cmv = token_minor(cache_mem_v)
    transposed_pages = lambda c: jnp.transpose(c, (0, 2, 3, 1)).reshape(n_pool, -1, page)
    paged = (page_table, transposed_pages(cache_kv_k), transposed_pages(cache_kv_v))
    y_s, conv_s, s_s, k_s, v_s = _trunk(p, x_sample, past_len, cmk, cmv, state_dn_conv[0],
                                        state_dn_S[0], paged)
    return (y_p, y_s, conv_p, s_p, k_p, v_p, mem_k_p, mem_v_p, conv_s, s_s, k_s, v_s)
```

```python
import functools
import math

import jax
import jax.numpy as jnp
from jax import lax
from jax.experimental import pallas as pl
from jax.experimental.pallas import tpu as pltpu

F32 = jnp.float32
BF16 = jnp.bfloat16
HI = lax.Precision.HIGHEST

HEAD_DIM = 64
N_HEADS = 12
MIX_W = N_HEADS * HEAD_DIM
N_MEM_HEADS = 4
MEM_W = N_MEM_HEADS * HEAD_DIM
N_KV_B = 4
KV_GROUP = N_HEADS // N_KV_B
GROUP_W = KV_GROUP * HEAD_DIM
CONV_W = 4
DN_CHUNK = 64
MOBA_BLOCK = 256
MOBA_TOPK = 3
ROT_DIM = HEAD_DIM // 4
ROPE_THETA = 500000.0
EPS = 1e-6
QK_SCALE = HEAD_DIM ** -0.5

LANES = 128
SUBLANES = 8
VMEM_BYTES_V7X = 64 * 2 ** 20
NEG_INF = float("-inf")

NT = (((1,), (1,)), ((), ()))
TN = (((0,), (0,)), ((), ()))


def _params(semantics, est_bytes):
    limit = min(max(int(est_bytes * 1.5), 16 * 2 ** 20), VMEM_BYTES_V7X - 8 * 2 ** 20)
    return pltpu.CompilerParams(dimension_semantics=semantics, vmem_limit_bytes=limit)


def _row_tile(m, pref):
    t = min(m, pref)
    while m % t or t % SUBLANES:
        t -= 1
    return t


def _rms(x, g):
    return x * lax.rsqrt(jnp.mean(x * x, axis=-1, keepdims=True) + EPS) * g


def _sigmoid(x):
    return 1.0 / (1.0 + jnp.exp(-x))


def _softplus(x):
    return jnp.maximum(x, 0.0) + jnp.log(1.0 + jnp.exp(-jnp.abs(x)))


def _bdot(a, b):
    return jnp.dot(a.astype(BF16), b.astype(BF16), preferred_element_type=F32)


def _bdot_nt(a, b):
    return lax.dot_general(a.astype(BF16), b.astype(BF16), NT, preferred_element_type=F32)


def _ffn_kernel(*refs, n_f, final):
    if final:
        x_ref, g_ref, w1_ref, w3_ref, w2_ref, fg_ref, o_ref, h_scr, acc_scr = refs
    else:
        x_ref, g_ref, w1_ref, w3_ref, w2_ref, o_ref, h_scr, acc_scr = refs
    f = pl.program_id(1)

    @pl.when(f == 0)
    def _():
        h_scr[...] = _rms(x_ref[...], g_ref[...]).astype(BF16)
        acc_scr[...] = jnp.zeros_like(acc_scr)

    h = h_scr[...]
    a = jnp.dot(h, w1_ref[...], preferred_element_type=F32)
    b = jnp.dot(h, w3_ref[...], preferred_element_type=F32)
    u = (a * _sigmoid(a)) * b
    acc_scr[...] += jnp.dot(u.astype(BF16), w2_ref[...], preferred_element_type=F32)

    @pl.when(f == n_f - 1)
    def _():
        y = x_ref[...] + 0.5 * acc_scr[...]
        if final:
            y = _rms(y, fg_ref[...])
        o_ref[...] = y


def _ffn(x, g, w1, w3, w2, final_gain=None):
    m, d = x.shape
    d_ff = w1.shape[1]
    tm = _row_tile(m, 512)
    n_f = 2 if d_ff % (2 * LANES) == 0 else 1
    tf = d_ff // n_f
    final = final_gain is not None
    in_specs = [
        pl.BlockSpec((tm, d), lambda i, f: (i, 0)),
        pl.BlockSpec((1, d), lambda i, f: (0, 0)),
        pl.BlockSpec((d, tf), lambda i, f: (0, f)),
        pl.BlockSpec((d, tf), lambda i, f: (0, f)),
        pl.BlockSpec((tf, d), lambda i, f: (f, 0)),
    ]
    args = [x, g.reshape(1, d), w1, w3, w2]
    if final:
        in_specs.append(pl.BlockSpec((1, d), lambda i, f: (0, 0)))
        args.append(final_gain.reshape(1, d))
    est = 4 * tm * d * 4 + 6 * d * tf * 2 + tm * d * 6 + 3 * tm * tf * 4
    return pl.pallas_call(
        functools.partial(_ffn_kernel, n_f=n_f, final=final),
        grid=(m // tm, n_f),
        in_specs=in_specs,
        out_specs=pl.BlockSpec((tm, d), lambda i, f: (i, 0)),
        out_shape=jax.ShapeDtypeStruct((m, d), F32),
        scratch_shapes=[pltpu.VMEM((tm, d), BF16), pltpu.VMEM((tm, d), F32)],
        compiler_params=_params(("parallel", "arbitrary"), est),
        name="ffn_final" if final else "ffn",
    )(*args)


def _norm_proj_kernel(x_ref, g_ref, w_ref, *out_refs, splits):
    h = _rms(x_ref[...], g_ref[...]).astype(BF16)
    for o_ref, (a, b) in zip(out_refs, splits):
        o_ref[...] = jnp.dot(h, w_ref[:, a:b], preferred_element_type=F32)


def _norm_proj(x, g, w, widths, name):
    m, d = x.shape
    n = w.shape[1]
    assert sum(widths) == n and all(c % LANES == 0 for c in widths)
    tm = _row_tile(m, 256)
    splits, a = [], 0
    for c in widths:
        splits.append((a, a + c))
        a += c
    est = 2 * tm * d * 4 + 2 * d * n * 2 + 3 * tm * n * 4
    return pl.pallas_call(
        functools.partial(_norm_proj_kernel, splits=tuple(splits)),
        grid=(m // tm,),
        in_specs=[
            pl.BlockSpec((tm, d), lambda i: (i, 0)),
            pl.BlockSpec((1, d), lambda i: (0, 0)),
            pl.BlockSpec((d, n), lambda i: (0, 0)),
        ],
        out_specs=[pl.BlockSpec((tm, c), lambda i: (i, 0)) for c in widths],
        out_shape=[jax.ShapeDtypeStruct((m, c), F32) for c in widths],
        compiler_params=_params(("parallel",), est),
        name=name,
    )(x, g.reshape(1, d), w)


def _rope_table_kernel(inv_ref, c_ref, s1_ref, s2_ref, *, period, base, tl):
    i = pl.program_id(0)
    row = lax.broadcasted_iota(jnp.int32, (tl, LANES), 0) + i * tl
    pos = base + lax.rem(row, period)
    ang = pos.astype(F32) * inv_ref[...]
    in_head = lax.broadcasted_iota(jnp.int32, (tl, LANES), 1) % HEAD_DIM
    half = ROT_DIM // 2
    c = jnp.cos(ang)
    s = jnp.sin(ang)
    c_ref[...] = jnp.where(in_head < ROT_DIM, c, 1.0)
    s1_ref[...] = jnp.where(in_head < half, -s, 0.0)
    s2_ref[...] = jnp.where((in_head >= half) & (in_head < ROT_DIM), s, 0.0)


def _rope_tables(n_rows, period, base):
    half = ROT_DIM // 2
    inv = ROPE_THETA ** (-jnp.arange(half, dtype=F32) * 2.0 / ROT_DIM)
    in_head = jnp.arange(LANES) % HEAD_DIM
    inv_lane = jnp.where(in_head < ROT_DIM, inv[in_head % half], 0.0).astype(F32).reshape(1, LANES)
    tl = _row_tile(n_rows, 1024)
    shp = jax.ShapeDtypeStruct((n_rows, LANES), F32)
    return pl.pallas_call(
        functools.partial(_rope_table_kernel, period=period, base=base, tl=tl),
        grid=(n_rows // tl,),
        in_specs=[pl.BlockSpec((1, LANES), lambda i: (0, 0))],
        out_specs=[pl.BlockSpec((tl, LANES), lambda i: (i, 0))] * 3,
        out_shape=[shp] * 3,
        compiler_params=_params(("parallel",), 16 * tl * LANES * 4),
        name="rope_tables",
    )(inv_lane)


def _apply_rope(t, c, s1, s2):
    w = t.shape[1]
    reps = w // LANES
    half = ROT_DIM // 2
    tile = lambda a: jnp.concatenate([a] * reps, axis=1) if reps > 1 else a
    up = pltpu.roll(t, w - half, 1)
    down = pltpu.roll(t, half, 1)
    return t * tile(c) + up * tile(s1) + down * tile(s2)


def _kv_proj_kernel(x_ref, g_ref, w_ref, c_ref, s1_ref, s2_ref, *out_refs, for_prompt):
    h = _rms(x_ref[...], g_ref[...]).astype(BF16)
    kv_w = N_KV_B * HEAD_DIM
    k = jnp.dot(h, w_ref[:, :kv_w], preferred_element_type=F32)
    v = jnp.dot(h, w_ref[:, kv_w:], preferred_element_type=F32)
    k = _apply_rope(k, c_ref[...], s1_ref[...], s2_ref[...])
    if for_prompt:
        kt_ref, vt_ref, kh_ref, vtb_ref, kbar_ref, v_scr = out_refs
        v_scr[...] = v
        v_t = v_scr[...].T
        kt_ref[0] = k.T
        vt_ref[0] = v_t
        vtb_ref[0] = v_t.astype(BF16)
        for hk in range(N_KV_B):
            kh_ref[hk] = k[:, hk * HEAD_DIM:(hk + 1) * HEAD_DIM].astype(BF16)
        kbar_ref[0] = jnp.sum(k, axis=0, keepdims=True) * (1.0 / MOBA_BLOCK)
    else:
        out_refs[0][...] = k
        out_refs[1][...] = v


def _kv_proj(x, g, w, tables, seq_len, for_prompt):
    m, d = x.shape
    kv_w = N_KV_B * HEAD_DIM
    tm = MOBA_BLOCK if for_prompt else _row_tile(m, 256)
    n_tab = tables[0].shape[0] // tm
    row = lambda i: (i, 0)
    tab = lambda i: (i % n_tab, 0)
    if for_prompt:
        nb = seq_len // tm
        t_spec = pl.BlockSpec((1, kv_w, tm), lambda i: (i // nb, 0, i % nb))
        t_shape = jax.ShapeDtypeStruct((m // seq_len, kv_w, seq_len), F32)
        out_specs = [t_spec, t_spec,
                     pl.BlockSpec((N_KV_B, tm, HEAD_DIM), lambda i: (0, i, 0)),
                     pl.BlockSpec((1, kv_w, tm), lambda i: (i, 0, 0)),
                     pl.BlockSpec((1, 1, kv_w), lambda i: (i, 0, 0))]
        out_shape = [t_shape, t_shape,
                     jax.ShapeDtypeStruct((N_KV_B, m, HEAD_DIM), BF16),
                     jax.ShapeDtypeStruct((m // tm, kv_w, tm), BF16),
                     jax.ShapeDtypeStruct((m // tm, 1, kv_w), F32)]
    else:
        out_specs = [pl.BlockSpec((tm, kv_w), row), pl.BlockSpec((tm, kv_w), row)]
        out_shape = [jax.ShapeDtypeStruct((m, kv_w), F32)] * 2
    est = 2 * tm * d * 4 + 2 * d * 2 * kv_w * 2 + 12 * tm * kv_w * 4
    return pl.pallas_call(
        functools.partial(_kv_proj_kernel, for_prompt=for_prompt),
        grid=(m // tm,),
        in_specs=[
            pl.BlockSpec((tm, d), row),
            pl.BlockSpec((1, d), lambda i: (0, 0)),
            pl.BlockSpec((d, 2 * kv_w), lambda i: (0, 0)),
            pl.BlockSpec((tm, LANES), tab),
            pl.BlockSpec((tm, LANES), tab),
            pl.BlockSpec((tm, LANES), tab),
        ],
        out_specs=out_specs,
        out_shape=out_shape,
        scratch_shapes=[pltpu.VMEM((tm, kv_w), F32)] if for_prompt else [],
        compiler_params=_params(("parallel",), est),
        name="kv_proj",
    )(x, g.reshape(1, d), w, *tables)


def _q_proj_kernel(x_ref, g_ref, w_ref, c_ref, s1_ref, s2_ref, qh_ref, qm_ref):
    h = _rms(x_ref[...], g_ref[...]).astype(BF16)
    q = jnp.dot(h, w_ref[:, :MIX_W], preferred_element_type=F32)
    qm_ref[...] = jnp.dot(h, w_ref[:, MIX_W:], preferred_element_type=F32)
    q = _apply_rope(q, c_ref[...], s1_ref[...], s2_ref[...]) * QK_SCALE
    for hk in range(N_KV_B):
        qh_ref[hk] = q[:, hk * GROUP_W:(hk + 1) * GROUP_W]


def _q_proj(x, g, w, tables):
    m, d = x.shape
    n = w.shape[1]
    tm = _row_tile(m, 256)
    n_tab = tables[0].shape[0] // tm
    row = lambda i: (i, 0)
    tab = lambda i: (i % n_tab, 0)
    est = 2 * tm * d * 4 + 2 * d * n * 2 + 16 * tm * MIX_W * 4
    return pl.pallas_call(
        _q_proj_kernel,
        grid=(m // tm,),
        in_specs=[
            pl.BlockSpec((tm, d), row),
            pl.BlockSpec((1, d), lambda i: (0, 0)),
            pl.BlockSpec((d, n), lambda i: (0, 0)),
            pl.BlockSpec((tm, LANES), tab),
            pl.BlockSpec((tm, LANES), tab),
            pl.BlockSpec((tm, LANES), tab),
        ],
        out_specs=[
            pl.BlockSpec((N_KV_B, tm, GROUP_W), lambda i: (0, i, 0)),
            pl.BlockSpec((tm, MEM_W), row),
        ],
        out_shape=[
            jax.ShapeDtypeStruct((N_KV_B, m, GROUP_W), F32),
            jax.ShapeDtypeStruct((m, MEM_W), F32),
        ],
        compiler_params=_params(("parallel",), est),
        name="q_proj",
    )(x, g.reshape(1, d), w, *tables)


def _conv_kernel(u_ref, prev_ref, buf_ref, w_ref, y_ref, nb_ref, pad_scr, *, tl, n_t):
    i = pl.program_id(1)
    halo = CONV_W - 1
    lo = SUBLANES - halo
    pad_scr[SUBLANES:SUBLANES + tl, :] = u_ref[0]

    @pl.when(i == 0)
    def _():
        pad_scr[lo:SUBLANES, :] = buf_ref[0]

    @pl.when(i > 0)
    def _():
        p = prev_ref[0]
        pad_scr[lo:SUBLANES, :] = p[p.shape[0] - halo:, :]

    y = pad_scr[lo:lo + tl, :] * w_ref[0:1, :]
    for j in range(1, CONV_W):
        y = y + pad_scr[lo + j:lo + j + tl, :] * w_ref[j:j + 1, :]
    y_ref[0] = y * _sigmoid(y)

    @pl.when(i == n_t - 1)
    def _():
        nb_ref[0] = pad_scr[lo + tl:SUBLANES + tl, :]


def _conv(u, buf, w):
    b, l, c = u.shape
    tl = _row_tile(l, 512) if l % SUBLANES == 0 else l
    n_t = l // tl
    pr = min(SUBLANES, l)
    per = tl // pr
    est = 6 * tl * c * 4
    return pl.pallas_call(
        functools.partial(_conv_kernel, tl=tl, n_t=n_t),
        grid=(b, n_t),
        in_specs=[
            pl.BlockSpec((1, tl, c), lambda bi, i: (bi, i, 0)),
            pl.BlockSpec((1, pr, c), lambda bi, i: (bi, jnp.maximum(i * per - 1, 0), 0)),
            pl.BlockSpec((1, CONV_W - 1, c), lambda bi, i: (bi, 0, 0)),
            pl.BlockSpec((CONV_W, c), lambda bi, i: (0, 0)),
        ],
        out_specs=[
            pl.BlockSpec((1, tl, c), lambda bi, i: (bi, i, 0)),
            pl.BlockSpec((1, CONV_W - 1, c), lambda bi, i: (bi, 0, 0)),
        ],
        out_shape=[
            jax.ShapeDtypeStruct((b, l, c), F32),
            jax.ShapeDtypeStruct((b, CONV_W - 1, c), F32),
        ],
        scratch_shapes=[pltpu.VMEM((tl + SUBLANES, c), F32)],
        compiler_params=_params(("parallel", "arbitrary"), est),
        name="short_conv",
    )(u, u, buf, w)


HEADS_PER_SLAB = LANES * 2 // HEAD_DIM
SLAB_W = HEADS_PER_SLAB * HEAD_DIM
N_SLABS = N_HEADS // HEADS_PER_SLAB


def _split2(x):
    hi = x.astype(BF16)
    return hi, (x - hi.astype(F32)).astype(BF16)


def _split3(x):
    hi = x.astype(BF16)
    r = x - hi.astype(F32)
    mid = r.astype(BF16)
    return hi, mid, (r - mid.astype(F32)).astype(BF16)


def _dot_right01(x, sel):
    hi, mid, lo = _split3(x)
    d = lambda a: jnp.dot(a, sel, preferred_element_type=F32)
    return d(hi) + (d(mid) + d(lo))


def _dot_left01(sel, x):
    hi, mid, lo = _split3(x)
    d = lambda a: jnp.dot(sel, a, preferred_element_type=F32)
    return d(hi) + (d(mid) + d(lo))


def _dot3(a, b_hi, b_lo):
    a_hi, a_lo = _split2(a)
    d = lambda x, y: jnp.dot(x, y, preferred_element_type=F32)
    return d(a_hi, b_hi) + (d(a_hi, b_lo) + d(a_lo, b_hi))


def _block_diag(x, mask):
    return jnp.concatenate([x] * HEADS_PER_SLAB, axis=0) * mask


def _block_diag_pieces(x, mask):
    hi, lo = _split2(x)
    return _block_diag(hi, mask), _block_diag(lo, mask)


def _unit_lower_inverses(ms, eye_t, mask):
    c = ms[0].shape[0]
    xs = [eye_t - m for m in ms]
    ps = [_dot3(m, *_block_diag_pieces(m, mask)) for m in ms]
    power = 2
    while 2 * power < c:
        xps = [_dot3(jnp.concatenate([x, p], axis=0), *_block_diag_pieces(p, mask))
               for x, p in zip(xs, ps)]
        xs = [x + xp[:c] for x, xp in zip(xs, xps)]
        ps = [xp[c:] for xp in xps]
        power *= 2
    return [x + _dot3(x, *_block_diag_pieces(p, mask)) for x, p in zip(xs, ps)]


def _gdn_kernel(q_ref, k_ref, v_ref, gate_ref, ab_ref, alog_ref, dt_ref, gain_ref, s0_ref,
                o_ref, sfin_ref,
                s_scr, qn_scr, kn_scr, gcb_scr, bb_scr, u_scr, w_scr, a_scr, qg_scr, kd_scr, egl_scr,
                *, t_rows, n_t, l_valid):
    t = pl.program_id(1)
    c = DN_CHUNK
    d = HEAD_DIM
    n_chunks = t_rows // c

    r_s = lax.broadcasted_iota(jnp.int32, (SLAB_W, SLAB_W), 0)
    c_s = lax.broadcasted_iota(jnp.int32, (SLAB_W, SLAB_W), 1)
    same_head = (r_s // d) == (c_s // d)
    bd_f32 = jnp.where(same_head, 1.0, 0.0).astype(F32)
    bd_mask = bd_f32.astype(BF16)
    r_t = lax.broadcasted_iota(jnp.int32, (c, SLAB_W), 0)
    c_t = lax.broadcasted_iota(jnp.int32, (c, SLAB_W), 1) % d
    lower_t = r_t >= c_t
    strict_t = r_t > c_t
    eye_t = jnp.where(r_t == c_t, 1.0, 0.0).astype(F32)
    r_c = lax.broadcasted_iota(jnp.int32, (c, c), 0)
    c_c = lax.broadcasted_iota(jnp.int32, (c, c), 1)
    tril = jnp.where(r_c >= c_c, 1.0, 0.0).astype(BF16)
    er = lax.broadcasted_iota(jnp.int32, (LANES, MIX_W), 0)
    ec = lax.broadcasted_iota(jnp.int32, (LANES, MIX_W), 1) // d
    sel_g = jnp.where(er == ec, 1.0, 0.0).astype(BF16)
    sel_b = jnp.where(er == ec + N_HEADS, 1.0, 0.0).astype(BF16)

    @pl.when(t == 0)
    def _():
        for s in range(N_SLABS):
            rows = jnp.concatenate([s0_ref[0, s * HEADS_PER_SLAB + h] for h in range(HEADS_PER_SLAB)],
                                   axis=0)
            s_scr[s] = jnp.concatenate([rows] * HEADS_PER_SLAB, axis=1) * bd_f32

    ab = ab_ref[0]
    row = lax.broadcasted_iota(jnp.int32, ab.shape, 0) + t * t_rows
    valid = row < l_valid
    g = jnp.where(valid, -jnp.exp(alog_ref[...]) * _softplus(ab + dt_ref[...]), 0.0)
    beta = jnp.where(valid, _sigmoid(ab), 0.0)
    gc = jnp.concatenate([_dot_left01(tril, g[i * c:(i + 1) * c]) for i in range(n_chunks)], axis=0)
    gcb_scr[...] = _dot_right01(gc, sel_g)
    bb_scr[...] = _dot_right01(beta, sel_b)
    for s in range(N_SLABS):
        cs = slice(s * SLAB_W, (s + 1) * SLAB_W)
        for src, dst, scale in ((q_ref, qn_scr, QK_SCALE), (k_ref, kn_scr, 1.0)):
            x = src[0, :, cs]
            hi, lo = _split2(x * x)
            ssq = (jnp.dot(hi, bd_mask, preferred_element_type=F32)
                   + jnp.dot(lo, bd_mask, preferred_element_type=F32))
            dst[:, cs] = x * (lax.rsqrt(ssq + EPS) * scale)

    per_iter = 2 if n_chunks % 2 == 0 else 1
    slabs = [slice(s * SLAB_W, (s + 1) * SLAB_W) for s in range(N_SLABS)]

    def factors(it, carry):
        items = []
        for u in range(per_iter):
            ci = it * per_iter + u
            rows = pl.ds(pl.multiple_of(ci * c, c), c)
            items += [(ci, rows, cs) for cs in slabs]
        pre = []
        for ci, rows, cs in items:
            gcb = gcb_scr[rows, cs]
            kn = kn_scr[rows, cs]
            qn = qn_scr[rows, cs]
            b_ = bb_scr[rows, cs]
            kb = kn * b_
            g_row = jnp.sum(gcb * eye_t, axis=0, keepdims=True)
            dec = jnp.exp(jnp.where(lower_t, gcb - g_row, NEG_INF))
            both = lax.dot_general(jnp.concatenate([kb, qn], axis=0).astype(BF16),
                                   _block_diag(kn.astype(BF16), bd_mask), NT,
                                   preferred_element_type=F32)
            pre.append((gcb, kn, qn, kb, dec, both))
        ms = []
        for (ci, rows, cs), (gcb, kn, qn, kb, dec, both) in zip(items, pre):
            ms.append(jnp.where(strict_t, both[:c] * dec, 0.0))
            a_scr[rows, cs] = (both[c:] * dec).astype(BF16)
        tinvs = _unit_lower_inverses(ms, eye_t, bd_mask)
        sols = []
        for (ci, rows, cs), (gcb, kn, qn, kb, dec, both), tinv in zip(items, pre, tinvs):
            eg = jnp.exp(gcb)
            v_hi, v_lo = _block_diag_pieces(v_ref[0, rows, cs] * bb_scr[rows, cs], bd_mask)
            k_hi, k_lo = _block_diag_pieces(kb * eg, bd_mask)
            sols.append(_dot3(tinv, jnp.concatenate([v_hi, k_hi], axis=1),
                              jnp.concatenate([v_lo, k_lo], axis=1)))
            glast = gcb[c - 1:c, :]
            qg_scr[rows, cs] = (qn * eg).astype(BF16)
            kd_scr[rows, cs] = (kn * jnp.exp(glast - gcb)).astype(BF16)
            egl_scr[pl.ds(ci, 1), cs] = jnp.exp(glast)
        for (ci, rows, cs), sol in zip(items, sols):
            u_scr[rows, cs] = sol[:, :SLAB_W]
            w_scr[rows, cs] = sol[:, SLAB_W:].astype(BF16)
        return carry

    lax.fori_loop(0, n_chunks // per_iter, factors, 0)

    gain = gain_ref[...]

    def recur(ci, carry):
        rows = pl.ds(pl.multiple_of(ci * c, c), c)
        olds = [s_scr[s] for s in range(N_SLABS)]
        rs = [jnp.dot(jnp.concatenate([w_scr[rows, cs], qg_scr[rows, cs]], axis=0),
                      s_old.astype(BF16), preferred_element_type=F32)
              for cs, s_old in zip(slabs, olds)]
        v_news = [(u_scr[rows, cs] - r[:c]).astype(BF16) for cs, r in zip(slabs, rs)]
        upds = [lax.dot_general(kd_scr[rows, cs], v_new, TN, preferred_element_type=F32)
                for cs, v_new in zip(slabs, v_news)]
        os_ = [r[c:] + jnp.dot(a_scr[rows, cs], _block_diag(v_new, bd_mask),
                               preferred_element_type=F32)
               for cs, r, v_new in zip(slabs, rs, v_news)]
        for s, (cs, s_old, upd) in enumerate(zip(slabs, olds, upds)):
            s_scr[s] = (s_old * egl_scr[pl.ds(ci, 1), cs] + upd) * bd_f32
        ssqs = []
        for o in os_:
            hi, lo = _split2(o * o)
            ssqs.append(jnp.dot(hi, bd_mask, preferred_element_type=F32)
                        + jnp.dot(lo, bd_mask, preferred_element_type=F32))
        for cs, o, ssq in zip(slabs, os_, ssqs):
            gt = gate_ref[0, rows, cs]
            o_ref[0, rows, cs] = (o * lax.rsqrt(ssq * (1.0 / d) + EPS) * gain[:, cs]
                                  * (gt * _sigmoid(gt)))
        return carry

    lax.fori_loop(0, n_chunks, recur, 0)

    @pl.when(t == n_t - 1)
    def _():
        for s in range(N_SLABS):
            full = s_scr[s]
            for h in range(HEADS_PER_SLAB):
                sfin_ref[0, s * HEADS_PER_SLAB + h] = full[h * d:(h + 1) * d, h * d:(h + 1) * d]


def _gdn(qkv, gate, ab, a_log, dt_bias, out_gain, s0, l_valid):
    b, lp, _ = qkv.shape
    assert lp % DN_CHUNK == 0
    t_rows = DN_CHUNK * math.gcd(lp // DN_CHUNK, 8)
    n_t = lp // t_rows
    n_chunks = t_rows // DN_CHUNK
    pad = lambda a: jnp.pad(a.astype(F32), (0, LANES - N_HEADS)).reshape(1, LANES)
    blk = lambda j: pl.BlockSpec((1, t_rows, MIX_W), lambda bi, t, j=j: (bi, t, j))
    state = pl.BlockSpec((1, N_HEADS, HEAD_DIM, HEAD_DIM), lambda bi, t: (bi, 0, 0, 0))
    vec = pl.BlockSpec((1, LANES), lambda bi, t: (0, 0))
    wide_f32 = pltpu.VMEM((t_rows, MIX_W), F32)
    wide_bf16 = pltpu.VMEM((t_rows, MIX_W), BF16)
    est = 10 * t_rows * MIX_W * 4 + 5 * t_rows * MIX_W * 4 + 4 * t_rows * MIX_W * 2 + 64 * SLAB_W * SLAB_W * 4
    return pl.pallas_call(
        functools.partial(_gdn_kernel, t_rows=t_rows, n_t=n_t, l_valid=l_valid),
        grid=(b, n_t),
        in_specs=[
            blk(0), blk(1), blk(2),
            pl.BlockSpec((1, t_rows, MIX_W), lambda bi, t: (bi, t, 0)),
            pl.BlockSpec((1, t_rows, LANES), lambda bi, t: (bi, t, 0)),
            vec, vec,
            pl.BlockSpec((1, MIX_W), lambda bi, t: (0, 0)),
            state,
        ],
        out_specs=[pl.BlockSpec((1, t_rows, MIX_W), lambda bi, t: (bi, t, 0)), state],
        out_shape=[
            jax.ShapeDtypeStruct((b, lp, MIX_W), F32),
            jax.ShapeDtypeStruct((b, N_HEADS, HEAD_DIM, HEAD_DIM), F32),
        ],
        scratch_shapes=[
            pltpu.VMEM((N_SLABS, SLAB_W, SLAB_W), F32),
            wide_f32, wide_f32,
            wide_f32, wide_f32,
            wide_f32,
            wide_bf16, wide_bf16, wide_bf16, wide_bf16,
            pltpu.VMEM((max(n_chunks, SUBLANES), MIX_W), F32),
        ],
        compiler_params=_params(("parallel", "arbitrary"), est),
        name="gated_delta_rule",
    )(qkv, qkv, qkv, gate, ab, pad(a_log), pad(dt_bias),
      jnp.tile(out_gain.astype(F32), N_HEADS).reshape(1, MIX_W), s0)


def _mem_kv_kernel(x_ref, g_ref, w_ref, kt_ref, vt_ref, kv_scr):
    h = _rms(x_ref[...], g_ref[...]).astype(BF16)
    kv_scr[...] = jnp.dot(h, w_ref[...], preferred_element_type=F32)
    kt_ref[0] = kv_scr[:, :MEM_W].T
    vt_ref[0] = kv_scr[:, MEM_W:].T


def _mem_kv(mem, g, w):
    b, n_mem, d = mem.shape
    shp = jax.ShapeDtypeStruct((b, MEM_W, n_mem), F32)
    spec = pl.BlockSpec((1, MEM_W, n_mem), lambda i: (i, 0, 0))
    return pl.pallas_call(
        _mem_kv_kernel,
        grid=(b,),
        in_specs=[
            pl.BlockSpec((n_mem, d), lambda i: (i, 0)),
            pl.BlockSpec((1, d), lambda i: (0, 0)),
            pl.BlockSpec((d, 2 * MEM_W), lambda i: (0, 0)),
        ],
        out_specs=[spec, spec],
        out_shape=[shp, shp],
        scratch_shapes=[pltpu.VMEM((n_mem, 2 * MEM_W), F32)],
        compiler_params=_params(("parallel",), 2 * n_mem * d * 4 + 2 * d * 2 * MEM_W * 2 + 8 * n_mem * MEM_W * 4),
        name="mem_kv",
    )(mem.reshape(b * n_mem, d), g.reshape(1, d), w)


def _mem_attn_kernel(q_ref, kt_ref, vt_ref, o_ref):
    q = q_ref[0]
    kt = kt_ref[0]
    vt = vt_ref[0]
    outs = []
    for h in range(N_MEM_HEADS):
        sl = slice(h * HEAD_DIM, (h + 1) * HEAD_DIM)
        s = _bdot(q[:, sl], kt[sl, :]) * QK_SCALE
        p = jnp.exp(s - jnp.max(s, axis=-1, keepdims=True))
        outs.append(_bdot_nt(p, vt[sl, :]) / jnp.sum(p, axis=-1, keepdims=True))
    o_ref[0] = jnp.concatenate(outs, axis=1)


def _mem_attn(q, mk_t, mv_t):
    b, l, w = q.shape
    n_mem = mk_t.shape[2]
    tl = _row_tile(l, 512)
    est = 4 * tl * w * 4 + 4 * n_mem * w * 4 + 8 * tl * n_mem * 4
    return pl.pallas_call(
        _mem_attn_kernel,
        grid=(b, l // tl),
        in_specs=[
            pl.BlockSpec((1, tl, w), lambda bi, i: (bi, i, 0)),
            pl.BlockSpec((1, w, n_mem), lambda bi, i: (bi, 0, 0)),
            pl.BlockSpec((1, w, n_mem), lambda bi, i: (bi, 0, 0)),
        ],
        out_specs=pl.BlockSpec((1, tl, w), lambda bi, i: (bi, i, 0)),
        out_shape=jax.ShapeDtypeStruct((b, l, w), F32),
        compiler_params=_params(("parallel", "parallel"), est),
        name="mem_attn",
    )(q, mk_t, mv_t)


def _out_proj_kernel(x_ref, o_ref, mo_ref, wo_ref, wm_ref, y_ref, *, grouped):
    y = x_ref[...] + _bdot(mo_ref[...], wm_ref[...])
    if grouped:
        for hk in range(N_KV_B):
            y = y + _bdot(o_ref[hk], wo_ref[hk])
    else:
        y = y + _bdot(o_ref[...], wo_ref[...])
    y_ref[...] = y


def _out_proj(x, o, mo, w_out, grouped):
    m, d = x.shape
    tm = _row_tile(m, 512)
    row = lambda i: (i, 0)
    if grouped:
        wo = w_out[:MIX_W].reshape(N_KV_B, GROUP_W, d)
        o_spec = pl.BlockSpec((N_KV_B, tm, GROUP_W), lambda i: (0, i, 0))
        wo_spec = pl.BlockSpec((N_KV_B, GROUP_W, d), lambda i: (0, 0, 0))
    else:
        wo = w_out[:MIX_W]
        o_spec = pl.BlockSpec((tm, MIX_W), row)
        wo_spec = pl.BlockSpec((MIX_W, d), lambda i: (0, 0))
    est = 6 * tm * d * 4 + 4 * tm * d * 4 + 4 * d * d * 2
    return pl.pallas_call(
        functools.partial(_out_proj_kernel, grouped=grouped),
        grid=(m // tm,),
        in_specs=[
            pl.BlockSpec((tm, d), row),
            o_spec,
            pl.BlockSpec((tm, MEM_W), row),
            wo_spec,
            pl.BlockSpec((MEM_W, d), lambda i: (0, 0)),
        ],
        out_specs=pl.BlockSpec((tm, d), row),
        out_shape=jax.ShapeDtypeStruct((m, d), F32),
        compiler_params=_params(("parallel",), est),
        name="out_proj",
    )(x, o, mo, wo, w_out[MIX_W:])


def _select_blocks(gate, n_past, k_sel):
    lane = lax.broadcasted_iota(jnp.int32, gate.shape, 1)
    lane_f = lane.astype(F32)
    gate = jnp.where(lane < n_past, gate, NEG_INF)
    sel = jnp.zeros(gate.shape, F32)
    for _ in range(k_sel):
        mx = jnp.max(gate, axis=1, keepdims=True)
        first = jnp.min(jnp.where(gate == mx, lane_f, float(gate.shape[1])), axis=1, keepdims=True)
        pick = (lane_f == first) & (mx > NEG_INF)
        sel = jnp.where(pick, 1.0, sel)
        gate = jnp.where(pick, NEG_INF, gate)
    return sel


def _select_blocks_t(gate, n_past, k_sel):
    blk = lax.broadcasted_iota(jnp.int32, gate.shape, 0)
    blk_f = blk.astype(F32)
    gate = jnp.where(blk < n_past, gate, NEG_INF)
    sel = jnp.zeros(gate.shape, F32)
    for _ in range(k_sel):
        mx = jnp.max(gate, axis=0, keepdims=True)
        first = jnp.min(jnp.where(gate == mx, blk_f, float(gate.shape[0])), axis=0, keepdims=True)
        pick = (blk_f == first) & (mx > NEG_INF)
        sel = jnp.where(pick, 1.0, sel)
        gate = jnp.where(pick, NEG_INF, gate)
    return sel


def _moba_prompt_kernel(q_ref, k_ref, vt_ref, kbar_ref, o_ref, q_scr, m_scr, l_scr, acc_scr, sel_scr,
                        s_scr, p_scr, alpha_scr, *, k_sel):
    i = pl.program_id(2)
    blk = MOBA_BLOCK
    d = HEAD_DIM
    n_chunks = KV_GROUP * blk // LANES
    q = q_ref[0]
    q3 = jnp.concatenate([q[:, g * d:(g + 1) * d] for g in range(KV_GROUP)], axis=0)
    gate_t = lax.dot_general(kbar_ref[0, 0], q3, NT, precision=HI, preferred_element_type=F32)
    sel_scr[...] = _select_blocks_t(gate_t, i, k_sel)
    q_scr[...] = q3.astype(BF16)

    def block_scores(k_blk, c):
        return lax.dot_general(k_blk, q_scr[c * LANES:(c + 1) * LANES, :], NT,
                               preferred_element_type=F32)

    k_own = k_ref[0, pl.ds(pl.multiple_of(i * blk, blk), blk), :]
    vt_own = vt_ref[i, :, :]
    scores = [block_scores(k_own, c) for c in range(n_chunks)]
    probs = []
    for c in range(n_chunks):
        cs = slice(c * LANES, (c + 1) * LANES)
        kpos = lax.broadcasted_iota(jnp.int32, (blk, LANES), 0)
        qpos = (lax.broadcasted_iota(jnp.int32, (blk, LANES), 1) + c * LANES) % blk
        s = jnp.where(kpos <= qpos, scores[c], NEG_INF)
        m0 = jnp.max(s, axis=0, keepdims=True)
        p = jnp.exp(s - m0)
        m_scr[:, cs] = m0
        l_scr[:, cs] = jnp.sum(p, axis=0, keepdims=True)
        probs.append(p.astype(BF16))
    for c in range(n_chunks):
        acc_scr[:, c * LANES:(c + 1) * LANES] = jnp.dot(vt_own, probs[c], preferred_element_type=F32)

    chunks = [slice(c * LANES, (c + 1) * LANES) for c in range(n_chunks)]
    k_first = k_ref[0, 0:blk, :]
    for c, cs in enumerate(chunks):
        s_scr[0, :, cs] = block_scores(k_first, c)
    p_scr[1] = jnp.zeros(p_scr.shape[1:], BF16)
    alpha_scr[...] = jnp.ones(alpha_scr.shape, F32)

    def pv_update(j_prev, slot):
        vt_prev = vt_ref[j_prev, :, :]
        pv = [jnp.dot(vt_prev, p_scr[slot, :, cs], preferred_element_type=F32) for cs in chunks]
        for cs, r in zip(chunks, pv):
            acc_scr[:, cs] = alpha_scr[:, cs] * acc_scr[:, cs] + r

    n_blocks = sel_scr.shape[0]

    def step(j, cur):
        nxt = 1 - cur
        j_next = jnp.minimum(j + 1, n_blocks - 1)
        k_next = k_ref[0, pl.ds(pl.multiple_of(j_next * blk, blk), blk), :]
        next_scores = [block_scores(k_next, c) for c in range(n_chunks)]
        pv_update(jnp.maximum(j - 1, 0), nxt)
        chosen = (sel_scr[pl.ds(j, 1), :] > 0.0) & (j < i)
        for c, cs in enumerate(chunks):
            s = s_scr[cur, :, cs]
            m_old = m_scr[:, cs]
            m_new = jnp.maximum(m_old, jnp.where(chosen[:, cs], jnp.max(s, axis=0, keepdims=True),
                                                 NEG_INF))
            p = jnp.exp(s - jnp.where(chosen[:, cs], m_new, float("inf")))
            alpha = jnp.exp(m_old - m_new)
            l_scr[:, cs] = alpha * l_scr[:, cs] + jnp.sum(p, axis=0, keepdims=True)
            m_scr[:, cs] = m_new
            alpha_scr[:, cs] = alpha
            p_scr[cur, :, cs] = p.astype(BF16)
            s_scr[nxt, :, cs] = next_scores[c]

    def pair(t, carry):
        step(2 * t, 0)
        step(2 * t + 1, 1)
        return carry

    n_pairs = (i + 1) // 2
    lax.fori_loop(0, n_pairs, pair, 0)
    pv_update(jnp.clip(2 * n_pairs - 1, 0, n_blocks - 1), 1)
    o_t = (acc_scr[...] / l_scr[...]).astype(BF16)
    eye = (lax.broadcasted_iota(jnp.int32, (blk, blk), 0)
           == lax.broadcasted_iota(jnp.int32, (blk, blk), 1)).astype(BF16)
    o_ref[0] = jnp.concatenate(
        [lax.dot_general(eye, o_t[:, g * blk:(g + 1) * blk], NT, preferred_element_type=F32)
         for g in range(KV_GROUP)], axis=1)


def _moba_prompt(qh, kh, vt, kbar, b, l):
    nb = l // MOBA_BLOCK
    rows = KV_GROUP * MOBA_BLOCK
    k_sel = min(MOBA_TOPK, nb)
    est = 4 * l * LANES * 2 + 4 * l * HEAD_DIM * 2 + 8 * MOBA_BLOCK * GROUP_W * 4 + 24 * rows * LANES * 4
    return pl.pallas_call(
        functools.partial(_moba_prompt_kernel, k_sel=k_sel),
        grid=(N_KV_B, b, nb),
        in_specs=[
            pl.BlockSpec((1, MOBA_BLOCK, GROUP_W), lambda hk, bi, i: (hk, bi * nb + i, 0)),
            pl.BlockSpec((1, l, HEAD_DIM), lambda hk, bi, i: (hk, bi, 0)),
            pl.BlockSpec((nb, HEAD_DIM, MOBA_BLOCK), lambda hk, bi, i: (bi, hk, 0)),
            pl.BlockSpec((1, 1, nb, HEAD_DIM), lambda hk, bi, i: (hk, bi, 0, 0)),
        ],
        out_specs=pl.BlockSpec((1, MOBA_BLOCK, GROUP_W), lambda hk, bi, i: (hk, bi * nb + i, 0)),
        out_shape=jax.ShapeDtypeStruct((N_KV_B, b * l, GROUP_W), F32),
        scratch_shapes=[
            pltpu.VMEM((rows, HEAD_DIM), BF16),
            pltpu.VMEM((1, rows), F32),
            pltpu.VMEM((1, rows), F32),
            pltpu.VMEM((HEAD_DIM, rows), F32),
            pltpu.VMEM((nb, rows), F32),
            pltpu.VMEM((2, MOBA_BLOCK, rows), F32),
            pltpu.VMEM((2, MOBA_BLOCK, rows), BF16),
            pltpu.VMEM((1, rows), F32),
        ],
        compiler_params=_params(("parallel", "parallel", "arbitrary"), est),
        name="moba_prompt",
    )(qh, kh, vt, kbar)


def _moba_sample_kernel(pt_ref, q_ref, kn_ref, vn_ref, kt_hbm, vt_hbm, o_ref,
                        kbuf, vbuf, sem, kbar_scr, s_scr, p_scr,
                        *, n_seq, n_pages, l_new, k_sel):
    b = pl.program_id(0)
    slot = lax.rem(b, 2)
    nb = n_pages // 2
    page = kbuf.shape[-1]
    unroll = math.gcd(nb, 4)

    def page_copy(hbm, buf, sl, which, pg, p):
        return pltpu.make_async_copy(hbm.at[pg], buf.at[sl, p], sem.at[sl, which])

    def fetch(seq, sl):
        def body(p, carry):
            pg = pt_ref[seq, p]
            page_copy(kt_hbm, kbuf, sl, 0, pg, p).start()
            page_copy(vt_hbm, vbuf, sl, 1, pg, p).start()
            return carry
        lax.fori_loop(0, n_pages, body, 0)

    def wait_pages(hbm, buf, which):
        def body(p, carry):
            page_copy(hbm, buf, slot, which, 0, p).wait()
            return carry
        lax.fori_loop(0, n_pages, body, 0)

    @pl.when(b == 0)
    def _():
        fetch(0, 0)

    @pl.when(b + 1 < n_seq)
    def _():
        fetch(b + 1, 1 - slot)

    q = q_ref[0]
    qb = q.astype(BF16)
    rows = q.shape[0]
    wait_pages(kt_hbm, kbuf, 0)

    ones = jnp.ones((SUBLANES, page), BF16)

    def mean_body(j, carry):
        t = kbuf[slot, 2 * j] + kbuf[slot, 2 * j + 1]
        hi, mid, lo = _split3(t)
        d = lambda a: lax.dot_general(ones, a, NT, preferred_element_type=F32)
        kbar_scr[pl.ds(j, 1), :] = (d(hi) + (d(mid) + d(lo)))[0:1] * (1.0 / MOBA_BLOCK)
        return carry

    lax.fori_loop(0, nb, mean_body, 0, unroll=unroll)
    gate = lax.dot_general(q, kbar_scr[...], NT, precision=HI, preferred_element_type=F32)
    sel = _select_blocks(gate, nb, k_sel)
    lane = lax.broadcasted_iota(jnp.int32, sel.shape, 1)

    def score_body(j, carry):
        chosen = jnp.max(jnp.where(lane == j, sel, 0.0), axis=1, keepdims=True)
        s = jnp.concatenate(
            [jnp.dot(qb, kbuf[slot, 2 * j + h].astype(BF16), preferred_element_type=F32)
             for h in range(2)], axis=1)
        s = jnp.where(chosen > 0.0, s, NEG_INF)
        s_scr[j] = s
        return jnp.maximum(carry, s)

    s_max = lax.fori_loop(0, nb, score_body, jnp.full((rows, 2 * page), NEG_INF, F32), unroll=unroll)
    s_new = _bdot_nt(qb, kn_ref[0])
    t_q = lax.broadcasted_iota(jnp.int32, s_new.shape, 0) % l_new
    t_k = lax.broadcasted_iota(jnp.int32, s_new.shape, 1)
    s_new = jnp.where(t_k <= t_q, s_new, NEG_INF)
    m = jnp.maximum(jnp.max(s_max, axis=1, keepdims=True), jnp.max(s_new, axis=1, keepdims=True))
    p_new = jnp.exp(s_new - m)

    def prob_body(j, carry):
        p = jnp.exp(s_scr[j] - m)
        p_scr[j] = p.astype(BF16)
        return carry + p

    p_sum = lax.fori_loop(0, nb, prob_body, jnp.zeros((rows, 2 * page), F32), unroll=unroll)
    denom = jnp.sum(p_sum, axis=1, keepdims=True) + jnp.sum(p_new, axis=1, keepdims=True)
    wait_pages(vt_hbm, vbuf, 1)

    def pv_body(j, acc):
        pj = p_scr[j]
        for h in range(2):
            acc = acc + lax.dot_general(pj[:, h * page:(h + 1) * page],
                                        vbuf[slot, 2 * j + h].astype(BF16), NT,
                                        preferred_element_type=F32)
        return acc

    acc = lax.fori_loop(0, nb, pv_body, _bdot(p_new, vn_ref[0]), unroll=unroll)
    o_ref[0] = acc / denom


def _moba_sample(q_bd, page_table, kt_pool, vt_pool, k_new, v_new, l_new):
    bs, rows, w = q_bd.shape
    n_pages = page_table.shape[1]
    page = kt_pool.shape[2]
    assert 2 * page == MOBA_BLOCK and kt_pool.shape[1] == w
    nb = n_pages // 2
    k_sel = min(MOBA_TOPK, nb + 1)
    lp = k_new.shape[1]
    per_b = lambda r: pl.BlockSpec((1, r, w), lambda b, pt: (b, 0, 0))
    pool_bytes = 2 * n_pages * w * page * 4
    est = 2 * pool_bytes + 8 * nb * rows * 2 * page * 4
    return pl.pallas_call(
        functools.partial(_moba_sample_kernel, n_seq=bs, n_pages=n_pages, l_new=l_new, k_sel=k_sel),
        grid_spec=pltpu.PrefetchScalarGridSpec(
            num_scalar_prefetch=1,
            grid=(bs,),
            in_specs=[per_b(rows), per_b(lp), per_b(lp),
                      pl.BlockSpec(memory_space=pl.ANY), pl.BlockSpec(memory_space=pl.ANY)],
            out_specs=per_b(rows),
            scratch_shapes=[
                pltpu.VMEM((2, n_pages, w, page), F32),
                pltpu.VMEM((2, n_pages, w, page), F32),
                pltpu.SemaphoreType.DMA((2, 2)),
                pltpu.VMEM((nb, w), F32),
                pltpu.VMEM((nb, rows, 2 * page), F32),
                pltpu.VMEM((nb, rows, 2 * page), BF16),
            ],
        ),
        out_shape=jax.ShapeDtypeStruct((bs, rows, w), F32),
        compiler_params=pltpu.CompilerParams(
            dimension_semantics=("arbitrary",),
            vmem_limit_bytes=min(int(est * 1.25), VMEM_BYTES_V7X - 8 * 2 ** 20)),
        name="moba_sample",
    )(page_table, q_bd, k_new, v_new, kt_pool, vt_pool)


def _pack_w_in_a(w):
    d = w.shape[0]
    qkv_gate = w[:, :4 * MIX_W]
    ab = w[:, 4 * MIX_W:4 * MIX_W + 2 * N_HEADS]
    q_mem = w[:, 4 * MIX_W + 2 * N_HEADS:]
    ab = jnp.pad(ab, ((0, 0), (0, LANES - 2 * N_HEADS)))
    return jnp.concatenate([qkv_gate, q_mem, ab], axis=1).astype(BF16)


def _pad_rows(a, rows):
    return jnp.pad(a, ((0, 0), (0, rows - a.shape[1]), (0, 0)))


def _trunk(p, x3, pos_base, mem_k, mem_v, conv0, s0, paged):
    b, l, d = x3.shape
    m = b * l
    x = x3.reshape(m, d)
    sample = paged is not None
    ffn = lambda x, tag, i, fg=None: _ffn(x, p[tag + "_norm"][i], p[tag + "_w1"][i], p[tag + "_w3"][i],
                                          p[tag + "_w2"][i], fg)
    lq = -(-l // SUBLANES) * SUBLANES

    def memory(q_mem, layer):
        q3 = _pad_rows(q_mem.reshape(b, l, MEM_W), lq)
        mo = _mem_attn(q3, mem_k[layer], mem_v[layer])
        return mo[:, :l].reshape(m, MEM_W)

    x = ffn(x, "ffn1", 0)
    qkv, gate, q_mem, ab = _norm_proj(x, p["mix_norm"][0], p["w_in_a"],
                                      (3 * MIX_W, MIX_W, MEM_W, LANES), "in_proj_a")
    qkv, new_conv = _conv(qkv.reshape(b, l, 3 * MIX_W), conv0, p["dn_conv_w"])
    lp = -(-l // DN_CHUNK) * DN_CHUNK
    o, s_fin = _gdn(_pad_rows(qkv, lp), _pad_rows(gate.reshape(b, l, MIX_W), lp),
                    _pad_rows(ab.reshape(b, l, LANES), lp),
                    p["dn_a_log"], p["dn_dt_bias"], p["dn_out_norm"], s0, l)
    o = o[:, :l].reshape(m, MIX_W)
    x = _out_proj(x, o, memory(q_mem, 0), p["w_out"][0], grouped=False)
    x = ffn(x, "ffn2", 0)

    if sample:
        tables = _rope_tables(m, l, pos_base)
    else:
        tables = _rope_tables(l, l, pos_base)
    kv = _kv_proj(x, p["kv_norm"], p["w_kv"], tables, l, for_prompt=not sample)
    x = ffn(x, "ffn1", 1)
    qh, q_mem = _q_proj(x, p["mix_norm"][1], p["w_in_b"], tables)
    if sample:
        page_table, kt_pool, vt_pool = paged
        k_new, v_new = kv
        k_out = k_new.reshape(b, l, N_KV_B, HEAD_DIM)
        v_out = v_new.reshape(b, l, N_KV_B, HEAD_DIM)
        eye = jnp.eye(N_KV_B, dtype=F32)
        q5 = qh.reshape(N_KV_B, b, l, KV_GROUP, HEAD_DIM).transpose(1, 0, 3, 2, 4)
        q_bd = (q5[:, :, :, :, None, :] * eye[None, :, None, None, :, None]).reshape(
            b, N_HEADS * l, N_KV_B * HEAD_DIM)
        o_bd = _moba_sample(q_bd, page_table, kt_pool, vt_pool,
                            _pad_rows(k_new.reshape(b, l, -1), lq),
                            _pad_rows(v_new.reshape(b, l, -1), lq), l)
        o6 = o_bd.reshape(b, N_KV_B, KV_GROUP, l, N_KV_B, HEAD_DIM)
        o = (o6 * eye[None, :, None, None, :, None]).sum(axis=4)
        o = o.transpose(0, 3, 1, 2, 4).reshape(m, MIX_W)
        x = _out_proj(x, o, memory(q_mem, 1), p["w_out"][1], grouped=False)
    else:
        k_t, v_t, kh, vt, kbar = kv
        to_rows = lambda a: a.reshape(b, N_KV_B, HEAD_DIM, l).transpose(0, 3, 1, 2)
        k_out, v_out = to_rows(k_t), to_rows(v_t)
        nb = l // MOBA_BLOCK
        kbar = kbar.reshape(b, nb, N_KV_B, HEAD_DIM).transpose(2, 0, 1, 3)
        o4 = _moba_prompt(qh, kh, vt, kbar, b, l)
        x = _out_proj(x, o4, memory(q_mem, 1), p["w_out"][1], grouped=True)
    y = ffn(x, "ffn2", 1, p["final_norm"])
    return y.reshape(b, l, d), new_conv[None], s_fin[None], k_out, v_out


def kernel(x_prompt, x_sample, mem_prompt, cache_mem_k, cache_mem_v, state_dn_conv, state_dn_S,
           cache_kv_k, cache_kv_v, page_table,
           ffn1_norm, ffn1_w1, ffn1_w3, ffn1_w2, mix_norm, w_in_a, w_in_b, w_out,
           dn_conv_w, dn_a_log, dn_dt_bias, dn_out_norm, kv_norm, w_kv, mem_norm, w_mem_kv,
           ffn2_norm, ffn2_w1, ffn2_w3, ffn2_w2, final_norm):
    assert w_in_a.shape[0] == 1 and w_in_b.shape[0] == 1 and ffn1_w1.shape[0] == 2
    bf = lambda a: a.astype(BF16)
    p = dict(ffn1_norm=ffn1_norm, ffn1_w1=bf(ffn1_w1), ffn1_w3=bf(ffn1_w3), ffn1_w2=bf(ffn1_w2),
             ffn2_norm=ffn2_norm, ffn2_w1=bf(ffn2_w1), ffn2_w3=bf(ffn2_w3), ffn2_w2=bf(ffn2_w2),
             mix_norm=mix_norm, w_in_a=_pack_w_in_a(w_in_a[0]), w_in_b=bf(w_in_b[0]), w_out=bf(w_out),
             dn_conv_w=dn_conv_w[0], dn_a_log=dn_a_log[0], dn_dt_bias=dn_dt_bias[0],
             dn_out_norm=dn_out_norm[0], kv_norm=kv_norm, w_kv=bf(w_kv), final_norm=final_norm)

    bp, lp, _ = x_prompt.shape
    assert lp % MOBA_BLOCK == 0
    n_mem = mem_prompt.shape[1]
    n_layers = mem_norm.shape[0]
    mem_t = [_mem_kv(mem_prompt, mem_norm[layer], bf(w_mem_kv[layer])) for layer in range(n_layers)]
    mk = [t[0] for t in mem_t]
    mv = [t[1] for t in mem_t]
    conv0 = jnp.zeros((bp, CONV_W - 1, 3 * MIX_W), F32)
    s0 = jnp.zeros((bp, N_HEADS, HEAD_DIM, HEAD_DIM), F32)
    y_p, conv_p, s_p, k_p, v_p = _trunk(p, x_prompt, 0, mk, mv, conv0, s0, None)
    token_major = lambda ts: jnp.stack(ts).reshape(n_layers, bp, N_MEM_HEADS, HEAD_DIM, n_mem).transpose(
        0, 1, 4, 2, 3)
    mem_k_p = token_major(mk)
    mem_v_p = token_major(mv)

    bs = x_sample.shape[0]
    n_pool, page = cache_kv_k.shape[:2]
    past_len = page_table.shape[1] * page
    assert past_len % MOBA_BLOCK == 0 and x_sample.shape[1] <= MOBA_BLOCK
    token_minor = lambda c: jnp.transpose(c, (0, 1, 3, 4, 2)).reshape(c.shape[0], bs, MEM_W, n_mem)
    cmk = token_minor(cache_mem_k)
    cmv = token_minor(cache_mem_v)
    transposed_pages = lambda c: jnp.transpose(c, (0, 2, 3, 1)).reshape(n_pool, -1, page)
    paged = (page_table, transposed_pages(cache_kv_k), transposed_pages(cache_kv_v))
    y_s, conv_s, s_s, k_s, v_s = _trunk(p, x_sample, past_len, cmk, cmv, state_dn_conv[0],
                                        state_dn_S[0], paged)
    return (y_p, y_s, conv_p, s_p, k_p, v_p, mem_k_p, mem_v_p, conv_s, s_s, k_s, v_s)
```

```python
import functools
import math

import jax
import jax.numpy as jnp
from jax import lax
from jax.experimental import pallas as pl
from jax.experimental.pallas import tpu as pltpu

F32 = jnp.float32
BF16 = jnp.bfloat16
HI = lax.Precision.HIGHEST

HEAD_DIM = 64
N_HEADS = 12
MIX_W = N_HEADS * HEAD_DIM
N_MEM_HEADS = 4
MEM_W = N_MEM_HEADS * HEAD_DIM
N_KV_B = 4
KV_GROUP = N_HEADS // N_KV_B
GROUP_W = KV_GROUP * HEAD_DIM
CONV_W = 4
DN_CHUNK = 64
MOBA_BLOCK = 256
MOBA_TOPK = 3
ROT_DIM = HEAD_DIM // 4
ROPE_THETA = 500000.0
EPS = 1e-6
QK_SCALE = HEAD_DIM ** -0.5
LOG2E = math.log2(math.e)
ONES_ROWS = 16

LANES = 128
SUBLANES = 8
VMEM_BYTES_V7X = 64 * 2 ** 20
NEG_INF = float("-inf")

NT = (((1,), (1,)), ((), ()))
TN = (((0,), (0,)), ((), ()))


def _params(semantics, est_bytes):
    limit = min(max(int(est_bytes * 1.5), 16 * 2 ** 20), VMEM_BYTES_V7X - 8 * 2 ** 20)
    return pltpu.CompilerParams(dimension_semantics=semantics, vmem_limit_bytes=limit)


def _row_tile(m, pref):
    t = min(m, pref)
    while m % t or t % SUBLANES:
        t -= 1
    return t


def _rms(x, g):
    return x * lax.rsqrt(jnp.mean(x * x, axis=-1, keepdims=True) + EPS) * g


def _sigmoid(x):
    return 1.0 / (1.0 + jnp.exp(-x))


def _softplus(x):
    return jnp.maximum(x, 0.0) + jnp.log(1.0 + jnp.exp(-jnp.abs(x)))


def _bdot(a, b):
    return jnp.dot(a.astype(BF16), b.astype(BF16), preferred_element_type=F32)


def _bdot_nt(a, b):
    return lax.dot_general(a.astype(BF16), b.astype(BF16), NT, preferred_element_type=F32)


def _ffn_kernel(*refs, n_f, final):
    if final:
        x_ref, g_ref, w1_ref, w3_ref, w2_ref, fg_ref, o_ref, h_scr, acc_scr = refs
    else:
        x_ref, g_ref, w1_ref, w3_ref, w2_ref, o_ref, h_scr, acc_scr = refs
    f = pl.program_id(1)

    @pl.when(f == 0)
    def _():
        h_scr[...] = _rms(x_ref[...], g_ref[...]).astype(BF16)
        acc_scr[...] = jnp.zeros_like(acc_scr)

    h = h_scr[...]
    a = jnp.dot(h, w1_ref[...], preferred_element_type=F32)
    b = jnp.dot(h, w3_ref[...], preferred_element_type=F32)
    u = (a * _sigmoid(a)) * b
    acc_scr[...] += jnp.dot(u.astype(BF16), w2_ref[...], preferred_element_type=F32)

    @pl.when(f == n_f - 1)
    def _():
        y = x_ref[...] + 0.5 * acc_scr[...]
        if final:
            y = _rms(y, fg_ref[...])
        o_ref[...] = y


def _ffn(x, g, w1, w3, w2, final_gain=None):
    m, d = x.shape
    d_ff = w1.shape[1]
    tm = _row_tile(m, 512)
    n_f = 2 if d_ff % (2 * LANES) == 0 else 1
    tf = d_ff // n_f
    final = final_gain is not None
    in_specs = [
        pl.BlockSpec((tm, d), lambda i, f: (i, 0)),
        pl.BlockSpec((1, d), lambda i, f: (0, 0)),
        pl.BlockSpec((d, tf), lambda i, f: (0, f)),
        pl.BlockSpec((d, tf), lambda i, f: (0, f)),
        pl.BlockSpec((tf, d), lambda i, f: (f, 0)),
    ]
    args = [x, g.reshape(1, d), w1, w3, w2]
    if final:
        in_specs.append(pl.BlockSpec((1, d), lambda i, f: (0, 0)))
        args.append(final_gain.reshape(1, d))
    est = 4 * tm * d * 4 + 6 * d * tf * 2 + tm * d * 6 + 3 * tm * tf * 4
    return pl.pallas_call(
        functools.partial(_ffn_kernel, n_f=n_f, final=final),
        grid=(m // tm, n_f),
        in_specs=in_specs,
        out_specs=pl.BlockSpec((tm, d), lambda i, f: (i, 0)),
        out_shape=jax.ShapeDtypeStruct((m, d), F32),
        scratch_shapes=[pltpu.VMEM((tm, d), BF16), pltpu.VMEM((tm, d), F32)],
        compiler_params=_params(("parallel", "arbitrary"), est),
        name="ffn_final" if final else "ffn",
    )(*args)


def _norm_proj_kernel(x_ref, g_ref, w_ref, *out_refs, splits):
    h = _rms(x_ref[...], g_ref[...]).astype(BF16)
    for o_ref, (a, b) in zip(out_refs, splits):
        o_ref[...] = jnp.dot(h, w_ref[:, a:b], preferred_element_type=F32)


def _norm_proj(x, g, w, widths, name):
    m, d = x.shape
    n = w.shape[1]
    assert sum(widths) == n and all(c % LANES == 0 for c in widths)
    tm = _row_tile(m, 256)
    splits, a = [], 0
    for c in widths:
        splits.append((a, a + c))
        a += c
    est = 2 * tm * d * 4 + 2 * d * n * 2 + 3 * tm * n * 4
    return pl.pallas_call(
        functools.partial(_norm_proj_kernel, splits=tuple(splits)),
        grid=(m // tm,),
        in_specs=[
            pl.BlockSpec((tm, d), lambda i: (i, 0)),
            pl.BlockSpec((1, d), lambda i: (0, 0)),
            pl.BlockSpec((d, n), lambda i: (0, 0)),
        ],
        out_specs=[pl.BlockSpec((tm, c), lambda i: (i, 0)) for c in widths],
        out_shape=[jax.ShapeDtypeStruct((m, c), F32) for c in widths],
        compiler_params=_params(("parallel",), est),
        name=name,
    )(x, g.reshape(1, d), w)


def _rope_table_kernel(inv_ref, c_ref, s1_ref, s2_ref, *, period, base, tl):
    i = pl.program_id(0)
    row = lax.broadcasted_iota(jnp.int32, (tl, LANES), 0) + i * tl
    pos = base + lax.rem(row, period)
    ang = pos.astype(F32) * inv_ref[...]
    in_head = lax.broadcasted_iota(jnp.int32, (tl, LANES), 1) % HEAD_DIM
    half = ROT_DIM // 2
    c = jnp.cos(ang)
    s = jnp.sin(ang)
    c_ref[...] = jnp.where(in_head < ROT_DIM, c, 1.0)
    s1_ref[...] = jnp.where(in_head < half, -s, 0.0)
    s2_ref[...] = jnp.where((in_head >= half) & (in_head < ROT_DIM), s, 0.0)


def _rope_tables(n_rows, period, base):
    half = ROT_DIM // 2
    inv = ROPE_THETA ** (-jnp.arange(half, dtype=F32) * 2.0 / ROT_DIM)
    in_head = jnp.arange(LANES) % HEAD_DIM
    inv_lane = jnp.where(in_head < ROT_DIM, inv[in_head % half], 0.0).astype(F32).reshape(1, LANES)
    tl = _row_tile(n_rows, 1024)
    shp = jax.ShapeDtypeStruct((n_rows, LANES), F32)
    return pl.pallas_call(
        functools.partial(_rope_table_kernel, period=period, base=base, tl=tl),
        grid=(n_rows // tl,),
        in_specs=[pl.BlockSpec((1, LANES), lambda i: (0, 0))],
        out_specs=[pl.BlockSpec((tl, LANES), lambda i: (i, 0))] * 3,
        out_shape=[shp] * 3,
        compiler_params=_params(("parallel",), 16 * tl * LANES * 4),
        name="rope_tables",
    )(inv_lane)


def _apply_rope(t, c, s1, s2):
    w = t.shape[1]
    reps = w // LANES
    half = ROT_DIM // 2
    tile = lambda a: jnp.concatenate([a] * reps, axis=1) if reps > 1 else a
    up = pltpu.roll(t, w - half, 1)
    down = pltpu.roll(t, half, 1)
    return t * tile(c) + up * tile(s1) + down * tile(s2)


def _kv_proj_kernel(x_ref, g_ref, w_ref, c_ref, s1_ref, s2_ref, *out_refs, for_prompt):
    h = _rms(x_ref[...], g_ref[...]).astype(BF16)
    kv_w = N_KV_B * HEAD_DIM
    k = jnp.dot(h, w_ref[:, :kv_w], preferred_element_type=F32)
    v = jnp.dot(h, w_ref[:, kv_w:], preferred_element_type=F32)
    k = _apply_rope(k, c_ref[...], s1_ref[...], s2_ref[...])
    if for_prompt:
        kt_ref, vt_ref, kh_ref, vtb_ref, kbar_ref, v_scr = out_refs
        v_scr[...] = v
        v_t = v_scr[...].T
        kt_ref[0] = k.T
        vt_ref[0] = v_t
        for hk in range(N_KV_B):
            sl = slice(hk * HEAD_DIM, (hk + 1) * HEAD_DIM)
            k_h = k[:, sl]
            kh_ref[hk] = jnp.concatenate([k_h, jnp.zeros_like(k_h)], axis=1).astype(BF16)
            vtb_ref[0, hk, 0:HEAD_DIM, :] = v_t[sl, :].astype(BF16)
            vtb_ref[0, hk, HEAD_DIM:, :] = jnp.ones((ONES_ROWS, v_t.shape[1]), BF16)
        kbar_ref[0] = jnp.sum(k, axis=0, keepdims=True) * (1.0 / MOBA_BLOCK)
    else:
        out_refs[0][...] = k
        out_refs[1][...] = v


def _kv_proj(x, g, w, tables, seq_len, for_prompt):
    m, d = x.shape
    kv_w = N_KV_B * HEAD_DIM
    tm = MOBA_BLOCK if for_prompt else _row_tile(m, 256)
    n_tab = tables[0].shape[0] // tm
    row = lambda i: (i, 0)
    tab = lambda i: (i % n_tab, 0)
    if for_prompt:
        nb = seq_len // tm
        t_spec = pl.BlockSpec((1, kv_w, tm), lambda i: (i // nb, 0, i % nb))
        t_shape = jax.ShapeDtypeStruct((m // seq_len, kv_w, seq_len), F32)
        vt_rows = HEAD_DIM + ONES_ROWS
        out_specs = [t_spec, t_spec,
                     pl.BlockSpec((N_KV_B, tm, LANES), lambda i: (0, i, 0)),
                     pl.BlockSpec((1, N_KV_B, vt_rows, tm), lambda i: (i, 0, 0, 0)),
                     pl.BlockSpec((1, 1, kv_w), lambda i: (i, 0, 0))]
        out_shape = [t_shape, t_shape,
                     jax.ShapeDtypeStruct((N_KV_B, m, LANES), BF16),
                     jax.ShapeDtypeStruct((m // tm, N_KV_B, vt_rows, tm), BF16),
                     jax.ShapeDtypeStruct((m // tm, 1, kv_w), F32)]
    else:
        out_specs = [pl.BlockSpec((tm, kv_w), row), pl.BlockSpec((tm, kv_w), row)]
        out_shape = [jax.ShapeDtypeStruct((m, kv_w), F32)] * 2
    est = 2 * tm * d * 4 + 2 * d * 2 * kv_w * 2 + 12 * tm * kv_w * 4
    return pl.pallas_call(
        functools.partial(_kv_proj_kernel, for_prompt=for_prompt),
        grid=(m // tm,),
        in_specs=[
            pl.BlockSpec((tm, d), row),
            pl.BlockSpec((1, d), lambda i: (0, 0)),
            pl.BlockSpec((d, 2 * kv_w), lambda i: (0, 0)),
            pl.BlockSpec((tm, LANES), tab),
            pl.BlockSpec((tm, LANES), tab),
            pl.BlockSpec((tm, LANES), tab),
        ],
        out_specs=out_specs,
        out_shape=out_shape,
        scratch_shapes=[pltpu.VMEM((tm, kv_w), F32)] if for_prompt else [],
        compiler_params=_params(("parallel",), est),
        name="kv_proj",
    )(x, g.reshape(1, d), w, *tables)


def _q_proj_kernel(x_ref, g_ref, w_ref, c_ref, s1_ref, s2_ref, qh_ref, qm_ref):
    h = _rms(x_ref[...], g_ref[...]).astype(BF16)
    q = jnp.dot(h, w_ref[:, :MIX_W], preferred_element_type=F32)
    qm_ref[...] = jnp.dot(h, w_ref[:, MIX_W:], preferred_element_type=F32)
    q = _apply_rope(q, c_ref[...], s1_ref[...], s2_ref[...]) * (QK_SCALE * LOG2E)
    for hk in range(N_KV_B):
        qh_ref[hk] = q[:, hk * GROUP_W:(hk + 1) * GROUP_W]


def _q_proj(x, g, w, tables):
    m, d = x.shape
    n = w.shape[1]
    tm = _row_tile(m, 256)
    n_tab = tables[0].shape[0] // tm
    row = lambda i: (i, 0)
    tab = lambda i: (i % n_tab, 0)
    est = 2 * tm * d * 4 + 2 * d * n * 2 + 16 * tm * MIX_W * 4
    return pl.pallas_call(
        _q_proj_kernel,
        grid=(m // tm,),
        in_specs=[
            pl.BlockSpec((tm, d), row),
            pl.BlockSpec((1, d), lambda i: (0, 0)),
            pl.BlockSpec((d, n), lambda i: (0, 0)),
            pl.BlockSpec((tm, LANES), tab),
            pl.BlockSpec((tm, LANES), tab),
            pl.BlockSpec((tm, LANES), tab),
        ],
        out_specs=[
            pl.BlockSpec((N_KV_B, tm, GROUP_W), lambda i: (0, i, 0)),
            pl.BlockSpec((tm, MEM_W), row),
        ],
        out_shape=[
            jax.ShapeDtypeStruct((N_KV_B, m, GROUP_W), F32),
            jax.ShapeDtypeStruct((m, MEM_W), F32),
        ],
        compiler_params=_params(("parallel",), est),
        name="q_proj",
    )(x, g.reshape(1, d), w, *tables)


def _conv_kernel(u_ref, prev_ref, buf_ref, w_ref, y_ref, nb_ref, pad_scr, *, tl, n_t):
    i = pl.program_id(1)
    halo = CONV_W - 1
    lo = SUBLANES - halo
    pad_scr[SUBLANES:SUBLANES + tl, :] = u_ref[0]

    @pl.when(i == 0)
    def _():
        pad_scr[lo:SUBLANES, :] = buf_ref[0]

    @pl.when(i > 0)
    def _():
        p = prev_ref[0]
        pad_scr[lo:SUBLANES, :] = p[p.shape[0] - halo:, :]

    y = pad_scr[lo:lo + tl, :] * w_ref[0:1, :]
    for j in range(1, CONV_W):
        y = y + pad_scr[lo + j:lo + j + tl, :] * w_ref[j:j + 1, :]
    y_ref[0] = y * _sigmoid(y)

    @pl.when(i == n_t - 1)
    def _():
        nb_ref[0] = pad_scr[lo + tl:SUBLANES + tl, :]


def _conv(u, buf, w):
    b, l, c = u.shape
    tl = _row_tile(l, 512) if l % SUBLANES == 0 else l
    n_t = l // tl
    pr = min(SUBLANES, l)
    per = tl // pr
    est = 6 * tl * c * 4
    return pl.pallas_call(
        functools.partial(_conv_kernel, tl=tl, n_t=n_t),
        grid=(b, n_t),
        in_specs=[
            pl.BlockSpec((1, tl, c), lambda bi, i: (bi, i, 0)),
            pl.BlockSpec((1, pr, c), lambda bi, i: (bi, jnp.maximum(i * per - 1, 0), 0)),
            pl.BlockSpec((1, CONV_W - 1, c), lambda bi, i: (bi, 0, 0)),
            pl.BlockSpec((CONV_W, c), lambda bi, i: (0, 0)),
        ],
        out_specs=[
            pl.BlockSpec((1, tl, c), lambda bi, i: (bi, i, 0)),
            pl.BlockSpec((1, CONV_W - 1, c), lambda bi, i: (bi, 0, 0)),
        ],
        out_shape=[
            jax.ShapeDtypeStruct((b, l, c), F32),
            jax.ShapeDtypeStruct((b, CONV_W - 1, c), F32),
        ],
        scratch_shapes=[pltpu.VMEM((tl + SUBLANES, c), F32)],
        compiler_params=_params(("parallel", "arbitrary"), est),
        name="short_conv",
    )(u, u, buf, w)


HEADS_PER_SLAB = LANES * 2 // HEAD_DIM
SLAB_W = HEADS_PER_SLAB * HEAD_DIM
N_SLABS = N_HEADS // HEADS_PER_SLAB


def _split2(x):
    hi = x.astype(BF16)
    return hi, (x - hi.astype(F32)).astype(BF16)


def _split3(x):
    hi = x.astype(BF16)
    r = x - hi.astype(F32)
    mid = r.astype(BF16)
    return hi, mid, (r - mid.astype(F32)).astype(BF16)


def _dot_right01(x, sel):
    hi, mid, lo = _split3(x)
    d = lambda a: jnp.dot(a, sel, preferred_element_type=F32)
    return d(hi) + (d(mid) + d(lo))


def _dot_left01(sel, x):
    hi, mid, lo = _split3(x)
    d = lambda a: jnp.dot(sel, a, preferred_element_type=F32)
    return d(hi) + (d(mid) + d(lo))


def _dot3(a, b_hi, b_lo):
    a_hi, a_lo = _split2(a)
    d = lambda x, y: jnp.dot(x, y, preferred_element_type=F32)
    return d(a_hi, b_hi) + (d(a_hi, b_lo) + d(a_lo, b_hi))


def _block_diag(x, mask):
    return jnp.concatenate([x] * HEADS_PER_SLAB, axis=0) * mask


def _block_diag_pieces(x, mask):
    hi, lo = _split2(x)
    return _block_diag(hi, mask), _block_diag(lo, mask)


def _unit_lower_inverses(ms, eye_t, mask):
    c = ms[0].shape[0]
    xs = [eye_t - m for m in ms]
    ps = [_dot3(m, *_block_diag_pieces(m, mask)) for m in ms]
    power = 2
    while 2 * power < c:
        xps = [_dot3(jnp.concatenate([x, p], axis=0), *_block_diag_pieces(p, mask))
               for x, p in zip(xs, ps)]
        xs = [x + xp[:c] for x, xp in zip(xs, xps)]
        ps = [xp[c:] for xp in xps]
        power *= 2
    return [x + _dot3(x, *_block_diag_pieces(p, mask)) for x, p in zip(xs, ps)]


def _gdn_kernel(q_ref, k_ref, v_ref, gate_ref, ab_ref, alog_ref, dt_ref, gain_ref, s0_ref,
                o_ref, sfin_ref,
                s_scr, qn_scr, kn_scr, gcb_scr, bb_scr, u_scr, w_scr, a_scr, qg_scr, kd_scr, egl_scr,
                *, t_rows, n_t, l_valid):
    t = pl.program_id(1)
    c = DN_CHUNK
    d = HEAD_DIM
    n_chunks = t_rows // c

    r_s = lax.broadcasted_iota(jnp.int32, (SLAB_W, SLAB_W), 0)
    c_s = lax.broadcasted_iota(jnp.int32, (SLAB_W, SLAB_W), 1)
    same_head = (r_s // d) == (c_s // d)
    bd_f32 = jnp.where(same_head, 1.0, 0.0).astype(F32)
    bd_mask = bd_f32.astype(BF16)
    r_t = lax.broadcasted_iota(jnp.int32, (c, SLAB_W), 0)
    c_t = lax.broadcasted_iota(jnp.int32, (c, SLAB_W), 1) % d
    lower_t = r_t >= c_t
    strict_t = r_t > c_t
    eye_t = jnp.where(r_t == c_t, 1.0, 0.0).astype(F32)
    r_c = lax.broadcasted_iota(jnp.int32, (c, c), 0)
    c_c = lax.broadcasted_iota(jnp.int32, (c, c), 1)
    tril = jnp.where(r_c >= c_c, 1.0, 0.0).astype(BF16)
    er = lax.broadcasted_iota(jnp.int32, (LANES, MIX_W), 0)
    ec = lax.broadcasted_iota(jnp.int32, (LANES, MIX_W), 1) // d
    sel_g = jnp.where(er == ec, 1.0, 0.0).astype(BF16)
    sel_b = jnp.where(er == ec + N_HEADS, 1.0, 0.0).astype(BF16)

    @pl.when(t == 0)
    def _():
        for s in range(N_SLABS):
            rows = jnp.concatenate([s0_ref[0, s * HEADS_PER_SLAB + h] for h in range(HEADS_PER_SLAB)],
                                   axis=0)
            s_scr[s] = jnp.concatenate([rows] * HEADS_PER_SLAB, axis=1) * bd_f32

    ab = ab_ref[0]
    row = lax.broadcasted_iota(jnp.int32, ab.shape, 0) + t * t_rows
    valid = row < l_valid
    g = jnp.where(valid, -jnp.exp(alog_ref[...]) * _softplus(ab + dt_ref[...]), 0.0)
    beta = jnp.where(valid, _sigmoid(ab), 0.0)
    gc = jnp.concatenate([_dot_left01(tril, g[i * c:(i + 1) * c]) for i in range(n_chunks)], axis=0)
    gcb_scr[...] = _dot_right01(gc, sel_g)
    bb_scr[...] = _dot_right01(beta, sel_b)
    for s in range(N_SLABS):
        cs = slice(s * SLAB_W, (s + 1) * SLAB_W)
        for src, dst, scale in ((q_ref, qn_scr, QK_SCALE), (k_ref, kn_scr, 1.0)):
            x = src[0, :, cs]
            hi, lo = _split2(x * x)
            ssq = (jnp.dot(hi, bd_mask, preferred_element_type=F32)
                   + jnp.dot(lo, bd_mask, preferred_element_type=F32))
            dst[:, cs] = x * (lax.rsqrt(ssq + EPS) * scale)

    per_iter = 2 if n_chunks % 2 == 0 else 1
    slabs = [slice(s * SLAB_W, (s + 1) * SLAB_W) for s in range(N_SLABS)]

    def factors(it, carry):
        items = []
        for u in range(per_iter):
            ci = it * per_iter + u
            rows = pl.ds(pl.multiple_of(ci * c, c), c)
            items += [(ci, rows, cs) for cs in slabs]
        pre = []
        for ci, rows, cs in items:
            gcb = gcb_scr[rows, cs]
            kn = kn_scr[rows, cs]
            qn = qn_scr[rows, cs]
            b_ = bb_scr[rows, cs]
            kb = kn * b_
            g_row = jnp.sum(gcb * eye_t, axis=0, keepdims=True)
            dec = jnp.exp(jnp.where(lower_t, gcb - g_row, NEG_INF))
            both = lax.dot_general(jnp.concatenate([kb, qn], axis=0).astype(BF16),
                                   _block_diag(kn.astype(BF16), bd_mask), NT,
                                   preferred_element_type=F32)
            pre.append((gcb, kn, qn, kb, dec, both))
        ms = []
        for (ci, rows, cs), (gcb, kn, qn, kb, dec, both) in zip(items, pre):
            ms.append(jnp.where(strict_t, both[:c] * dec, 0.0))
            a_scr[rows, cs] = (both[c:] * dec).astype(BF16)
        tinvs = _unit_lower_inverses(ms, eye_t, bd_mask)
        sols = []
        for (ci, rows, cs), (gcb, kn, qn, kb, dec, both), tinv in zip(items, pre, tinvs):
            eg = jnp.exp(gcb)
            v_hi, v_lo = _block_diag_pieces(v_ref[0, rows, cs] * bb_scr[rows, cs], bd_mask)
            k_hi, k_lo = _block_diag_pieces(kb * eg, bd_mask)
            sols.append(_dot3(tinv, jnp.concatenate([v_hi, k_hi], axis=1),
                              jnp.concatenate([v_lo, k_lo], axis=1)))
            glast = gcb[c - 1:c, :]
            qg_scr[rows, cs] = (qn * eg).astype(BF16)
            kd_scr[rows, cs] = (kn * jnp.exp(glast - gcb)).astype(BF16)
            egl_scr[pl.ds(ci, 1), cs] = jnp.exp(glast)
        for (ci, rows, cs), sol in zip(items, sols):
            u_scr[rows, cs] = sol[:, :SLAB_W]
            w_scr[rows, cs] = sol[:, SLAB_W:].astype(BF16)
        return carry

    lax.fori_loop(0, n_chunks // per_iter, factors, 0)

    gain = gain_ref[...]

    def recur(ci, carry):
        rows = pl.ds(pl.multiple_of(ci * c, c), c)
        olds = [s_scr[s] for s in range(N_SLABS)]
        rs = [jnp.dot(jnp.concatenate([w_scr[rows, cs], qg_scr[rows, cs]], axis=0),
                      s_old.astype(BF16), preferred_element_type=F32)
              for cs, s_old in zip(slabs, olds)]
        v_news = [(u_scr[rows, cs] - r[:c]).astype(BF16) for cs, r in zip(slabs, rs)]
        upds = [lax.dot_general(kd_scr[rows, cs], v_new, TN, preferred_element_type=F32)
                for cs, v_new in zip(slabs, v_news)]
        os_ = [r[c:] + jnp.dot(a_scr[rows, cs], _block_diag(v_new, bd_mask),
                               preferred_element_type=F32)
               for cs, r, v_new in zip(slabs, rs, v_news)]
        for s, (cs, s_old, upd) in enumerate(zip(slabs, olds, upds)):
            s_scr[s] = (s_old * egl_scr[pl.ds(ci, 1), cs] + upd) * bd_f32
        ssqs = []
        for o in os_:
            hi, lo = _split2(o * o)
            ssqs.append(jnp.dot(hi, bd_mask, preferred_element_type=F32)
                        + jnp.dot(lo, bd_mask, preferred_element_type=F32))
        for cs, o, ssq in zip(slabs, os_, ssqs):
            gt = gate_ref[0, rows, cs]
            o_ref[0, rows, cs] = (o * lax.rsqrt(ssq * (1.0 / d) + EPS) * gain[:, cs]
                                  * (gt * _sigmoid(gt)))
        return carry

    lax.fori_loop(0, n_chunks, recur, 0)

    @pl.when(t == n_t - 1)
    def _():
        for s in range(N_SLABS):
            full = s_scr[s]
            for h in range(HEADS_PER_SLAB):
                sfin_ref[0, s * HEADS_PER_SLAB + h] = full[h * d:(h + 1) * d, h * d:(h + 1) * d]


def _gdn(qkv, gate, ab, a_log, dt_bias, out_gain, s0, l_valid):
    b, lp, _ = qkv.shape
    assert lp % DN_CHUNK == 0
    t_rows = DN_CHUNK * math.gcd(lp // DN_CHUNK, 8)
    n_t = lp // t_rows
    n_chunks = t_rows // DN_CHUNK
    pad = lambda a: jnp.pad(a.astype(F32), (0, LANES - N_HEADS)).reshape(1, LANES)
    blk = lambda j: pl.BlockSpec((1, t_rows, MIX_W), lambda bi, t, j=j: (bi, t, j))
    state = pl.BlockSpec((1, N_HEADS, HEAD_DIM, HEAD_DIM), lambda bi, t: (bi, 0, 0, 0))
    vec = pl.BlockSpec((1, LANES), lambda bi, t: (0, 0))
    wide_f32 = pltpu.VMEM((t_rows, MIX_W), F32)
    wide_bf16 = pltpu.VMEM((t_rows, MIX_W), BF16)
    est = 10 * t_rows * MIX_W * 4 + 5 * t_rows * MIX_W * 4 + 4 * t_rows * MIX_W * 2 + 64 * SLAB_W * SLAB_W * 4
    return pl.pallas_call(
        functools.partial(_gdn_kernel, t_rows=t_rows, n_t=n_t, l_valid=l_valid),
        grid=(b, n_t),
        in_specs=[
            blk(0), blk(1), blk(2),
            pl.BlockSpec((1, t_rows, MIX_W), lambda bi, t: (bi, t, 0)),
            pl.BlockSpec((1, t_rows, LANES), lambda bi, t: (bi, t, 0)),
            vec, vec,
            pl.BlockSpec((1, MIX_W), lambda bi, t: (0, 0)),
            state,
        ],
        out_specs=[pl.BlockSpec((1, t_rows, MIX_W), lambda bi, t: (bi, t, 0)), state],
        out_shape=[
            jax.ShapeDtypeStruct((b, lp, MIX_W), F32),
            jax.ShapeDtypeStruct((b, N_HEADS, HEAD_DIM, HEAD_DIM), F32),
        ],
        scratch_shapes=[
            pltpu.VMEM((N_SLABS, SLAB_W, SLAB_W), F32),
            wide_f32, wide_f32,
            wide_f32, wide_f32,
            wide_f32,
            wide_bf16, wide_bf16, wide_bf16, wide_bf16,
            pltpu.VMEM((max(n_chunks, SUBLANES), MIX_W), F32),
        ],
        compiler_params=_params(("parallel", "arbitrary"), est),
        name="gated_delta_rule",
    )(qkv, qkv, qkv, gate, ab, pad(a_log), pad(dt_bias),
      jnp.tile(out_gain.astype(F32), N_HEADS).reshape(1, MIX_W), s0)


def _mem_kv_kernel(x_ref, g_ref, w_ref, kt_ref, vt_ref, kv_scr):
    h = _rms(x_ref[...], g_ref[...]).astype(BF16)
    kv_scr[...] = jnp.dot(h, w_ref[...], preferred_element_type=F32)
    kt_ref[0] = kv_scr[:, :MEM_W].T
    vt_ref[0] = kv_scr[:, MEM_W:].T


def _mem_kv(mem, g, w):
    b, n_mem, d = mem.shape
    shp = jax.ShapeDtypeStruct((b, MEM_W, n_mem), F32)
    spec = pl.BlockSpec((1, MEM_W, n_mem), lambda i: (i, 0, 0))
    return pl.pallas_call(
        _mem_kv_kernel,
        grid=(b,),
        in_specs=[
            pl.BlockSpec((n_mem, d), lambda i: (i, 0)),
            pl.BlockSpec((1, d), lambda i: (0, 0)),
            pl.BlockSpec((d, 2 * MEM_W), lambda i: (0, 0)),
        ],
        out_specs=[spec, spec],
        out_shape=[shp, shp],
        scratch_shapes=[pltpu.VMEM((n_mem, 2 * MEM_W), F32)],
        compiler_params=_params(("parallel",), 2 * n_mem * d * 4 + 2 * d * 2 * MEM_W * 2 + 8 * n_mem * MEM_W * 4),
        name="mem_kv",
    )(mem.reshape(b * n_mem, d), g.reshape(1, d), w)


def _mem_attn_kernel(q_ref, kt_ref, vt_ref, o_ref):
    q = q_ref[0]
    kt = kt_ref[0]
    vt = vt_ref[0]
    outs = []
    for h in range(N_MEM_HEADS):
        sl = slice(h * HEAD_DIM, (h + 1) * HEAD_DIM)
        s = _bdot(q[:, sl], kt[sl, :]) * QK_SCALE
        p = jnp.exp(s - jnp.max(s, axis=-1, keepdims=True))
        outs.append(_bdot_nt(p, vt[sl, :]) / jnp.sum(p, axis=-1, keepdims=True))
    o_ref[0] = jnp.concatenate(outs, axis=1)


def _mem_attn(q, mk_t, mv_t):
    b, l, w = q.shape
    n_mem = mk_t.shape[2]
    tl = _row_tile(l, 512)
    est = 4 * tl * w * 4 + 4 * n_mem * w * 4 + 8 * tl * n_mem * 4
    return pl.pallas_call(
        _mem_attn_kernel,
        grid=(b, l // tl),
        in_specs=[
            pl.BlockSpec((1, tl, w), lambda bi, i: (bi, i, 0)),
            pl.BlockSpec((1, w, n_mem), lambda bi, i: (bi, 0, 0)),
            pl.BlockSpec((1, w, n_mem), lambda bi, i: (bi, 0, 0)),
        ],
        out_specs=pl.BlockSpec((1, tl, w), lambda bi, i: (bi, i, 0)),
        out_shape=jax.ShapeDtypeStruct((b, l, w), F32),
        compiler_params=_params(("parallel", "parallel"), est),
        name="mem_attn",
    )(q, mk_t, mv_t)


def _out_proj_kernel(x_ref, o_ref, mo_ref, wo_ref, wm_ref, y_ref, *, grouped):
    y = x_ref[...] + _bdot(mo_ref[...], wm_ref[...])
    if grouped:
        for hk in range(N_KV_B):
            y = y + _bdot(o_ref[hk], wo_ref[hk])
    else:
        y = y + _bdot(o_ref[...], wo_ref[...])
    y_ref[...] = y


def _out_proj(x, o, mo, w_out, grouped):
    m, d = x.shape
    tm = _row_tile(m, 512)
    row = lambda i: (i, 0)
    if grouped:
        wo = w_out[:MIX_W].reshape(N_KV_B, GROUP_W, d)
        o_spec = pl.BlockSpec((N_KV_B, tm, GROUP_W), lambda i: (0, i, 0))
        wo_spec = pl.BlockSpec((N_KV_B, GROUP_W, d), lambda i: (0, 0, 0))
    else:
        wo = w_out[:MIX_W]
        o_spec = pl.BlockSpec((tm, MIX_W), row)
        wo_spec = pl.BlockSpec((MIX_W, d), lambda i: (0, 0))
    est = 6 * tm * d * 4 + 4 * tm * d * 4 + 4 * d * d * 2
    return pl.pallas_call(
        functools.partial(_out_proj_kernel, grouped=grouped),
        grid=(m // tm,),
        in_specs=[
            pl.BlockSpec((tm, d), row),
            o_spec,
            pl.BlockSpec((tm, MEM_W), row),
            wo_spec,
            pl.BlockSpec((MEM_W, d), lambda i: (0, 0)),
        ],
        out_specs=pl.BlockSpec((tm, d), row),
        out_shape=jax.ShapeDtypeStruct((m, d), F32),
        compiler_params=_params(("parallel",), est),
        name="out_proj",
    )(x, o, mo, wo, w_out[MIX_W:])


def _select_blocks(gate, n_past, k_sel):
    lane = lax.broadcasted_iota(jnp.int32, gate.shape, 1)
    lane_f = lane.astype(F32)
    gate = jnp.where(lane < n_past, gate, NEG_INF)
    sel = jnp.zeros(gate.shape, F32)
    for _ in range(k_sel):
        mx = jnp.max(gate, axis=1, keepdims=True)
        first = jnp.min(jnp.where(gate == mx, lane_f, float(gate.shape[1])), axis=1, keepdims=True)
        pick = (lane_f == first) & (mx > NEG_INF)
        sel = jnp.where(pick, 1.0, sel)
        gate = jnp.where(pick, NEG_INF, gate)
    return sel


def _select_blocks_t(gate, n_past, k_sel):
    blk = lax.broadcasted_iota(jnp.int32, gate.shape, 0)
    blk_f = blk.astype(F32)
    gate = jnp.where(blk < n_past, gate, NEG_INF)
    sel = jnp.zeros(gate.shape, F32)
    for _ in range(k_sel):
        mx = jnp.max(gate, axis=0, keepdims=True)
        first = jnp.min(jnp.where(gate == mx, blk_f, float(gate.shape[0])), axis=0, keepdims=True)
        pick = (blk_f == first) & (mx > NEG_INF)
        sel = jnp.where(pick, 1.0, sel)
        gate = jnp.where(pick, NEG_INF, gate)
    return sel


def _moba_prompt_kernel(q_ref, k_ref, vt_ref, kbar_ref, o_ref, q_scr, m_scr, acc_scr, sel_scr,
                        s_scr, p_scr, alpha_scr, *, k_sel):
    i = pl.program_id(2)
    blk = MOBA_BLOCK
    d = HEAD_DIM
    cw = LANES
    n_chunks = KV_GROUP * blk // cw
    n_blocks = sel_scr.shape[0]
    chunks = [slice(c * cw, (c + 1) * cw) for c in range(n_chunks)]
    q = q_ref[0]
    q3 = jnp.concatenate([q[:, g * d:(g + 1) * d] for g in range(KV_GROUP)], axis=0)
    q_scr[...] = jnp.concatenate([q3, jnp.zeros_like(q3)], axis=1).astype(BF16)

    def block_scores(k_blk, c):
        return lax.dot_general(k_blk, q_scr[c * cw:(c + 1) * cw, :], NT,
                               preferred_element_type=F32)

    k_own = k_ref[0, pl.ds(pl.multiple_of(i * blk, blk), blk), :]
    own_scores = [block_scores(k_own, c) for c in range(n_chunks)]
    k_first = k_ref[0, 0:blk, :]
    first_scores = [block_scores(k_first, c) for c in range(n_chunks)]
    gate_t = lax.dot_general(kbar_ref[0, 0], q3, NT, precision=HI, preferred_element_type=F32)
    for c, cs in enumerate(chunks):
        s_scr[0, :, cs] = first_scores[c]
    for c, cs in enumerate(chunks):
        kpos = lax.broadcasted_iota(jnp.int32, (blk, cw), 0)
        qpos = (lax.broadcasted_iota(jnp.int32, (blk, cw), 1) + c * cw) % blk
        s = jnp.where(kpos <= qpos, own_scores[c], NEG_INF)
        m0 = jnp.max(s, axis=0, keepdims=True)
        m_scr[:, cs] = m0
        p_scr[1, :, cs] = jnp.exp2(s - m0).astype(BF16)
    acc_scr[...] = jnp.zeros(acc_scr.shape, F32)
    alpha_scr[...] = jnp.ones(alpha_scr.shape, F32)
    sel_scr[...] = _select_blocks_t(gate_t, i, k_sel)

    def pv_update(j_prev, slot):
        vt_prev = vt_ref[j_prev, 0]
        pv = [jnp.dot(vt_prev, p_scr[slot, :, cs], preferred_element_type=F32) for cs in chunks]
        for cs, r in zip(chunks, pv):
            acc_scr[:, cs] = alpha_scr[:, cs] * acc_scr[:, cs] + r

    def step(j, cur):
        nxt = 1 - cur
        j_next = jnp.minimum(j + 1, n_blocks - 1)
        k_next = k_ref[0, pl.ds(pl.multiple_of(j_next * blk, blk), blk), :]
        for c, cs in enumerate(chunks):
            s_scr[nxt, :, cs] = block_scores(k_next, c)
        pv_update(jnp.where(j == 0, i, j - 1), nxt)
        chosen = (sel_scr[pl.ds(j, 1), :] > 0.0) & (j < i)
        for c, cs in enumerate(chunks):
            s = s_scr[cur, :, cs]
            m_old = m_scr[:, cs]
            m_new = jnp.maximum(m_old, jnp.where(chosen[:, cs], jnp.max(s, axis=0, keepdims=True),
                                                 NEG_INF))
            p_scr[cur, :, cs] = jnp.exp2(s - jnp.where(chosen[:, cs], m_new, float("inf"))).astype(BF16)
            alpha_scr[:, cs] = jnp.exp2(m_old - m_new)
            m_scr[:, cs] = m_new

    def pair(t, carry):
        step(2 * t, 0)
        step(2 * t + 1, 1)
        return carry

    n_pairs = (i + 1) // 2
    lax.fori_loop(0, n_pairs, pair, 0)
    pv_update(jnp.where(n_pairs == 0, i, jnp.minimum(2 * n_pairs - 1, n_blocks - 1)), 1)
    acc = acc_scr[...]
    o_t = (acc[:d] / acc[d:d + 1]).astype(BF16)
    eye = (lax.broadcasted_iota(jnp.int32, (blk, blk), 0)
           == lax.broadcasted_iota(jnp.int32, (blk, blk), 1)).astype(BF16)
    o_ref[0] = jnp.concatenate(
        [lax.dot_general(eye, o_t[:, g * blk:(g + 1) * blk], NT, preferred_element_type=F32)
         for g in range(KV_GROUP)], axis=1)


def _moba_prompt(qh, kh, vt, kbar, b, l):
    nb = l // MOBA_BLOCK
    rows = KV_GROUP * MOBA_BLOCK
    k_sel = min(MOBA_TOPK, nb)
    vt_rows = vt.shape[2]
    est = 4 * l * LANES * 2 + 4 * l * vt_rows * 2 + 8 * MOBA_BLOCK * GROUP_W * 4 + 24 * rows * LANES * 4
    return pl.pallas_call(
        functools.partial(_moba_prompt_kernel, k_sel=k_sel),
        grid=(N_KV_B, b, nb),
        in_specs=[
            pl.BlockSpec((1, MOBA_BLOCK, GROUP_W), lambda hk, bi, i: (hk, bi * nb + i, 0)),
            pl.BlockSpec((1, l, LANES), lambda hk, bi, i: (hk, bi, 0)),
            pl.BlockSpec((nb, 1, vt_rows, MOBA_BLOCK), lambda hk, bi, i: (bi, hk, 0, 0)),
            pl.BlockSpec((1, 1, nb, HEAD_DIM), lambda hk, bi, i: (hk, bi, 0, 0)),
        ],
        out_specs=pl.BlockSpec((1, MOBA_BLOCK, GROUP_W), lambda hk, bi, i: (hk, bi * nb + i, 0)),
        out_shape=jax.ShapeDtypeStruct((N_KV_B, b * l, GROUP_W), F32),
        scratch_shapes=[
            pltpu.VMEM((rows, LANES), BF16),
            pltpu.VMEM((1, rows), F32),
            pltpu.VMEM((vt_rows, rows), F32),
            pltpu.VMEM((nb, rows), F32),
            pltpu.VMEM((2, MOBA_BLOCK, rows), F32),
            pltpu.VMEM((2, MOBA_BLOCK, rows), BF16),
            pltpu.VMEM((1, rows), F32),
        ],
        compiler_params=_params(("parallel", "parallel", "arbitrary"), est),
        name="moba_prompt",
    )(qh, kh, vt, kbar)


def _moba_sample_kernel(pt_ref, q_ref, kn_ref, vn_ref, kt_hbm, vt_hbm, o_ref,
                        kbuf, vbuf, sem, kbar_scr, s_scr, p_scr,
                        *, n_seq, n_pages, l_new, k_sel):
    b = pl.program_id(0)
    slot = lax.rem(b, 2)
    nb = n_pages // 2
    page = kbuf.shape[-1]
    unroll = math.gcd(nb, 4)

    def page_copy(hbm, buf, sl, which, pg, p):
        return pltpu.make_async_copy(hbm.at[pg], buf.at[sl, p], sem.at[sl, which])

    def fetch(seq, sl):
        def body(p, carry):
            pg = pt_ref[seq, p]
            page_copy(kt_hbm, kbuf, sl, 0, pg, p).start()
            page_copy(vt_hbm, vbuf, sl, 1, pg, p).start()
            return carry
        lax.fori_loop(0, n_pages, body, 0)

    def wait_pages(hbm, buf, which):
        def body(p, carry):
            page_copy(hbm, buf, slot, which, 0, p).wait()
            return carry
        lax.fori_loop(0, n_pages, body, 0)

    @pl.when(b == 0)
    def _():
        fetch(0, 0)

    @pl.when(b + 1 < n_seq)
    def _():
        fetch(b + 1, 1 - slot)

    q = q_ref[0]
    qb = q.astype(BF16)
    rows = q.shape[0]
    wait_pages(kt_hbm, kbuf, 0)

    ones = jnp.ones((SUBLANES, page), BF16)

    def mean_body(j, carry):
        t = kbuf[slot, 2 * j] + kbuf[slot, 2 * j + 1]
        hi, mid, lo = _split3(t)
        d = lambda a: lax.dot_general(ones, a, NT, preferred_element_type=F32)
        kbar_scr[pl.ds(j, 1), :] = (d(hi) + (d(mid) + d(lo)))[0:1] * (1.0 / MOBA_BLOCK)
        return carry

    lax.fori_loop(0, nb, mean_body, 0, unroll=unroll)
    gate = lax.dot_general(q, kbar_scr[...], NT, precision=HI, preferred_element_type=F32)
    sel = _select_blocks(gate, nb, k_sel)
    lane = lax.broadcasted_iota(jnp.int32, sel.shape, 1)

    def score_body(j, carry):
        chosen = jnp.max(jnp.where(lane == j, sel, 0.0), axis=1, keepdims=True)
        s = jnp.concatenate(
            [jnp.dot(qb, kbuf[slot, 2 * j + h].astype(BF16), preferred_element_type=F32)
             for h in range(2)], axis=1)
        s = jnp.where(chosen > 0.0, s, NEG_INF)
        s_scr[j] = s
        return jnp.maximum(carry, s)

    s_max = lax.fori_loop(0, nb, score_body, jnp.full((rows, 2 * page), NEG_INF, F32), unroll=unroll)
    s_new = _bdot_nt(qb, kn_ref[0])
    t_q = lax.broadcasted_iota(jnp.int32, s_new.shape, 0) % l_new
    t_k = lax.broadcasted_iota(jnp.int32, s_new.shape, 1)
    s_new = jnp.where(t_k <= t_q, s_new, NEG_INF)
    m = jnp.maximum(jnp.max(s_max, axis=1, keepdims=True), jnp.max(s_new, axis=1, keepdims=True))
    p_new = jnp.exp2(s_new - m)

    def prob_body(j, carry):
        p = jnp.exp2(s_scr[j] - m)
        p_scr[j] = p.astype(BF16)
        return carry + p

    p_sum = lax.fori_loop(0, nb, prob_body, jnp.zeros((rows, 2 * page), F32), unroll=unroll)
    denom = jnp.sum(p_sum, axis=1, keepdims=True) + jnp.sum(p_new, axis=1, keepdims=True)
    wait_pages(vt_hbm, vbuf, 1)

    def pv_body(j, acc):
        pj = p_scr[j]
        for h in range(2):
            acc = acc + lax.dot_general(pj[:, h * page:(h + 1) * page],
                                        vbuf[slot, 2 * j + h].astype(BF16), NT,
                                        preferred_element_type=F32)
        return acc

    acc = lax.fori_loop(0, nb, pv_body, _bdot(p_new, vn_ref[0]), unroll=unroll)
    o_ref[0] = acc / denom


def _moba_sample(q_bd, page_table, kt_pool, vt_pool, k_new, v_new, l_new):
    bs, rows, w = q_bd.shape
    n_pages = page_table.shape[1]
    page = kt_pool.shape[2]
    assert 2 * page == MOBA_BLOCK and kt_pool.shape[1] == w
    nb = n_pages // 2
    k_sel = min(MOBA_TOPK, nb + 1)
    lp = k_new.shape[1]
    per_b = lambda r: pl.BlockSpec((1, r, w), lambda b, pt: (b, 0, 0))
    pool_bytes = 2 * n_pages * w * page * 4
    est = 2 * pool_bytes + 8 * nb * rows * 2 * page * 4
    return pl.pallas_call(
        functools.partial(_moba_sample_kernel, n_seq=bs, n_pages=n_pages, l_new=l_new, k_sel=k_sel),
        grid_spec=pltpu.PrefetchScalarGridSpec(
            num_scalar_prefetch=1,
            grid=(bs,),
            in_specs=[per_b(rows), per_b(lp), per_b(lp),
                      pl.BlockSpec(memory_space=pl.ANY), pl.BlockSpec(memory_space=pl.ANY)],
            out_specs=per_b(rows),
            scratch_shapes=[
                pltpu.VMEM((2, n_pages, w, page), F32),
                pltpu.VMEM((2, n_pages, w, page), F32),
                pltpu.SemaphoreType.DMA((2, 2)),
                pltpu.VMEM((nb, w), F32),
                pltpu.VMEM((nb, rows, 2 * page), F32),
                pltpu.VMEM((nb, rows, 2 * page), BF16),
            ],
        ),
        out_shape=jax.ShapeDtypeStruct((bs, rows, w), F32),
        compiler_params=pltpu.CompilerParams(
            dimension_semantics=("arbitrary",),
            vmem_limit_bytes=min(int(est * 1.25), VMEM_BYTES_V7X - 8 * 2 ** 20)),
        name="moba_sample",
    )(page_table, q_bd, k_new, v_new, kt_pool, vt_pool)


def _pack_w_in_a(w):
    d = w.shape[0]
    qkv_gate = w[:, :4 * MIX_W]
    ab = w[:, 4 * MIX_W:4 * MIX_W + 2 * N_HEADS]
    q_mem = w[:, 4 * MIX_W + 2 * N_HEADS:]
    ab = jnp.pad(ab, ((0, 0), (0, LANES - 2 * N_HEADS)))
    return jnp.concatenate([qkv_gate, q_mem, ab], axis=1).astype(BF16)


def _pad_rows(a, rows):
    return jnp.pad(a, ((0, 0), (0, rows - a.shape[1]), (0, 0)))


def _trunk(p, x3, pos_base, mem_k, mem_v, conv0, s0, paged):
    b, l, d = x3.shape
    m = b * l
    x = x3.reshape(m, d)
    sample = paged is not None
    ffn = lambda x, tag, i, fg=None: _ffn(x, p[tag + "_norm"][i], p[tag + "_w1"][i], p[tag + "_w3"][i],
                                          p[tag + "_w2"][i], fg)
    lq = -(-l // SUBLANES) * SUBLANES

    def memory(q_mem, layer):
        q3 = _pad_rows(q_mem.reshape(b, l, MEM_W), lq)
        mo = _mem_attn(q3, mem_k[layer], mem_v[layer])
        return mo[:, :l].reshape(m, MEM_W)

    x = ffn(x, "ffn1", 0)
    qkv, gate, q_mem, ab = _norm_proj(x, p["mix_norm"][0], p["w_in_a"],
                                      (3 * MIX_W, MIX_W, MEM_W, LANES), "in_proj_a")
    qkv, new_conv = _conv(qkv.reshape(b, l, 3 * MIX_W), conv0, p["dn_conv_w"])
    lp = -(-l // DN_CHUNK) * DN_CHUNK
    o, s_fin = _gdn(_pad_rows(qkv, lp), _pad_rows(gate.reshape(b, l, MIX_W), lp),
                    _pad_rows(ab.reshape(b, l, LANES), lp),
                    p["dn_a_log"], p["dn_dt_bias"], p["dn_out_norm"], s0, l)
    o = o[:, :l].reshape(m, MIX_W)
    x = _out_proj(x, o, memory(q_mem, 0), p["w_out"][0], grouped=False)
    x = ffn(x, "ffn2", 0)

    if sample:
        tables = _rope_tables(m, l, pos_base)
    else:
        tables = _rope_tables(l, l, pos_base)
    kv = _kv_proj(x, p["kv_norm"], p["w_kv"], tables, l, for_prompt=not sample)
    x = ffn(x, "ffn1", 1)
    qh, q_mem = _q_proj(x, p["mix_norm"][1], p["w_in_b"], tables)
    if sample:
        page_table, kt_pool, vt_pool = paged
        k_new, v_new = kv
        k_out = k_new.reshape(b, l, N_KV_B, HEAD_DIM)
        v_out = v_new.reshape(b, l, N_KV_B, HEAD_DIM)
        eye = jnp.eye(N_KV_B, dtype=F32)
        q5 = qh.reshape(N_KV_B, b, l, KV_GROUP, HEAD_DIM).transpose(1, 0, 3, 2, 4)
        q_bd = (q5[:, :, :, :, None, :] * eye[None, :, None, None, :, None]).reshape(
            b, N_HEADS * l, N_KV_B * HEAD_DIM)
        o_bd = _moba_sample(q_bd, page_table, kt_pool, vt_pool,
                            _pad_rows(k_new.reshape(b, l, -1), lq),
                            _pad_rows(v_new.reshape(b, l, -1), lq), l)
        o6 = o_bd.reshape(b, N_KV_B, KV_GROUP, l, N_KV_B, HEAD_DIM)
        o = (o6 * eye[None, :, None, None, :, None]).sum(axis=4)
        o = o.transpose(0, 3, 1, 2, 4).reshape(m, MIX_W)
        x = _out_proj(x, o, memory(q_mem, 1), p["w_out"][1], grouped=False)
    else:
        k_t, v_t, kh, vt, kbar = kv
        to_rows = lambda a: a.reshape(b, N_KV_B, HEAD_DIM, l).transpose(0, 3, 1, 2)
        k_out, v_out = to_rows(k_t), to_rows(v_t)
        nb = l // MOBA_BLOCK
        kbar = kbar.reshape(b, nb, N_KV_B, HEAD_DIM).transpose(2, 0, 1, 3)
        o4 = _moba_prompt(qh, kh, vt, kbar, b, l)
        x = _out_proj(x, o4, memory(q_mem, 1), p["w_out"][1], grouped=True)
    y = ffn(x, "ffn2", 1, p["final_norm"])
    return y.reshape(b, l, d), new_conv[None], s_fin[None], k_out, v_out


def kernel(x_prompt, x_sample, mem_prompt, cache_mem_k, cache_mem_v, state_dn_conv, state_dn_S,
           cache_kv_k, cache_kv_v, page_table,
           ffn1_norm, ffn1_w1, ffn1_w3, ffn1_w2, mix_norm, w_in_a, w_in_b, w_out,
           dn_conv_w, dn_a_log, dn_dt_bias, dn_out_norm, kv_norm, w_kv, mem_norm, w_mem_kv,
           ffn2_norm, ffn2_w1, ffn2_w3, ffn2_w2, final_norm):
    assert w_in_a.shape[0] == 1 and w_in_b.shape[0] == 1 and ffn1_w1.shape[0] == 2
    bf = lambda a: a.astype(BF16)
    p = dict(ffn1_norm=ffn1_norm, ffn1_w1=bf(ffn1_w1), ffn1_w3=bf(ffn1_w3), ffn1_w2=bf(ffn1_w2),
             ffn2_norm=ffn2_norm, ffn2_w1=bf(ffn2_w1), ffn2_w3=bf(ffn2_w3), ffn2_w2=bf(ffn2_w2),
             mix_norm=mix_norm, w_in_a=_pack_w_in_a(w_in_a[0]), w_in_b=bf(w_in_b[0]), w_out=bf(w_out),
             dn_conv_w=dn_conv_w[0], dn_a_log=dn_a_log[0], dn_dt_bias=dn_dt_bias[0],
             dn_out_norm=dn_out_norm[0], kv_norm=kv_norm, w_kv=bf(w_kv), final_norm=final_norm)

    bp, lp, _ = x_prompt.shape
    assert lp % MOBA_BLOCK == 0
    n_mem = mem_prompt.shape[1]
    n_layers = mem_norm.shape[0]
    mem_t = [_mem_kv(mem_prompt, mem_norm[layer], bf(w_mem_kv[layer])) for layer in range(n_layers)]
    mk = [t[0] for t in mem_t]
    mv = [t[1] for t in mem_t]
    conv0 = jnp.zeros((bp, CONV_W - 1, 3 * MIX_W), F32)
    s0 = jnp.zeros((bp, N_HEADS, HEAD_DIM, HEAD_DIM), F32)
    y_p, conv_p, s_p, k_p, v_p = _trunk(p, x_prompt, 0, mk, mv, conv0, s0, None)
    token_major = lambda ts: jnp.stack(ts).reshape(n_layers, bp, N_MEM_HEADS, HEAD_DIM, n_mem).transpose(
        0, 1, 4, 2, 3)
    mem_k_p = token_major(mk)
    mem_v_p = token_major(mv)

    bs = x_sample.shape[0]
    n_pool, page = cache_kv_k.shape[:2]
    past_len = page_table.shape[1] * page
    assert past_len % MOBA_BLOCK == 0 and x_sample.shape[1] <= MOBA_BLOCK
    token_minor = lambda c: jnp.transpose(c, (0, 1, 3, 4, 2)).reshape(c.shape[0], bs, MEM_W, n_mem)
    cmk = token_minor(cache_mem_k)
    cmv = token_minor(cache_mem_v)
    transposed_pages = lambda c: jnp.transpose(c, (0, 2, 3, 1)).reshape(n_pool, -1, page)
    paged = (page_table, transposed_pages(cache_kv_k), transposed_pages(cache_kv_v))
    y_s, conv_s, s_s, k_s, v_s = _trunk(p, x_sample, past_len, cmk, cmv, state_dn_conv[0],
                                        state_dn_S[0], paged)
    return (y_p, y_s, conv_p, s_p, k_p, v_p, mem_k_p, mem_v_p, conv_s, s_s, k_s, v_s)
```

```python
import functools
import math

import jax
import jax.numpy as jnp
from jax import lax
from jax.experimental import pallas as pl
from jax.experimental.pallas import tpu as pltpu

F32 = jnp.float32
BF16 = jnp.bfloat16
HI = lax.Precision.HIGHEST

HEAD_DIM = 64
N_HEADS = 12
MIX_W = N_HEADS * HEAD_DIM
N_MEM_HEADS = 4
MEM_W = N_MEM_HEADS * HEAD_DIM
N_KV_B = 4
KV_GROUP = N_HEADS // N_KV_B
GROUP_W = KV_GROUP * HEAD_DIM
CONV_W = 4
DN_CHUNK = 64
MOBA_BLOCK = 256
MOBA_TOPK = 3
ROT_DIM = HEAD_DIM // 4
ROPE_THETA = 500000.0
EPS = 1e-6
QK_SCALE = HEAD_DIM ** -0.5
LOG2E = math.log2(math.e)
ONES_ROWS = 16

LANES = 128
SUBLANES = 8
VMEM_BYTES_V7X = 64 * 2 ** 20
NEG_INF = float("-inf")

NT = (((1,), (1,)), ((), ()))
TN = (((0,), (0,)), ((), ()))


def _params(semantics, est_bytes):
    limit = min(max(int(est_bytes * 1.5), 16 * 2 ** 20), VMEM_BYTES_V7X - 8 * 2 ** 20)
    return pltpu.CompilerParams(dimension_semantics=semantics, vmem_limit_bytes=limit)


def _row_tile(m, pref):
    t = min(m, pref)
    while m % t or t % SUBLANES:
        t -= 1
    return t


def _rms(x, g):
    return x * lax.rsqrt(jnp.mean(x * x, axis=-1, keepdims=True) + EPS) * g


def _sigmoid(x):
    return 1.0 / (1.0 + jnp.exp(-x))


def _softplus(x):
    return jnp.maximum(x, 0.0) + jnp.log(1.0 + jnp.exp(-jnp.abs(x)))


def _bdot(a, b):
    return jnp.dot(a.astype(BF16), b.astype(BF16), preferred_element_type=F32)


def _bdot_nt(a, b):
    return lax.dot_general(a.astype(BF16), b.astype(BF16), NT, preferred_element_type=F32)


def _ffn_kernel(*refs, n_f, final):
    if final:
        x_ref, g_ref, w1_ref, w3_ref, w2_ref, fg_ref, o_ref, h_scr, acc_scr = refs
    else:
        x_ref, g_ref, w1_ref, w3_ref, w2_ref, o_ref, h_scr, acc_scr = refs
    f = pl.program_id(1)

    @pl.when(f == 0)
    def _():
        h_scr[...] = _rms(x_ref[...], g_ref[...]).astype(BF16)
        acc_scr[...] = jnp.zeros_like(acc_scr)

    h = h_scr[...]
    a = jnp.dot(h, w1_ref[...], preferred_element_type=F32)
    b = jnp.dot(h, w3_ref[...], preferred_element_type=F32)
    u = (a * _sigmoid(a)) * b
    acc_scr[...] += jnp.dot(u.astype(BF16), w2_ref[...], preferred_element_type=F32)

    @pl.when(f == n_f - 1)
    def _():
        y = x_ref[...] + 0.5 * acc_scr[...]
        if final:
            y = _rms(y, fg_ref[...])
        o_ref[...] = y


def _ffn(x, g, w1, w3, w2, layer, final_gain=None):
    m, d = x.shape
    d_ff = w1.shape[2]
    tm = _row_tile(m, 512)
    n_f = 2 if d_ff % (2 * LANES) == 0 else 1
    tf = d_ff // n_f
    final = final_gain is not None
    in_specs = [
        pl.BlockSpec((tm, d), lambda i, f: (i, 0)),
        pl.BlockSpec((1, d), lambda i, f: (0, 0)),
        pl.BlockSpec((None, d, tf), lambda i, f: (layer, 0, f)),
        pl.BlockSpec((None, d, tf), lambda i, f: (layer, 0, f)),
        pl.BlockSpec((None, tf, d), lambda i, f: (layer, f, 0)),
    ]
    args = [x, g.reshape(1, d), w1, w3, w2]
    if final:
        in_specs.append(pl.BlockSpec((1, d), lambda i, f: (0, 0)))
        args.append(final_gain.reshape(1, d))
    est = 4 * tm * d * 4 + 6 * d * tf * 2 + tm * d * 6 + 3 * tm * tf * 4
    return pl.pallas_call(
        functools.partial(_ffn_kernel, n_f=n_f, final=final),
        grid=(m // tm, n_f),
        in_specs=in_specs,
        out_specs=pl.BlockSpec((tm, d), lambda i, f: (i, 0)),
        out_shape=jax.ShapeDtypeStruct((m, d), F32),
        scratch_shapes=[pltpu.VMEM((tm, d), BF16), pltpu.VMEM((tm, d), F32)],
        compiler_params=_params(("parallel", "arbitrary"), est),
        name="ffn_final" if final else "ffn",
    )(*args)


def _norm_proj_kernel(x_ref, g_ref, w_ref, *out_refs, splits):
    h = _rms(x_ref[...], g_ref[...]).astype(BF16)
    for o_ref, (a, b) in zip(out_refs, splits):
        o_ref[...] = jnp.dot(h, w_ref[:, a:b], preferred_element_type=F32)


def _norm_proj(x, g, w, widths, name):
    m, d = x.shape
    n = w.shape[1]
    assert sum(widths) == n and all(c % LANES == 0 for c in widths)
    tm = _row_tile(m, 256)
    splits, a = [], 0
    for c in widths:
        splits.append((a, a + c))
        a += c
    est = 2 * tm * d * 4 + 2 * d * n * 2 + 3 * tm * n * 4
    return pl.pallas_call(
        functools.partial(_norm_proj_kernel, splits=tuple(splits)),
        grid=(m // tm,),
        in_specs=[
            pl.BlockSpec((tm, d), lambda i: (i, 0)),
            pl.BlockSpec((1, d), lambda i: (0, 0)),
            pl.BlockSpec((d, n), lambda i: (0, 0)),
        ],
        out_specs=[pl.BlockSpec((tm, c), lambda i: (i, 0)) for c in widths],
        out_shape=[jax.ShapeDtypeStruct((m, c), F32) for c in widths],
        compiler_params=_params(("parallel",), est),
        name=name,
    )(x, g.reshape(1, d), w)


def _rope_table_kernel(inv_ref, c_ref, s1_ref, s2_ref, *, period, base, tl):
    i = pl.program_id(0)
    row = lax.broadcasted_iota(jnp.int32, (tl, LANES), 0) + i * tl
    pos = base + lax.rem(row, period)
    ang = pos.astype(F32) * inv_ref[...]
    in_head = lax.broadcasted_iota(jnp.int32, (tl, LANES), 1) % HEAD_DIM
    half = ROT_DIM // 2
    c = jnp.cos(ang)
    s = jnp.sin(ang)
    c_ref[...] = jnp.where(in_head < ROT_DIM, c, 1.0)
    s1_ref[...] = jnp.where(in_head < half, -s, 0.0)
    s2_ref[...] = jnp.where((in_head >= half) & (in_head < ROT_DIM), s, 0.0)


def _rope_tables(n_rows, period, base):
    half = ROT_DIM // 2
    inv = ROPE_THETA ** (-jnp.arange(half, dtype=F32) * 2.0 / ROT_DIM)
    in_head = jnp.arange(LANES) % HEAD_DIM
    inv_lane = jnp.where(in_head < ROT_DIM, inv[in_head % half], 0.0).astype(F32).reshape(1, LANES)
    tl = _row_tile(n_rows, 1024)
    shp = jax.ShapeDtypeStruct((n_rows, LANES), F32)
    return pl.pallas_call(
        functools.partial(_rope_table_kernel, period=period, base=base, tl=tl),
        grid=(n_rows // tl,),
        in_specs=[pl.BlockSpec((1, LANES), lambda i: (0, 0))],
        out_specs=[pl.BlockSpec((tl, LANES), lambda i: (i, 0))] * 3,
        out_shape=[shp] * 3,
        compiler_params=_params(("parallel",), 16 * tl * LANES * 4),
        name="rope_tables",
    )(inv_lane)


def _apply_rope(t, c, s1, s2):
    w = t.shape[1]
    reps = w // LANES
    half = ROT_DIM // 2
    tile = lambda a: jnp.concatenate([a] * reps, axis=1) if reps > 1 else a
    up = pltpu.roll(t, w - half, 1)
    down = pltpu.roll(t, half, 1)
    return t * tile(c) + up * tile(s1) + down * tile(s2)


def _kv_proj_kernel(x_ref, g_ref, w_ref, c_ref, s1_ref, s2_ref, *out_refs, for_prompt):
    h = _rms(x_ref[...], g_ref[...]).astype(BF16)
    kv_w = N_KV_B * HEAD_DIM
    k = jnp.dot(h, w_ref[:, :kv_w], preferred_element_type=F32)
    v = jnp.dot(h, w_ref[:, kv_w:], preferred_element_type=F32)
    k = _apply_rope(k, c_ref[...], s1_ref[...], s2_ref[...])
    if for_prompt:
        kt_ref, vt_ref, kh_ref, vtb_ref, kbar_ref, v_scr = out_refs
        v_scr[...] = v
        v_t = v_scr[...].T
        kt_ref[0] = k.T
        vt_ref[0] = v_t
        for hk in range(N_KV_B):
            sl = slice(hk * HEAD_DIM, (hk + 1) * HEAD_DIM)
            k_h = k[:, sl]
            kh_ref[hk] = jnp.concatenate([k_h, jnp.zeros_like(k_h)], axis=1).astype(BF16)
            vtb_ref[0, hk, 0:HEAD_DIM, :] = v_t[sl, :].astype(BF16)
            vtb_ref[0, hk, HEAD_DIM:, :] = jnp.ones((ONES_ROWS, v_t.shape[1]), BF16)
        kbar_ref[0] = jnp.sum(k, axis=0, keepdims=True) * (1.0 / MOBA_BLOCK)
    else:
        out_refs[0][...] = k
        out_refs[1][...] = v


def _kv_proj(x, g, w, tables, seq_len, for_prompt):
    m, d = x.shape
    kv_w = N_KV_B * HEAD_DIM
    tm = MOBA_BLOCK if for_prompt else _row_tile(m, 256)
    n_tab = tables[0].shape[0] // tm
    row = lambda i: (i, 0)
    tab = lambda i: (i % n_tab, 0)
    if for_prompt:
        nb = seq_len // tm
        t_spec = pl.BlockSpec((1, kv_w, tm), lambda i: (i // nb, 0, i % nb))
        t_shape = jax.ShapeDtypeStruct((m // seq_len, kv_w, seq_len), F32)
        vt_rows = HEAD_DIM + ONES_ROWS
        out_specs = [t_spec, t_spec,
                     pl.BlockSpec((N_KV_B, tm, LANES), lambda i: (0, i, 0)),
                     pl.BlockSpec((1, N_KV_B, vt_rows, tm), lambda i: (i, 0, 0, 0)),
                     pl.BlockSpec((1, 1, kv_w), lambda i: (i, 0, 0))]
        out_shape = [t_shape, t_shape,
                     jax.ShapeDtypeStruct((N_KV_B, m, LANES), BF16),
                     jax.ShapeDtypeStruct((m // tm, N_KV_B, vt_rows, tm), BF16),
                     jax.ShapeDtypeStruct((m // tm, 1, kv_w), F32)]
    else:
        out_specs = [pl.BlockSpec((tm, kv_w), row), pl.BlockSpec((tm, kv_w), row)]
        out_shape = [jax.ShapeDtypeStruct((m, kv_w), F32)] * 2
    est = 2 * tm * d * 4 + 2 * d * 2 * kv_w * 2 + 12 * tm * kv_w * 4
    return pl.pallas_call(
        functools.partial(_kv_proj_kernel, for_prompt=for_prompt),
        grid=(m // tm,),
        in_specs=[
            pl.BlockSpec((tm, d), row),
            pl.BlockSpec((1, d), lambda i: (0, 0)),
            pl.BlockSpec((d, 2 * kv_w), lambda i: (0, 0)),
            pl.BlockSpec((tm, LANES), tab),
            pl.BlockSpec((tm, LANES), tab),
            pl.BlockSpec((tm, LANES), tab),
        ],
        out_specs=out_specs,
        out_shape=out_shape,
        scratch_shapes=[pltpu.VMEM((tm, kv_w), F32)] if for_prompt else [],
        compiler_params=_params(("parallel",), est),
        name="kv_proj",
    )(x, g.reshape(1, d), w, *tables)


def _q_proj_kernel(x_ref, g_ref, w_ref, c_ref, s1_ref, s2_ref, qh_ref, qm_ref):
    h = _rms(x_ref[...], g_ref[...]).astype(BF16)
    q = jnp.dot(h, w_ref[:, :MIX_W], preferred_element_type=F32)
    qm_ref[...] = jnp.dot(h, w_ref[:, MIX_W:], preferred_element_type=F32)
    q = _apply_rope(q, c_ref[...], s1_ref[...], s2_ref[...]) * (QK_SCALE * LOG2E)
    for hk in range(N_KV_B):
        qh_ref[hk] = q[:, hk * GROUP_W:(hk + 1) * GROUP_W]


def _q_proj(x, g, w, tables):
    m, d = x.shape
    n = w.shape[1]
    tm = _row_tile(m, 256)
    n_tab = tables[0].shape[0] // tm
    row = lambda i: (i, 0)
    tab = lambda i: (i % n_tab, 0)
    est = 2 * tm * d * 4 + 2 * d * n * 2 + 16 * tm * MIX_W * 4
    return pl.pallas_call(
        _q_proj_kernel,
        grid=(m // tm,),
        in_specs=[
            pl.BlockSpec((tm, d), row),
            pl.BlockSpec((1, d), lambda i: (0, 0)),
            pl.BlockSpec((d, n), lambda i: (0, 0)),
            pl.BlockSpec((tm, LANES), tab),
            pl.BlockSpec((tm, LANES), tab),
            pl.BlockSpec((tm, LANES), tab),
        ],
        out_specs=[
            pl.BlockSpec((N_KV_B, tm, GROUP_W), lambda i: (0, i, 0)),
            pl.BlockSpec((tm, MEM_W), row),
        ],
        out_shape=[
            jax.ShapeDtypeStruct((N_KV_B, m, GROUP_W), F32),
            jax.ShapeDtypeStruct((m, MEM_W), F32),
        ],
        compiler_params=_params(("parallel",), est),
        name="q_proj",
    )(x, g.reshape(1, d), w, *tables)


def _conv_kernel(u_ref, prev_ref, buf_ref, w_ref, y_ref, nb_ref, pad_scr, *, tl, n_t):
    i = pl.program_id(1)
    halo = CONV_W - 1
    lo = SUBLANES - halo
    pad_scr[SUBLANES:SUBLANES + tl, :] = u_ref[0]

    @pl.when(i == 0)
    def _():
        pad_scr[lo:SUBLANES, :] = buf_ref[0]

    @pl.when(i > 0)
    def _():
        p = prev_ref[0]
        pad_scr[lo:SUBLANES, :] = p[p.shape[0] - halo:, :]

    y = pad_scr[lo:lo + tl, :] * w_ref[0:1, :]
    for j in range(1, CONV_W):
        y = y + pad_scr[lo + j:lo + j + tl, :] * w_ref[j:j + 1, :]
    y_ref[0] = y * _sigmoid(y)

    @pl.when(i == n_t - 1)
    def _():
        nb_ref[0] = pad_scr[lo + tl:SUBLANES + tl, :]


def _conv(u, buf, w):
    b, l, c = u.shape
    tl = _row_tile(l, 512) if l % SUBLANES == 0 else l
    n_t = l // tl
    pr = min(SUBLANES, l)
    per = tl // pr
    est = 6 * tl * c * 4
    return pl.pallas_call(
        functools.partial(_conv_kernel, tl=tl, n_t=n_t),
        grid=(b, n_t),
        in_specs=[
            pl.BlockSpec((1, tl, c), lambda bi, i: (bi, i, 0)),
            pl.BlockSpec((1, pr, c), lambda bi, i: (bi, jnp.maximum(i * per - 1, 0), 0)),
            pl.BlockSpec((1, CONV_W - 1, c), lambda bi, i: (bi, 0, 0)),
            pl.BlockSpec((CONV_W, c), lambda bi, i: (0, 0)),
        ],
        out_specs=[
            pl.BlockSpec((1, tl, c), lambda bi, i: (bi, i, 0)),
            pl.BlockSpec((1, CONV_W - 1, c), lambda bi, i: (bi, 0, 0)),
        ],
        out_shape=[
            jax.ShapeDtypeStruct((b, l, c), F32),
            jax.ShapeDtypeStruct((b, CONV_W - 1, c), F32),
        ],
        scratch_shapes=[pltpu.VMEM((tl + SUBLANES, c), F32)],
        compiler_params=_params(("parallel", "arbitrary"), est),
        name="short_conv",
    )(u, u, buf, w)


IN_PROJ_TILE = 256


def _in_proj_conv_kernel(x_ref, g_ref, w_ref, buf_ref, cw_ref, qkv_ref, gate_ref, qm_ref, ab_ref,
                         nb_ref, pad_scr, *, tm, n_t):
    i = pl.program_id(1)
    halo = CONV_W - 1
    lo = SUBLANES - halo
    c3 = 3 * MIX_W
    h = _rms(x_ref[...], g_ref[...]).astype(BF16)

    @pl.when(i == 0)
    def _():
        pad_scr[lo:SUBLANES, :] = buf_ref[0]

    pad_scr[SUBLANES:SUBLANES + tm, :] = jnp.dot(h, w_ref[:, :c3], preferred_element_type=F32)
    gate_ref[...] = jnp.dot(h, w_ref[:, c3:c3 + MIX_W], preferred_element_type=F32)
    qm_ref[...] = jnp.dot(h, w_ref[:, c3 + MIX_W:c3 + MIX_W + MEM_W], preferred_element_type=F32)
    ab_ref[...] = jnp.dot(h, w_ref[:, c3 + MIX_W + MEM_W:], preferred_element_type=F32)
    y = pad_scr[lo:lo + tm, :] * cw_ref[0:1, :]
    for j in range(1, CONV_W):
        y = y + pad_scr[lo + j:lo + j + tm, :] * cw_ref[j:j + 1, :]
    qkv_ref[...] = y * _sigmoid(y)
    tail = pad_scr[lo + tm:SUBLANES + tm, :]

    @pl.when(i == n_t - 1)
    def _():
        nb_ref[0] = tail

    pad_scr[lo:SUBLANES, :] = tail


def _in_proj_conv(x, g, w, buf, conv_w, seq_len):
    m, d = x.shape
    n = w.shape[1]
    c3 = 3 * MIX_W
    tm = IN_PROJ_TILE
    n_t = seq_len // tm
    b = m // seq_len
    row = lambda bi, i: (bi * n_t + i, 0)
    const = lambda bi, i: (0, 0)
    widths = (c3, MIX_W, MEM_W, n - c3 - MIX_W - MEM_W)
    est = 2 * tm * d * 4 + 2 * d * n * 2 + 4 * tm * n * 4
    return pl.pallas_call(
        functools.partial(_in_proj_conv_kernel, tm=tm, n_t=n_t),
        grid=(b, n_t),
        in_specs=[
            pl.BlockSpec((tm, d), row),
            pl.BlockSpec((1, d), const),
            pl.BlockSpec((d, n), const),
            pl.BlockSpec((1, CONV_W - 1, c3), lambda bi, i: (bi, 0, 0)),
            pl.BlockSpec((CONV_W, c3), const),
        ],
        out_specs=[pl.BlockSpec((tm, c), row) for c in widths]
        + [pl.BlockSpec((1, CONV_W - 1, c3), lambda bi, i: (bi, 0, 0))],
        out_shape=[jax.ShapeDtypeStruct((m, c), F32) for c in widths]
        + [jax.ShapeDtypeStruct((b, CONV_W - 1, c3), F32)],
        scratch_shapes=[pltpu.VMEM((tm + SUBLANES, c3), F32)],
        compiler_params=_params(("parallel", "arbitrary"), est),
        name="in_proj_conv",
    )(x, g.reshape(1, d), w, buf, conv_w)


HEADS_PER_SLAB = LANES * 2 // HEAD_DIM
SLAB_W = HEADS_PER_SLAB * HEAD_DIM
N_SLABS = N_HEADS // HEADS_PER_SLAB


def _split2(x):
    hi = x.astype(BF16)
    return hi, (x - hi.astype(F32)).astype(BF16)


def _split3(x):
    hi = x.astype(BF16)
    r = x - hi.astype(F32)
    mid = r.astype(BF16)
    return hi, mid, (r - mid.astype(F32)).astype(BF16)


def _dot_right01(x, sel):
    hi, mid, lo = _split3(x)
    d = lambda a: jnp.dot(a, sel, preferred_element_type=F32)
    return d(hi) + (d(mid) + d(lo))


def _dot_left01(sel, x):
    hi, mid, lo = _split3(x)
    d = lambda a: jnp.dot(sel, a, preferred_element_type=F32)
    return d(hi) + (d(mid) + d(lo))


def _dot3(a, b_hi, b_lo):
    a_hi, a_lo = _split2(a)
    d = lambda x, y: jnp.dot(x, y, preferred_element_type=F32)
    return d(a_hi, b_hi) + (d(a_hi, b_lo) + d(a_lo, b_hi))


def _block_diag(x, mask):
    return jnp.concatenate([x] * HEADS_PER_SLAB, axis=0) * mask


def _block_diag_pieces(x, mask):
    hi, lo = _split2(x)
    return _block_diag(hi, mask), _block_diag(lo, mask)


def _unit_lower_inverses(ms, eye_t, mask):
    c = ms[0].shape[0]
    xs = [eye_t - m for m in ms]
    ps = [_dot3(m, *_block_diag_pieces(m, mask)) for m in ms]
    power = 2
    while 2 * power < c:
        xps = [_dot3(jnp.concatenate([x, p], axis=0), *_block_diag_pieces(p, mask))
               for x, p in zip(xs, ps)]
        xs = [x + xp[:c] for x, xp in zip(xs, xps)]
        ps = [xp[c:] for xp in xps]
        power *= 2
    return [x + _dot3(x, *_block_diag_pieces(p, mask)) for x, p in zip(xs, ps)]


def _gdn_kernel(q_ref, k_ref, v_ref, gate_ref, ab_ref, alog_ref, dt_ref, gain_ref, s0_ref,
                o_ref, sfin_ref,
                s_scr, qn_scr, kn_scr, gcb_scr, bb_scr, u_scr, w_scr, a_scr, qg_scr, kd_scr, egl_scr,
                *, t_rows, n_t, l_valid):
    t = pl.program_id(1)
    c = DN_CHUNK
    d = HEAD_DIM
    n_chunks = t_rows // c

    r_s = lax.broadcasted_iota(jnp.int32, (SLAB_W, SLAB_W), 0)
    c_s = lax.broadcasted_iota(jnp.int32, (SLAB_W, SLAB_W), 1)
    same_head = (r_s // d) == (c_s // d)
    bd_f32 = jnp.where(same_head, 1.0, 0.0).astype(F32)
    bd_mask = bd_f32.astype(BF16)
    r_t = lax.broadcasted_iota(jnp.int32, (c, SLAB_W), 0)
    c_t = lax.broadcasted_iota(jnp.int32, (c, SLAB_W), 1) % d
    lower_t = r_t >= c_t
    strict_t = r_t > c_t
    eye_t = jnp.where(r_t == c_t, 1.0, 0.0).astype(F32)
    r_c = lax.broadcasted_iota(jnp.int32, (c, c), 0)
    c_c = lax.broadcasted_iota(jnp.int32, (c, c), 1)
    tril = jnp.where(r_c >= c_c, 1.0, 0.0).astype(BF16)
    er = lax.broadcasted_iota(jnp.int32, (LANES, MIX_W), 0)
    ec = lax.broadcasted_iota(jnp.int32, (LANES, MIX_W), 1) // d
    sel_g = jnp.where(er == ec, 1.0, 0.0).astype(BF16)
    sel_b = jnp.where(er == ec + N_HEADS, 1.0, 0.0).astype(BF16)

    @pl.when(t == 0)
    def _():
        for s in range(N_SLABS):
            rows = jnp.concatenate([s0_ref[0, s * HEADS_PER_SLAB + h] for h in range(HEADS_PER_SLAB)],
                                   axis=0)
            s_scr[s] = jnp.concatenate([rows] * HEADS_PER_SLAB, axis=1) * bd_f32

    ab = ab_ref[0]
    row = lax.broadcasted_iota(jnp.int32, ab.shape, 0) + t * t_rows
    valid = row < l_valid
    g = jnp.where(valid, -jnp.exp(alog_ref[...]) * _softplus(ab + dt_ref[...]), 0.0)
    beta = jnp.where(valid, _sigmoid(ab), 0.0)
    gc = jnp.concatenate([_dot_left01(tril, g[i * c:(i + 1) * c]) for i in range(n_chunks)], axis=0)
    gcb_scr[...] = _dot_right01(gc, sel_g)
    bb_scr[...] = _dot_right01(beta, sel_b)
    for s in range(N_SLABS):
        cs = slice(s * SLAB_W, (s + 1) * SLAB_W)
        for src, dst, scale in ((q_ref, qn_scr, QK_SCALE), (k_ref, kn_scr, 1.0)):
            x = src[0, :, cs]
            hi, lo = _split2(x * x)
            ssq = (jnp.dot(hi, bd_mask, preferred_element_type=F32)
                   + jnp.dot(lo, bd_mask, preferred_element_type=F32))
            dst[:, cs] = x * (lax.rsqrt(ssq + EPS) * scale)

    per_iter = 2 if n_chunks % 2 == 0 else 1
    slabs = [slice(s * SLAB_W, (s + 1) * SLAB_W) for s in range(N_SLABS)]

    def factors(it, carry):
        items = []
        for u in range(per_iter):
            ci = it * per_iter + u
            rows = pl.ds(pl.multiple_of(ci * c, c), c)
            items += [(ci, rows, cs) for cs in slabs]
        pre = []
        for ci, rows, cs in items:
            gcb = gcb_scr[rows, cs]
            kn = kn_scr[rows, cs]
            qn = qn_scr[rows, cs]
            b_ = bb_scr[rows, cs]
            kb = kn * b_
            g_row = jnp.sum(gcb * eye_t, axis=0, keepdims=True)
            dec = jnp.exp(jnp.where(lower_t, gcb - g_row, NEG_INF))
            both = lax.dot_general(jnp.concatenate([kb, qn], axis=0).astype(BF16),
                                   _block_diag(kn.astype(BF16), bd_mask), NT,
                                   preferred_element_type=F32)
            pre.append((gcb, kn, qn, kb, dec, both))
        ms = []
        for (ci, rows, cs), (gcb, kn, qn, kb, dec, both) in zip(items, pre):
            ms.append(jnp.where(strict_t, both[:c] * dec, 0.0))
            a_scr[rows, cs] = (both[c:] * dec).astype(BF16)
        tinvs = _unit_lower_inverses(ms, eye_t, bd_mask)
        sols = []
        for (ci, rows, cs), (gcb, kn, qn, kb, dec, both), tinv in zip(items, pre, tinvs):
            eg = jnp.exp(gcb)
            v_hi, v_lo = _block_diag_pieces(v_ref[0, rows, cs] * bb_scr[rows, cs], bd_mask)
            k_hi, k_lo = _block_diag_pieces(kb * eg, bd_mask)
            sols.append(_dot3(tinv, jnp.concatenate([v_hi, k_hi], axis=1),
                              jnp.concatenate([v_lo, k_lo], axis=1)))
            glast = gcb[c - 1:c, :]
            qg_scr[rows, cs] = (qn * eg).astype(BF16)
            kd_scr[rows, cs] = (kn * jnp.exp(glast - gcb)).astype(BF16)
            egl_scr[pl.ds(ci, 1), cs] = jnp.exp(glast)
        for (ci, rows, cs), sol in zip(items, sols):
            u_scr[rows, cs] = sol[:, :SLAB_W]
            w_scr[rows, cs] = sol[:, SLAB_W:].astype(BF16)
        return carry

    lax.fori_loop(0, n_chunks // per_iter, factors, 0)

    gain = gain_ref[...]

    def recur(ci, carry):
        rows = pl.ds(pl.multiple_of(ci * c, c), c)
        olds = [s_scr[s] for s in range(N_SLABS)]
        rs = [jnp.dot(jnp.concatenate([w_scr[rows, cs], qg_scr[rows, cs]], axis=0),
                      s_old.astype(BF16), preferred_element_type=F32)
              for cs, s_old in zip(slabs, olds)]
        v_news = [(u_scr[rows, cs] - r[:c]).astype(BF16) for cs, r in zip(slabs, rs)]
        upds = [lax.dot_general(kd_scr[rows, cs], v_new, TN, preferred_element_type=F32)
                for cs, v_new in zip(slabs, v_news)]
        os_ = [r[c:] + jnp.dot(a_scr[rows, cs], _block_diag(v_new, bd_mask),
                               preferred_element_type=F32)
               for cs, r, v_new in zip(slabs, rs, v_news)]
        for s, (cs, s_old, upd) in enumerate(zip(slabs, olds, upds)):
            s_scr[s] = (s_old * egl_scr[pl.ds(ci, 1), cs] + upd) * bd_f32
        ssqs = []
        for o in os_:
            hi, lo = _split2(o * o)
            ssqs.append(jnp.dot(hi, bd_mask, preferred_element_type=F32)
                        + jnp.dot(lo, bd_mask, preferred_element_type=F32))
        for cs, o, ssq in zip(slabs, os_, ssqs):
            gt = gate_ref[0, rows, cs]
            o_ref[0, rows, cs] = (o * lax.rsqrt(ssq * (1.0 / d) + EPS) * gain[:, cs]
                                  * (gt * _sigmoid(gt)))
        return carry

    lax.fori_loop(0, n_chunks, recur, 0)

    @pl.when(t == n_t - 1)
    def _():
        for s in range(N_SLABS):
            full = s_scr[s]
            for h in range(HEADS_PER_SLAB):
                sfin_ref[0, s * HEADS_PER_SLAB + h] = full[h * d:(h + 1) * d, h * d:(h + 1) * d]


def _gdn(qkv, gate, ab, a_log, dt_bias, out_gain, s0, l_valid):
    b, lp, _ = qkv.shape
    assert lp % DN_CHUNK == 0
    t_rows = DN_CHUNK * math.gcd(lp // DN_CHUNK, 8)
    n_t = lp // t_rows
    n_chunks = t_rows // DN_CHUNK
    pad = lambda a: jnp.pad(a.astype(F32), (0, LANES - N_HEADS)).reshape(1, LANES)
    blk = lambda j: pl.BlockSpec((1, t_rows, MIX_W), lambda bi, t, j=j: (bi, t, j))
    state = pl.BlockSpec((1, N_HEADS, HEAD_DIM, HEAD_DIM), lambda bi, t: (bi, 0, 0, 0))
    vec = pl.BlockSpec((1, LANES), lambda bi, t: (0, 0))
    wide_f32 = pltpu.VMEM((t_rows, MIX_W), F32)
    wide_bf16 = pltpu.VMEM((t_rows, MIX_W), BF16)
    est = 10 * t_rows * MIX_W * 4 + 5 * t_rows * MIX_W * 4 + 4 * t_rows * MIX_W * 2 + 64 * SLAB_W * SLAB_W * 4
    return pl.pallas_call(
        functools.partial(_gdn_kernel, t_rows=t_rows, n_t=n_t, l_valid=l_valid),
        grid=(b, n_t),
        in_specs=[
            blk(0), blk(1), blk(2),
            pl.BlockSpec((1, t_rows, MIX_W), lambda bi, t: (bi, t, 0)),
            pl.BlockSpec((1, t_rows, LANES), lambda bi, t: (bi, t, 0)),
            vec, vec,
            pl.BlockSpec((1, MIX_W), lambda bi, t: (0, 0)),
            state,
        ],
        out_specs=[pl.BlockSpec((1, t_rows, MIX_W), lambda bi, t: (bi, t, 0)), state],
        out_shape=[
            jax.ShapeDtypeStruct((b, lp, MIX_W), F32),
            jax.ShapeDtypeStruct((b, N_HEADS, HEAD_DIM, HEAD_DIM), F32),
        ],
        scratch_shapes=[
            pltpu.VMEM((N_SLABS, SLAB_W, SLAB_W), F32),
            wide_f32, wide_f32,
            wide_f32, wide_f32,
            wide_f32,
            wide_bf16, wide_bf16, wide_bf16, wide_bf16,
            pltpu.VMEM((max(n_chunks, SUBLANES), MIX_W), F32),
        ],
        compiler_params=_params(("parallel", "arbitrary"), est),
        name="gated_delta_rule",
    )(qkv, qkv, qkv, gate, ab, pad(a_log), pad(dt_bias),
      jnp.tile(out_gain.astype(F32), N_HEADS).reshape(1, MIX_W), s0)


def _mem_kv_kernel(x_ref, g_ref, w_ref, kt_ref, vt_ref, kv_scr):
    h = _rms(x_ref[...], g_ref[...]).astype(BF16)
    kv_scr[...] = jnp.dot(h, w_ref[...], preferred_element_type=F32)
    kt_ref[0] = kv_scr[:, :MEM_W].T
    vt_ref[0] = kv_scr[:, MEM_W:].T


def _mem_kv(mem, g, w):
    b, n_mem, d = mem.shape
    shp = jax.ShapeDtypeStruct((b, MEM_W, n_mem), F32)
    spec = pl.BlockSpec((1, MEM_W, n_mem), lambda i: (i, 0, 0))
    return pl.pallas_call(
        _mem_kv_kernel,
        grid=(b,),
        in_specs=[
            pl.BlockSpec((n_mem, d), lambda i: (i, 0)),
            pl.BlockSpec((1, d), lambda i: (0, 0)),
            pl.BlockSpec((d, 2 * MEM_W), lambda i: (0, 0)),
        ],
        out_specs=[spec, spec],
        out_shape=[shp, shp],
        scratch_shapes=[pltpu.VMEM((n_mem, 2 * MEM_W), F32)],
        compiler_params=_params(("parallel",), 2 * n_mem * d * 4 + 2 * d * 2 * MEM_W * 2 + 8 * n_mem * MEM_W * 4),
        name="mem_kv",
    )(mem.reshape(b * n_mem, d), g.reshape(1, d), w)


def _mem_attn_kernel(q_ref, kt_ref, vt_ref, o_ref):
    q = q_ref[0]
    kt = kt_ref[0]
    vt = vt_ref[0]
    outs = []
    for h in range(N_MEM_HEADS):
        sl = slice(h * HEAD_DIM, (h + 1) * HEAD_DIM)
        s = _bdot(q[:, sl], kt[sl, :]) * QK_SCALE
        p = jnp.exp(s - jnp.max(s, axis=-1, keepdims=True))
        outs.append(_bdot_nt(p, vt[sl, :]) / jnp.sum(p, axis=-1, keepdims=True))
    o_ref[0] = jnp.concatenate(outs, axis=1)


def _mem_attn(q, mk_t, mv_t):
    b, l, w = q.shape
    n_mem = mk_t.shape[2]
    tl = _row_tile(l, 512)
    est = 4 * tl * w * 4 + 4 * n_mem * w * 4 + 8 * tl * n_mem * 4
    return pl.pallas_call(
        _mem_attn_kernel,
        grid=(b, l // tl),
        in_specs=[
            pl.BlockSpec((1, tl, w), lambda bi, i: (bi, i, 0)),
            pl.BlockSpec((1, w, n_mem), lambda bi, i: (bi, 0, 0)),
            pl.BlockSpec((1, w, n_mem), lambda bi, i: (bi, 0, 0)),
        ],
        out_specs=pl.BlockSpec((1, tl, w), lambda bi, i: (bi, i, 0)),
        out_shape=jax.ShapeDtypeStruct((b, l, w), F32),
        compiler_params=_params(("parallel", "parallel"), est),
        name="mem_attn",
    )(q, mk_t, mv_t)


def _out_proj_kernel(x_ref, o_ref, mo_ref, wo_ref, wm_ref, y_ref, *, grouped):
    y = x_ref[...] + _bdot(mo_ref[...], wm_ref[...])
    if grouped:
        for hk in range(N_KV_B):
            y = y + _bdot(o_ref[hk], wo_ref[hk])
    else:
        y = y + _bdot(o_ref[...], wo_ref[...])
    y_ref[...] = y


def _out_proj(x, o, mo, w_out, grouped):
    m, d = x.shape
    tm = _row_tile(m, 512)
    row = lambda i: (i, 0)
    if grouped:
        wo = w_out[:MIX_W].reshape(N_KV_B, GROUP_W, d)
        o_spec = pl.BlockSpec((N_KV_B, tm, GROUP_W), lambda i: (0, i, 0))
        wo_spec = pl.BlockSpec((N_KV_B, GROUP_W, d), lambda i: (0, 0, 0))
    else:
        wo = w_out[:MIX_W]
        o_spec = pl.BlockSpec((tm, MIX_W), row)
        wo_spec = pl.BlockSpec((MIX_W, d), lambda i: (0, 0))
    est = 6 * tm * d * 4 + 4 * tm * d * 4 + 4 * d * d * 2
    return pl.pallas_call(
        functools.partial(_out_proj_kernel, grouped=grouped),
        grid=(m // tm,),
        in_specs=[
            pl.BlockSpec((tm, d), row),
            o_spec,
            pl.BlockSpec((tm, MEM_W), row),
            wo_spec,
            pl.BlockSpec((MEM_W, d), lambda i: (0, 0)),
        ],
        out_specs=pl.BlockSpec((tm, d), row),
        out_shape=jax.ShapeDtypeStruct((m, d), F32),
        compiler_params=_params(("parallel",), est),
        name="out_proj",
    )(x, o, mo, wo, w_out[MIX_W:])


def _select_blocks(gate, n_past, k_sel):
    lane = lax.broadcasted_iota(jnp.int32, gate.shape, 1)
    lane_f = lane.astype(F32)
    gate = jnp.where(lane < n_past, gate, NEG_INF)
    sel = jnp.zeros(gate.shape, F32)
    for _ in range(k_sel):
        mx = jnp.max(gate, axis=1, keepdims=True)
        first = jnp.min(jnp.where(gate == mx, lane_f, float(gate.shape[1])), axis=1, keepdims=True)
        pick = (lane_f == first) & (mx > NEG_INF)
        sel = jnp.where(pick, 1.0, sel)
        gate = jnp.where(pick, NEG_INF, gate)
    return sel


def _select_blocks_t(gate, n_past, k_sel):
    blk = lax.broadcasted_iota(jnp.int32, gate.shape, 0)
    blk_f = blk.astype(F32)
    gate = jnp.where(blk < n_past, gate, NEG_INF)
    sel = jnp.zeros(gate.shape, F32)
    for _ in range(k_sel):
        mx = jnp.max(gate, axis=0, keepdims=True)
        first = jnp.min(jnp.where(gate == mx, blk_f, float(gate.shape[0])), axis=0, keepdims=True)
        pick = (blk_f == first) & (mx > NEG_INF)
        sel = jnp.where(pick, 1.0, sel)
        gate = jnp.where(pick, NEG_INF, gate)
    return sel


def _moba_prompt_kernel(q_ref, k_ref, vt_ref, kbar_ref, o_ref, q_scr, m_scr, acc_scr, sel_scr,
                        s_scr, p_scr, alpha_scr, *, k_sel):
    i = pl.program_id(2)
    blk = MOBA_BLOCK
    d = HEAD_DIM
    cw = LANES
    n_chunks = KV_GROUP * blk // cw
    n_blocks = sel_scr.shape[0]
    chunks = [slice(c * cw, (c + 1) * cw) for c in range(n_chunks)]
    q = q_ref[0]
    q3 = jnp.concatenate([q[:, g * d:(g + 1) * d] for g in range(KV_GROUP)], axis=0)
    q_scr[...] = jnp.concatenate([q3, jnp.zeros_like(q3)], axis=1).astype(BF16)

    def block_scores(k_blk, c):
        return lax.dot_general(k_blk, q_scr[c * cw:(c + 1) * cw, :], NT,
                               preferred_element_type=F32)

    k_own = k_ref[0, pl.ds(pl.multiple_of(i * blk, blk), blk), :]
    own_scores = [block_scores(k_own, c) for c in range(n_chunks)]
    k_first = k_ref[0, 0:blk, :]
    first_scores = [block_scores(k_first, c) for c in range(n_chunks)]
    gate_t = lax.dot_general(kbar_ref[0, 0], q3, NT, precision=HI, preferred_element_type=F32)
    for c, cs in enumerate(chunks):
        s_scr[0, :, cs] = first_scores[c]
    for c, cs in enumerate(chunks):
        kpos = lax.broadcasted_iota(jnp.int32, (blk, cw), 0)
        qpos = (lax.broadcasted_iota(jnp.int32, (blk, cw), 1) + c * cw) % blk
        s = jnp.where(kpos <= qpos, own_scores[c], NEG_INF)
        m0 = jnp.max(s, axis=0, keepdims=True)
        m_scr[:, cs] = m0
        p_scr[1, :, cs] = jnp.exp2(s - m0).astype(BF16)
    acc_scr[...] = jnp.zeros(acc_scr.shape, F32)
    alpha_scr[...] = jnp.ones(alpha_scr.shape, F32)
    sel_scr[...] = _select_blocks_t(gate_t, i, k_sel)

    def pv_update(j_prev, slot):
        vt_prev = vt_ref[j_prev, 0]
        pv = [jnp.dot(vt_prev, p_scr[slot, :, cs], preferred_element_type=F32) for cs in chunks]
        for cs, r in zip(chunks, pv):
            acc_scr[:, cs] = alpha_scr[:, cs] * acc_scr[:, cs] + r

    def step(j, cur):
        nxt = 1 - cur
        j_next = jnp.minimum(j + 1, n_blocks - 1)
        k_next = k_ref[0, pl.ds(pl.multiple_of(j_next * blk, blk), blk), :]
        for c, cs in enumerate(chunks):
            s_scr[nxt, :, cs] = block_scores(k_next, c)
        pv_update(jnp.where(j == 0, i, jnp.minimum(j - 1, n_blocks - 1)), nxt)
        chosen = (sel_scr[pl.ds(jnp.minimum(j, n_blocks - 1), 1), :] > 0.0) & (j < i)
        for c, cs in enumerate(chunks):
            s = s_scr[cur, :, cs]
            m_old = m_scr[:, cs]
            m_new = jnp.maximum(m_old, jnp.where(chosen[:, cs], jnp.max(s, axis=0, keepdims=True),
                                                 NEG_INF))
            p_scr[cur, :, cs] = jnp.exp2(s - jnp.where(chosen[:, cs], m_new, float("inf"))).astype(BF16)
            alpha_scr[:, cs] = jnp.exp2(m_old - m_new)
            m_scr[:, cs] = m_new

    per_trip = 4

    def trip(t, carry):
        for h in range(per_trip):
            step(per_trip * t + h, h % 2)
        return carry

    n_trips = (i + per_trip - 1) // per_trip
    lax.fori_loop(0, n_trips, trip, 0)
    pv_update(jnp.where(n_trips == 0, i, jnp.minimum(per_trip * n_trips - 1, n_blocks - 1)), 1)
    acc = acc_scr[...]
    o_t = (acc[:d] / acc[d:d + 1]).astype(BF16)
    eye = (lax.broadcasted_iota(jnp.int32, (blk, blk), 0)
           == lax.broadcasted_iota(jnp.int32, (blk, blk), 1)).astype(BF16)
    o_ref[0] = jnp.concatenate(
        [lax.dot_general(eye, o_t[:, g * blk:(g + 1) * blk], NT, preferred_element_type=F32)
         for g in range(KV_GROUP)], axis=1)


def _moba_prompt(qh, kh, vt, kbar, b, l):
    nb = l // MOBA_BLOCK
    rows = KV_GROUP * MOBA_BLOCK
    k_sel = min(MOBA_TOPK, nb)
    vt_rows = vt.shape[2]
    est = 4 * l * LANES * 2 + 4 * l * vt_rows * 2 + 8 * MOBA_BLOCK * GROUP_W * 4 + 24 * rows * LANES * 4
    return pl.pallas_call(
        functools.partial(_moba_prompt_kernel, k_sel=k_sel),
        grid=(N_KV_B, b, nb),
        in_specs=[
            pl.BlockSpec((1, MOBA_BLOCK, GROUP_W), lambda hk, bi, i: (hk, bi * nb + i, 0)),
            pl.BlockSpec((1, l, LANES), lambda hk, bi, i: (hk, bi, 0)),
            pl.BlockSpec((nb, 1, vt_rows, MOBA_BLOCK), lambda hk, bi, i: (bi, hk, 0, 0)),
            pl.BlockSpec((1, 1, nb, HEAD_DIM), lambda hk, bi, i: (hk, bi, 0, 0)),
        ],
        out_specs=pl.BlockSpec((1, MOBA_BLOCK, GROUP_W), lambda hk, bi, i: (hk, bi * nb + i, 0)),
        out_shape=jax.ShapeDtypeStruct((N_KV_B, b * l, GROUP_W), F32),
        scratch_shapes=[
            pltpu.VMEM((rows, LANES), BF16),
            pltpu.VMEM((1, rows), F32),
            pltpu.VMEM((vt_rows, rows), F32),
            pltpu.VMEM((nb, rows), F32),
            pltpu.VMEM((2, MOBA_BLOCK, rows), F32),
            pltpu.VMEM((2, MOBA_BLOCK, rows), BF16),
            pltpu.VMEM((1, rows), F32),
        ],
        compiler_params=_params(("parallel", "parallel", "arbitrary"), est),
        name="moba_prompt",
    )(qh, kh, vt, kbar)


def _moba_sample_kernel(pt_ref, q_ref, kn_ref, vn_ref, kt_hbm, vt_hbm, o_ref,
                        kbuf, vbuf, sem, kbar_scr, s_scr, p_scr,
                        *, n_seq, n_pages, l_new, k_sel):
    b = pl.program_id(0)
    slot = lax.rem(b, 2)
    nb = n_pages // 2
    page = kbuf.shape[-1]
    unroll = math.gcd(nb, 4)

    def page_copy(hbm, buf, sl, which, pg, p):
        return pltpu.make_async_copy(hbm.at[pg], buf.at[sl, p], sem.at[sl, which])

    def fetch(seq, sl):
        def body(p, carry):
            pg = pt_ref[seq, p]
            page_copy(kt_hbm, kbuf, sl, 0, pg, p).start()
            page_copy(vt_hbm, vbuf, sl, 1, pg, p).start()
            return carry
        lax.fori_loop(0, n_pages, body, 0)

    def wait_pages(hbm, buf, which):
        def body(p, carry):
            page_copy(hbm, buf, slot, which, 0, p).wait()
            return carry
        lax.fori_loop(0, n_pages, body, 0)

    @pl.when(b == 0)
    def _():
        fetch(0, 0)

    @pl.when(b + 1 < n_seq)
    def _():
        fetch(b + 1, 1 - slot)

    q = q_ref[0]
    qb = q.astype(BF16)
    rows = q.shape[0]
    wait_pages(kt_hbm, kbuf, 0)

    ones = jnp.ones((SUBLANES, page), BF16)

    def mean_body(j, carry):
        t = kbuf[slot, 2 * j] + kbuf[slot, 2 * j + 1]
        hi, mid, lo = _split3(t)
        d = lambda a: lax.dot_general(ones, a, NT, preferred_element_type=F32)
        kbar_scr[pl.ds(j, 1), :] = (d(hi) + (d(mid) + d(lo)))[0:1] * (1.0 / MOBA_BLOCK)
        return carry

    lax.fori_loop(0, nb, mean_body, 0, unroll=unroll)
    gate = lax.dot_general(q, kbar_scr[...], NT, precision=HI, preferred_element_type=F32)
    sel = _select_blocks(gate, nb, k_sel)
    lane = lax.broadcasted_iota(jnp.int32, sel.shape, 1)

    def score_body(j, carry):
        chosen = jnp.max(jnp.where(lane == j, sel, 0.0), axis=1, keepdims=True)
        s = jnp.concatenate(
            [jnp.dot(qb, kbuf[slot, 2 * j + h].astype(BF16), preferred_element_type=F32)
             for h in range(2)], axis=1)
        s = jnp.where(chosen > 0.0, s, NEG_INF)
        s_scr[j] = s
        return jnp.maximum(carry, s)

    s_max = lax.fori_loop(0, nb, score_body, jnp.full((rows, 2 * page), NEG_INF, F32), unroll=unroll)
    s_new = _bdot_nt(qb, kn_ref[0])
    t_q = lax.broadcasted_iota(jnp.int32, s_new.shape, 0) % l_new
    t_k = lax.broadcasted_iota(jnp.int32, s_new.shape, 1)
    s_new = jnp.where(t_k <= t_q, s_new, NEG_INF)
    m = jnp.maximum(jnp.max(s_max, axis=1, keepdims=True), jnp.max(s_new, axis=1, keepdims=True))
    p_new = jnp.exp2(s_new - m)

    def prob_body(j, carry):
        p = jnp.exp2(s_scr[j] - m)
        p_scr[j] = p.astype(BF16)
        return carry + p

    p_sum = lax.fori_loop(0, nb, prob_body, jnp.zeros((rows, 2 * page), F32), unroll=unroll)
    denom = jnp.sum(p_sum, axis=1, keepdims=True) + jnp.sum(p_new, axis=1, keepdims=True)
    wait_pages(vt_hbm, vbuf, 1)

    def pv_body(j, acc):
        pj = p_scr[j]
        for h in range(2):
            acc = acc + lax.dot_general(pj[:, h * page:(h + 1) * page],
                                        vbuf[slot, 2 * j + h].astype(BF16), NT,
                                        preferred_element_type=F32)
        return acc

    acc = lax.fori_loop(0, nb, pv_body, _bdot(p_new, vn_ref[0]), unroll=unroll)
    o_ref[0] = acc / denom


def _moba_sample(q_bd, page_table, kt_pool, vt_pool, k_new, v_new, l_new):
    bs, rows, w = q_bd.shape
    n_pages = page_table.shape[1]
    page = kt_pool.shape[2]
    assert 2 * page == MOBA_BLOCK and kt_pool.shape[1] == w
    nb = n_pages // 2
    k_sel = min(MOBA_TOPK, nb + 1)
    lp = k_new.shape[1]
    per_b = lambda r: pl.BlockSpec((1, r, w), lambda b, pt: (b, 0, 0))
    pool_bytes = 2 * n_pages * w * page * 4
    est = 2 * pool_bytes + 8 * nb * rows * 2 * page * 4
    return pl.pallas_call(
        functools.partial(_moba_sample_kernel, n_seq=bs, n_pages=n_pages, l_new=l_new, k_sel=k_sel),
        grid_spec=pltpu.PrefetchScalarGridSpec(
            num_scalar_prefetch=1,
            grid=(bs,),
            in_specs=[per_b(rows), per_b(lp), per_b(lp),
                      pl.BlockSpec(memory_space=pl.ANY), pl.BlockSpec(memory_space=pl.ANY)],
            out_specs=per_b(rows),
            scratch_shapes=[
                pltpu.VMEM((2, n_pages, w, page), F32),
                pltpu.VMEM((2, n_pages, w, page), F32),
                pltpu.SemaphoreType.DMA((2, 2)),
                pltpu.VMEM((nb, w), F32),
                pltpu.VMEM((nb, rows, 2 * page), F32),
                pltpu.VMEM((nb, rows, 2 * page), BF16),
            ],
        ),
        out_shape=jax.ShapeDtypeStruct((bs, rows, w), F32),
        compiler_params=pltpu.CompilerParams(
            dimension_semantics=("arbitrary",),
            vmem_limit_bytes=min(int(est * 1.25), VMEM_BYTES_V7X - 8 * 2 ** 20)),
        name="moba_sample",
    )(page_table, q_bd, k_new, v_new, kt_pool, vt_pool)


def _pack_w_in_a(w):
    d = w.shape[0]
    qkv_gate = w[:, :4 * MIX_W]
    ab = w[:, 4 * MIX_W:4 * MIX_W + 2 * N_HEADS]
    q_mem = w[:, 4 * MIX_W + 2 * N_HEADS:]
    ab = jnp.pad(ab, ((0, 0), (0, LANES - 2 * N_HEADS)))
    return jnp.concatenate([qkv_gate, q_mem, ab], axis=1).astype(BF16)


def _pad_rows(a, rows):
    return jnp.pad(a, ((0, 0), (0, rows - a.shape[1]), (0, 0)))


def _trunk(p, x3, pos_base, mem_k, mem_v, conv0, s0, paged):
    b, l, d = x3.shape
    m = b * l
    x = x3.reshape(m, d)
    sample = paged is not None
    ffn = lambda x, tag, i, fg=None: _ffn(x, p[tag + "_norm"][i], p[tag + "_w1"], p[tag + "_w3"],
                                          p[tag + "_w2"], i, fg)
    lq = -(-l // SUBLANES) * SUBLANES

    def memory(q_mem, layer):
        q3 = _pad_rows(q_mem.reshape(b, l, MEM_W), lq)
        mo = _mem_attn(q3, mem_k[layer], mem_v[layer])
        return mo[:, :l].reshape(m, MEM_W)

    x = ffn(x, "ffn1", 0)
    if l % IN_PROJ_TILE == 0:
        qkv, gate, q_mem, ab, new_conv = _in_proj_conv(x, p["mix_norm"][0], p["w_in_a"], conv0,
                                                       p["dn_conv_w"], l)
        qkv = qkv.reshape(b, l, 3 * MIX_W)
    else:
        qkv, gate, q_mem, ab = _norm_proj(x, p["mix_norm"][0], p["w_in_a"],
                                          (3 * MIX_W, MIX_W, MEM_W, LANES), "in_proj_a")
        qkv, new_conv = _conv(qkv.reshape(b, l, 3 * MIX_W), conv0, p["dn_conv_w"])
    lp = -(-l // DN_CHUNK) * DN_CHUNK
    o, s_fin = _gdn(_pad_rows(qkv, lp), _pad_rows(gate.reshape(b, l, MIX_W), lp),
                    _pad_rows(ab.reshape(b, l, LANES), lp),
                    p["dn_a_log"], p["dn_dt_bias"], p["dn_out_norm"], s0, l)
    o = o[:, :l].reshape(m, MIX_W)
    x = _out_proj(x, o, memory(q_mem, 0), p["w_out"][0], grouped=False)
    x = ffn(x, "ffn2", 0)

    if sample:
        tables = _rope_tables(m, l, pos_base)
    else:
        tables = _rope_tables(l, l, pos_base)
    kv = _kv_proj(x, p["kv_norm"], p["w_kv"], tables, l, for_prompt=not sample)
    x = ffn(x, "ffn1", 1)
    qh, q_mem = _q_proj(x, p["mix_norm"][1], p["w_in_b"], tables)
    if sample:
        page_table, kt_pool, vt_pool = paged
        k_new, v_new = kv
        k_out = k_new.reshape(b, l, N_KV_B, HEAD_DIM)
        v_out = v_new.reshape(b, l, N_KV_B, HEAD_DIM)
        eye = jnp.eye(N_KV_B, dtype=F32)
        q5 = qh.reshape(N_KV_B, b, l, KV_GROUP, HEAD_DIM).transpose(1, 0, 3, 2, 4)
        q_bd = (q5[:, :, :, :, None, :] * eye[None, :, None, None, :, None]).reshape(
            b, N_HEADS * l, N_KV_B * HEAD_DIM)
        o_bd = _moba_sample(q_bd, page_table, kt_pool, vt_pool,
                            _pad_rows(k_new.reshape(b, l, -1), lq),
                            _pad_rows(v_new.reshape(b, l, -1), lq), l)
        o6 = o_bd.reshape(b, N_KV_B, KV_GROUP, l, N_KV_B, HEAD_DIM)
        o = (o6 * eye[None, :, None, None, :, None]).sum(axis=4)
        o = o.transpose(0, 3, 1, 2, 4).reshape(m, MIX_W)
        x = _out_proj(x, o, memory(q_mem, 1), p["w_out"][1], grouped=False)
    else:
        k_t, v_t, kh, vt, kbar = kv
        to_rows = lambda a: a.reshape(b, N_KV_B, HEAD_DIM, l).transpose(0, 3, 1, 2)
        k_out, v_out = to_rows(k_t), to_rows(v_t)
        nb = l // MOBA_BLOCK
        kbar = kbar.reshape(b, nb, N_KV_B, HEAD_DIM).transpose(2, 0, 1, 3)
        o4 = _moba_prompt(qh, kh, vt, kbar, b, l)
        x = _out_proj(x, o4, memory(q_mem, 1), p["w_out"][1], grouped=True)
    y = ffn(x, "ffn2", 1, p["final_norm"])
    return y.reshape(b, l, d), new_conv[None], s_fin[None], k_out, v_out


def kernel(x_prompt, x_sample, mem_prompt, cache_mem_k, cache_mem_v, state_dn_conv, state_dn_S,
           cache_kv_k, cache_kv_v, page_table,
           ffn1_norm, ffn1_w1, ffn1_w3, ffn1_w2, mix_norm, w_in_a, w_in_b, w_out,
           dn_conv_w, dn_a_log, dn_dt_bias, dn_out_norm, kv_norm, w_kv, mem_norm, w_mem_kv,
           ffn2_norm, ffn2_w1, ffn2_w3, ffn2_w2, final_norm):
    assert w_in_a.shape[0] == 1 and w_in_b.shape[0] == 1 and ffn1_w1.shape[0] == 2
    bf = lambda a: a.astype(BF16)
    p = dict(ffn1_norm=ffn1_norm, ffn1_w1=bf(ffn1_w1), ffn1_w3=bf(ffn1_w3), ffn1_w2=bf(ffn1_w2),
             ffn2_norm=ffn2_norm, ffn2_w1=bf(ffn2_w1), ffn2_w3=bf(ffn2_w3), ffn2_w2=bf(ffn2_w2),
             mix_norm=mix_norm, w_in_a=_pack_w_in_a(w_in_a[0]), w_in_b=bf(w_in_b[0]), w_out=bf(w_out),
             dn_conv_w=dn_conv_w[0], dn_a_log=dn_a_log[0], dn_dt_bias=dn_dt_bias[0],
             dn_out_norm=dn_out_norm[0], kv_norm=kv_norm, w_kv=bf(w_kv), final_norm=final_norm)

    bp, lp, _ = x_prompt.shape
    assert lp % MOBA_BLOCK == 0
    n_mem = mem_prompt.shape[1]
    n_layers = mem_norm.shape[0]
    mem_t = [_mem_kv(mem_prompt, mem_norm[layer], bf(w_mem_kv[layer])) for layer in range(n_layers)]
    mk = [t[0] for t in mem_t]
    mv = [t[1] for t in mem_t]
    conv0 = jnp.zeros((bp, CONV_W - 1, 3 * MIX_W), F32)
    s0 = jnp.zeros((bp, N_HEADS, HEAD_DIM, HEAD_DIM), F32)
    y_p, conv_p, s_p, k_p, v_p = _trunk(p, x_prompt, 0, mk, mv, conv0, s0, None)
    token_major = lambda ts: jnp.stack(ts).reshape(n_layers, bp, N_MEM_HEADS, HEAD_DIM, n_mem).transpose(
        0, 1, 4, 2, 3)
    mem_k_p = token_major(mk)
    mem_v_p = token_major(mv)

    bs = x_sample.shape[0]
    n_pool, page = cache_kv_k.shape[:2]
    past_len = page_table.shape[1] * page
    assert past_len % MOBA_BLOCK == 0 and x_sample.shape[1] <= MOBA_BLOCK
    token_minor = lambda c: jnp.transpose(c, (0, 1, 3, 4, 2)).reshape(c.shape[0], bs, MEM_W, n_mem)
    cmk = token_minor(cache_mem_k)
    cmv = token_minor(cache_mem_v)
    transposed_pages = lambda c: jnp.transpose(c, (0, 2, 3, 1)).reshape(n_pool, -1, page)
    paged = (page_table, transposed_pages(cache_kv_k), transposed_pages(cache_kv_v))
    y_s, conv_s, s_s, k_s, v_s = _trunk(p, x_sample, past_len, cmk, cmv, state_dn_conv[0],
                                        state_dn_S[0], paged)
    return (y_p, y_s, conv_p, s_p, k_p, v_p, mem_k_p, mem_v_p, conv_s, s_s, k_s, v_s)
```

```python
import functools
import math

import jax
import jax.numpy as jnp
from jax import lax
from jax.experimental import pallas as pl
from jax.experimental.pallas import tpu as pltpu

F32 = jnp.float32
BF16 = jnp.bfloat16
HI = lax.Precision.HIGHEST

HEAD_DIM = 64
N_HEADS = 12
MIX_W = N_HEADS * HEAD_DIM
N_MEM_HEADS = 4
MEM_W = N_MEM_HEADS * HEAD_DIM
N_KV_B = 4
KV_GROUP = N_HEADS // N_KV_B
GROUP_W = KV_GROUP * HEAD_DIM
CONV_W = 4
DN_CHUNK = 64
MOBA_BLOCK = 256
MOBA_TOPK = 3
ROT_DIM = HEAD_DIM // 4
ROPE_THETA = 500000.0
EPS = 1e-6
QK_SCALE = HEAD_DIM ** -0.5
LOG2E = math.log2(math.e)
ONES_ROWS = 16

LANES = 128
SUBLANES = 8
VMEM_BYTES_V7X = 64 * 2 ** 20
NEG_INF = float("-inf")

NT = (((1,), (1,)), ((), ()))
TN = (((0,), (0,)), ((), ()))


def _params(semantics, est_bytes):
    limit = min(max(int(est_bytes * 1.5), 16 * 2 ** 20), VMEM_BYTES_V7X - 8 * 2 ** 20)
    return pltpu.CompilerParams(dimension_semantics=semantics, vmem_limit_bytes=limit)


def _row_tile(m, pref):
    t = min(m, pref)
    while m % t or t % SUBLANES:
        t -= 1
    return t


def _rms(x, g):
    return x * lax.rsqrt(jnp.mean(x * x, axis=-1, keepdims=True) + EPS) * g


def _sigmoid(x):
    return 1.0 / (1.0 + jnp.exp(-x))


def _softplus(x):
    return jnp.maximum(x, 0.0) + jnp.log(1.0 + jnp.exp(-jnp.abs(x)))


def _bdot(a, b):
    return jnp.dot(a.astype(BF16), b.astype(BF16), preferred_element_type=F32)


def _bdot_nt(a, b):
    return lax.dot_general(a.astype(BF16), b.astype(BF16), NT, preferred_element_type=F32)


MXU_WIDTH = 2 * LANES


def _ffn_kernel(*refs, splits, final):
    if final:
        x_ref, g_ref, w1_ref, w3_ref, w2_ref, fg_ref, o_ref = refs
    else:
        x_ref, g_ref, w1_ref, w3_ref, w2_ref, o_ref = refs
    x = x_ref[...]
    h = _rms(x, g_ref[...]).astype(BF16)
    acc = None
    for lo, hi in splits:
        a = jnp.dot(h, w1_ref[:, lo:hi], preferred_element_type=F32)
        b = jnp.dot(h, w3_ref[:, lo:hi], preferred_element_type=F32)
        u = ((a * _sigmoid(a)) * b).astype(BF16)
        part = jnp.dot(u, w2_ref[lo:hi, :], preferred_element_type=F32)
        acc = part if acc is None else acc + part
    y = x + 0.5 * acc
    if final:
        y = _rms(y, fg_ref[...])
    o_ref[...] = y


def _ffn(x, g, w1, w3, w2, layer, final_gain=None):
    m, d = x.shape
    d_ff = w1.shape[2]
    tm = _row_tile(m, 512)
    tiles = d_ff // MXU_WIDTH
    cut = (tiles // 2) * MXU_WIDTH
    splits = ((0, cut), (cut, d_ff)) if cut else ((0, d_ff),)
    final = final_gain is not None
    resident = lambda shape: pl.BlockSpec((None,) + shape, lambda i: (layer, 0, 0),
                                          pipeline_mode=pl.Buffered(1))
    in_specs = [
        pl.BlockSpec((tm, d), lambda i: (i, 0)),
        pl.BlockSpec((1, d), lambda i: (0, 0)),
        resident((d, d_ff)), resident((d, d_ff)), resident((d_ff, d)),
    ]
    args = [x, g.reshape(1, d), w1, w3, w2]
    if final:
        in_specs.append(pl.BlockSpec((1, d), lambda i: (0, 0)))
        args.append(final_gain.reshape(1, d))
    widest = max(hi - lo for lo, hi in splits)
    est = 4 * tm * d * 4 + 3 * d * d_ff * 2 + 2 * tm * d * 4 + 3 * tm * widest * 4
    return pl.pallas_call(
        functools.partial(_ffn_kernel, splits=splits, final=final),
        grid=(m // tm,),
        in_specs=in_specs,
        out_specs=pl.BlockSpec((tm, d), lambda i: (i, 0)),
        out_shape=jax.ShapeDtypeStruct((m, d), F32),
        compiler_params=_params(("parallel",), est),
        name="ffn_final" if final else "ffn",
    )(*args)


def _norm_proj_kernel(x_ref, g_ref, w_ref, *out_refs, splits):
    h = _rms(x_ref[...], g_ref[...]).astype(BF16)
    for o_ref, (a, b) in zip(out_refs, splits):
        o_ref[...] = jnp.dot(h, w_ref[:, a:b], preferred_element_type=F32)


def _norm_proj(x, g, w, widths, name):
    m, d = x.shape
    n = w.shape[1]
    assert sum(widths) == n and all(c % LANES == 0 for c in widths)
    tm = _row_tile(m, 256)
    splits, a = [], 0
    for c in widths:
        splits.append((a, a + c))
        a += c
    est = 2 * tm * d * 4 + 2 * d * n * 2 + 3 * tm * n * 4
    return pl.pallas_call(
        functools.partial(_norm_proj_kernel, splits=tuple(splits)),
        grid=(m // tm,),
        in_specs=[
            pl.BlockSpec((tm, d), lambda i: (i, 0)),
            pl.BlockSpec((1, d), lambda i: (0, 0)),
            pl.BlockSpec((d, n), lambda i: (0, 0)),
        ],
        out_specs=[pl.BlockSpec((tm, c), lambda i: (i, 0)) for c in widths],
        out_shape=[jax.ShapeDtypeStruct((m, c), F32) for c in widths],
        compiler_params=_params(("parallel",), est),
        name=name,
    )(x, g.reshape(1, d), w)


def _rope_table_kernel(inv_ref, c_ref, s1_ref, s2_ref, *, period, base, tl):
    i = pl.program_id(0)
    row = lax.broadcasted_iota(jnp.int32, (tl, LANES), 0) + i * tl
    pos = base + lax.rem(row, period)
    ang = pos.astype(F32) * inv_ref[...]
    in_head = lax.broadcasted_iota(jnp.int32, (tl, LANES), 1) % HEAD_DIM
    half = ROT_DIM // 2
    c = jnp.cos(ang)
    s = jnp.sin(ang)
    c_ref[...] = jnp.where(in_head < ROT_DIM, c, 1.0)
    s1_ref[...] = jnp.where(in_head < half, -s, 0.0)
    s2_ref[...] = jnp.where((in_head >= half) & (in_head < ROT_DIM), s, 0.0)


def _rope_tables(n_rows, period, base):
    half = ROT_DIM // 2
    inv = ROPE_THETA ** (-jnp.arange(half, dtype=F32) * 2.0 / ROT_DIM)
    in_head = jnp.arange(LANES) % HEAD_DIM
    inv_lane = jnp.where(in_head < ROT_DIM, inv[in_head % half], 0.0).astype(F32).reshape(1, LANES)
    tl = _row_tile(n_rows, 1024)
    shp = jax.ShapeDtypeStruct((n_rows, LANES), F32)
    return pl.pallas_call(
        functools.partial(_rope_table_kernel, period=period, base=base, tl=tl),
        grid=(n_rows // tl,),
        in_specs=[pl.BlockSpec((1, LANES), lambda i: (0, 0))],
        out_specs=[pl.BlockSpec((tl, LANES), lambda i: (i, 0))] * 3,
        out_shape=[shp] * 3,
        compiler_params=_params(("parallel",), 16 * tl * LANES * 4),
        name="rope_tables",
    )(inv_lane)


def _apply_rope(t, c, s1, s2):
    w = t.shape[1]
    reps = w // LANES
    half = ROT_DIM // 2
    tile = lambda a: jnp.concatenate([a] * reps, axis=1) if reps > 1 else a
    up = pltpu.roll(t, w - half, 1)
    down = pltpu.roll(t, half, 1)
    return t * tile(c) + up * tile(s1) + down * tile(s2)


def _kv_proj_kernel(x_ref, g_ref, w_ref, c_ref, s1_ref, s2_ref, *out_refs, for_prompt):
    h = _rms(x_ref[...], g_ref[...]).astype(BF16)
    kv_w = N_KV_B * HEAD_DIM
    k = jnp.dot(h, w_ref[:, :kv_w], preferred_element_type=F32)
    v = jnp.dot(h, w_ref[:, kv_w:], preferred_element_type=F32)
    k = _apply_rope(k, c_ref[...], s1_ref[...], s2_ref[...])
    if for_prompt:
        kt_ref, vt_ref, kh_ref, vtb_ref, kbar_ref, v_scr = out_refs
        v_scr[...] = v
        v_t = v_scr[...].T
        kt_ref[0] = k.T
        vt_ref[0] = v_t
        for hk in range(N_KV_B):
            sl = slice(hk * HEAD_DIM, (hk + 1) * HEAD_DIM)
            k_h = k[:, sl]
            kh_ref[hk] = jnp.concatenate([k_h, jnp.zeros_like(k_h)], axis=1).astype(BF16)
            vtb_ref[0, hk, 0:HEAD_DIM, :] = v_t[sl, :].astype(BF16)
            vtb_ref[0, hk, HEAD_DIM:, :] = jnp.ones((ONES_ROWS, v_t.shape[1]), BF16)
        kbar_ref[0] = jnp.sum(k, axis=0, keepdims=True) * (1.0 / MOBA_BLOCK)
    else:
        out_refs[0][...] = k
        out_refs[1][...] = v


def _kv_proj(x, g, w, tables, seq_len, for_prompt):
    m, d = x.shape
    kv_w = N_KV_B * HEAD_DIM
    tm = MOBA_BLOCK if for_prompt else _row_tile(m, 256)
    n_tab = tables[0].shape[0] // tm
    row = lambda i: (i, 0)
    tab = lambda i: (i % n_tab, 0)
    if for_prompt:
        nb = seq_len // tm
        t_spec = pl.BlockSpec((1, kv_w, tm), lambda i: (i // nb, 0, i % nb))
        t_shape = jax.ShapeDtypeStruct((m // seq_len, kv_w, seq_len), F32)
        vt_rows = HEAD_DIM + ONES_ROWS
        out_specs = [t_spec, t_spec,
                     pl.BlockSpec((N_KV_B, tm, LANES), lambda i: (0, i, 0)),
                     pl.BlockSpec((1, N_KV_B, vt_rows, tm), lambda i: (i, 0, 0, 0)),
                     pl.BlockSpec((1, 1, kv_w), lambda i: (i, 0, 0))]
        out_shape = [t_shape, t_shape,
                     jax.ShapeDtypeStruct((N_KV_B, m, LANES), BF16),
                     jax.ShapeDtypeStruct((m // tm, N_KV_B, vt_rows, tm), BF16),
                     jax.ShapeDtypeStruct((m // tm, 1, kv_w), F32)]
    else:
        out_specs = [pl.BlockSpec((tm, kv_w), row), pl.BlockSpec((tm, kv_w), row)]
        out_shape = [jax.ShapeDtypeStruct((m, kv_w), F32)] * 2
    est = 2 * tm * d * 4 + 2 * d * 2 * kv_w * 2 + 12 * tm * kv_w * 4
    return pl.pallas_call(
        functools.partial(_kv_proj_kernel, for_prompt=for_prompt),
        grid=(m // tm,),
        in_specs=[
            pl.BlockSpec((tm, d), row),
            pl.BlockSpec((1, d), lambda i: (0, 0)),
            pl.BlockSpec((d, 2 * kv_w), lambda i: (0, 0)),
            pl.BlockSpec((tm, LANES), tab),
            pl.BlockSpec((tm, LANES), tab),
            pl.BlockSpec((tm, LANES), tab),
        ],
        out_specs=out_specs,
        out_shape=out_shape,
        scratch_shapes=[pltpu.VMEM((tm, kv_w), F32)] if for_prompt else [],
        compiler_params=_params(("parallel",), est),
        name="kv_proj",
    )(x, g.reshape(1, d), w, *tables)


def _q_proj_kernel(x_ref, g_ref, w_ref, c_ref, s1_ref, s2_ref, qh_ref, qm_ref):
    h = _rms(x_ref[...], g_ref[...]).astype(BF16)
    q = jnp.dot(h, w_ref[:, :MIX_W], preferred_element_type=F32)
    qm_ref[...] = jnp.dot(h, w_ref[:, MIX_W:], preferred_element_type=F32)
    q = _apply_rope(q, c_ref[...], s1_ref[...], s2_ref[...]) * (QK_SCALE * LOG2E)
    for hk in range(N_KV_B):
        qh_ref[hk] = q[:, hk * GROUP_W:(hk + 1) * GROUP_W]


def _q_proj(x, g, w, tables):
    m, d = x.shape
    n = w.shape[1]
    tm = _row_tile(m, 256)
    n_tab = tables[0].shape[0] // tm
    row = lambda i: (i, 0)
    tab = lambda i: (i % n_tab, 0)
    est = 2 * tm * d * 4 + 2 * d * n * 2 + 16 * tm * MIX_W * 4
    return pl.pallas_call(
        _q_proj_kernel,
        grid=(m // tm,),
        in_specs=[
            pl.BlockSpec((tm, d), row),
            pl.BlockSpec((1, d), lambda i: (0, 0)),
            pl.BlockSpec((d, n), lambda i: (0, 0)),
            pl.BlockSpec((tm, LANES), tab),
            pl.BlockSpec((tm, LANES), tab),
            pl.BlockSpec((tm, LANES), tab),
        ],
        out_specs=[
            pl.BlockSpec((N_KV_B, tm, GROUP_W), lambda i: (0, i, 0)),
            pl.BlockSpec((tm, MEM_W), row),
        ],
        out_shape=[
            jax.ShapeDtypeStruct((N_KV_B, m, GROUP_W), F32),
            jax.ShapeDtypeStruct((m, MEM_W), F32),
        ],
        compiler_params=_params(("parallel",), est),
        name="q_proj",
    )(x, g.reshape(1, d), w, *tables)


def _conv_kernel(u_ref, prev_ref, buf_ref, w_ref, y_ref, nb_ref, pad_scr, *, tl, n_t):
    i = pl.program_id(1)
    halo = CONV_W - 1
    lo = SUBLANES - halo
    pad_scr[SUBLANES:SUBLANES + tl, :] = u_ref[0]

    @pl.when(i == 0)
    def _():
        pad_scr[lo:SUBLANES, :] = buf_ref[0]

    @pl.when(i > 0)
    def _():
        p = prev_ref[0]
        pad_scr[lo:SUBLANES, :] = p[p.shape[0] - halo:, :]

    y = pad_scr[lo:lo + tl, :] * w_ref[0:1, :]
    for j in range(1, CONV_W):
        y = y + pad_scr[lo + j:lo + j + tl, :] * w_ref[j:j + 1, :]
    y_ref[0] = y * _sigmoid(y)

    @pl.when(i == n_t - 1)
    def _():
        nb_ref[0] = pad_scr[lo + tl:SUBLANES + tl, :]


def _conv(u, buf, w):
    b, l, c = u.shape
    tl = _row_tile(l, 512) if l % SUBLANES == 0 else l
    n_t = l // tl
    pr = min(SUBLANES, l)
    per = tl // pr
    est = 6 * tl * c * 4
    return pl.pallas_call(
        functools.partial(_conv_kernel, tl=tl, n_t=n_t),
        grid=(b, n_t),
        in_specs=[
            pl.BlockSpec((1, tl, c), lambda bi, i: (bi, i, 0)),
            pl.BlockSpec((1, pr, c), lambda bi, i: (bi, jnp.maximum(i * per - 1, 0), 0)),
            pl.BlockSpec((1, CONV_W - 1, c), lambda bi, i: (bi, 0, 0)),
            pl.BlockSpec((CONV_W, c), lambda bi, i: (0, 0)),
        ],
        out_specs=[
            pl.BlockSpec((1, tl, c), lambda bi, i: (bi, i, 0)),
            pl.BlockSpec((1, CONV_W - 1, c), lambda bi, i: (bi, 0, 0)),
        ],
        out_shape=[
            jax.ShapeDtypeStruct((b, l, c), F32),
            jax.ShapeDtypeStruct((b, CONV_W - 1, c), F32),
        ],
        scratch_shapes=[pltpu.VMEM((tl + SUBLANES, c), F32)],
        compiler_params=_params(("parallel", "arbitrary"), est),
        name="short_conv",
    )(u, u, buf, w)


IN_PROJ_TILE = 256


def _in_proj_conv_kernel(x_ref, g_ref, w_ref, buf_ref, cw_ref, qkv_ref, gate_ref, qm_ref, ab_ref,
                         nb_ref, pad_scr, *, tm, n_t):
    i = pl.program_id(1)
    halo = CONV_W - 1
    lo = SUBLANES - halo
    c3 = 3 * MIX_W
    h = _rms(x_ref[...], g_ref[...]).astype(BF16)

    @pl.when(i == 0)
    def _():
        pad_scr[lo:SUBLANES, :] = buf_ref[0]

    pad_scr[SUBLANES:SUBLANES + tm, :] = jnp.dot(h, w_ref[:, :c3], preferred_element_type=F32)
    gate_ref[...] = jnp.dot(h, w_ref[:, c3:c3 + MIX_W], preferred_element_type=F32)
    qm_ref[...] = jnp.dot(h, w_ref[:, c3 + MIX_W:c3 + MIX_W + MEM_W], preferred_element_type=F32)
    ab_ref[...] = jnp.dot(h, w_ref[:, c3 + MIX_W + MEM_W:], preferred_element_type=F32)
    y = pad_scr[lo:lo + tm, :] * cw_ref[0:1, :]
    for j in range(1, CONV_W):
        y = y + pad_scr[lo + j:lo + j + tm, :] * cw_ref[j:j + 1, :]
    qkv_ref[...] = y * _sigmoid(y)
    tail = pad_scr[lo + tm:SUBLANES + tm, :]

    @pl.when(i == n_t - 1)
    def _():
        nb_ref[0] = tail

    pad_scr[lo:SUBLANES, :] = tail


def _in_proj_conv(x, g, w, buf, conv_w, seq_len):
    m, d = x.shape
    n = w.shape[1]
    c3 = 3 * MIX_W
    tm = IN_PROJ_TILE
    n_t = seq_len // tm
    b = m // seq_len
    row = lambda bi, i: (bi * n_t + i, 0)
    const = lambda bi, i: (0, 0)
    widths = (c3, MIX_W, MEM_W, n - c3 - MIX_W - MEM_W)
    est = 2 * tm * d * 4 + 2 * d * n * 2 + 4 * tm * n * 4
    return pl.pallas_call(
        functools.partial(_in_proj_conv_kernel, tm=tm, n_t=n_t),
        grid=(b, n_t),
        in_specs=[
            pl.BlockSpec((tm, d), row),
            pl.BlockSpec((1, d), const),
            pl.BlockSpec((d, n), const),
            pl.BlockSpec((1, CONV_W - 1, c3), lambda bi, i: (bi, 0, 0)),
            pl.BlockSpec((CONV_W, c3), const),
        ],
        out_specs=[pl.BlockSpec((tm, c), row) for c in widths]
        + [pl.BlockSpec((1, CONV_W - 1, c3), lambda bi, i: (bi, 0, 0))],
        out_shape=[jax.ShapeDtypeStruct((m, c), F32) for c in widths]
        + [jax.ShapeDtypeStruct((b, CONV_W - 1, c3), F32)],
        scratch_shapes=[pltpu.VMEM((tm + SUBLANES, c3), F32)],
        compiler_params=_params(("parallel", "arbitrary"), est),
        name="in_proj_conv",
    )(x, g.reshape(1, d), w, buf, conv_w)


HEADS_PER_SLAB = LANES * 2 // HEAD_DIM
SLAB_W = HEADS_PER_SLAB * HEAD_DIM
N_SLABS = N_HEADS // HEADS_PER_SLAB


def _split2(x):
    hi = x.astype(BF16)
    return hi, (x - hi.astype(F32)).astype(BF16)


def _split3(x):
    hi = x.astype(BF16)
    r = x - hi.astype(F32)
    mid = r.astype(BF16)
    return hi, mid, (r - mid.astype(F32)).astype(BF16)


def _dot_right01(x, sel):
    hi, mid, lo = _split3(x)
    d = lambda a: jnp.dot(a, sel, preferred_element_type=F32)
    return d(hi) + (d(mid) + d(lo))


def _dot_left01(sel, x):
    hi, mid, lo = _split3(x)
    d = lambda a: jnp.dot(sel, a, preferred_element_type=F32)
    return d(hi) + (d(mid) + d(lo))


def _dot3(a, b_hi, b_lo):
    a_hi, a_lo = _split2(a)
    d = lambda x, y: jnp.dot(x, y, preferred_element_type=F32)
    return d(a_hi, b_hi) + (d(a_hi, b_lo) + d(a_lo, b_hi))


def _block_diag(x, mask):
    return jnp.concatenate([x] * HEADS_PER_SLAB, axis=0) * mask


def _block_diag_pieces(x, mask):
    hi, lo = _split2(x)
    return _block_diag(hi, mask), _block_diag(lo, mask)


def _unit_lower_inverses(ms, eye_t, mask):
    c = ms[0].shape[0]
    xs = [eye_t - m for m in ms]
    ps = [_dot3(m, *_block_diag_pieces(m, mask)) for m in ms]
    power = 2
    while 2 * power < c:
        xps = [_dot3(jnp.concatenate([x, p], axis=0), *_block_diag_pieces(p, mask))
               for x, p in zip(xs, ps)]
        xs = [x + xp[:c] for x, xp in zip(xs, xps)]
        ps = [xp[c:] for xp in xps]
        power *= 2
    return [x + _dot3(x, *_block_diag_pieces(p, mask)) for x, p in zip(xs, ps)]


def _gdn_kernel(q_ref, k_ref, v_ref, gate_ref, ab_ref, alog_ref, dt_ref, gain_ref, s0_ref,
                o_ref, sfin_ref,
                s_scr, qn_scr, kn_scr, gcb_scr, bb_scr, u_scr, w_scr, a_scr, qg_scr, kd_scr, egl_scr,
                *, t_rows, n_t, l_valid):
    t = pl.program_id(1)
    c = DN_CHUNK
    d = HEAD_DIM
    n_chunks = t_rows // c

    r_s = lax.broadcasted_iota(jnp.int32, (SLAB_W, SLAB_W), 0)
    c_s = lax.broadcasted_iota(jnp.int32, (SLAB_W, SLAB_W), 1)
    same_head = (r_s // d) == (c_s // d)
    bd_f32 = jnp.where(same_head, 1.0, 0.0).astype(F32)
    bd_mask = bd_f32.astype(BF16)
    r_t = lax.broadcasted_iota(jnp.int32, (c, SLAB_W), 0)
    c_t = lax.broadcasted_iota(jnp.int32, (c, SLAB_W), 1) % d
    lower_t = r_t >= c_t
    strict_t = r_t > c_t
    eye_t = jnp.where(r_t == c_t, 1.0, 0.0).astype(F32)
    r_c = lax.broadcasted_iota(jnp.int32, (c, c), 0)
    c_c = lax.broadcasted_iota(jnp.int32, (c, c), 1)
    tril = jnp.where(r_c >= c_c, 1.0, 0.0).astype(BF16)
    er = lax.broadcasted_iota(jnp.int32, (LANES, MIX_W), 0)
    ec = lax.broadcasted_iota(jnp.int32, (LANES, MIX_W), 1) // d
    sel_g = jnp.where(er == ec, 1.0, 0.0).astype(BF16)
    sel_b = jnp.where(er == ec + N_HEADS, 1.0, 0.0).astype(BF16)

    @pl.when(t == 0)
    def _():
        for s in range(N_SLABS):
            rows = jnp.concatenate([s0_ref[0, s * HEADS_PER_SLAB + h] for h in range(HEADS_PER_SLAB)],
                                   axis=0)
            s_scr[s] = jnp.concatenate([rows] * HEADS_PER_SLAB, axis=1) * bd_f32

    ab = ab_ref[0]
    row = lax.broadcasted_iota(jnp.int32, ab.shape, 0) + t * t_rows
    valid = row < l_valid
    g = jnp.where(valid, -jnp.exp(alog_ref[...]) * _softplus(ab + dt_ref[...]), 0.0)
    beta = jnp.where(valid, _sigmoid(ab), 0.0)
    gc = jnp.concatenate([_dot_left01(tril, g[i * c:(i + 1) * c]) for i in range(n_chunks)], axis=0)
    gcb_scr[...] = _dot_right01(gc, sel_g)
    bb_scr[...] = _dot_right01(beta, sel_b)
    for s in range(N_SLABS):
        cs = slice(s * SLAB_W, (s + 1) * SLAB_W)
        for src, dst, scale in ((q_ref, qn_scr, QK_SCALE), (k_ref, kn_scr, 1.0)):
            x = src[0, :, cs]
            hi, lo = _split2(x * x)
            ssq = (jnp.dot(hi, bd_mask, preferred_element_type=F32)
                   + jnp.dot(lo, bd_mask, preferred_element_type=F32))
            dst[:, cs] = x * (lax.rsqrt(ssq + EPS) * scale)

    per_iter = 2 if n_chunks % 2 == 0 else 1
    slabs = [slice(s * SLAB_W, (s + 1) * SLAB_W) for s in range(N_SLABS)]

    def factors(it, carry):
        items = []
        for u in range(per_iter):
            ci = it * per_iter + u
            rows = pl.ds(pl.multiple_of(ci * c, c), c)
            items += [(ci, rows, cs) for cs in slabs]
        pre = []
        for ci, rows, cs in items:
            gcb = gcb_scr[rows, cs]
            kn = kn_scr[rows, cs]
            qn = qn_scr[rows, cs]
            b_ = bb_scr[rows, cs]
            kb = kn * b_
            g_row = jnp.sum(gcb * eye_t, axis=0, keepdims=True)
            dec = jnp.exp(jnp.where(lower_t, gcb - g_row, NEG_INF))
            both = lax.dot_general(jnp.concatenate([kb, qn], axis=0).astype(BF16),
                                   _block_diag(kn.astype(BF16), bd_mask), NT,
                                   preferred_element_type=F32)
            pre.append((gcb, kn, qn, kb, dec, both))
        ms = []
        for (ci, rows, cs), (gcb, kn, qn, kb, dec, both) in zip(items, pre):
            ms.append(jnp.where(strict_t, both[:c] * dec, 0.0))
            a_scr[rows, cs] = (both[c:] * dec).astype(BF16)
        tinvs = _unit_lower_inverses(ms, eye_t, bd_mask)
        sols = []
        for (ci, rows, cs), (gcb, kn, qn, kb, dec, both), tinv in zip(items, pre, tinvs):
            eg = jnp.exp(gcb)
            v_hi, v_lo = _block_diag_pieces(v_ref[0, rows, cs] * bb_scr[rows, cs], bd_mask)
            k_hi, k_lo = _block_diag_pieces(kb * eg, bd_mask)
            sols.append(_dot3(tinv, jnp.concatenate([v_hi, k_hi], axis=1),
                              jnp.concatenate([v_lo, k_lo], axis=1)))
            glast = gcb[c - 1:c, :]
            qg_scr[rows, cs] = (qn * eg).astype(BF16)
            kd_scr[rows, cs] = (kn * jnp.exp(glast - gcb)).astype(BF16)
            egl_scr[pl.ds(ci, 1), cs] = jnp.exp(glast)
        for (ci, rows, cs), sol in zip(items, sols):
            u_scr[rows, cs] = sol[:, :SLAB_W]
            w_scr[rows, cs] = sol[:, SLAB_W:].astype(BF16)
        return carry

    lax.fori_loop(0, n_chunks // per_iter, factors, 0)

    gain = gain_ref[...]

    def recur(ci, carry):
        rows = pl.ds(pl.multiple_of(ci * c, c), c)
        olds = [s_scr[s] for s in range(N_SLABS)]
        rs = [jnp.dot(jnp.concatenate([w_scr[rows, cs], qg_scr[rows, cs]], axis=0),
                      s_old.astype(BF16), preferred_element_type=F32)
              for cs, s_old in zip(slabs, olds)]
        v_news = [(u_scr[rows, cs] - r[:c]).astype(BF16) for cs, r in zip(slabs, rs)]
        upds = [lax.dot_general(kd_scr[rows, cs], v_new, TN, preferred_element_type=F32)
                for cs, v_new in zip(slabs, v_news)]
        os_ = [r[c:] + jnp.dot(a_scr[rows, cs], _block_diag(v_new, bd_mask),
                               preferred_element_type=F32)
               for cs, r, v_new in zip(slabs, rs, v_news)]
        for s, (cs, s_old, upd) in enumerate(zip(slabs, olds, upds)):
            s_scr[s] = (s_old * egl_scr[pl.ds(ci, 1), cs] + upd) * bd_f32
        ssqs = []
        for o in os_:
            hi, lo = _split2(o * o)
            ssqs.append(jnp.dot(hi, bd_mask, preferred_element_type=F32)
                        + jnp.dot(lo, bd_mask, preferred_element_type=F32))
        for cs, o, ssq in zip(slabs, os_, ssqs):
            gt = gate_ref[0, rows, cs]
            o_ref[0, rows, cs] = (o * lax.rsqrt(ssq * (1.0 / d) + EPS) * gain[:, cs]
                                  * (gt * _sigmoid(gt)))
        return carry

    lax.fori_loop(0, n_chunks, recur, 0)

    @pl.when(t == n_t - 1)
    def _():
        for s in range(N_SLABS):
            full = s_scr[s]
            for h in range(HEADS_PER_SLAB):
                sfin_ref[0, s * HEADS_PER_SLAB + h] = full[h * d:(h + 1) * d, h * d:(h + 1) * d]


def _gdn(qkv, gate, ab, a_log, dt_bias, out_gain, s0, l_valid):
    b, lp, _ = qkv.shape
    assert lp % DN_CHUNK == 0
    t_rows = DN_CHUNK * math.gcd(lp // DN_CHUNK, 8)
    n_t = lp // t_rows
    n_chunks = t_rows // DN_CHUNK
    pad = lambda a: jnp.pad(a.astype(F32), (0, LANES - N_HEADS)).reshape(1, LANES)
    blk = lambda j: pl.BlockSpec((1, t_rows, MIX_W), lambda bi, t, j=j: (bi, t, j))
    state = pl.BlockSpec((1, N_HEADS, HEAD_DIM, HEAD_DIM), lambda bi, t: (bi, 0, 0, 0))
    vec = pl.BlockSpec((1, LANES), lambda bi, t: (0, 0))
    wide_f32 = pltpu.VMEM((t_rows, MIX_W), F32)
    wide_bf16 = pltpu.VMEM((t_rows, MIX_W), BF16)
    est = 10 * t_rows * MIX_W * 4 + 5 * t_rows * MIX_W * 4 + 4 * t_rows * MIX_W * 2 + 64 * SLAB_W * SLAB_W * 4
    return pl.pallas_call(
        functools.partial(_gdn_kernel, t_rows=t_rows, n_t=n_t, l_valid=l_valid),
        grid=(b, n_t),
        in_specs=[
            blk(0), blk(1), blk(2),
            pl.BlockSpec((1, t_rows, MIX_W), lambda bi, t: (bi, t, 0)),
            pl.BlockSpec((1, t_rows, LANES), lambda bi, t: (bi, t, 0)),
            vec, vec,
            pl.BlockSpec((1, MIX_W), lambda bi, t: (0, 0)),
            state,
        ],
        out_specs=[pl.BlockSpec((1, t_rows, MIX_W), lambda bi, t: (bi, t, 0)), state],
        out_shape=[
            jax.ShapeDtypeStruct((b, lp, MIX_W), F32),
            jax.ShapeDtypeStruct((b, N_HEADS, HEAD_DIM, HEAD_DIM), F32),
        ],
        scratch_shapes=[
            pltpu.VMEM((N_SLABS, SLAB_W, SLAB_W), F32),
            wide_f32, wide_f32,
            wide_f32, wide_f32,
            wide_f32,
            wide_bf16, wide_bf16, wide_bf16, wide_bf16,
            pltpu.VMEM((max(n_chunks, SUBLANES), MIX_W), F32),
        ],
        compiler_params=_params(("parallel", "arbitrary"), est),
        name="gated_delta_rule",
    )(qkv, qkv, qkv, gate, ab, pad(a_log), pad(dt_bias),
      jnp.tile(out_gain.astype(F32), N_HEADS).reshape(1, MIX_W), s0)


def _mem_kv_kernel(x_ref, g_ref, w_ref, kt_ref, vt_ref, kv_scr):
    h = _rms(x_ref[...], g_ref[...]).astype(BF16)
    kv_scr[...] = jnp.dot(h, w_ref[...], preferred_element_type=F32)
    kt_ref[0] = kv_scr[:, :MEM_W].T
    vt_ref[0] = kv_scr[:, MEM_W:].T


def _mem_kv(mem, g, w):
    b, n_mem, d = mem.shape
    shp = jax.ShapeDtypeStruct((b, MEM_W, n_mem), F32)
    spec = pl.BlockSpec((1, MEM_W, n_mem), lambda i: (i, 0, 0))
    return pl.pallas_call(
        _mem_kv_kernel,
        grid=(b,),
        in_specs=[
            pl.BlockSpec((n_mem, d), lambda i: (i, 0)),
            pl.BlockSpec((1, d), lambda i: (0, 0)),
            pl.BlockSpec((d, 2 * MEM_W), lambda i: (0, 0)),
        ],
        out_specs=[spec, spec],
        out_shape=[shp, shp],
        scratch_shapes=[pltpu.VMEM((n_mem, 2 * MEM_W), F32)],
        compiler_params=_params(("parallel",), 2 * n_mem * d * 4 + 2 * d * 2 * MEM_W * 2 + 8 * n_mem * MEM_W * 4),
        name="mem_kv",
    )(mem.reshape(b * n_mem, d), g.reshape(1, d), w)


def _mem_attn_kernel(q_ref, kt_ref, vt_ref, o_ref):
    q = q_ref[0]
    kt = kt_ref[0]
    vt = vt_ref[0]
    outs = []
    for h in range(N_MEM_HEADS):
        sl = slice(h * HEAD_DIM, (h + 1) * HEAD_DIM)
        s = _bdot(q[:, sl], kt[sl, :]) * QK_SCALE
        p = jnp.exp(s - jnp.max(s, axis=-1, keepdims=True))
        outs.append(_bdot_nt(p, vt[sl, :]) / jnp.sum(p, axis=-1, keepdims=True))
    o_ref[0] = jnp.concatenate(outs, axis=1)


def _mem_attn(q, mk_t, mv_t):
    b, l, w = q.shape
    n_mem = mk_t.shape[2]
    tl = _row_tile(l, 512)
    est = 4 * tl * w * 4 + 4 * n_mem * w * 4 + 8 * tl * n_mem * 4
    return pl.pallas_call(
        _mem_attn_kernel,
        grid=(b, l // tl),
        in_specs=[
            pl.BlockSpec((1, tl, w), lambda bi, i: (bi, i, 0)),
            pl.BlockSpec((1, w, n_mem), lambda bi, i: (bi, 0, 0)),
            pl.BlockSpec((1, w, n_mem), lambda bi, i: (bi, 0, 0)),
        ],
        out_specs=pl.BlockSpec((1, tl, w), lambda bi, i: (bi, i, 0)),
        out_shape=jax.ShapeDtypeStruct((b, l, w), F32),
        compiler_params=_params(("parallel", "parallel"), est),
        name="mem_attn",
    )(q, mk_t, mv_t)


def _out_proj_kernel(x_ref, o_ref, mo_ref, wo_ref, wm_ref, y_ref, *, grouped):
    y = x_ref[...] + _bdot(mo_ref[...], wm_ref[...])
    if grouped:
        for hk in range(N_KV_B):
            y = y + _bdot(o_ref[hk], wo_ref[hk])
    else:
        y = y + _bdot(o_ref[...], wo_ref[...])
    y_ref[...] = y


def _out_proj(x, o, mo, w_out, grouped):
    m, d = x.shape
    tm = _row_tile(m, 512)
    row = lambda i: (i, 0)
    if grouped:
        wo = w_out[:MIX_W].reshape(N_KV_B, GROUP_W, d)
        o_spec = pl.BlockSpec((N_KV_B, tm, GROUP_W), lambda i: (0, i, 0))
        wo_spec = pl.BlockSpec((N_KV_B, GROUP_W, d), lambda i: (0, 0, 0))
    else:
        wo = w_out[:MIX_W]
        o_spec = pl.BlockSpec((tm, MIX_W), row)
        wo_spec = pl.BlockSpec((MIX_W, d), lambda i: (0, 0))
    est = 6 * tm * d * 4 + 4 * tm * d * 4 + 4 * d * d * 2
    return pl.pallas_call(
        functools.partial(_out_proj_kernel, grouped=grouped),
        grid=(m // tm,),
        in_specs=[
            pl.BlockSpec((tm, d), row),
            o_spec,
            pl.BlockSpec((tm, MEM_W), row),
            wo_spec,
            pl.BlockSpec((MEM_W, d), lambda i: (0, 0)),
        ],
        out_specs=pl.BlockSpec((tm, d), row),
        out_shape=jax.ShapeDtypeStruct((m, d), F32),
        compiler_params=_params(("parallel",), est),
        name="out_proj",
    )(x, o, mo, wo, w_out[MIX_W:])


def _select_blocks(gate, n_past, k_sel):
    lane = lax.broadcasted_iota(jnp.int32, gate.shape, 1)
    lane_f = lane.astype(F32)
    gate = jnp.where(lane < n_past, gate, NEG_INF)
    sel = jnp.zeros(gate.shape, F32)
    for _ in range(k_sel):
        mx = jnp.max(gate, axis=1, keepdims=True)
        first = jnp.min(jnp.where(gate == mx, lane_f, float(gate.shape[1])), axis=1, keepdims=True)
        pick = (lane_f == first) & (mx > NEG_INF)
        sel = jnp.where(pick, 1.0, sel)
        gate = jnp.where(pick, NEG_INF, gate)
    return sel


def _select_blocks_t(gate, n_past, k_sel):
    blk = lax.broadcasted_iota(jnp.int32, gate.shape, 0)
    blk_f = blk.astype(F32)
    gate = jnp.where(blk < n_past, gate, NEG_INF)
    sel = jnp.zeros(gate.shape, F32)
    for _ in range(k_sel):
        mx = jnp.max(gate, axis=0, keepdims=True)
        first = jnp.min(jnp.where(gate == mx, blk_f, float(gate.shape[0])), axis=0, keepdims=True)
        pick = (blk_f == first) & (mx > NEG_INF)
        sel = jnp.where(pick, 1.0, sel)
        gate = jnp.where(pick, NEG_INF, gate)
    return sel


def _moba_prompt_kernel(q_ref, k_ref, vt_ref, kbar_ref, o_ref, q_scr, m_scr, acc_scr, sel_scr,
                        s_scr, p_scr, alpha_scr, *, k_sel):
    i = pl.program_id(2)
    blk = MOBA_BLOCK
    d = HEAD_DIM
    cw = LANES
    n_chunks = KV_GROUP * blk // cw
    n_blocks = sel_scr.shape[0]
    chunks = [slice(c * cw, (c + 1) * cw) for c in range(n_chunks)]
    q = q_ref[0]
    q3 = jnp.concatenate([q[:, g * d:(g + 1) * d] for g in range(KV_GROUP)], axis=0)
    q_scr[...] = jnp.concatenate([q3, jnp.zeros_like(q3)], axis=1).astype(BF16)

    def block_scores(k_blk, c):
        return lax.dot_general(k_blk, q_scr[c * cw:(c + 1) * cw, :], NT,
                               preferred_element_type=F32)

    k_own = k_ref[0, pl.ds(pl.multiple_of(i * blk, blk), blk), :]
    own_scores = [block_scores(k_own, c) for c in range(n_chunks)]
    k_first = k_ref[0, 0:blk, :]
    first_scores = [block_scores(k_first, c) for c in range(n_chunks)]
    gate_t = lax.dot_general(kbar_ref[0, 0], q3, NT, precision=HI, preferred_element_type=F32)
    for c, cs in enumerate(chunks):
        s_scr[0, :, cs] = first_scores[c]
    for c, cs in enumerate(chunks):
        kpos = lax.broadcasted_iota(jnp.int32, (blk, cw), 0)
        qpos = (lax.broadcasted_iota(jnp.int32, (blk, cw), 1) + c * cw) % blk
        s = jnp.where(kpos <= qpos, own_scores[c], NEG_INF)
        m0 = jnp.max(s, axis=0, keepdims=True)
        m_scr[:, cs] = m0
        p_scr[1, :, cs] = jnp.exp2(s - m0).astype(BF16)
    acc_scr[...] = jnp.zeros(acc_scr.shape, F32)
    alpha_scr[...] = jnp.ones(alpha_scr.shape, F32)
    sel_scr[...] = _select_blocks_t(gate_t, i, k_sel)

    def pv_update(j_prev, slot):
        vt_prev = vt_ref[j_prev, 0]
        pv = [jnp.dot(vt_prev, p_scr[slot, :, cs], preferred_element_type=F32) for cs in chunks]
        for cs, r in zip(chunks, pv):
            acc_scr[:, cs] = alpha_scr[:, cs] * acc_scr[:, cs] + r

    def step(j, cur):
        nxt = 1 - cur
        j_next = jnp.minimum(j + 1, n_blocks - 1)
        k_next = k_ref[0, pl.ds(pl.multiple_of(j_next * blk, blk), blk), :]
        for c, cs in enumerate(chunks):
            s_scr[nxt, :, cs] = block_scores(k_next, c)
        pv_update(jnp.where(j == 0, i, jnp.minimum(j - 1, n_blocks - 1)), nxt)
        chosen = (sel_scr[pl.ds(jnp.minimum(j, n_blocks - 1), 1), :] > 0.0) & (j < i)
        for c, cs in enumerate(chunks):
            s = s_scr[cur, :, cs]
            m_old = m_scr[:, cs]
            m_new = jnp.maximum(m_old, jnp.where(chosen[:, cs], jnp.max(s, axis=0, keepdims=True),
                                                 NEG_INF))
            p_scr[cur, :, cs] = jnp.exp2(s - jnp.where(chosen[:, cs], m_new, float("inf"))).astype(BF16)
            alpha_scr[:, cs] = jnp.exp2(m_old - m_new)
            m_scr[:, cs] = m_new

    per_trip = 4

    def trip(t, carry):
        for h in range(per_trip):
            step(per_trip * t + h, h % 2)
        return carry

    n_trips = (i + per_trip - 1) // per_trip
    lax.fori_loop(0, n_trips, trip, 0)
    pv_update(jnp.where(n_trips == 0, i, jnp.minimum(per_trip * n_trips - 1, n_blocks - 1)), 1)
    acc = acc_scr[...]
    o_t = (acc[:d] / acc[d:d + 1]).astype(BF16)
    eye = (lax.broadcasted_iota(jnp.int32, (blk, blk), 0)
           == lax.broadcasted_iota(jnp.int32, (blk, blk), 1)).astype(BF16)
    o_ref[0] = jnp.concatenate(
        [lax.dot_general(eye, o_t[:, g * blk:(g + 1) * blk], NT, preferred_element_type=F32)
         for g in range(KV_GROUP)], axis=1)


def _moba_prompt(qh, kh, vt, kbar, b, l):
    nb = l // MOBA_BLOCK
    rows = KV_GROUP * MOBA_BLOCK
    k_sel = min(MOBA_TOPK, nb)
    vt_rows = vt.shape[2]
    est = 4 * l * LANES * 2 + 4 * l * vt_rows * 2 + 8 * MOBA_BLOCK * GROUP_W * 4 + 24 * rows * LANES * 4
    return pl.pallas_call(
        functools.partial(_moba_prompt_kernel, k_sel=k_sel),
        grid=(N_KV_B, b, nb),
        in_specs=[
            pl.BlockSpec((1, MOBA_BLOCK, GROUP_W), lambda hk, bi, i: (hk, bi * nb + i, 0)),
            pl.BlockSpec((1, l, LANES), lambda hk, bi, i: (hk, bi, 0)),
            pl.BlockSpec((nb, 1, vt_rows, MOBA_BLOCK), lambda hk, bi, i: (bi, hk, 0, 0)),
            pl.BlockSpec((1, 1, nb, HEAD_DIM), lambda hk, bi, i: (hk, bi, 0, 0)),
        ],
        out_specs=pl.BlockSpec((1, MOBA_BLOCK, GROUP_W), lambda hk, bi, i: (hk, bi * nb + i, 0)),
        out_shape=jax.ShapeDtypeStruct((N_KV_B, b * l, GROUP_W), F32),
        scratch_shapes=[
            pltpu.VMEM((rows, LANES), BF16),
            pltpu.VMEM((1, rows), F32),
            pltpu.VMEM((vt_rows, rows), F32),
            pltpu.VMEM((nb, rows), F32),
            pltpu.VMEM((2, MOBA_BLOCK, rows), F32),
            pltpu.VMEM((2, MOBA_BLOCK, rows), BF16),
            pltpu.VMEM((1, rows), F32),
        ],
        compiler_params=_params(("parallel", "parallel", "arbitrary"), est),
        name="moba_prompt",
    )(qh, kh, vt, kbar)


def _moba_sample_kernel(pt_ref, q_ref, kn_ref, vn_ref, kt_hbm, vt_hbm, o_ref,
                        kbuf, vbuf, sem, kbar_scr, s_scr, p_scr,
                        *, n_seq, n_pages, l_new, k_sel):
    b = pl.program_id(0)
    slot = lax.rem(b, 2)
    nb = n_pages // 2
    page = kbuf.shape[-1]
    unroll = math.gcd(nb, 4)

    def page_copy(hbm, buf, sl, which, pg, p):
        return pltpu.make_async_copy(hbm.at[pg], buf.at[sl, p], sem.at[sl, which])

    def fetch(seq, sl):
        def body(p, carry):
            pg = pt_ref[seq, p]
            page_copy(kt_hbm, kbuf, sl, 0, pg, p).start()
            page_copy(vt_hbm, vbuf, sl, 1, pg, p).start()
            return carry
        lax.fori_loop(0, n_pages, body, 0)

    def wait_pages(hbm, buf, which):
        def body(p, carry):
            page_copy(hbm, buf, slot, which, 0, p).wait()
            return carry
        lax.fori_loop(0, n_pages, body, 0)

    @pl.when(b == 0)
    def _():
        fetch(0, 0)

    @pl.when(b + 1 < n_seq)
    def _():
        fetch(b + 1, 1 - slot)

    q = q_ref[0]
    qb = q.astype(BF16)
    rows = q.shape[0]
    wait_pages(kt_hbm, kbuf, 0)

    ones = jnp.ones((SUBLANES, page), BF16)

    def mean_body(j, carry):
        t = kbuf[slot, 2 * j] + kbuf[slot, 2 * j + 1]
        hi, mid, lo = _split3(t)
        d = lambda a: lax.dot_general(ones, a, NT, preferred_element_type=F32)
        kbar_scr[pl.ds(j, 1), :] = (d(hi) + (d(mid) + d(lo)))[0:1] * (1.0 / MOBA_BLOCK)
        return carry

    lax.fori_loop(0, nb, mean_body, 0, unroll=unroll)
    gate = lax.dot_general(q, kbar_scr[...], NT, precision=HI, preferred_element_type=F32)
    sel = _select_blocks(gate, nb, k_sel)
    lane = lax.broadcasted_iota(jnp.int32, sel.shape, 1)

    def score_body(j, carry):
        chosen = jnp.max(jnp.where(lane == j, sel, 0.0), axis=1, keepdims=True)
        s = jnp.concatenate(
            [jnp.dot(qb, kbuf[slot, 2 * j + h].astype(BF16), preferred_element_type=F32)
             for h in range(2)], axis=1)
        s = jnp.where(chosen > 0.0, s, NEG_INF)
        s_scr[j] = s
        return jnp.maximum(carry, s)

    s_max = lax.fori_loop(0, nb, score_body, jnp.full((rows, 2 * page), NEG_INF, F32), unroll=unroll)
    s_new = _bdot_nt(qb, kn_ref[0])
    t_q = lax.broadcasted_iota(jnp.int32, s_new.shape, 0) % l_new
    t_k = lax.broadcasted_iota(jnp.int32, s_new.shape, 1)
    s_new = jnp.where(t_k <= t_q, s_new, NEG_INF)
    m = jnp.maximum(jnp.max(s_max, axis=1, keepdims=True), jnp.max(s_new, axis=1, keepdims=True))
    p_new = jnp.exp2(s_new - m)

    def prob_body(j, carry):
        p = jnp.exp2(s_scr[j] - m)
        p_scr[j] = p.astype(BF16)
        return carry + p

    p_sum = lax.fori_loop(0, nb, prob_body, jnp.zeros((rows, 2 * page), F32), unroll=unroll)
    denom = jnp.sum(p_sum, axis=1, keepdims=True) + jnp.sum(p_new, axis=1, keepdims=True)
    wait_pages(vt_hbm, vbuf, 1)

    def pv_body(j, acc):
        pj = p_scr[j]
        for h in range(2):
            acc = acc + lax.dot_general(pj[:, h * page:(h + 1) * page],
                                        vbuf[slot, 2 * j + h].astype(BF16), NT,
                                        preferred_element_type=F32)
        return acc

    acc = lax.fori_loop(0, nb, pv_body, _bdot(p_new, vn_ref[0]), unroll=unroll)
    o_ref[0] = acc / denom


def _moba_sample(q_bd, page_table, kt_pool, vt_pool, k_new, v_new, l_new):
    bs, rows, w = q_bd.shape
    n_pages = page_table.shape[1]
    page = kt_pool.shape[2]
    assert 2 * page == MOBA_BLOCK and kt_pool.shape[1] == w
    nb = n_pages // 2
    k_sel = min(MOBA_TOPK, nb + 1)
    lp = k_new.shape[1]
    per_b = lambda r: pl.BlockSpec((1, r, w), lambda b, pt: (b, 0, 0))
    pool_bytes = 2 * n_pages * w * page * 4
    est = 2 * pool_bytes + 8 * nb * rows * 2 * page * 4
    return pl.pallas_call(
        functools.partial(_moba_sample_kernel, n_seq=bs, n_pages=n_pages, l_new=l_new, k_sel=k_sel),
        grid_spec=pltpu.PrefetchScalarGridSpec(
            num_scalar_prefetch=1,
            grid=(bs,),
            in_specs=[per_b(rows), per_b(lp), per_b(lp),
                      pl.BlockSpec(memory_space=pl.ANY), pl.BlockSpec(memory_space=pl.ANY)],
            out_specs=per_b(rows),
            scratch_shapes=[
                pltpu.VMEM((2, n_pages, w, page), F32),
                pltpu.VMEM((2, n_pages, w, page), F32),
                pltpu.SemaphoreType.DMA((2, 2)),
                pltpu.VMEM((nb, w), F32),
                pltpu.VMEM((nb, rows, 2 * page), F32),
                pltpu.VMEM((nb, rows, 2 * page), BF16),
            ],
        ),
        out_shape=jax.ShapeDtypeStruct((bs, rows, w), F32),
        compiler_params=pltpu.CompilerParams(
            dimension_semantics=("arbitrary",),
            vmem_limit_bytes=min(int(est * 1.25), VMEM_BYTES_V7X - 8 * 2 ** 20)),
        name="moba_sample",
    )(page_table, q_bd, k_new, v_new, kt_pool, vt_pool)


def _pack_w_in_a(w):
    d = w.shape[0]
    qkv_gate = w[:, :4 * MIX_W]
    ab = w[:, 4 * MIX_W:4 * MIX_W + 2 * N_HEADS]
    q_mem = w[:, 4 * MIX_W + 2 * N_HEADS:]
    ab = jnp.pad(ab, ((0, 0), (0, LANES - 2 * N_HEADS)))
    return jnp.concatenate([qkv_gate, q_mem, ab], axis=1).astype(BF16)


def _pad_rows(a, rows):
    return jnp.pad(a, ((0, 0), (0, rows - a.shape[1]), (0, 0)))


def _trunk(p, x3, pos_base, mem_k, mem_v, conv0, s0, paged):
    b, l, d = x3.shape
    m = b * l
    x = x3.reshape(m, d)
    sample = paged is not None
    ffn = lambda x, tag, i, fg=None: _ffn(x, p[tag + "_norm"][i], p[tag + "_w1"], p[tag + "_w3"],
                                          p[tag + "_w2"], i, fg)
    lq = -(-l // SUBLANES) * SUBLANES

    def memory(q_mem, layer):
        q3 = _pad_rows(q_mem.reshape(b, l, MEM_W), lq)
        mo = _mem_attn(q3, mem_k[layer], mem_v[layer])
        return mo[:, :l].reshape(m, MEM_W)

    x = ffn(x, "ffn1", 0)
    if l % IN_PROJ_TILE == 0:
        qkv, gate, q_mem, ab, new_conv = _in_proj_conv(x, p["mix_norm"][0], p["w_in_a"], conv0,
                                                       p["dn_conv_w"], l)
        qkv = qkv.reshape(b, l, 3 * MIX_W)
    else:
        qkv, gate, q_mem, ab = _norm_proj(x, p["mix_norm"][0], p["w_in_a"],
                                          (3 * MIX_W, MIX_W, MEM_W, LANES), "in_proj_a")
        qkv, new_conv = _conv(qkv.reshape(b, l, 3 * MIX_W), conv0, p["dn_conv_w"])
    lp = -(-l // DN_CHUNK) * DN_CHUNK
    o, s_fin = _gdn(_pad_rows(qkv, lp), _pad_rows(gate.reshape(b, l, MIX_W), lp),
                    _pad_rows(ab.reshape(b, l, LANES), lp),
                    p["dn_a_log"], p["dn_dt_bias"], p["dn_out_norm"], s0, l)
    o = o[:, :l].reshape(m, MIX_W)
    x = _out_proj(x, o, memory(q_mem, 0), p["w_out"][0], grouped=False)
    x = ffn(x, "ffn2", 0)

    if sample:
        tables = _rope_tables(m, l, pos_base)
    else:
        tables = _rope_tables(l, l, pos_base)
    kv = _kv_proj(x, p["kv_norm"], p["w_kv"], tables, l, for_prompt=not sample)
    x = ffn(x, "ffn1", 1)
    qh, q_mem = _q_proj(x, p["mix_norm"][1], p["w_in_b"], tables)
    if sample:
        page_table, kt_pool, vt_pool = paged
        k_new, v_new = kv
        k_out = k_new.reshape(b, l, N_KV_B, HEAD_DIM)
        v_out = v_new.reshape(b, l, N_KV_B, HEAD_DIM)
        eye = jnp.eye(N_KV_B, dtype=F32)
        q5 = qh.reshape(N_KV_B, b, l, KV_GROUP, HEAD_DIM).transpose(1, 0, 3, 2, 4)
        q_bd = (q5[:, :, :, :, None, :] * eye[None, :, None, None, :, None]).reshape(
            b, N_HEADS * l, N_KV_B * HEAD_DIM)
        o_bd = _moba_sample(q_bd, page_table, kt_pool, vt_pool,
                            _pad_rows(k_new.reshape(b, l, -1), lq),
                            _pad_rows(v_new.reshape(b, l, -1), lq), l)
        o6 = o_bd.reshape(b, N_KV_B, KV_GROUP, l, N_KV_B, HEAD_DIM)
        o = (o6 * eye[None, :, None, None, :, None]).sum(axis=4)
        o = o.transpose(0, 3, 1, 2, 4).reshape(m, MIX_W)
        x = _out_proj(x, o, memory(q_mem, 1), p["w_out"][1], grouped=False)
    else:
        k_t, v_t, kh, vt, kbar = kv
        to_rows = lambda a: a.reshape(b, N_KV_B, HEAD_DIM, l).transpose(0, 3, 1, 2)
        k_out, v_out = to_rows(k_t), to_rows(v_t)
        nb = l // MOBA_BLOCK
        kbar = kbar.reshape(b, nb, N_KV_B, HEAD_DIM).transpose(2, 0, 1, 3)
        o4 = _moba_prompt(qh, kh, vt, kbar, b, l)
        x = _out_proj(x, o4, memory(q_mem, 1), p["w_out"][1], grouped=True)
    y = ffn(x, "ffn2", 1, p["final_norm"])
    return y.reshape(b, l, d), new_conv[None], s_fin[None], k_out, v_out


def kernel(x_prompt, x_sample, mem_prompt, cache_mem_k, cache_mem_v, state_dn_conv, state_dn_S,
           cache_kv_k, cache_kv_v, page_table,
           ffn1_norm, ffn1_w1, ffn1_w3, ffn1_w2, mix_norm, w_in_a, w_in_b, w_out,
           dn_conv_w, dn_a_log, dn_dt_bias, dn_out_norm, kv_norm, w_kv, mem_norm, w_mem_kv,
           ffn2_norm, ffn2_w1, ffn2_w3, ffn2_w2, final_norm):
    assert w_in_a.shape[0] == 1 and w_in_b.shape[0] == 1 and ffn1_w1.shape[0] == 2
    bf = lambda a: a.astype(BF16)
    p = dict(ffn1_norm=ffn1_norm, ffn1_w1=bf(ffn1_w1), ffn1_w3=bf(ffn1_w3), ffn1_w2=bf(ffn1_w2),
             ffn2_norm=ffn2_norm, ffn2_w1=bf(ffn2_w1), ffn2_w3=bf(ffn2_w3), ffn2_w2=bf(ffn2_w2),
             mix_norm=mix_norm, w_in_a=_pack_w_in_a(w_in_a[0]), w_in_b=bf(w_in_b[0]), w_out=bf(w_out),
             dn_conv_w=dn_conv_w[0], dn_a_log=dn_a_log[0], dn_dt_bias=dn_dt_bias[0],
             dn_out_norm=dn_out_norm[0], kv_norm=kv_norm, w_kv=bf(w_kv), final_norm=final_norm)

    bp, lp, _ = x_prompt.shape
    assert lp % MOBA_BLOCK == 0
    n_mem = mem_prompt.shape[1]
    n_layers = mem_norm.shape[0]
    mem_t = [_mem_kv(mem_prompt, mem_norm[layer], bf(w_mem_kv[layer])) for layer in range(n_layers)]
    mk = [t[0] for t in mem_t]
    mv = [t[1] for t in mem_t]
    conv0 = jnp.zeros((bp, CONV_W - 1, 3 * MIX_W), F32)
    s0 = jnp.zeros((bp, N_HEADS, HEAD_DIM, HEAD_DIM), F32)
    y_p, conv_p, s_p, k_p, v_p = _trunk(p, x_prompt, 0, mk, mv, conv0, s0, None)
    token_major = lambda ts: jnp.stack(ts).reshape(n_layers, bp, N_MEM_HEADS, HEAD_DIM, n_mem).transpose(
        0, 1, 4, 2, 3)
    mem_k_p = token_major(mk)
    mem_v_p = token_major(mv)

    bs = x_sample.shape[0]
    n_pool, page = cache_kv_k.shape[:2]
    past_len = page_table.shape[1] * page
    assert past_len % MOBA_BLOCK == 0 and x_sample.shape[1] <= MOBA_BLOCK
    token_minor = lambda c: jnp.transpose(c, (0, 1, 3, 4, 2)).reshape(c.shape[0], bs, MEM_W, n_mem)
    cmk = token_minor(cache_mem_k)
    cmv = token_minor(cache_mem_v)
    transposed_pages = lambda c: jnp.transpose(c, (0, 2, 3, 1)).reshape(n_pool, -1, page)
    paged = (page_table, transposed_pages(cache_kv_k), transposed_pages(cache_kv_v))
    y_s, conv_s, s_s, k_s, v_s = _trunk(p, x_sample, past_len, cmk, cmv, state_dn_conv[0],
                                        state_dn_S[0], paged)
    return (y_p, y_s, conv_p, s_p, k_p, v_p, mem_k_p, mem_v_p, conv_s, s_s, k_s, v_s)
```

```python
import functools
import math

import jax
import jax.numpy as jnp
from jax import lax
from jax.experimental import pallas as pl
from jax.experimental.pallas import tpu as pltpu

F32 = jnp.float32
BF16 = jnp.bfloat16
HI = lax.Precision.HIGHEST

HEAD_DIM = 64
N_HEADS = 12
MIX_W = N_HEADS * HEAD_DIM
N_MEM_HEADS = 4
MEM_W = N_MEM_HEADS * HEAD_DIM
N_KV_B = 4
KV_GROUP = N_HEADS // N_KV_B
GROUP_W = KV_GROUP * HEAD_DIM
CONV_W = 4
DN_CHUNK = 64
MOBA_BLOCK = 256
MOBA_TOPK = 3
ROT_DIM = HEAD_DIM // 4
ROPE_THETA = 500000.0
EPS = 1e-6
QK_SCALE = HEAD_DIM ** -0.5
LOG2E = math.log2(math.e)
ONES_ROWS = 16

LANES = 128
SUBLANES = 8
VMEM_BYTES_V7X = 64 * 2 ** 20
NEG_INF = float("-inf")

NT = (((1,), (1,)), ((), ()))
TN = (((0,), (0,)), ((), ()))


def _params(semantics, est_bytes):
    limit = min(max(int(est_bytes * 1.5), 16 * 2 ** 20), VMEM_BYTES_V7X - 8 * 2 ** 20)
    return pltpu.CompilerParams(dimension_semantics=semantics, vmem_limit_bytes=limit)


def _row_tile(m, pref):
    t = min(m, pref)
    while m % t or t % SUBLANES:
        t -= 1
    return t


def _rms(x, g):
    return x * lax.rsqrt(jnp.mean(x * x, axis=-1, keepdims=True) + EPS) * g


def _sigmoid(x):
    return 1.0 / (1.0 + jnp.exp(-x))


def _softplus(x):
    return jnp.maximum(x, 0.0) + jnp.log(1.0 + jnp.exp(-jnp.abs(x)))


def _bdot(a, b):
    return jnp.dot(a.astype(BF16), b.astype(BF16), preferred_element_type=F32)


def _bdot_nt(a, b):
    return lax.dot_general(a.astype(BF16), b.astype(BF16), NT, preferred_element_type=F32)


MXU_WIDTH = 2 * LANES


def _ffn_kernel(*refs, splits, final):
    if final:
        x_ref, g_ref, w1_ref, w3_ref, w2_ref, fg_ref, o_ref = refs
    else:
        x_ref, g_ref, w1_ref, w3_ref, w2_ref, o_ref = refs
    x = x_ref[...]
    h = _rms(x, g_ref[...]).astype(BF16)
    acc = None
    for lo, hi in splits:
        a = jnp.dot(h, w1_ref[:, lo:hi], preferred_element_type=F32)
        b = jnp.dot(h, w3_ref[:, lo:hi], preferred_element_type=F32)
        u = ((a * _sigmoid(a)) * b).astype(BF16)
        part = jnp.dot(u, w2_ref[lo:hi, :], preferred_element_type=F32)
        acc = part if acc is None else acc + part
    y = x + 0.5 * acc
    if final:
        y = _rms(y, fg_ref[...])
    o_ref[...] = y


def _ffn(x, g, w1, w3, w2, layer, final_gain=None):
    m, d = x.shape
    d_ff = w1.shape[2]
    tm = _row_tile(m, 512)
    tiles = d_ff // MXU_WIDTH
    cut = (tiles // 2) * MXU_WIDTH
    splits = ((0, cut), (cut, d_ff)) if cut else ((0, d_ff),)
    final = final_gain is not None
    resident = lambda shape: pl.BlockSpec((None,) + shape, lambda i: (layer, 0, 0),
                                          pipeline_mode=pl.Buffered(1))
    in_specs = [
        pl.BlockSpec((tm, d), lambda i: (i, 0)),
        pl.BlockSpec((1, d), lambda i: (0, 0)),
        resident((d, d_ff)), resident((d, d_ff)), resident((d_ff, d)),
    ]
    args = [x, g.reshape(1, d), w1, w3, w2]
    if final:
        in_specs.append(pl.BlockSpec((1, d), lambda i: (0, 0)))
        args.append(final_gain.reshape(1, d))
    widest = max(hi - lo for lo, hi in splits)
    est = 4 * tm * d * 4 + 3 * d * d_ff * 2 + 2 * tm * d * 4 + 3 * tm * widest * 4
    return pl.pallas_call(
        functools.partial(_ffn_kernel, splits=splits, final=final),
        grid=(m // tm,),
        in_specs=in_specs,
        out_specs=pl.BlockSpec((tm, d), lambda i: (i, 0)),
        out_shape=jax.ShapeDtypeStruct((m, d), F32),
        compiler_params=_params(("parallel",), est),
        name="ffn_final" if final else "ffn",
    )(*args)


def _norm_proj_kernel(x_ref, g_ref, w_ref, *out_refs, splits):
    h = _rms(x_ref[...], g_ref[...]).astype(BF16)
    for o_ref, (a, b) in zip(out_refs, splits):
        o_ref[...] = jnp.dot(h, w_ref[:, a:b], preferred_element_type=F32)


def _norm_proj(x, g, w, widths, name):
    m, d = x.shape
    n = w.shape[1]
    assert sum(widths) == n and all(c % LANES == 0 for c in widths)
    tm = _row_tile(m, 256)
    splits, a = [], 0
    for c in widths:
        splits.append((a, a + c))
        a += c
    est = 2 * tm * d * 4 + 2 * d * n * 2 + 3 * tm * n * 4
    return pl.pallas_call(
        functools.partial(_norm_proj_kernel, splits=tuple(splits)),
        grid=(m // tm,),
        in_specs=[
            pl.BlockSpec((tm, d), lambda i: (i, 0)),
            pl.BlockSpec((1, d), lambda i: (0, 0)),
            pl.BlockSpec((d, n), lambda i: (0, 0)),
        ],
        out_specs=[pl.BlockSpec((tm, c), lambda i: (i, 0)) for c in widths],
        out_shape=[jax.ShapeDtypeStruct((m, c), F32) for c in widths],
        compiler_params=_params(("parallel",), est),
        name=name,
    )(x, g.reshape(1, d), w)


def _rope_table_kernel(inv_ref, c_ref, s1_ref, s2_ref, *, period, base, tl):
    i = pl.program_id(0)
    row = lax.broadcasted_iota(jnp.int32, (tl, LANES), 0) + i * tl
    pos = base + lax.rem(row, period)
    ang = pos.astype(F32) * inv_ref[...]
    in_head = lax.broadcasted_iota(jnp.int32, (tl, LANES), 1) % HEAD_DIM
    half = ROT_DIM // 2
    c = jnp.cos(ang)
    s = jnp.sin(ang)
    c_ref[...] = jnp.where(in_head < ROT_DIM, c, 1.0)
    s1_ref[...] = jnp.where(in_head < half, -s, 0.0)
    s2_ref[...] = jnp.where((in_head >= half) & (in_head < ROT_DIM), s, 0.0)


def _rope_tables(n_rows, period, base):
    half = ROT_DIM // 2
    inv = ROPE_THETA ** (-jnp.arange(half, dtype=F32) * 2.0 / ROT_DIM)
    in_head = jnp.arange(LANES) % HEAD_DIM
    inv_lane = jnp.where(in_head < ROT_DIM, inv[in_head % half], 0.0).astype(F32).reshape(1, LANES)
    tl = _row_tile(n_rows, 1024)
    shp = jax.ShapeDtypeStruct((n_rows, LANES), F32)
    return pl.pallas_call(
        functools.partial(_rope_table_kernel, period=period, base=base, tl=tl),
        grid=(n_rows // tl,),
        in_specs=[pl.BlockSpec((1, LANES), lambda i: (0, 0))],
        out_specs=[pl.BlockSpec((tl, LANES), lambda i: (i, 0))] * 3,
        out_shape=[shp] * 3,
        compiler_params=_params(("parallel",), 16 * tl * LANES * 4),
        name="rope_tables",
    )(inv_lane)


def _apply_rope(t, c, s1, s2):
    w = t.shape[1]
    reps = w // LANES
    half = ROT_DIM // 2
    tile = lambda a: jnp.concatenate([a] * reps, axis=1) if reps > 1 else a
    up = pltpu.roll(t, w - half, 1)
    down = pltpu.roll(t, half, 1)
    return t * tile(c) + up * tile(s1) + down * tile(s2)


def _kv_proj_kernel(x_ref, g_ref, w_ref, c_ref, s1_ref, s2_ref, *out_refs, for_prompt):
    h = _rms(x_ref[...], g_ref[...]).astype(BF16)
    kv_w = N_KV_B * HEAD_DIM
    k = jnp.dot(h, w_ref[:, :kv_w], preferred_element_type=F32)
    v = jnp.dot(h, w_ref[:, kv_w:], preferred_element_type=F32)
    k = _apply_rope(k, c_ref[...], s1_ref[...], s2_ref[...])
    if for_prompt:
        kt_ref, vt_ref, kh_ref, vtb_ref, kbar_ref, v_scr = out_refs
        v_scr[...] = v
        v_t = v_scr[...].T
        kt_ref[0] = k.T
        vt_ref[0] = v_t
        for hk in range(N_KV_B):
            sl = slice(hk * HEAD_DIM, (hk + 1) * HEAD_DIM)
            k_h = k[:, sl]
            kh_ref[hk] = jnp.concatenate([k_h, jnp.zeros_like(k_h)], axis=1).astype(BF16)
            vtb_ref[0, hk, 0:HEAD_DIM, :] = v_t[sl, :].astype(BF16)
            vtb_ref[0, hk, HEAD_DIM:, :] = jnp.ones((ONES_ROWS, v_t.shape[1]), BF16)
        kbar_ref[0] = jnp.sum(k, axis=0, keepdims=True) * (1.0 / MOBA_BLOCK)
    else:
        out_refs[0][...] = k
        out_refs[1][...] = v


def _kv_proj(x, g, w, tables, seq_len, for_prompt):
    m, d = x.shape
    kv_w = N_KV_B * HEAD_DIM
    tm = MOBA_BLOCK if for_prompt else _row_tile(m, 256)
    n_tab = tables[0].shape[0] // tm
    row = lambda i: (i, 0)
    tab = lambda i: (i % n_tab, 0)
    if for_prompt:
        nb = seq_len // tm
        t_spec = pl.BlockSpec((1, kv_w, tm), lambda i: (i // nb, 0, i % nb))
        t_shape = jax.ShapeDtypeStruct((m // seq_len, kv_w, seq_len), F32)
        vt_rows = HEAD_DIM + ONES_ROWS
        out_specs = [t_spec, t_spec,
                     pl.BlockSpec((N_KV_B, tm, LANES), lambda i: (0, i, 0)),
                     pl.BlockSpec((1, N_KV_B, vt_rows, tm), lambda i: (i, 0, 0, 0)),
                     pl.BlockSpec((1, 1, kv_w), lambda i: (i, 0, 0))]
        out_shape = [t_shape, t_shape,
                     jax.ShapeDtypeStruct((N_KV_B, m, LANES), BF16),
                     jax.ShapeDtypeStruct((m // tm, N_KV_B, vt_rows, tm), BF16),
                     jax.ShapeDtypeStruct((m // tm, 1, kv_w), F32)]
    else:
        out_specs = [pl.BlockSpec((tm, kv_w), row), pl.BlockSpec((tm, kv_w), row)]
        out_shape = [jax.ShapeDtypeStruct((m, kv_w), F32)] * 2
    est = 2 * tm * d * 4 + 2 * d * 2 * kv_w * 2 + 12 * tm * kv_w * 4
    return pl.pallas_call(
        functools.partial(_kv_proj_kernel, for_prompt=for_prompt),
        grid=(m // tm,),
        in_specs=[
            pl.BlockSpec((tm, d), row),
            pl.BlockSpec((1, d), lambda i: (0, 0)),
            pl.BlockSpec((d, 2 * kv_w), lambda i: (0, 0)),
            pl.BlockSpec((tm, LANES), tab),
            pl.BlockSpec((tm, LANES), tab),
            pl.BlockSpec((tm, LANES), tab),
        ],
        out_specs=out_specs,
        out_shape=out_shape,
        scratch_shapes=[pltpu.VMEM((tm, kv_w), F32)] if for_prompt else [],
        compiler_params=_params(("parallel",), est),
        name="kv_proj",
    )(x, g.reshape(1, d), w, *tables)


def _q_proj_kernel(x_ref, g_ref, w_ref, c_ref, s1_ref, s2_ref, qh_ref, qm_ref):
    h = _rms(x_ref[...], g_ref[...]).astype(BF16)
    q = jnp.dot(h, w_ref[:, :MIX_W], preferred_element_type=F32)
    qm_ref[...] = jnp.dot(h, w_ref[:, MIX_W:], preferred_element_type=F32)
    q = _apply_rope(q, c_ref[...], s1_ref[...], s2_ref[...]) * (QK_SCALE * LOG2E)
    for hk in range(N_KV_B):
        qh_ref[hk] = q[:, hk * GROUP_W:(hk + 1) * GROUP_W]


def _q_proj(x, g, w, tables):
    m, d = x.shape
    n = w.shape[1]
    tm = _row_tile(m, 256)
    n_tab = tables[0].shape[0] // tm
    row = lambda i: (i, 0)
    tab = lambda i: (i % n_tab, 0)
    est = 2 * tm * d * 4 + 2 * d * n * 2 + 16 * tm * MIX_W * 4
    return pl.pallas_call(
        _q_proj_kernel,
        grid=(m // tm,),
        in_specs=[
            pl.BlockSpec((tm, d), row),
            pl.BlockSpec((1, d), lambda i: (0, 0)),
            pl.BlockSpec((d, n), lambda i: (0, 0)),
            pl.BlockSpec((tm, LANES), tab),
            pl.BlockSpec((tm, LANES), tab),
            pl.BlockSpec((tm, LANES), tab),
        ],
        out_specs=[
            pl.BlockSpec((N_KV_B, tm, GROUP_W), lambda i: (0, i, 0)),
            pl.BlockSpec((tm, MEM_W), row),
        ],
        out_shape=[
            jax.ShapeDtypeStruct((N_KV_B, m, GROUP_W), F32),
            jax.ShapeDtypeStruct((m, MEM_W), F32),
        ],
        compiler_params=_params(("parallel",), est),
        name="q_proj",
    )(x, g.reshape(1, d), w, *tables)


def _conv_kernel(u_ref, prev_ref, buf_ref, w_ref, y_ref, nb_ref, pad_scr, *, tl, n_t):
    i = pl.program_id(1)
    halo = CONV_W - 1
    lo = SUBLANES - halo
    pad_scr[SUBLANES:SUBLANES + tl, :] = u_ref[0]

    @pl.when(i == 0)
    def _():
        pad_scr[lo:SUBLANES, :] = buf_ref[0]

    @pl.when(i > 0)
    def _():
        p = prev_ref[0]
        pad_scr[lo:SUBLANES, :] = p[p.shape[0] - halo:, :]

    y = pad_scr[lo:lo + tl, :] * w_ref[0:1, :]
    for j in range(1, CONV_W):
        y = y + pad_scr[lo + j:lo + j + tl, :] * w_ref[j:j + 1, :]
    y_ref[0] = y * _sigmoid(y)

    @pl.when(i == n_t - 1)
    def _():
        nb_ref[0] = pad_scr[lo + tl:SUBLANES + tl, :]


def _conv(u, buf, w):
    b, l, c = u.shape
    tl = _row_tile(l, 512) if l % SUBLANES == 0 else l
    n_t = l // tl
    pr = min(SUBLANES, l)
    per = tl // pr
    est = 6 * tl * c * 4
    return pl.pallas_call(
        functools.partial(_conv_kernel, tl=tl, n_t=n_t),
        grid=(b, n_t),
        in_specs=[
            pl.BlockSpec((1, tl, c), lambda bi, i: (bi, i, 0)),
            pl.BlockSpec((1, pr, c), lambda bi, i: (bi, jnp.maximum(i * per - 1, 0), 0)),
            pl.BlockSpec((1, CONV_W - 1, c), lambda bi, i: (bi, 0, 0)),
            pl.BlockSpec((CONV_W, c), lambda bi, i: (0, 0)),
        ],
        out_specs=[
            pl.BlockSpec((1, tl, c), lambda bi, i: (bi, i, 0)),
            pl.BlockSpec((1, CONV_W - 1, c), lambda bi, i: (bi, 0, 0)),
        ],
        out_shape=[
            jax.ShapeDtypeStruct((b, l, c), F32),
            jax.ShapeDtypeStruct((b, CONV_W - 1, c), F32),
        ],
        scratch_shapes=[pltpu.VMEM((tl + SUBLANES, c), F32)],
        compiler_params=_params(("parallel", "arbitrary"), est),
        name="short_conv",
    )(u, u, buf, w)


IN_PROJ_TILE = 256


def _in_proj_conv_kernel(x_ref, g_ref, w_ref, buf_ref, cw_ref, qkv_ref, gate_ref, qm_ref, ab_ref,
                         nb_ref, pad_scr, *, tm, n_t):
    i = pl.program_id(1)
    halo = CONV_W - 1
    lo = SUBLANES - halo
    c3 = 3 * MIX_W
    h = _rms(x_ref[...], g_ref[...]).astype(BF16)

    @pl.when(i == 0)
    def _():
        pad_scr[lo:SUBLANES, :] = buf_ref[0]

    pad_scr[SUBLANES:SUBLANES + tm, :] = jnp.dot(h, w_ref[:, :c3], preferred_element_type=F32)
    gate_ref[...] = jnp.dot(h, w_ref[:, c3:c3 + MIX_W], preferred_element_type=F32)
    qm_ref[...] = jnp.dot(h, w_ref[:, c3 + MIX_W:c3 + MIX_W + MEM_W], preferred_element_type=F32)
    ab_ref[...] = jnp.dot(h, w_ref[:, c3 + MIX_W + MEM_W:], preferred_element_type=F32)
    y = pad_scr[lo:lo + tm, :] * cw_ref[0:1, :]
    for j in range(1, CONV_W):
        y = y + pad_scr[lo + j:lo + j + tm, :] * cw_ref[j:j + 1, :]
    qkv_ref[...] = y * _sigmoid(y)
    tail = pad_scr[lo + tm:SUBLANES + tm, :]

    @pl.when(i == n_t - 1)
    def _():
        nb_ref[0] = tail

    pad_scr[lo:SUBLANES, :] = tail


def _in_proj_conv(x, g, w, buf, conv_w, seq_len):
    m, d = x.shape
    n = w.shape[1]
    c3 = 3 * MIX_W
    tm = IN_PROJ_TILE
    n_t = seq_len // tm
    b = m // seq_len
    row = lambda bi, i: (bi * n_t + i, 0)
    const = lambda bi, i: (0, 0)
    widths = (c3, MIX_W, MEM_W, n - c3 - MIX_W - MEM_W)
    est = 2 * tm * d * 4 + 2 * d * n * 2 + 4 * tm * n * 4
    return pl.pallas_call(
        functools.partial(_in_proj_conv_kernel, tm=tm, n_t=n_t),
        grid=(b, n_t),
        in_specs=[
            pl.BlockSpec((tm, d), row),
            pl.BlockSpec((1, d), const),
            pl.BlockSpec((d, n), const),
            pl.BlockSpec((1, CONV_W - 1, c3), lambda bi, i: (bi, 0, 0)),
            pl.BlockSpec((CONV_W, c3), const),
        ],
        out_specs=[pl.BlockSpec((tm, c), row) for c in widths]
        + [pl.BlockSpec((1, CONV_W - 1, c3), lambda bi, i: (bi, 0, 0))],
        out_shape=[jax.ShapeDtypeStruct((m, c), F32) for c in widths]
        + [jax.ShapeDtypeStruct((b, CONV_W - 1, c3), F32)],
        scratch_shapes=[pltpu.VMEM((tm + SUBLANES, c3), F32)],
        compiler_params=_params(("parallel", "arbitrary"), est),
        name="in_proj_conv",
    )(x, g.reshape(1, d), w, buf, conv_w)


HEADS_PER_SLAB = LANES * 2 // HEAD_DIM
SLAB_W = HEADS_PER_SLAB * HEAD_DIM
N_SLABS = N_HEADS // HEADS_PER_SLAB


def _split2(x):
    hi = x.astype(BF16)
    return hi, (x - hi.astype(F32)).astype(BF16)


def _split3(x):
    hi = x.astype(BF16)
    r = x - hi.astype(F32)
    mid = r.astype(BF16)
    return hi, mid, (r - mid.astype(F32)).astype(BF16)


def _dot_right01(x, sel):
    hi, mid, lo = _split3(x)
    d = lambda a: jnp.dot(a, sel, preferred_element_type=F32)
    return d(hi) + (d(mid) + d(lo))


def _dot_left01(sel, x):
    hi, mid, lo = _split3(x)
    d = lambda a: jnp.dot(sel, a, preferred_element_type=F32)
    return d(hi) + (d(mid) + d(lo))


def _dot3(a, b_hi, b_lo):
    a_hi, a_lo = _split2(a)
    d = lambda x, y: jnp.dot(x, y, preferred_element_type=F32)
    return d(a_hi, b_hi) + (d(a_hi, b_lo) + d(a_lo, b_hi))


def _block_diag(x, mask):
    return jnp.concatenate([x] * HEADS_PER_SLAB, axis=0) * mask


def _block_diag_pieces(x, mask):
    hi, lo = _split2(x)
    return _block_diag(hi, mask), _block_diag(lo, mask)


def _unit_lower_inverses(ms, eye_t, mask):
    c = ms[0].shape[0]
    xs = [eye_t - m for m in ms]
    ps = [_dot3(m, *_block_diag_pieces(m, mask)) for m in ms]
    power = 2
    while 2 * power < c:
        xps = [_dot3(jnp.concatenate([x, p], axis=0), *_block_diag_pieces(p, mask))
               for x, p in zip(xs, ps)]
        xs = [x + xp[:c] for x, xp in zip(xs, xps)]
        ps = [xp[c:] for xp in xps]
        power *= 2
    return [x + _dot3(x, *_block_diag_pieces(p, mask)) for x, p in zip(xs, ps)]


def _gdn_kernel(q_ref, k_ref, v_ref, gate_ref, ab_ref, alog_ref, dt_ref, gain_ref, s0_ref,
                o_ref, sfin_ref,
                s_scr, qn_scr, kn_scr, gcb_scr, bb_scr, u_scr, w_scr, a_scr, qg_scr, kd_scr, egl_scr,
                *, t_rows, n_t, l_valid):
    t = pl.program_id(1)
    c = DN_CHUNK
    d = HEAD_DIM
    n_chunks = t_rows // c

    r_s = lax.broadcasted_iota(jnp.int32, (SLAB_W, SLAB_W), 0)
    c_s = lax.broadcasted_iota(jnp.int32, (SLAB_W, SLAB_W), 1)
    same_head = (r_s // d) == (c_s // d)
    bd_f32 = jnp.where(same_head, 1.0, 0.0).astype(F32)
    bd_mask = bd_f32.astype(BF16)
    r_t = lax.broadcasted_iota(jnp.int32, (c, SLAB_W), 0)
    c_t = lax.broadcasted_iota(jnp.int32, (c, SLAB_W), 1) % d
    lower_t = r_t >= c_t
    strict_t = r_t > c_t
    eye_t = jnp.where(r_t == c_t, 1.0, 0.0).astype(F32)
    r_c = lax.broadcasted_iota(jnp.int32, (c, c), 0)
    c_c = lax.broadcasted_iota(jnp.int32, (c, c), 1)
    tril = jnp.where(r_c >= c_c, 1.0, 0.0).astype(BF16)
    er = lax.broadcasted_iota(jnp.int32, (LANES, MIX_W), 0)
    ec = lax.broadcasted_iota(jnp.int32, (LANES, MIX_W), 1) // d
    sel_g = jnp.where(er == ec, 1.0, 0.0).astype(BF16)
    sel_b = jnp.where(er == ec + N_HEADS, 1.0, 0.0).astype(BF16)

    @pl.when(t == 0)
    def _():
        for s in range(N_SLABS):
            rows = jnp.concatenate([s0_ref[0, s * HEADS_PER_SLAB + h] for h in range(HEADS_PER_SLAB)],
                                   axis=0)
            s_scr[s] = jnp.concatenate([rows] * HEADS_PER_SLAB, axis=1) * bd_f32

    ab = ab_ref[0]
    row = lax.broadcasted_iota(jnp.int32, ab.shape, 0) + t * t_rows
    valid = row < l_valid
    g = jnp.where(valid, -jnp.exp(alog_ref[...]) * _softplus(ab + dt_ref[...]), 0.0)
    beta = jnp.where(valid, _sigmoid(ab), 0.0)
    gc = jnp.concatenate([_dot_left01(tril, g[i * c:(i + 1) * c]) for i in range(n_chunks)], axis=0)
    gcb_scr[...] = _dot_right01(gc, sel_g)
    bb_scr[...] = _dot_right01(beta, sel_b)
    for s in range(N_SLABS):
        cs = slice(s * SLAB_W, (s + 1) * SLAB_W)
        for src, dst, scale in ((q_ref, qn_scr, QK_SCALE), (k_ref, kn_scr, 1.0)):
            x = src[0, :, cs]
            hi, lo = _split2(x * x)
            ssq = (jnp.dot(hi, bd_mask, preferred_element_type=F32)
                   + jnp.dot(lo, bd_mask, preferred_element_type=F32))
            dst[:, cs] = x * (lax.rsqrt(ssq + EPS) * scale)

    per_iter = math.gcd(n_chunks, 2)
    slabs = [slice(s * SLAB_W, (s + 1) * SLAB_W) for s in range(N_SLABS)]

    def factors(it, carry):
        items = []
        for u in range(per_iter):
            ci = it * per_iter + u
            rows = pl.ds(pl.multiple_of(ci * c, c), c)
            items += [(ci, rows, cs) for cs in slabs]
        pre = []
        for ci, rows, cs in items:
            gcb = gcb_scr[rows, cs]
            kn = kn_scr[rows, cs]
            qn = qn_scr[rows, cs]
            b_ = bb_scr[rows, cs]
            kb = kn * b_
            g_row = jnp.sum(gcb * eye_t, axis=0, keepdims=True)
            dec = jnp.exp(jnp.where(lower_t, gcb - g_row, NEG_INF))
            both = lax.dot_general(jnp.concatenate([kb, qn], axis=0).astype(BF16),
                                   _block_diag(kn.astype(BF16), bd_mask), NT,
                                   preferred_element_type=F32)
            pre.append((gcb, kn, qn, kb, dec, both))
        ms = []
        for (ci, rows, cs), (gcb, kn, qn, kb, dec, both) in zip(items, pre):
            ms.append(jnp.where(strict_t, both[:c] * dec, 0.0))
            a_scr[rows, cs] = (both[c:] * dec).astype(BF16)
        tinvs = _unit_lower_inverses(ms, eye_t, bd_mask)
        sols = []
        for (ci, rows, cs), (gcb, kn, qn, kb, dec, both), tinv in zip(items, pre, tinvs):
            eg = jnp.exp(gcb)
            v_hi, v_lo = _block_diag_pieces(v_ref[0, rows, cs] * bb_scr[rows, cs], bd_mask)
            k_hi, k_lo = _block_diag_pieces(kb * eg, bd_mask)
            sols.append(_dot3(tinv, jnp.concatenate([v_hi, k_hi], axis=1),
                              jnp.concatenate([v_lo, k_lo], axis=1)))
            glast = gcb[c - 1:c, :]
            qg_scr[rows, cs] = (qn * eg).astype(BF16)
            kd_scr[rows, cs] = (kn * jnp.exp(glast - gcb)).astype(BF16)
            egl_scr[pl.ds(ci, 1), cs] = jnp.exp(glast)
        for (ci, rows, cs), sol in zip(items, sols):
            u_scr[rows, cs] = sol[:, :SLAB_W]
            w_scr[rows, cs] = sol[:, SLAB_W:].astype(BF16)
        return carry

    lax.fori_loop(0, n_chunks // per_iter, factors, 0)

    gain = gain_ref[...]

    def advance(ci, rows):
        olds = [s_scr[s] for s in range(N_SLABS)]
        rs = [jnp.dot(jnp.concatenate([w_scr[rows, cs], qg_scr[rows, cs]], axis=0),
                      s_old.astype(BF16), preferred_element_type=F32)
              for cs, s_old in zip(slabs, olds)]
        v_news = [(u_scr[rows, cs] - r[:c]).astype(BF16) for cs, r in zip(slabs, rs)]
        upds = [lax.dot_general(kd_scr[rows, cs], v_new, TN, preferred_element_type=F32)
                for cs, v_new in zip(slabs, v_news)]
        os_ = [r[c:] + jnp.dot(a_scr[rows, cs], _block_diag(v_new, bd_mask),
                               preferred_element_type=F32)
               for cs, r, v_new in zip(slabs, rs, v_news)]
        for s, (cs, s_old, upd) in enumerate(zip(slabs, olds, upds)):
            s_scr[s] = (s_old * egl_scr[pl.ds(ci, 1), cs] + upd) * bd_f32
        return os_

    def emit(rows, os_):
        ssqs = []
        for o in os_:
            hi, lo = _split2(o * o)
            ssqs.append(jnp.dot(hi, bd_mask, preferred_element_type=F32)
                        + jnp.dot(lo, bd_mask, preferred_element_type=F32))
        for cs, o, ssq in zip(slabs, os_, ssqs):
            gt = gate_ref[0, rows, cs]
            o_ref[0, rows, cs] = (o * lax.rsqrt(ssq * (1.0 / d) + EPS) * gain[:, cs]
                                  * (gt * _sigmoid(gt)))

    def recur(it, carry):
        done = []
        for u in range(per_iter):
            ci = it * per_iter + u
            rows = pl.ds(pl.multiple_of(ci * c, c), c)
            done.append((rows, advance(ci, rows)))
        for rows, os_ in done:
            emit(rows, os_)
        return carry

    lax.fori_loop(0, n_chunks // per_iter, recur, 0)

    @pl.when(t == n_t - 1)
    def _():
        for s in range(N_SLABS):
            full = s_scr[s]
            for h in range(HEADS_PER_SLAB):
                sfin_ref[0, s * HEADS_PER_SLAB + h] = full[h * d:(h + 1) * d, h * d:(h + 1) * d]


def _gdn(qkv, gate, ab, a_log, dt_bias, out_gain, s0, l_valid):
    b, lp, _ = qkv.shape
    assert lp % DN_CHUNK == 0
    t_rows = DN_CHUNK * math.gcd(lp // DN_CHUNK, 8)
    n_t = lp // t_rows
    n_chunks = t_rows // DN_CHUNK
    pad = lambda a: jnp.pad(a.astype(F32), (0, LANES - N_HEADS)).reshape(1, LANES)
    blk = lambda j: pl.BlockSpec((1, t_rows, MIX_W), lambda bi, t, j=j: (bi, t, j))
    state = pl.BlockSpec((1, N_HEADS, HEAD_DIM, HEAD_DIM), lambda bi, t: (bi, 0, 0, 0))
    vec = pl.BlockSpec((1, LANES), lambda bi, t: (0, 0))
    wide_f32 = pltpu.VMEM((t_rows, MIX_W), F32)
    wide_bf16 = pltpu.VMEM((t_rows, MIX_W), BF16)
    est = 10 * t_rows * MIX_W * 4 + 5 * t_rows * MIX_W * 4 + 4 * t_rows * MIX_W * 2 + 64 * SLAB_W * SLAB_W * 4
    return pl.pallas_call(
        functools.partial(_gdn_kernel, t_rows=t_rows, n_t=n_t, l_valid=l_valid),
        grid=(b, n_t),
        in_specs=[
            blk(0), blk(1), blk(2),
            pl.BlockSpec((1, t_rows, MIX_W), lambda bi, t: (bi, t, 0)),
            pl.BlockSpec((1, t_rows, LANES), lambda bi, t: (bi, t, 0)),
            vec, vec,
            pl.BlockSpec((1, MIX_W), lambda bi, t: (0, 0)),
            state,
        ],
        out_specs=[pl.BlockSpec((1, t_rows, MIX_W), lambda bi, t: (bi, t, 0)), state],
        out_shape=[
            jax.ShapeDtypeStruct((b, lp, MIX_W), F32),
            jax.ShapeDtypeStruct((b, N_HEADS, HEAD_DIM, HEAD_DIM), F32),
        ],
        scratch_shapes=[
            pltpu.VMEM((N_SLABS, SLAB_W, SLAB_W), F32),
            wide_f32, wide_f32,
            wide_f32, wide_f32,
            wide_f32,
            wide_bf16, wide_bf16, wide_bf16, wide_bf16,
            pltpu.VMEM((max(n_chunks, SUBLANES), MIX_W), F32),
        ],
        compiler_params=_params(("parallel", "arbitrary"), est),
        name="gated_delta_rule",
    )(qkv, qkv, qkv, gate, ab, pad(a_log), pad(dt_bias),
      jnp.tile(out_gain.astype(F32), N_HEADS).reshape(1, MIX_W), s0)


def _mem_kv_kernel(x_ref, g_ref, w_ref, kt_ref, vt_ref, kv_scr):
    h = _rms(x_ref[...], g_ref[...]).astype(BF16)
    kv_scr[...] = jnp.dot(h, w_ref[...], preferred_element_type=F32)
    kt_ref[0] = kv_scr[:, :MEM_W].T
    vt_ref[0] = kv_scr[:, MEM_W:].T


def _mem_kv(mem, g, w):
    b, n_mem, d = mem.shape
    shp = jax.ShapeDtypeStruct((b, MEM_W, n_mem), F32)
    spec = pl.BlockSpec((1, MEM_W, n_mem), lambda i: (i, 0, 0))
    return pl.pallas_call(
        _mem_kv_kernel,
        grid=(b,),
        in_specs=[
            pl.BlockSpec((n_mem, d), lambda i: (i, 0)),
            pl.BlockSpec((1, d), lambda i: (0, 0)),
            pl.BlockSpec((d, 2 * MEM_W), lambda i: (0, 0)),
        ],
        out_specs=[spec, spec],
        out_shape=[shp, shp],
        scratch_shapes=[pltpu.VMEM((n_mem, 2 * MEM_W), F32)],
        compiler_params=_params(("parallel",), 2 * n_mem * d * 4 + 2 * d * 2 * MEM_W * 2 + 8 * n_mem * MEM_W * 4),
        name="mem_kv",
    )(mem.reshape(b * n_mem, d), g.reshape(1, d), w)


def _mem_attn_kernel(q_ref, kt_ref, vt_ref, o_ref):
    q = q_ref[0]
    kt = kt_ref[0]
    vt = vt_ref[0]
    outs = []
    for h in range(N_MEM_HEADS):
        sl = slice(h * HEAD_DIM, (h + 1) * HEAD_DIM)
        s = _bdot(q[:, sl], kt[sl, :]) * QK_SCALE
        p = jnp.exp(s - jnp.max(s, axis=-1, keepdims=True))
        outs.append(_bdot_nt(p, vt[sl, :]) / jnp.sum(p, axis=-1, keepdims=True))
    o_ref[0] = jnp.concatenate(outs, axis=1)


def _mem_attn(q, mk_t, mv_t):
    b, l, w = q.shape
    n_mem = mk_t.shape[2]
    tl = _row_tile(l, 512)
    est = 4 * tl * w * 4 + 4 * n_mem * w * 4 + 8 * tl * n_mem * 4
    return pl.pallas_call(
        _mem_attn_kernel,
        grid=(b, l // tl),
        in_specs=[
            pl.BlockSpec((1, tl, w), lambda bi, i: (bi, i, 0)),
            pl.BlockSpec((1, w, n_mem), lambda bi, i: (bi, 0, 0)),
            pl.BlockSpec((1, w, n_mem), lambda bi, i: (bi, 0, 0)),
        ],
        out_specs=pl.BlockSpec((1, tl, w), lambda bi, i: (bi, i, 0)),
        out_shape=jax.ShapeDtypeStruct((b, l, w), F32),
        compiler_params=_params(("parallel", "parallel"), est),
        name="mem_attn",
    )(q, mk_t, mv_t)


def _out_proj_kernel(x_ref, o_ref, mo_ref, wo_ref, wm_ref, y_ref, *, grouped):
    y = x_ref[...] + _bdot(mo_ref[...], wm_ref[...])
    if grouped:
        for hk in range(N_KV_B):
            y = y + _bdot(o_ref[hk], wo_ref[hk])
    else:
        y = y + _bdot(o_ref[...], wo_ref[...])
    y_ref[...] = y


def _out_proj(x, o, mo, w_out, grouped):
    m, d = x.shape
    tm = _row_tile(m, 512)
    row = lambda i: (i, 0)
    if grouped:
        wo = w_out[:MIX_W].reshape(N_KV_B, GROUP_W, d)
        o_spec = pl.BlockSpec((N_KV_B, tm, GROUP_W), lambda i: (0, i, 0))
        wo_spec = pl.BlockSpec((N_KV_B, GROUP_W, d), lambda i: (0, 0, 0))
    else:
        wo = w_out[:MIX_W]
        o_spec = pl.BlockSpec((tm, MIX_W), row)
        wo_spec = pl.BlockSpec((MIX_W, d), lambda i: (0, 0))
    est = 6 * tm * d * 4 + 4 * tm * d * 4 + 4 * d * d * 2
    return pl.pallas_call(
        functools.partial(_out_proj_kernel, grouped=grouped),
        grid=(m // tm,),
        in_specs=[
            pl.BlockSpec((tm, d), row),
            o_spec,
            pl.BlockSpec((tm, MEM_W), row),
            wo_spec,
            pl.BlockSpec((MEM_W, d), lambda i: (0, 0)),
        ],
        out_specs=pl.BlockSpec((tm, d), row),
        out_shape=jax.ShapeDtypeStruct((m, d), F32),
        compiler_params=_params(("parallel",), est),
        name="out_proj",
    )(x, o, mo, wo, w_out[MIX_W:])


def _select_blocks(gate, n_past, k_sel):
    lane = lax.broadcasted_iota(jnp.int32, gate.shape, 1)
    lane_f = lane.astype(F32)
    gate = jnp.where(lane < n_past, gate, NEG_INF)
    sel = jnp.zeros(gate.shape, F32)
    for _ in range(k_sel):
        mx = jnp.max(gate, axis=1, keepdims=True)
        first = jnp.min(jnp.where(gate == mx, lane_f, float(gate.shape[1])), axis=1, keepdims=True)
        pick = (lane_f == first) & (mx > NEG_INF)
        sel = jnp.where(pick, 1.0, sel)
        gate = jnp.where(pick, NEG_INF, gate)
    return sel


def _select_blocks_t(gate, n_past, k_sel):
    blk = lax.broadcasted_iota(jnp.int32, gate.shape, 0)
    blk_f = blk.astype(F32)
    gate = jnp.where(blk < n_past, gate, NEG_INF)
    sel = jnp.zeros(gate.shape, F32)
    for _ in range(k_sel):
        mx = jnp.max(gate, axis=0, keepdims=True)
        first = jnp.min(jnp.where(gate == mx, blk_f, float(gate.shape[0])), axis=0, keepdims=True)
        pick = (blk_f == first) & (mx > NEG_INF)
        sel = jnp.where(pick, 1.0, sel)
        gate = jnp.where(pick, NEG_INF, gate)
    return sel


def _moba_prompt_kernel(q_ref, k_ref, vt_ref, kbar_ref, o_ref, q_scr, m_scr, acc_scr, sel_scr,
                        s_scr, p_scr, alpha_scr, *, k_sel):
    i = pl.program_id(2)
    blk = MOBA_BLOCK
    d = HEAD_DIM
    cw = LANES
    n_chunks = KV_GROUP * blk // cw
    n_blocks = sel_scr.shape[0]
    chunks = [slice(c * cw, (c + 1) * cw) for c in range(n_chunks)]
    q = q_ref[0]
    q3 = jnp.concatenate([q[:, g * d:(g + 1) * d] for g in range(KV_GROUP)], axis=0)
    q_scr[...] = jnp.concatenate([q3, jnp.zeros_like(q3)], axis=1).astype(BF16)

    def block_scores(k_blk, c):
        return lax.dot_general(k_blk, q_scr[c * cw:(c + 1) * cw, :], NT,
                               preferred_element_type=F32)

    k_own = k_ref[0, pl.ds(pl.multiple_of(i * blk, blk), blk), :]
    own_scores = [block_scores(k_own, c) for c in range(n_chunks)]
    k_first = k_ref[0, 0:blk, :]
    first_scores = [block_scores(k_first, c) for c in range(n_chunks)]
    gate_t = lax.dot_general(kbar_ref[0, 0], q3, NT, precision=HI, preferred_element_type=F32)
    for c, cs in enumerate(chunks):
        s_scr[0, :, cs] = first_scores[c]
    for c, cs in enumerate(chunks):
        kpos = lax.broadcasted_iota(jnp.int32, (blk, cw), 0)
        qpos = (lax.broadcasted_iota(jnp.int32, (blk, cw), 1) + c * cw) % blk
        s = jnp.where(kpos <= qpos, own_scores[c], NEG_INF)
        m0 = jnp.max(s, axis=0, keepdims=True)
        m_scr[:, cs] = m0
        p_scr[1, :, cs] = jnp.exp2(s - m0).astype(BF16)
    acc_scr[...] = jnp.zeros(acc_scr.shape, F32)
    alpha_scr[...] = jnp.ones(alpha_scr.shape, F32)
    sel_scr[...] = _select_blocks_t(gate_t, i, k_sel)

    def pv_update(j_prev, slot):
        vt_prev = vt_ref[j_prev, 0]
        pv = [jnp.dot(vt_prev, p_scr[slot, :, cs], preferred_element_type=F32) for cs in chunks]
        for cs, r in zip(chunks, pv):
            acc_scr[:, cs] = alpha_scr[:, cs] * acc_scr[:, cs] + r

    def step(j, cur):
        nxt = 1 - cur
        j_next = jnp.minimum(j + 1, n_blocks - 1)
        k_next = k_ref[0, pl.ds(pl.multiple_of(j_next * blk, blk), blk), :]
        for c, cs in enumerate(chunks):
            s_scr[nxt, :, cs] = block_scores(k_next, c)
        pv_update(jnp.where(j == 0, i, jnp.minimum(j - 1, n_blocks - 1)), nxt)
        chosen = (sel_scr[pl.ds(jnp.minimum(j, n_blocks - 1), 1), :] > 0.0) & (j < i)
        for c, cs in enumerate(chunks):
            s = s_scr[cur, :, cs]
            m_old = m_scr[:, cs]
            m_new = jnp.maximum(m_old, jnp.where(chosen[:, cs], jnp.max(s, axis=0, keepdims=True),
                                                 NEG_INF))
            p_scr[cur, :, cs] = jnp.exp2(s - jnp.where(chosen[:, cs], m_new, float("inf"))).astype(BF16)
            alpha_scr[:, cs] = jnp.exp2(m_old - m_new)
            m_scr[:, cs] = m_new

    per_trip = 4

    def trip(t, carry):
        for h in range(per_trip):
            step(per_trip * t + h, h % 2)
        return carry

    n_trips = (i + per_trip - 1) // per_trip
    lax.fori_loop(0, n_trips, trip, 0)
    pv_update(jnp.where(n_trips == 0, i, jnp.minimum(per_trip * n_trips - 1, n_blocks - 1)), 1)
    acc = acc_scr[...]
    o_t = (acc[:d] / acc[d:d + 1]).astype(BF16)
    eye = (lax.broadcasted_iota(jnp.int32, (blk, blk), 0)
           == lax.broadcasted_iota(jnp.int32, (blk, blk), 1)).astype(BF16)
    o_ref[0] = jnp.concatenate(
        [lax.dot_general(eye, o_t[:, g * blk:(g + 1) * blk], NT, preferred_element_type=F32)
         for g in range(KV_GROUP)], axis=1)


def _moba_prompt(qh, kh, vt, kbar, b, l):
    nb = l // MOBA_BLOCK
    rows = KV_GROUP * MOBA_BLOCK
    k_sel = min(MOBA_TOPK, nb)
    vt_rows = vt.shape[2]
    est = 4 * l * LANES * 2 + 4 * l * vt_rows * 2 + 8 * MOBA_BLOCK * GROUP_W * 4 + 24 * rows * LANES * 4
    return pl.pallas_call(
        functools.partial(_moba_prompt_kernel, k_sel=k_sel),
        grid=(N_KV_B, b, nb),
        in_specs=[
            pl.BlockSpec((1, MOBA_BLOCK, GROUP_W), lambda hk, bi, i: (hk, bi * nb + i, 0)),
            pl.BlockSpec((1, l, LANES), lambda hk, bi, i: (hk, bi, 0)),
            pl.BlockSpec((nb, 1, vt_rows, MOBA_BLOCK), lambda hk, bi, i: (bi, hk, 0, 0)),
            pl.BlockSpec((1, 1, nb, HEAD_DIM), lambda hk, bi, i: (hk, bi, 0, 0)),
        ],
        out_specs=pl.BlockSpec((1, MOBA_BLOCK, GROUP_W), lambda hk, bi, i: (hk, bi * nb + i, 0)),
        out_shape=jax.ShapeDtypeStruct((N_KV_B, b * l, GROUP_W), F32),
        scratch_shapes=[
            pltpu.VMEM((rows, LANES), BF16),
            pltpu.VMEM((1, rows), F32),
            pltpu.VMEM((vt_rows, rows), F32),
            pltpu.VMEM((nb, rows), F32),
            pltpu.VMEM((2, MOBA_BLOCK, rows), F32),
            pltpu.VMEM((2, MOBA_BLOCK, rows), BF16),
            pltpu.VMEM((1, rows), F32),
        ],
        compiler_params=_params(("parallel", "parallel", "arbitrary"), est),
        name="moba_prompt",
    )(qh, kh, vt, kbar)


def _moba_sample_kernel(pt_ref, q_ref, kn_ref, vn_ref, kt_hbm, vt_hbm, o_ref,
                        kbuf, vbuf, sem, kbar_scr, s_scr, p_scr,
                        *, n_seq, n_pages, l_new, k_sel):
    b = pl.program_id(0)
    slot = lax.rem(b, 2)
    nb = n_pages // 2
    page = kbuf.shape[-1]
    unroll = math.gcd(nb, 4)

    def page_copy(hbm, buf, sl, which, pg, p):
        return pltpu.make_async_copy(hbm.at[pg], buf.at[sl, p], sem.at[sl, which])

    def fetch(seq, sl):
        def body(p, carry):
            pg = pt_ref[seq, p]
            page_copy(kt_hbm, kbuf, sl, 0, pg, p).start()
            page_copy(vt_hbm, vbuf, sl, 1, pg, p).start()
            return carry
        lax.fori_loop(0, n_pages, body, 0)

    def wait_pages(hbm, buf, which):
        def body(p, carry):
            page_copy(hbm, buf, slot, which, 0, p).wait()
            return carry
        lax.fori_loop(0, n_pages, body, 0)

    @pl.when(b == 0)
    def _():
        fetch(0, 0)

    @pl.when(b + 1 < n_seq)
    def _():
        fetch(b + 1, 1 - slot)

    q = q_ref[0]
    qb = q.astype(BF16)
    rows = q.shape[0]
    wait_pages(kt_hbm, kbuf, 0)

    ones = jnp.ones((SUBLANES, page), BF16)

    def mean_body(j, carry):
        t = kbuf[slot, 2 * j] + kbuf[slot, 2 * j + 1]
        hi, mid, lo = _split3(t)
        d = lambda a: lax.dot_general(ones, a, NT, preferred_element_type=F32)
        kbar_scr[pl.ds(j, 1), :] = (d(hi) + (d(mid) + d(lo)))[0:1] * (1.0 / MOBA_BLOCK)
        return carry

    lax.fori_loop(0, nb, mean_body, 0, unroll=unroll)
    gate = lax.dot_general(q, kbar_scr[...], NT, precision=HI, preferred_element_type=F32)
    sel = _select_blocks(gate, nb, k_sel)
    lane = lax.broadcasted_iota(jnp.int32, sel.shape, 1)

    def score_body(j, carry):
        chosen = jnp.max(jnp.where(lane == j, sel, 0.0), axis=1, keepdims=True)
        s = jnp.concatenate(
            [jnp.dot(qb, kbuf[slot, 2 * j + h].astype(BF16), preferred_element_type=F32)
             for h in range(2)], axis=1)
        s = jnp.where(chosen > 0.0, s, NEG_INF)
        s_scr[j] = s
        return jnp.maximum(carry, s)

    s_max = lax.fori_loop(0, nb, score_body, jnp.full((rows, 2 * page), NEG_INF, F32), unroll=unroll)
    s_new = _bdot_nt(qb, kn_ref[0])
    t_q = lax.broadcasted_iota(jnp.int32, s_new.shape, 0) % l_new
    t_k = lax.broadcasted_iota(jnp.int32, s_new.shape, 1)
    s_new = jnp.where(t_k <= t_q, s_new, NEG_INF)
    m = jnp.maximum(jnp.max(s_max, axis=1, keepdims=True), jnp.max(s_new, axis=1, keepdims=True))
    p_new = jnp.exp2(s_new - m)

    def prob_body(j, carry):
        p = jnp.exp2(s_scr[j] - m)
        p_scr[j] = p.astype(BF16)
        return carry + p

    p_sum = lax.fori_loop(0, nb, prob_body, jnp.zeros((rows, 2 * page), F32), unroll=unroll)
    denom = jnp.sum(p_sum, axis=1, keepdims=True) + jnp.sum(p_new, axis=1, keepdims=True)
    wait_pages(vt_hbm, vbuf, 1)

    def pv_body(j, acc):
        pj = p_scr[j]
        for h in range(2):
            acc = acc + lax.dot_general(pj[:, h * page:(h + 1) * page],
                                        vbuf[slot, 2 * j + h].astype(BF16), NT,
                                        preferred_element_type=F32)
        return acc

    acc = lax.fori_loop(0, nb, pv_body, _bdot(p_new, vn_ref[0]), unroll=unroll)
    o_ref[0] = acc / denom


def _moba_sample(q_bd, page_table, kt_pool, vt_pool, k_new, v_new, l_new):
    bs, rows, w = q_bd.shape
    n_pages = page_table.shape[1]
    page = kt_pool.shape[2]
    assert 2 * page == MOBA_BLOCK and kt_pool.shape[1] == w
    nb = n_pages // 2
    k_sel = min(MOBA_TOPK, nb + 1)
    lp = k_new.shape[1]
    per_b = lambda r: pl.BlockSpec((1, r, w), lambda b, pt: (b, 0, 0))
    pool_bytes = 2 * n_pages * w * page * 4
    est = 2 * pool_bytes + 8 * nb * rows * 2 * page * 4
    return pl.pallas_call(
        functools.partial(_moba_sample_kernel, n_seq=bs, n_pages=n_pages, l_new=l_new, k_sel=k_sel),
        grid_spec=pltpu.PrefetchScalarGridSpec(
            num_scalar_prefetch=1,
            grid=(bs,),
            in_specs=[per_b(rows), per_b(lp), per_b(lp),
                      pl.BlockSpec(memory_space=pl.ANY), pl.BlockSpec(memory_space=pl.ANY)],
            out_specs=per_b(rows),
            scratch_shapes=[
                pltpu.VMEM((2, n_pages, w, page), F32),
                pltpu.VMEM((2, n_pages, w, page), F32),
                pltpu.SemaphoreType.DMA((2, 2)),
                pltpu.VMEM((nb, w), F32),
                pltpu.VMEM((nb, rows, 2 * page), F32),
                pltpu.VMEM((nb, rows, 2 * page), BF16),
            ],
        ),
        out_shape=jax.ShapeDtypeStruct((bs, rows, w), F32),
        compiler_params=pltpu.CompilerParams(
            dimension_semantics=("arbitrary",),
            vmem_limit_bytes=min(int(est * 1.25), VMEM_BYTES_V7X - 8 * 2 ** 20)),
        name="moba_sample",
    )(page_table, q_bd, k_new, v_new, kt_pool, vt_pool)


def _pack_w_in_a(w):
    d = w.shape[0]
    qkv_gate = w[:, :4 * MIX_W]
    ab = w[:, 4 * MIX_W:4 * MIX_W + 2 * N_HEADS]
    q_mem = w[:, 4 * MIX_W + 2 * N_HEADS:]
    ab = jnp.pad(ab, ((0, 0), (0, LANES - 2 * N_HEADS)))
    return jnp.concatenate([qkv_gate, q_mem, ab], axis=1).astype(BF16)


def _pad_rows(a, rows):
    return jnp.pad(a, ((0, 0), (0, rows - a.shape[1]), (0, 0)))


def _trunk(p, x3, pos_base, mem_k, mem_v, conv0, s0, paged):
    b, l, d = x3.shape
    m = b * l
    x = x3.reshape(m, d)
    sample = paged is not None
    ffn = lambda x, tag, i, fg=None: _ffn(x, p[tag + "_norm"][i], p[tag + "_w1"], p[tag + "_w3"],
                                          p[tag + "_w2"], i, fg)
    lq = -(-l // SUBLANES) * SUBLANES

    def memory(q_mem, layer):
        q3 = _pad_rows(q_mem.reshape(b, l, MEM_W), lq)
        mo = _mem_attn(q3, mem_k[layer], mem_v[layer])
        return mo[:, :l].reshape(m, MEM_W)

    x = ffn(x, "ffn1", 0)
    if l % IN_PROJ_TILE == 0:
        qkv, gate, q_mem, ab, new_conv = _in_proj_conv(x, p["mix_norm"][0], p["w_in_a"], conv0,
                                                       p["dn_conv_w"], l)
        qkv = qkv.reshape(b, l, 3 * MIX_W)
    else:
        qkv, gate, q_mem, ab = _norm_proj(x, p["mix_norm"][0], p["w_in_a"],
                                          (3 * MIX_W, MIX_W, MEM_W, LANES), "in_proj_a")
        qkv, new_conv = _conv(qkv.reshape(b, l, 3 * MIX_W), conv0, p["dn_conv_w"])
    lp = -(-l // DN_CHUNK) * DN_CHUNK
    o, s_fin = _gdn(_pad_rows(qkv, lp), _pad_rows(gate.reshape(b, l, MIX_W), lp),
                    _pad_rows(ab.reshape(b, l, LANES), lp),
                    p["dn_a_log"], p["dn_dt_bias"], p["dn_out_norm"], s0, l)
    o = o[:, :l].reshape(m, MIX_W)
    x = _out_proj(x, o, memory(q_mem, 0), p["w_out"][0], grouped=False)
    x = ffn(x, "ffn2", 0)

    if sample:
        tables = _rope_tables(m, l, pos_base)
    else:
        tables = _rope_tables(l, l, pos_base)
    kv = _kv_proj(x, p["kv_norm"], p["w_kv"], tables, l, for_prompt=not sample)
    x = ffn(x, "ffn1", 1)
    qh, q_mem = _q_proj(x, p["mix_norm"][1], p["w_in_b"], tables)
    if sample:
        page_table, kt_pool, vt_pool = paged
        k_new, v_new = kv
        k_out = k_new.reshape(b, l, N_KV_B, HEAD_DIM)
        v_out = v_new.reshape(b, l, N_KV_B, HEAD_DIM)
        eye = jnp.eye(N_KV_B, dtype=F32)
        q5 = qh.reshape(N_KV_B, b, l, KV_GROUP, HEAD_DIM).transpose(1, 0, 3, 2, 4)
        q_bd = (q5[:, :, :, :, None, :] * eye[None, :, None, None, :, None]).reshape(
            b, N_HEADS * l, N_KV_B * HEAD_DIM)
        o_bd = _moba_sample(q_bd, page_table, kt_pool, vt_pool,
                            _pad_rows(k_new.reshape(b, l, -1), lq),
                            _pad_rows(v_new.reshape(b, l, -1), lq), l)
        o6 = o_bd.reshape(b, N_KV_B, KV_GROUP, l, N_KV_B, HEAD_DIM)
        o = (o6 * eye[None, :, None, None, :, None]).sum(axis=4)
        o = o.transpose(0, 3, 1, 2, 4).reshape(m, MIX_W)
        x = _out_proj(x, o, memory(q_mem, 1), p["w_out"][1], grouped=False)
    else:
        k_t, v_t, kh, vt, kbar = kv
        to_rows = lambda a: a.reshape(b, N_KV_B, HEAD_DIM, l).transpose(0, 3, 1, 2)
        k_out, v_out = to_rows(k_t), to_rows(v_t)
        nb = l // MOBA_BLOCK
        kbar = kbar.reshape(b, nb, N_KV_B, HEAD_DIM).transpose(2, 0, 1, 3)
        o4 = _moba_prompt(qh, kh, vt, kbar, b, l)
        x = _out_proj(x, o4, memory(q_mem, 1), p["w_out"][1], grouped=True)
    y = ffn(x, "ffn2", 1, p["final_norm"])
    return y.reshape(b, l, d), new_conv[None], s_fin[None], k_out, v_out


def kernel(x_prompt, x_sample, mem_prompt, cache_mem_k, cache_mem_v, state_dn_conv, state_dn_S,
           cache_kv_k, cache_kv_v, page_table,
           ffn1_norm, ffn1_w1, ffn1_w3, ffn1_w2, mix_norm, w_in_a, w_in_b, w_out,
           dn_conv_w, dn_a_log, dn_dt_bias, dn_out_norm, kv_norm, w_kv, mem_norm, w_mem_kv,
           ffn2_norm, ffn2_w1, ffn2_w3, ffn2_w2, final_norm):
    assert w_in_a.shape[0] == 1 and w_in_b.shape[0] == 1 and ffn1_w1.shape[0] == 2
    bf = lambda a: a.astype(BF16)
    p = dict(ffn1_norm=ffn1_norm, ffn1_w1=bf(ffn1_w1), ffn1_w3=bf(ffn1_w3), ffn1_w2=bf(ffn1_w2),
             ffn2_norm=ffn2_norm, ffn2_w1=bf(ffn2_w1), ffn2_w3=bf(ffn2_w3), ffn2_w2=bf(ffn2_w2),
             mix_norm=mix_norm, w_in_a=_pack_w_in_a(w_in_a[0]), w_in_b=bf(w_in_b[0]), w_out=bf(w_out),
             dn_conv_w=dn_conv_w[0], dn_a_log=dn_a_log[0], dn_dt_bias=dn_dt_bias[0],
             dn_out_norm=dn_out_norm[0], kv_norm=kv_norm, w_kv=bf(w_kv), final_norm=final_norm)

    bp, lp, _ = x_prompt.shape
    assert lp % MOBA_BLOCK == 0
    n_mem = mem_prompt.shape[1]
    n_layers = mem_norm.shape[0]
    mem_t = [_mem_kv(mem_prompt, mem_norm[layer], bf(w_mem_kv[layer])) for layer in range(n_layers)]
    mk = [t[0] for t in mem_t]
    mv = [t[1] for t in mem_t]
    conv0 = jnp.zeros((bp, CONV_W - 1, 3 * MIX_W), F32)
    s0 = jnp.zeros((bp, N_HEADS, HEAD_DIM, HEAD_DIM), F32)
    y_p, conv_p, s_p, k_p, v_p = _trunk(p, x_prompt, 0, mk, mv, conv0, s0, None)
    token_major = lambda ts: jnp.stack(ts).reshape(n_layers, bp, N_MEM_HEADS, HEAD_DIM, n_mem).transpose(
        0, 1, 4, 2, 3)
    mem_k_p = token_major(mk)
    mem_v_p = token_major(mv)

    bs = x_sample.shape[0]
    n_pool, page = cache_kv_k.shape[:2]
    past_len = page_table.shape[1] * page
    assert past_len % MOBA_BLOCK == 0 and x_sample.shape[1] <= MOBA_BLOCK
    token_minor = lambda c: jnp.transpose(c, (0, 1, 3, 4, 2)).reshape(c.shape[0], bs, MEM_W, n_mem)
    cmk = token_minor(cache_mem_k)
    cmv = token_minor(cache_mem_v)
    transposed_pages = lambda c: jnp.transpose(c, (0, 2, 3, 1)).reshape(n_pool, -1, page)
    paged = (page_table, transposed_pages(cache_kv_k), transposed_pages(cache_kv_v))
    y_s, conv_s, s_s, k_s, v_s = _trunk(p, x_sample, past_len, cmk, cmv, state_dn_conv[0],
                                        state_dn_S[0], paged)
    return (y_p, y_s, conv_p, s_p, k_p, v_p, mem_k_p, mem_v_p, conv_s, s_s, k_s, v_s)
```

```python
import functools
import math

import jax
import jax.numpy as jnp
from jax import lax
from jax.experimental import pallas as pl
from jax.experimental.pallas import tpu as pltpu

F32 = jnp.float32
BF16 = jnp.bfloat16
HI = lax.Precision.HIGHEST

HEAD_DIM = 64
N_HEADS = 12
MIX_W = N_HEADS * HEAD_DIM
N_MEM_HEADS = 4
MEM_W = N_MEM_HEADS * HEAD_DIM
N_KV_B = 4
KV_GROUP = N_HEADS // N_KV_B
GROUP_W = KV_GROUP * HEAD_DIM
CONV_W = 4
DN_CHUNK = 64
MOBA_BLOCK = 256
MOBA_TOPK = 3
ROT_DIM = HEAD_DIM // 4
ROPE_THETA = 500000.0
EPS = 1e-6
QK_SCALE = HEAD_DIM ** -0.5
LOG2E = math.log2(math.e)
ONES_ROWS = 16

LANES = 128
SUBLANES = 8
VMEM_BYTES_V7X = 64 * 2 ** 20
NEG_INF = float("-inf")

NT = (((1,), (1,)), ((), ()))
TN = (((0,), (0,)), ((), ()))


def _params(semantics, est_bytes):
    limit = min(max(int(est_bytes * 1.5), 16 * 2 ** 20), VMEM_BYTES_V7X - 8 * 2 ** 20)
    return pltpu.CompilerParams(dimension_semantics=semantics, vmem_limit_bytes=limit)


def _row_tile(m, pref):
    t = min(m, pref)
    while m % t or t % SUBLANES:
        t -= 1
    return t


def _rms(x, g):
    return x * lax.rsqrt(jnp.mean(x * x, axis=-1, keepdims=True) + EPS) * g


def _sigmoid(x):
    return 1.0 / (1.0 + jnp.exp(-x))


def _softplus(x):
    return jnp.maximum(x, 0.0) + jnp.log(1.0 + jnp.exp(-jnp.abs(x)))


def _bdot(a, b):
    return jnp.dot(a.astype(BF16), b.astype(BF16), preferred_element_type=F32)


def _bdot_nt(a, b):
    return lax.dot_general(a.astype(BF16), b.astype(BF16), NT, preferred_element_type=F32)


MXU_WIDTH = 2 * LANES


def _ffn_kernel(*refs, splits, final):
    if final:
        x_ref, g_ref, w1_ref, w3_ref, w2_ref, fg_ref, o_ref = refs
    else:
        x_ref, g_ref, w1_ref, w3_ref, w2_ref, o_ref = refs
    x = x_ref[...]
    h = _rms(x, g_ref[...]).astype(BF16)
    acc = None
    for lo, hi in splits:
        a = jnp.dot(h, w1_ref[:, lo:hi], preferred_element_type=F32)
        b = jnp.dot(h, w3_ref[:, lo:hi], preferred_element_type=F32)
        u = ((a * _sigmoid(a)) * b).astype(BF16)
        part = jnp.dot(u, w2_ref[lo:hi, :], preferred_element_type=F32)
        acc = part if acc is None else acc + part
    y = x + 0.5 * acc
    if final:
        y = _rms(y, fg_ref[...])
    o_ref[...] = y


def _ffn(x, g, w1, w3, w2, layer, final_gain=None):
    m, d = x.shape
    d_ff = w1.shape[2]
    tm = _row_tile(m, 512)
    tiles = d_ff // MXU_WIDTH
    cut = (tiles // 2) * MXU_WIDTH
    splits = ((0, cut), (cut, d_ff)) if cut else ((0, d_ff),)
    final = final_gain is not None
    resident = lambda shape: pl.BlockSpec((None,) + shape, lambda i: (layer, 0, 0),
                                          pipeline_mode=pl.Buffered(1))
    in_specs = [
        pl.BlockSpec((tm, d), lambda i: (i, 0)),
        pl.BlockSpec((1, d), lambda i: (0, 0)),
        resident((d, d_ff)), resident((d, d_ff)), resident((d_ff, d)),
    ]
    args = [x, g.reshape(1, d), w1, w3, w2]
    if final:
        in_specs.append(pl.BlockSpec((1, d), lambda i: (0, 0)))
        args.append(final_gain.reshape(1, d))
    widest = max(hi - lo for lo, hi in splits)
    est = 4 * tm * d * 4 + 3 * d * d_ff * 2 + 2 * tm * d * 4 + 3 * tm * widest * 4
    return pl.pallas_call(
        functools.partial(_ffn_kernel, splits=splits, final=final),
        grid=(m // tm,),
        in_specs=in_specs,
        out_specs=pl.BlockSpec((tm, d), lambda i: (i, 0)),
        out_shape=jax.ShapeDtypeStruct((m, d), F32),
        compiler_params=_params(("parallel",), est),
        name="ffn_final" if final else "ffn",
    )(*args)


def _norm_proj_kernel(x_ref, g_ref, w_ref, *out_refs, splits):
    h = _rms(x_ref[...], g_ref[...]).astype(BF16)
    for o_ref, (a, b) in zip(out_refs, splits):
        o_ref[...] = jnp.dot(h, w_ref[:, a:b], preferred_element_type=F32)


def _norm_proj(x, g, w, widths, name):
    m, d = x.shape
    n = w.shape[1]
    assert sum(widths) == n and all(c % LANES == 0 for c in widths)
    tm = _row_tile(m, 256)
    splits, a = [], 0
    for c in widths:
        splits.append((a, a + c))
        a += c
    est = 2 * tm * d * 4 + 2 * d * n * 2 + 3 * tm * n * 4
    return pl.pallas_call(
        functools.partial(_norm_proj_kernel, splits=tuple(splits)),
        grid=(m // tm,),
        in_specs=[
            pl.BlockSpec((tm, d), lambda i: (i, 0)),
            pl.BlockSpec((1, d), lambda i: (0, 0)),
            pl.BlockSpec((d, n), lambda i: (0, 0)),
        ],
        out_specs=[pl.BlockSpec((tm, c), lambda i: (i, 0)) for c in widths],
        out_shape=[jax.ShapeDtypeStruct((m, c), F32) for c in widths],
        compiler_params=_params(("parallel",), est),
        name=name,
    )(x, g.reshape(1, d), w)


def _rope_table_kernel(inv_ref, c_ref, s1_ref, s2_ref, *, period, base, tl):
    i = pl.program_id(0)
    row = lax.broadcasted_iota(jnp.int32, (tl, LANES), 0) + i * tl
    pos = base + lax.rem(row, period)
    ang = pos.astype(F32) * inv_ref[...]
    in_head = lax.broadcasted_iota(jnp.int32, (tl, LANES), 1) % HEAD_DIM
    half = ROT_DIM // 2
    c = jnp.cos(ang)
    s = jnp.sin(ang)
    c_ref[...] = jnp.where(in_head < ROT_DIM, c, 1.0)
    s1_ref[...] = jnp.where(in_head < half, -s, 0.0)
    s2_ref[...] = jnp.where((in_head >= half) & (in_head < ROT_DIM), s, 0.0)


def _rope_tables(n_rows, period, base):
    half = ROT_DIM // 2
    inv = ROPE_THETA ** (-jnp.arange(half, dtype=F32) * 2.0 / ROT_DIM)
    in_head = jnp.arange(LANES) % HEAD_DIM
    inv_lane = jnp.where(in_head < ROT_DIM, inv[in_head % half], 0.0).astype(F32).reshape(1, LANES)
    tl = _row_tile(n_rows, 1024)
    shp = jax.ShapeDtypeStruct((n_rows, LANES), F32)
    return pl.pallas_call(
        functools.partial(_rope_table_kernel, period=period, base=base, tl=tl),
        grid=(n_rows // tl,),
        in_specs=[pl.BlockSpec((1, LANES), lambda i: (0, 0))],
        out_specs=[pl.BlockSpec((tl, LANES), lambda i: (i, 0))] * 3,
        out_shape=[shp] * 3,
        compiler_params=_params(("parallel",), 16 * tl * LANES * 4),
        name="rope_tables",
    )(inv_lane)


def _apply_rope(t, c, s1, s2):
    w = t.shape[1]
    reps = w // LANES
    half = ROT_DIM // 2
    tile = lambda a: jnp.concatenate([a] * reps, axis=1) if reps > 1 else a
    up = pltpu.roll(t, w - half, 1)
    down = pltpu.roll(t, half, 1)
    return t * tile(c) + up * tile(s1) + down * tile(s2)


def _kv_proj_kernel(x_ref, g_ref, w_ref, c_ref, s1_ref, s2_ref, *out_refs, for_prompt):
    h = _rms(x_ref[...], g_ref[...]).astype(BF16)
    kv_w = N_KV_B * HEAD_DIM
    k = jnp.dot(h, w_ref[:, :kv_w], preferred_element_type=F32)
    v = jnp.dot(h, w_ref[:, kv_w:], preferred_element_type=F32)
    k = _apply_rope(k, c_ref[...], s1_ref[...], s2_ref[...])
    if for_prompt:
        kt_ref, vt_ref, kh_ref, vtb_ref, kbar_ref, v_scr = out_refs
        v_scr[...] = v
        v_t = v_scr[...].T
        kt_ref[0] = k.T
        vt_ref[0] = v_t
        for hk in range(N_KV_B):
            sl = slice(hk * HEAD_DIM, (hk + 1) * HEAD_DIM)
            k_h = k[:, sl]
            kh_ref[hk] = jnp.concatenate([k_h, jnp.zeros_like(k_h)], axis=1).astype(BF16)
            vtb_ref[0, hk, 0:HEAD_DIM, :] = v_t[sl, :].astype(BF16)
            vtb_ref[0, hk, HEAD_DIM:, :] = jnp.ones((ONES_ROWS, v_t.shape[1]), BF16)
        kbar_ref[0] = jnp.sum(k, axis=0, keepdims=True) * (1.0 / MOBA_BLOCK)
    else:
        out_refs[0][...] = k
        out_refs[1][...] = v


def _kv_proj(x, g, w, tables, seq_len, for_prompt):
    m, d = x.shape
    kv_w = N_KV_B * HEAD_DIM
    tm = MOBA_BLOCK if for_prompt else _row_tile(m, 256)
    n_tab = tables[0].shape[0] // tm
    row = lambda i: (i, 0)
    tab = lambda i: (i % n_tab, 0)
    if for_prompt:
        nb = seq_len // tm
        t_spec = pl.BlockSpec((1, kv_w, tm), lambda i: (i // nb, 0, i % nb))
        t_shape = jax.ShapeDtypeStruct((m // seq_len, kv_w, seq_len), F32)
        vt_rows = HEAD_DIM + ONES_ROWS
        out_specs = [t_spec, t_spec,
                     pl.BlockSpec((N_KV_B, tm, LANES), lambda i: (0, i, 0)),
                     pl.BlockSpec((1, N_KV_B, vt_rows, tm), lambda i: (i, 0, 0, 0)),
                     pl.BlockSpec((1, 1, kv_w), lambda i: (i, 0, 0))]
        out_shape = [t_shape, t_shape,
                     jax.ShapeDtypeStruct((N_KV_B, m, LANES), BF16),
                     jax.ShapeDtypeStruct((m // tm, N_KV_B, vt_rows, tm), BF16),
                     jax.ShapeDtypeStruct((m // tm, 1, kv_w), F32)]
    else:
        out_specs = [pl.BlockSpec((tm, kv_w), row), pl.BlockSpec((tm, kv_w), row)]
        out_shape = [jax.ShapeDtypeStruct((m, kv_w), F32)] * 2
    est = 2 * tm * d * 4 + 2 * d * 2 * kv_w * 2 + 12 * tm * kv_w * 4
    return pl.pallas_call(
        functools.partial(_kv_proj_kernel, for_prompt=for_prompt),
        grid=(m // tm,),
        in_specs=[
            pl.BlockSpec((tm, d), row),
            pl.BlockSpec((1, d), lambda i: (0, 0)),
            pl.BlockSpec((d, 2 * kv_w), lambda i: (0, 0)),
            pl.BlockSpec((tm, LANES), tab),
            pl.BlockSpec((tm, LANES), tab),
            pl.BlockSpec((tm, LANES), tab),
        ],
        out_specs=out_specs,
        out_shape=out_shape,
        scratch_shapes=[pltpu.VMEM((tm, kv_w), F32)] if for_prompt else [],
        compiler_params=_params(("parallel",), est),
        name="kv_proj",
    )(x, g.reshape(1, d), w, *tables)


def _q_proj_kernel(x_ref, g_ref, w_ref, c_ref, s1_ref, s2_ref, qh_ref, qm_ref):
    h = _rms(x_ref[...], g_ref[...]).astype(BF16)
    q = jnp.dot(h, w_ref[:, :MIX_W], preferred_element_type=F32)
    qm_ref[...] = jnp.dot(h, w_ref[:, MIX_W:], preferred_element_type=F32)
    q = _apply_rope(q, c_ref[...], s1_ref[...], s2_ref[...]) * (QK_SCALE * LOG2E)
    for hk in range(N_KV_B):
        qh_ref[hk] = q[:, hk * GROUP_W:(hk + 1) * GROUP_W]


def _q_proj(x, g, w, tables):
    m, d = x.shape
    n = w.shape[1]
    tm = _row_tile(m, 256)
    n_tab = tables[0].shape[0] // tm
    row = lambda i: (i, 0)
    tab = lambda i: (i % n_tab, 0)
    est = 2 * tm * d * 4 + 2 * d * n * 2 + 16 * tm * MIX_W * 4
    return pl.pallas_call(
        _q_proj_kernel,
        grid=(m // tm,),
        in_specs=[
            pl.BlockSpec((tm, d), row),
            pl.BlockSpec((1, d), lambda i: (0, 0)),
            pl.BlockSpec((d, n), lambda i: (0, 0)),
            pl.BlockSpec((tm, LANES), tab),
            pl.BlockSpec((tm, LANES), tab),
            pl.BlockSpec((tm, LANES), tab),
        ],
        out_specs=[
            pl.BlockSpec((N_KV_B, tm, GROUP_W), lambda i: (0, i, 0)),
            pl.BlockSpec((tm, MEM_W), row),
        ],
        out_shape=[
            jax.ShapeDtypeStruct((N_KV_B, m, GROUP_W), F32),
            jax.ShapeDtypeStruct((m, MEM_W), F32),
        ],
        compiler_params=_params(("parallel",), est),
        name="q_proj",
    )(x, g.reshape(1, d), w, *tables)


def _conv_kernel(u_ref, prev_ref, buf_ref, w_ref, y_ref, nb_ref, pad_scr, *, tl, n_t):
    i = pl.program_id(1)
    halo = CONV_W - 1
    lo = SUBLANES - halo
    pad_scr[SUBLANES:SUBLANES + tl, :] = u_ref[0]

    @pl.when(i == 0)
    def _():
        pad_scr[lo:SUBLANES, :] = buf_ref[0]

    @pl.when(i > 0)
    def _():
        p = prev_ref[0]
        pad_scr[lo:SUBLANES, :] = p[p.shape[0] - halo:, :]

    y = pad_scr[lo:lo + tl, :] * w_ref[0:1, :]
    for j in range(1, CONV_W):
        y = y + pad_scr[lo + j:lo + j + tl, :] * w_ref[j:j + 1, :]
    y_ref[0] = y * _sigmoid(y)

    @pl.when(i == n_t - 1)
    def _():
        nb_ref[0] = pad_scr[lo + tl:SUBLANES + tl, :]


def _conv(u, buf, w):
    b, l, c = u.shape
    tl = _row_tile(l, 512) if l % SUBLANES == 0 else l
    n_t = l // tl
    pr = min(SUBLANES, l)
    per = tl // pr
    est = 6 * tl * c * 4
    return pl.pallas_call(
        functools.partial(_conv_kernel, tl=tl, n_t=n_t),
        grid=(b, n_t),
        in_specs=[
            pl.BlockSpec((1, tl, c), lambda bi, i: (bi, i, 0)),
            pl.BlockSpec((1, pr, c), lambda bi, i: (bi, jnp.maximum(i * per - 1, 0), 0)),
            pl.BlockSpec((1, CONV_W - 1, c), lambda bi, i: (bi, 0, 0)),
            pl.BlockSpec((CONV_W, c), lambda bi, i: (0, 0)),
        ],
        out_specs=[
            pl.BlockSpec((1, tl, c), lambda bi, i: (bi, i, 0)),
            pl.BlockSpec((1, CONV_W - 1, c), lambda bi, i: (bi, 0, 0)),
        ],
        out_shape=[
            jax.ShapeDtypeStruct((b, l, c), F32),
            jax.ShapeDtypeStruct((b, CONV_W - 1, c), F32),
        ],
        scratch_shapes=[pltpu.VMEM((tl + SUBLANES, c), F32)],
        compiler_params=_params(("parallel", "arbitrary"), est),
        name="short_conv",
    )(u, u, buf, w)


IN_PROJ_TILE = 256


def _in_proj_conv_kernel(x_ref, g_ref, w_ref, buf_ref, cw_ref, qkv_ref, gate_ref, qm_ref, ab_ref,
                         nb_ref, pad_scr, *, tm, n_t):
    i = pl.program_id(1)
    halo = CONV_W - 1
    lo = SUBLANES - halo
    c3 = 3 * MIX_W
    h = _rms(x_ref[...], g_ref[...]).astype(BF16)

    @pl.when(i == 0)
    def _():
        pad_scr[lo:SUBLANES, :] = buf_ref[0]

    pad_scr[SUBLANES:SUBLANES + tm, :] = jnp.dot(h, w_ref[:, :c3], preferred_element_type=F32)
    gate_ref[...] = jnp.dot(h, w_ref[:, c3:c3 + MIX_W], preferred_element_type=F32)
    qm_ref[...] = jnp.dot(h, w_ref[:, c3 + MIX_W:c3 + MIX_W + MEM_W], preferred_element_type=F32)
    ab_ref[...] = jnp.dot(h, w_ref[:, c3 + MIX_W + MEM_W:], preferred_element_type=F32)
    y = pad_scr[lo:lo + tm, :] * cw_ref[0:1, :]
    for j in range(1, CONV_W):
        y = y + pad_scr[lo + j:lo + j + tm, :] * cw_ref[j:j + 1, :]
    qkv_ref[...] = y * _sigmoid(y)
    tail = pad_scr[lo + tm:SUBLANES + tm, :]

    @pl.when(i == n_t - 1)
    def _():
        nb_ref[0] = tail

    pad_scr[lo:SUBLANES, :] = tail


def _in_proj_conv(x, g, w, buf, conv_w, seq_len):
    m, d = x.shape
    n = w.shape[1]
    c3 = 3 * MIX_W
    tm = IN_PROJ_TILE
    n_t = seq_len // tm
    b = m // seq_len
    row = lambda bi, i: (bi * n_t + i, 0)
    const = lambda bi, i: (0, 0)
    widths = (c3, MIX_W, MEM_W, n - c3 - MIX_W - MEM_W)
    est = 2 * tm * d * 4 + 2 * d * n * 2 + 4 * tm * n * 4
    return pl.pallas_call(
        functools.partial(_in_proj_conv_kernel, tm=tm, n_t=n_t),
        grid=(b, n_t),
        in_specs=[
            pl.BlockSpec((tm, d), row),
            pl.BlockSpec((1, d), const),
            pl.BlockSpec((d, n), const),
            pl.BlockSpec((1, CONV_W - 1, c3), lambda bi, i: (bi, 0, 0)),
            pl.BlockSpec((CONV_W, c3), const),
        ],
        out_specs=[pl.BlockSpec((tm, c), row) for c in widths]
        + [pl.BlockSpec((1, CONV_W - 1, c3), lambda bi, i: (bi, 0, 0))],
        out_shape=[jax.ShapeDtypeStruct((m, c), F32) for c in widths]
        + [jax.ShapeDtypeStruct((b, CONV_W - 1, c3), F32)],
        scratch_shapes=[pltpu.VMEM((tm + SUBLANES, c3), F32)],
        compiler_params=_params(("parallel", "arbitrary"), est),
        name="in_proj_conv",
    )(x, g.reshape(1, d), w, buf, conv_w)


HEADS_PER_SLAB = LANES * 2 // HEAD_DIM
SLAB_W = HEADS_PER_SLAB * HEAD_DIM
N_SLABS = N_HEADS // HEADS_PER_SLAB


def _split2(x):
    hi = x.astype(BF16)
    return hi, (x - hi.astype(F32)).astype(BF16)


def _split3(x):
    hi = x.astype(BF16)
    r = x - hi.astype(F32)
    mid = r.astype(BF16)
    return hi, mid, (r - mid.astype(F32)).astype(BF16)


def _dot_right01(x, sel):
    hi, mid, lo = _split3(x)
    d = lambda a: jnp.dot(a, sel, preferred_element_type=F32)
    return d(hi) + (d(mid) + d(lo))


def _dot_left01(sel, x):
    hi, mid, lo = _split3(x)
    d = lambda a: jnp.dot(sel, a, preferred_element_type=F32)
    return d(hi) + (d(mid) + d(lo))


def _dot3(a, b_hi, b_lo):
    a_hi, a_lo = _split2(a)
    d = lambda x, y: jnp.dot(x, y, preferred_element_type=F32)
    return d(a_hi, b_hi) + (d(a_hi, b_lo) + d(a_lo, b_hi))


def _block_diag(x, mask):
    return jnp.concatenate([x] * HEADS_PER_SLAB, axis=0) * mask


def _block_diag_pieces(x, mask):
    hi, lo = _split2(x)
    return _block_diag(hi, mask), _block_diag(lo, mask)


def _unit_lower_inverses(ms, eye_t, mask):
    c = ms[0].shape[0]
    xs = [eye_t - m for m in ms]
    ps = [_dot3(m, *_block_diag_pieces(m, mask)) for m in ms]
    power = 2
    while 2 * power < c:
        xps = [_dot3(jnp.concatenate([x, p], axis=0), *_block_diag_pieces(p, mask))
               for x, p in zip(xs, ps)]
        xs = [x + xp[:c] for x, xp in zip(xs, xps)]
        ps = [xp[c:] for xp in xps]
        power *= 2
    return [x + _dot3(x, *_block_diag_pieces(p, mask)) for x, p in zip(xs, ps)]


def _gdn_kernel(q_ref, k_ref, v_ref, gate_ref, ab_ref, alog_ref, dt_ref, gain_ref, s0_ref,
                o_ref, sfin_ref,
                s_scr, qn_scr, kn_scr, gcb_scr, bb_scr, u_scr, w_scr, a_scr, qg_scr, kd_scr, egl_scr,
                *, t_rows, n_t, l_valid):
    t = pl.program_id(1)
    c = DN_CHUNK
    d = HEAD_DIM
    n_chunks = t_rows // c

    r_s = lax.broadcasted_iota(jnp.int32, (SLAB_W, SLAB_W), 0)
    c_s = lax.broadcasted_iota(jnp.int32, (SLAB_W, SLAB_W), 1)
    same_head = (r_s // d) == (c_s // d)
    bd_f32 = jnp.where(same_head, 1.0, 0.0).astype(F32)
    bd_mask = bd_f32.astype(BF16)
    r_t = lax.broadcasted_iota(jnp.int32, (c, SLAB_W), 0)
    c_t = lax.broadcasted_iota(jnp.int32, (c, SLAB_W), 1) % d
    lower_t = r_t >= c_t
    strict_t = r_t > c_t
    eye_t = jnp.where(r_t == c_t, 1.0, 0.0).astype(F32)
    r_c = lax.broadcasted_iota(jnp.int32, (c, c), 0)
    c_c = lax.broadcasted_iota(jnp.int32, (c, c), 1)
    tril = jnp.where(r_c >= c_c, 1.0, 0.0).astype(BF16)
    er = lax.broadcasted_iota(jnp.int32, (LANES, MIX_W), 0)
    ec = lax.broadcasted_iota(jnp.int32, (LANES, MIX_W), 1) // d
    sel_g = jnp.where(er == ec, 1.0, 0.0).astype(BF16)
    sel_b = jnp.where(er == ec + N_HEADS, 1.0, 0.0).astype(BF16)

    @pl.when(t == 0)
    def _():
        for s in range(N_SLABS):
            rows = jnp.concatenate([s0_ref[0, s * HEADS_PER_SLAB + h] for h in range(HEADS_PER_SLAB)],
                                   axis=0)
            s_scr[s] = jnp.concatenate([rows] * HEADS_PER_SLAB, axis=1) * bd_f32

    ab = ab_ref[0]
    row = lax.broadcasted_iota(jnp.int32, ab.shape, 0) + t * t_rows
    valid = row < l_valid
    g = jnp.where(valid, -jnp.exp(alog_ref[...]) * _softplus(ab + dt_ref[...]), 0.0)
    beta = jnp.where(valid, _sigmoid(ab), 0.0)
    gc = jnp.concatenate([_dot_left01(tril, g[i * c:(i + 1) * c]) for i in range(n_chunks)], axis=0)
    gcb_scr[...] = _dot_right01(gc, sel_g)
    bb_scr[...] = _dot_right01(beta, sel_b)
    for s in range(N_SLABS):
        cs = slice(s * SLAB_W, (s + 1) * SLAB_W)
        for src, dst, scale in ((q_ref, qn_scr, QK_SCALE), (k_ref, kn_scr, 1.0)):
            x = src[0, :, cs]
            hi, lo = _split2(x * x)
            ssq = (jnp.dot(hi, bd_mask, preferred_element_type=F32)
                   + jnp.dot(lo, bd_mask, preferred_element_type=F32))
            dst[:, cs] = x * (lax.rsqrt(ssq + EPS) * scale)

    per_iter = math.gcd(n_chunks, 2)
    slabs = [slice(s * SLAB_W, (s + 1) * SLAB_W) for s in range(N_SLABS)]

    def factors(it, carry):
        items = []
        for u in range(per_iter):
            ci = it * per_iter + u
            rows = pl.ds(pl.multiple_of(ci * c, c), c)
            items += [(ci, rows, cs) for cs in slabs]
        pre = []
        for ci, rows, cs in items:
            gcb = gcb_scr[rows, cs]
            kn = kn_scr[rows, cs]
            qn = qn_scr[rows, cs]
            b_ = bb_scr[rows, cs]
            kb = kn * b_
            g_row = jnp.sum(gcb * eye_t, axis=0, keepdims=True)
            dec = jnp.exp(jnp.where(lower_t, gcb - g_row, NEG_INF))
            both = lax.dot_general(jnp.concatenate([kb, qn], axis=0).astype(BF16),
                                   _block_diag(kn.astype(BF16), bd_mask), NT,
                                   preferred_element_type=F32)
            pre.append((gcb, kn, qn, kb, dec, both))
        ms = []
        for (ci, rows, cs), (gcb, kn, qn, kb, dec, both) in zip(items, pre):
            ms.append(jnp.where(strict_t, both[:c] * dec, 0.0))
            a_scr[rows, cs] = (both[c:] * dec).astype(BF16)
        tinvs = _unit_lower_inverses(ms, eye_t, bd_mask)
        sols = []
        for (ci, rows, cs), (gcb, kn, qn, kb, dec, both), tinv in zip(items, pre, tinvs):
            eg = jnp.exp(gcb)
            v_hi, v_lo = _block_diag_pieces(v_ref[0, rows, cs] * bb_scr[rows, cs], bd_mask)
            k_hi, k_lo = _block_diag_pieces(kb * eg, bd_mask)
            sols.append(_dot3(tinv, jnp.concatenate([v_hi, k_hi], axis=1),
                              jnp.concatenate([v_lo, k_lo], axis=1)))
            glast = gcb[c - 1:c, :]
            qg_scr[rows, cs] = (qn * eg).astype(BF16)
            kd_scr[rows, cs] = (kn * jnp.exp(glast - gcb)).astype(BF16)
            egl_scr[pl.ds(ci, 1), cs] = jnp.exp(glast)
        for (ci, rows, cs), sol in zip(items, sols):
            u_scr[rows, cs] = sol[:, :SLAB_W]
            w_scr[rows, cs] = sol[:, SLAB_W:].astype(BF16)
        return carry

    lax.fori_loop(0, n_chunks // per_iter, factors, 0)

    gain = gain_ref[...]

    def advance(ci, rows):
        olds = [s_scr[s] for s in range(N_SLABS)]
        rs = [jnp.dot(jnp.concatenate([w_scr[rows, cs], qg_scr[rows, cs]], axis=0),
                      s_old.astype(BF16), preferred_element_type=F32)
              for cs, s_old in zip(slabs, olds)]
        v_news = [(u_scr[rows, cs] - r[:c]).astype(BF16) for cs, r in zip(slabs, rs)]
        upds = [lax.dot_general(kd_scr[rows, cs], v_new, TN, preferred_element_type=F32)
                for cs, v_new in zip(slabs, v_news)]
        os_ = [r[c:] + jnp.dot(a_scr[rows, cs], _block_diag(v_new, bd_mask),
                               preferred_element_type=F32)
               for cs, r, v_new in zip(slabs, rs, v_news)]
        for s, (cs, s_old, upd) in enumerate(zip(slabs, olds, upds)):
            s_scr[s] = (s_old * egl_scr[pl.ds(ci, 1), cs] + upd) * bd_f32
        return os_

    def emit(rows, os_):
        ssqs = []
        for o in os_:
            hi, lo = _split2(o * o)
            ssqs.append(jnp.dot(hi, bd_mask, preferred_element_type=F32)
                        + jnp.dot(lo, bd_mask, preferred_element_type=F32))
        for cs, o, ssq in zip(slabs, os_, ssqs):
            gt = gate_ref[0, rows, cs]
            o_ref[0, rows, cs] = (o * lax.rsqrt(ssq * (1.0 / d) + EPS) * gain[:, cs]
                                  * (gt * _sigmoid(gt)))

    def recur(it, carry):
        done = []
        for u in range(per_iter):
            ci = it * per_iter + u
            rows = pl.ds(pl.multiple_of(ci * c, c), c)
            done.append((rows, advance(ci, rows)))
        for rows, os_ in done:
            emit(rows, os_)
        return carry

    lax.fori_loop(0, n_chunks // per_iter, recur, 0)

    @pl.when(t == n_t - 1)
    def _():
        for s in range(N_SLABS):
            full = s_scr[s]
            for h in range(HEADS_PER_SLAB):
                sfin_ref[0, s * HEADS_PER_SLAB + h] = full[h * d:(h + 1) * d, h * d:(h + 1) * d]


def _gdn(qkv, gate, ab, a_log, dt_bias, out_gain, s0, l_valid):
    b, lp, _ = qkv.shape
    assert lp % DN_CHUNK == 0
    t_rows = DN_CHUNK * math.gcd(lp // DN_CHUNK, 8)
    n_t = lp // t_rows
    n_chunks = t_rows // DN_CHUNK
    pad = lambda a: jnp.pad(a.astype(F32), (0, LANES - N_HEADS)).reshape(1, LANES)
    blk = lambda j: pl.BlockSpec((1, t_rows, MIX_W), lambda bi, t, j=j: (bi, t, j))
    state = pl.BlockSpec((1, N_HEADS, HEAD_DIM, HEAD_DIM), lambda bi, t: (bi, 0, 0, 0))
    vec = pl.BlockSpec((1, LANES), lambda bi, t: (0, 0))
    wide_f32 = pltpu.VMEM((t_rows, MIX_W), F32)
    wide_bf16 = pltpu.VMEM((t_rows, MIX_W), BF16)
    est = 10 * t_rows * MIX_W * 4 + 5 * t_rows * MIX_W * 4 + 4 * t_rows * MIX_W * 2 + 64 * SLAB_W * SLAB_W * 4
    return pl.pallas_call(
        functools.partial(_gdn_kernel, t_rows=t_rows, n_t=n_t, l_valid=l_valid),
        grid=(b, n_t),
        in_specs=[
            blk(0), blk(1), blk(2),
            pl.BlockSpec((1, t_rows, MIX_W), lambda bi, t: (bi, t, 0)),
            pl.BlockSpec((1, t_rows, LANES), lambda bi, t: (bi, t, 0)),
            vec, vec,
            pl.BlockSpec((1, MIX_W), lambda bi, t: (0, 0)),
            state,
        ],
        out_specs=[pl.BlockSpec((1, t_rows, MIX_W), lambda bi, t: (bi, t, 0)), state],
        out_shape=[
            jax.ShapeDtypeStruct((b, lp, MIX_W), F32),
            jax.ShapeDtypeStruct((b, N_HEADS, HEAD_DIM, HEAD_DIM), F32),
        ],
        scratch_shapes=[
            pltpu.VMEM((N_SLABS, SLAB_W, SLAB_W), F32),
            wide_f32, wide_f32,
            wide_f32, wide_f32,
            wide_f32,
            wide_bf16, wide_bf16, wide_bf16, wide_bf16,
            pltpu.VMEM((max(n_chunks, SUBLANES), MIX_W), F32),
        ],
        compiler_params=_params(("parallel", "arbitrary"), est),
        name="gated_delta_rule",
    )(qkv, qkv, qkv, gate, ab, pad(a_log), pad(dt_bias),
      jnp.tile(out_gain.astype(F32), N_HEADS).reshape(1, MIX_W), s0)


def _mem_kv_kernel(x_ref, g_ref, w_ref, kt_ref, vt_ref, kv_scr):
    h = _rms(x_ref[...], g_ref[...]).astype(BF16)
    kv_scr[...] = jnp.dot(h, w_ref[...], preferred_element_type=F32)
    kt_ref[0] = kv_scr[:, :MEM_W].T
    vt_ref[0] = kv_scr[:, MEM_W:].T


def _mem_kv(mem, g, w):
    b, n_mem, d = mem.shape
    shp = jax.ShapeDtypeStruct((b, MEM_W, n_mem), F32)
    spec = pl.BlockSpec((1, MEM_W, n_mem), lambda i: (i, 0, 0))
    return pl.pallas_call(
        _mem_kv_kernel,
        grid=(b,),
        in_specs=[
            pl.BlockSpec((n_mem, d), lambda i: (i, 0)),
            pl.BlockSpec((1, d), lambda i: (0, 0)),
            pl.BlockSpec((d, 2 * MEM_W), lambda i: (0, 0)),
        ],
        out_specs=[spec, spec],
        out_shape=[shp, shp],
        scratch_shapes=[pltpu.VMEM((n_mem, 2 * MEM_W), F32)],
        compiler_params=_params(("parallel",), 2 * n_mem * d * 4 + 2 * d * 2 * MEM_W * 2 + 8 * n_mem * MEM_W * 4),
        name="mem_kv",
    )(mem.reshape(b * n_mem, d), g.reshape(1, d), w)


def _mem_attn_kernel(q_ref, kt_ref, vt_ref, o_ref):
    q = q_ref[0]
    kt = kt_ref[0]
    vt = vt_ref[0]
    outs = []
    for h in range(N_MEM_HEADS):
        sl = slice(h * HEAD_DIM, (h + 1) * HEAD_DIM)
        s = _bdot(q[:, sl], kt[sl, :]) * QK_SCALE
        p = jnp.exp(s - jnp.max(s, axis=-1, keepdims=True))
        outs.append(_bdot_nt(p, vt[sl, :]) / jnp.sum(p, axis=-1, keepdims=True))
    o_ref[0] = jnp.concatenate(outs, axis=1)


def _mem_attn(q, mk_t, mv_t):
    b, l, w = q.shape
    n_mem = mk_t.shape[2]
    tl = _row_tile(l, 512)
    est = 4 * tl * w * 4 + 4 * n_mem * w * 4 + 8 * tl * n_mem * 4
    return pl.pallas_call(
        _mem_attn_kernel,
        grid=(b, l // tl),
        in_specs=[
            pl.BlockSpec((1, tl, w), lambda bi, i: (bi, i, 0)),
            pl.BlockSpec((1, w, n_mem), lambda bi, i: (bi, 0, 0)),
            pl.BlockSpec((1, w, n_mem), lambda bi, i: (bi, 0, 0)),
        ],
        out_specs=pl.BlockSpec((1, tl, w), lambda bi, i: (bi, i, 0)),
        out_shape=jax.ShapeDtypeStruct((b, l, w), F32),
        compiler_params=_params(("parallel", "parallel"), est),
        name="mem_attn",
    )(q, mk_t, mv_t)


def _out_proj_kernel(x_ref, o_ref, mo_ref, wo_ref, wm_ref, y_ref, *, grouped):
    y = x_ref[...] + _bdot(mo_ref[...], wm_ref[...])
    if grouped:
        for hk in range(N_KV_B):
            y = y + _bdot(o_ref[hk], wo_ref[hk])
    else:
        y = y + _bdot(o_ref[...], wo_ref[...])
    y_ref[...] = y


def _out_proj(x, o, mo, w_out, grouped):
    m, d = x.shape
    tm = _row_tile(m, 512)
    row = lambda i: (i, 0)
    if grouped:
        wo = w_out[:MIX_W].reshape(N_KV_B, GROUP_W, d)
        o_spec = pl.BlockSpec((N_KV_B, tm, GROUP_W), lambda i: (0, i, 0))
        wo_spec = pl.BlockSpec((N_KV_B, GROUP_W, d), lambda i: (0, 0, 0))
    else:
        wo = w_out[:MIX_W]
        o_spec = pl.BlockSpec((tm, MIX_W), row)
        wo_spec = pl.BlockSpec((MIX_W, d), lambda i: (0, 0))
    est = 6 * tm * d * 4 + 4 * tm * d * 4 + 4 * d * d * 2
    return pl.pallas_call(
        functools.partial(_out_proj_kernel, grouped=grouped),
        grid=(m // tm,),
        in_specs=[
            pl.BlockSpec((tm, d), row),
            o_spec,
            pl.BlockSpec((tm, MEM_W), row),
            wo_spec,
            pl.BlockSpec((MEM_W, d), lambda i: (0, 0)),
        ],
        out_specs=pl.BlockSpec((tm, d), row),
        out_shape=jax.ShapeDtypeStruct((m, d), F32),
        compiler_params=_params(("parallel",), est),
        name="out_proj",
    )(x, o, mo, wo, w_out[MIX_W:])


def _select_blocks(gate, n_past, k_sel):
    lane = lax.broadcasted_iota(jnp.int32, gate.shape, 1)
    lane_f = lane.astype(F32)
    gate = jnp.where(lane < n_past, gate, NEG_INF)
    sel = jnp.zeros(gate.shape, F32)
    for _ in range(k_sel):
        mx = jnp.max(gate, axis=1, keepdims=True)
        first = jnp.min(jnp.where(gate == mx, lane_f, float(gate.shape[1])), axis=1, keepdims=True)
        pick = (lane_f == first) & (mx > NEG_INF)
        sel = jnp.where(pick, 1.0, sel)
        gate = jnp.where(pick, NEG_INF, gate)
    return sel


def _select_blocks_t(gate, n_past, k_sel):
    blk = lax.broadcasted_iota(jnp.int32, gate.shape, 0)
    blk_f = blk.astype(F32)
    gate = jnp.where(blk < n_past, gate, NEG_INF)
    sel = jnp.zeros(gate.shape, F32)
    for _ in range(k_sel):
        mx = jnp.max(gate, axis=0, keepdims=True)
        first = jnp.min(jnp.where(gate == mx, blk_f, float(gate.shape[0])), axis=0, keepdims=True)
        pick = (blk_f == first) & (mx > NEG_INF)
        sel = jnp.where(pick, 1.0, sel)
        gate = jnp.where(pick, NEG_INF, gate)
    return sel


def _moba_prompt_kernel(q_ref, k_ref, vt_ref, kbar_ref, o_ref, q_scr, m_scr, acc_scr, sel_scr,
                        s_scr, p_scr, alpha_scr, *, k_sel):
    i = pl.program_id(2)
    blk = MOBA_BLOCK
    d = HEAD_DIM
    cw = LANES
    n_chunks = KV_GROUP * blk // cw
    n_blocks = sel_scr.shape[0]
    chunks = [slice(c * cw, (c + 1) * cw) for c in range(n_chunks)]
    q = q_ref[0]
    q3 = jnp.concatenate([q[:, g * d:(g + 1) * d] for g in range(KV_GROUP)], axis=0)
    q_scr[...] = jnp.concatenate([q3, jnp.zeros_like(q3)], axis=1).astype(BF16)

    def block_scores(k_blk, c):
        return lax.dot_general(k_blk, q_scr[c * cw:(c + 1) * cw, :], NT,
                               preferred_element_type=F32)

    k_own = k_ref[0, pl.ds(pl.multiple_of(i * blk, blk), blk), :]
    own_scores = [block_scores(k_own, c) for c in range(n_chunks)]
    k_first = k_ref[0, 0:blk, :]
    first_scores = [block_scores(k_first, c) for c in range(n_chunks)]
    gate_t = lax.dot_general(kbar_ref[0, 0], q3, NT, precision=HI, preferred_element_type=F32)
    for c, cs in enumerate(chunks):
        s_scr[0, :, cs] = first_scores[c]
    for c, cs in enumerate(chunks):
        kpos = lax.broadcasted_iota(jnp.int32, (blk, cw), 0)
        qpos = (lax.broadcasted_iota(jnp.int32, (blk, cw), 1) + c * cw) % blk
        s = jnp.where(kpos <= qpos, own_scores[c], NEG_INF)
        m0 = jnp.max(s, axis=0, keepdims=True)
        m_scr[:, cs] = m0
        p_scr[1, :, cs] = jnp.exp2(s - m0).astype(BF16)
    acc_scr[...] = jnp.zeros(acc_scr.shape, F32)
    alpha_scr[...] = jnp.ones(alpha_scr.shape, F32)
    sel_scr[...] = _select_blocks_t(gate_t, i, k_sel)

    def pv_update(j_prev, slot):
        vt_prev = vt_ref[j_prev, 0]
        pv = [jnp.dot(vt_prev, p_scr[slot, :, cs], preferred_element_type=F32) for cs in chunks]
        for cs, r in zip(chunks, pv):
            acc_scr[:, cs] = alpha_scr[:, cs] * acc_scr[:, cs] + r

    def step(j, cur):
        nxt = 1 - cur
        j_next = jnp.minimum(j + 1, n_blocks - 1)
        k_next = k_ref[0, pl.ds(pl.multiple_of(j_next * blk, blk), blk), :]
        for c, cs in enumerate(chunks):
            s_scr[nxt, :, cs] = block_scores(k_next, c)
        pv_update(jnp.where(j == 0, i, jnp.minimum(j - 1, n_blocks - 1)), nxt)
        chosen = (sel_scr[pl.ds(jnp.minimum(j, n_blocks - 1), 1), :] > 0.0) & (j < i)
        for c, cs in enumerate(chunks):
            s = s_scr[cur, :, cs]
            m_old = m_scr[:, cs]
            m_new = jnp.maximum(m_old, jnp.where(chosen[:, cs], jnp.max(s, axis=0, keepdims=True),
                                                 NEG_INF))
            p_scr[cur, :, cs] = jnp.exp2(s - jnp.where(chosen[:, cs], m_new, float("inf"))).astype(BF16)
            alpha_scr[:, cs] = jnp.exp2(m_old - m_new)
            m_scr[:, cs] = m_new

    long_trip, short_trip = 8, 2

    def trips(base, per_trip):
        def body(t, carry):
            for h in range(per_trip):
                step(base + per_trip * t + h, h % 2)
            return carry
        return body

    n_long = i // long_trip
    lax.fori_loop(0, n_long, trips(0, long_trip), 0)
    done = n_long * long_trip
    n_short = (i - done + short_trip - 1) // short_trip
    lax.fori_loop(0, n_short, trips(done, short_trip), 0)
    last = done + n_short * short_trip - 1
    pv_update(jnp.where(i == 0, i, jnp.minimum(last, n_blocks - 1)), 1)
    acc = acc_scr[...]
    o_t = (acc[:d] / acc[d:d + 1]).astype(BF16)
    eye = (lax.broadcasted_iota(jnp.int32, (blk, blk), 0)
           == lax.broadcasted_iota(jnp.int32, (blk, blk), 1)).astype(BF16)
    o_ref[0] = jnp.concatenate(
        [lax.dot_general(eye, o_t[:, g * blk:(g + 1) * blk], NT, preferred_element_type=F32)
         for g in range(KV_GROUP)], axis=1)


def _moba_prompt(qh, kh, vt, kbar, b, l):
    nb = l // MOBA_BLOCK
    rows = KV_GROUP * MOBA_BLOCK
    k_sel = min(MOBA_TOPK, nb)
    vt_rows = vt.shape[2]
    est = 4 * l * LANES * 2 + 4 * l * vt_rows * 2 + 8 * MOBA_BLOCK * GROUP_W * 4 + 24 * rows * LANES * 4
    return pl.pallas_call(
        functools.partial(_moba_prompt_kernel, k_sel=k_sel),
        grid=(N_KV_B, b, nb),
        in_specs=[
            pl.BlockSpec((1, MOBA_BLOCK, GROUP_W), lambda hk, bi, i: (hk, bi * nb + i, 0)),
            pl.BlockSpec((1, l, LANES), lambda hk, bi, i: (hk, bi, 0)),
            pl.BlockSpec((nb, 1, vt_rows, MOBA_BLOCK), lambda hk, bi, i: (bi, hk, 0, 0)),
            pl.BlockSpec((1, 1, nb, HEAD_DIM), lambda hk, bi, i: (hk, bi, 0, 0)),
        ],
        out_specs=pl.BlockSpec((1, MOBA_BLOCK, GROUP_W), lambda hk, bi, i: (hk, bi * nb + i, 0)),
        out_shape=jax.ShapeDtypeStruct((N_KV_B, b * l, GROUP_W), F32),
        scratch_shapes=[
            pltpu.VMEM((rows, LANES), BF16),
            pltpu.VMEM((1, rows), F32),
            pltpu.VMEM((vt_rows, rows), F32),
            pltpu.VMEM((nb, rows), F32),
            pltpu.VMEM((2, MOBA_BLOCK, rows), F32),
            pltpu.VMEM((2, MOBA_BLOCK, rows), BF16),
            pltpu.VMEM((1, rows), F32),
        ],
        compiler_params=_params(("parallel", "parallel", "arbitrary"), est),
        name="moba_prompt",
    )(qh, kh, vt, kbar)


def _moba_sample_kernel(pt_ref, q_ref, kn_ref, vn_ref, kt_hbm, vt_hbm, o_ref,
                        kbuf, vbuf, sem, s_scr, p_scr,
                        *, n_seq, n_pages, l_new, k_sel):
    b = pl.program_id(0)
    slot = lax.rem(b, 2)
    nb = n_pages // 2
    page = kbuf.shape[-1]
    unroll = math.gcd(nb, 4)

    def page_copy(hbm, buf, sl, which, pg, p):
        return pltpu.make_async_copy(hbm.at[pg], buf.at[sl, p], sem.at[sl, which])

    def fetch(seq, sl):
        def body(p, carry):
            pg = pt_ref[seq, p]
            page_copy(kt_hbm, kbuf, sl, 0, pg, p).start()
            page_copy(vt_hbm, vbuf, sl, 1, pg, p).start()
            return carry
        lax.fori_loop(0, n_pages, body, 0)

    def wait_pages(hbm, buf, which):
        def body(p, carry):
            page_copy(hbm, buf, slot, which, 0, p).wait()
            return carry
        lax.fori_loop(0, n_pages, body, 0)

    @pl.when(b == 0)
    def _():
        fetch(0, 0)

    @pl.when(b + 1 < n_seq)
    def _():
        fetch(b + 1, 1 - slot)

    q = q_ref[0]
    qb = q.astype(BF16)
    rows = q.shape[0]
    wait_pages(kt_hbm, kbuf, 0)

    blk_lane = lax.broadcasted_iota(jnp.int32, (q.shape[1], nb), 1)

    def mean_body(j, kbar_t):
        t = kbuf[slot, 2 * j] + kbuf[slot, 2 * j + 1]
        col = jnp.sum(t, axis=1, keepdims=True) * (1.0 / MOBA_BLOCK)
        return jnp.where(blk_lane == j, col, kbar_t)

    kbar_t = lax.fori_loop(0, nb, mean_body, jnp.zeros((q.shape[1], nb), F32), unroll=unroll)
    gate = jnp.dot(q, kbar_t, precision=HI, preferred_element_type=F32)
    sel = _select_blocks(gate, nb, k_sel)
    lane = lax.broadcasted_iota(jnp.int32, sel.shape, 1)

    def score_body(j, carry):
        chosen = jnp.max(jnp.where(lane == j, sel, 0.0), axis=1, keepdims=True)
        s = jnp.concatenate(
            [jnp.dot(qb, kbuf[slot, 2 * j + h].astype(BF16), preferred_element_type=F32)
             for h in range(2)], axis=1)
        s = jnp.where(chosen > 0.0, s, NEG_INF)
        s_scr[j] = s
        return jnp.maximum(carry, s)

    s_max = lax.fori_loop(0, nb, score_body, jnp.full((rows, 2 * page), NEG_INF, F32), unroll=unroll)
    s_new = _bdot_nt(qb, kn_ref[0])
    t_q = lax.broadcasted_iota(jnp.int32, s_new.shape, 0) % l_new
    t_k = lax.broadcasted_iota(jnp.int32, s_new.shape, 1)
    s_new = jnp.where(t_k <= t_q, s_new, NEG_INF)
    m = jnp.maximum(jnp.max(s_max, axis=1, keepdims=True), jnp.max(s_new, axis=1, keepdims=True))
    p_new = jnp.exp2(s_new - m)

    def prob_body(j, carry):
        p = jnp.exp2(s_scr[j] - m)
        p_scr[j] = p.astype(BF16)
        return carry + p

    p_sum = lax.fori_loop(0, nb, prob_body, jnp.zeros((rows, 2 * page), F32), unroll=unroll)
    denom = jnp.sum(p_sum, axis=1, keepdims=True) + jnp.sum(p_new, axis=1, keepdims=True)
    wait_pages(vt_hbm, vbuf, 1)

    def pv_body(j, acc):
        pj = p_scr[j]
        for h in range(2):
            acc = acc + lax.dot_general(pj[:, h * page:(h + 1) * page],
                                        vbuf[slot, 2 * j + h].astype(BF16), NT,
                                        preferred_element_type=F32)
        return acc

    acc = lax.fori_loop(0, nb, pv_body, _bdot(p_new, vn_ref[0]), unroll=unroll)
    o_ref[0] = acc / denom


def _moba_sample(q_bd, page_table, kt_pool, vt_pool, k_new, v_new, l_new):
    bs, rows, w = q_bd.shape
    n_pages = page_table.shape[1]
    page = kt_pool.shape[2]
    assert 2 * page == MOBA_BLOCK and kt_pool.shape[1] == w
    nb = n_pages // 2
    k_sel = min(MOBA_TOPK, nb + 1)
    lp = k_new.shape[1]
    per_b = lambda r: pl.BlockSpec((1, r, w), lambda b, pt: (b, 0, 0))
    pool_bytes = 2 * n_pages * w * page * 4
    est = 2 * pool_bytes + 8 * nb * rows * 2 * page * 4
    return pl.pallas_call(
        functools.partial(_moba_sample_kernel, n_seq=bs, n_pages=n_pages, l_new=l_new, k_sel=k_sel),
        grid_spec=pltpu.PrefetchScalarGridSpec(
            num_scalar_prefetch=1,
            grid=(bs,),
            in_specs=[per_b(rows), per_b(lp), per_b(lp),
                      pl.BlockSpec(memory_space=pl.ANY), pl.BlockSpec(memory_space=pl.ANY)],
            out_specs=per_b(rows),
            scratch_shapes=[
                pltpu.VMEM((2, n_pages, w, page), F32),
                pltpu.VMEM((2, n_pages, w, page), F32),
                pltpu.SemaphoreType.DMA((2, 2)),
                pltpu.VMEM((nb, rows, 2 * page), F32),
                pltpu.VMEM((nb, rows, 2 * page), BF16),
            ],
        ),
        out_shape=jax.ShapeDtypeStruct((bs, rows, w), F32),
        compiler_params=pltpu.CompilerParams(
            dimension_semantics=("arbitrary",),
            vmem_limit_bytes=min(int(est * 1.25), VMEM_BYTES_V7X - 8 * 2 ** 20)),
        name="moba_sample",
    )(page_table, q_bd, k_new, v_new, kt_pool, vt_pool)


def _pack_w_in_a(w):
    d = w.shape[0]
    qkv_gate = w[:, :4 * MIX_W]
    ab = w[:, 4 * MIX_W:4 * MIX_W + 2 * N_HEADS]
    q_mem = w[:, 4 * MIX_W + 2 * N_HEADS:]
    ab = jnp.pad(ab, ((0, 0), (0, LANES - 2 * N_HEADS)))
    return jnp.concatenate([qkv_gate, q_mem, ab], axis=1).astype(BF16)


def _pad_rows(a, rows):
    return jnp.pad(a, ((0, 0), (0, rows - a.shape[1]), (0, 0)))


def _trunk(p, x3, pos_base, mem_k, mem_v, conv0, s0, paged):
    b, l, d = x3.shape
    m = b * l
    x = x3.reshape(m, d)
    sample = paged is not None
    ffn = lambda x, tag, i, fg=None: _ffn(x, p[tag + "_norm"][i], p[tag + "_w1"], p[tag + "_w3"],
                                          p[tag + "_w2"], i, fg)
    lq = -(-l // SUBLANES) * SUBLANES

    def memory(q_mem, layer):
        q3 = _pad_rows(q_mem.reshape(b, l, MEM_W), lq)
        mo = _mem_attn(q3, mem_k[layer], mem_v[layer])
        return mo[:, :l].reshape(m, MEM_W)

    x = ffn(x, "ffn1", 0)
    if l % IN_PROJ_TILE == 0:
        qkv, gate, q_mem, ab, new_conv = _in_proj_conv(x, p["mix_norm"][0], p["w_in_a"], conv0,
                                                       p["dn_conv_w"], l)
        qkv = qkv.reshape(b, l, 3 * MIX_W)
    else:
        qkv, gate, q_mem, ab = _norm_proj(x, p["mix_norm"][0], p["w_in_a"],
                                          (3 * MIX_W, MIX_W, MEM_W, LANES), "in_proj_a")
        qkv, new_conv = _conv(qkv.reshape(b, l, 3 * MIX_W), conv0, p["dn_conv_w"])
    lp = -(-l // DN_CHUNK) * DN_CHUNK
    o, s_fin = _gdn(_pad_rows(qkv, lp), _pad_rows(gate.reshape(b, l, MIX_W), lp),
                    _pad_rows(ab.reshape(b, l, LANES), lp),
                    p["dn_a_log"], p["dn_dt_bias"], p["dn_out_norm"], s0, l)
    o = o[:, :l].reshape(m, MIX_W)
    x = _out_proj(x, o, memory(q_mem, 0), p["w_out"][0], grouped=False)
    x = ffn(x, "ffn2", 0)

    if sample:
        tables = _rope_tables(m, l, pos_base)
    else:
        tables = _rope_tables(l, l, pos_base)
    kv = _kv_proj(x, p["kv_norm"], p["w_kv"], tables, l, for_prompt=not sample)
    x = ffn(x, "ffn1", 1)
    qh, q_mem = _q_proj(x, p["mix_norm"][1], p["w_in_b"], tables)
    if sample:
        page_table, kt_pool, vt_pool = paged
        k_new, v_new = kv
        k_out = k_new.reshape(b, l, N_KV_B, HEAD_DIM)
        v_out = v_new.reshape(b, l, N_KV_B, HEAD_DIM)
        eye = jnp.eye(N_KV_B, dtype=F32)
        q5 = qh.reshape(N_KV_B, b, l, KV_GROUP, HEAD_DIM).transpose(1, 0, 3, 2, 4)
        q_bd = (q5[:, :, :, :, None, :] * eye[None, :, None, None, :, None]).reshape(
            b, N_HEADS * l, N_KV_B * HEAD_DIM)
        o_bd = _moba_sample(q_bd, page_table, kt_pool, vt_pool,
                            _pad_rows(k_new.reshape(b, l, -1), lq),
                            _pad_rows(v_new.reshape(b, l, -1), lq), l)
        o6 = o_bd.reshape(b, N_KV_B, KV_GROUP, l, N_KV_B, HEAD_DIM)
        o = (o6 * eye[None, :, None, None, :, None]).sum(axis=4)
        o = o.transpose(0, 3, 1, 2, 4).reshape(m, MIX_W)
        x = _out_proj(x, o, memory(q_mem, 1), p["w_out"][1], grouped=False)
    else:
        k_t, v_t, kh, vt, kbar = kv
        to_rows = lambda a: a.reshape(b, N_KV_B, HEAD_DIM, l).transpose(0, 3, 1, 2)
        k_out, v_out = to_rows(k_t), to_rows(v_t)
        nb = l // MOBA_BLOCK
        kbar = kbar.reshape(b, nb, N_KV_B, HEAD_DIM).transpose(2, 0, 1, 3)
        o4 = _moba_prompt(qh, kh, vt, kbar, b, l)
        x = _out_proj(x, o4, memory(q_mem, 1), p["w_out"][1], grouped=True)
    y = ffn(x, "ffn2", 1, p["final_norm"])
    return y.reshape(b, l, d), new_conv[None], s_fin[None], k_out, v_out


def kernel(x_prompt, x_sample, mem_prompt, cache_mem_k, cache_mem_v, state_dn_conv, state_dn_S,
           cache_kv_k, cache_kv_v, page_table,
           ffn1_norm, ffn1_w1, ffn1_w3, ffn1_w2, mix_norm, w_in_a, w_in_b, w_out,
           dn_conv_w, dn_a_log, dn_dt_bias, dn_out_norm, kv_norm, w_kv, mem_norm, w_mem_kv,
           ffn2_norm, ffn2_w1, ffn2_w3, ffn2_w2, final_norm):
    assert w_in_a.shape[0] == 1 and w_in_b.shape[0] == 1 and ffn1_w1.shape[0] == 2
    bf = lambda a: a.astype(BF16)
    p = dict(ffn1_norm=ffn1_norm, ffn1_w1=bf(ffn1_w1), ffn1_w3=bf(ffn1_w3), ffn1_w2=bf(ffn1_w2),
             ffn2_norm=ffn2_norm, ffn2_w1=bf(ffn2_w1), ffn2_w3=bf(ffn2_w3), ffn2_w2=bf(ffn2_w2),
             mix_norm=mix_norm, w_in_a=_pack_w_in_a(w_in_a[0]), w_in_b=bf(w_in_b[0]), w_out=bf(w_out),
             dn_conv_w=dn_conv_w[0], dn_a_log=dn_a_log[0], dn_dt_bias=dn_dt_bias[0],
             dn_out_norm=dn_out_norm[0], kv_norm=kv_norm, w_kv=bf(w_kv), final_norm=final_norm)

    bp, lp, _ = x_prompt.shape
    assert lp % MOBA_BLOCK == 0
    n_mem = mem_prompt.shape[1]
    n_layers = mem_norm.shape[0]
    mem_t = [_mem_kv(mem_prompt, mem_norm[layer], bf(w_mem_kv[layer])) for layer in range(n_layers)]
    mk = [t[0] for t in mem_t]
    mv = [t[1] for t in mem_t]
    conv0 = jnp.zeros((bp, CONV_W - 1, 3 * MIX_W), F32)
    s0 = jnp.zeros((bp, N_HEADS, HEAD_DIM, HEAD_DIM), F32)
    y_p, conv_p, s_p, k_p, v_p = _trunk(p, x_prompt, 0, mk, mv, conv0, s0, None)
    token_major = lambda ts: jnp.stack(ts).reshape(n_layers, bp, N_MEM_HEADS, HEAD_DIM, n_mem).transpose(
        0, 1, 4, 2, 3)
    mem_k_p = token_major(mk)
    mem_v_p = token_major(mv)

    bs = x_sample.shape[0]
    n_pool, page = cache_kv_k.shape[:2]
    past_len = page_table.shape[1] * page
    assert past_len % MOBA_BLOCK == 0 and x_sample.shape[1] <= MOBA_BLOCK
    token_minor = lambda c: jnp.transpose(c, (0, 1, 3, 4, 2)).reshape(c.shape[0], bs, MEM_W, n_mem)
    cmk = token_minor(cache_mem_k)
    cmv = token_minor(cache_mem_v)
    transposed_pages = lambda c: jnp.transpose(c, (0, 2, 3, 1)).reshape(n_pool, -1, page)
    paged = (page_table, transposed_pages(cache_kv_k), transposed_pages(cache_kv_v))
    y_s, conv_s, s_s, k_s, v_s = _trunk(p, x_sample, past_len, cmk, cmv, state_dn_conv[0],
                                        state_dn_S[0], paged)
    return (y_p, y_s, conv_p, s_p, k_p, v_p, mem_k_p, mem_v_p, conv_s, s_s, k_s, v_s)
```

```python
import functools
import math

import jax
import jax.numpy as jnp
from jax import lax
from jax.experimental import pallas as pl
from jax.experimental.pallas import tpu as pltpu

F32 = jnp.float32
BF16 = jnp.bfloat16
HI = lax.Precision.HIGHEST

HEAD_DIM = 64
N_HEADS = 12
MIX_W = N_HEADS * HEAD_DIM
N_MEM_HEADS = 4
MEM_W = N_MEM_HEADS * HEAD_DIM
N_KV_B = 4
KV_GROUP = N_HEADS // N_KV_B
GROUP_W = KV_GROUP * HEAD_DIM
CONV_W = 4
DN_CHUNK = 64
MOBA_BLOCK = 256
MOBA_TOPK = 3
ROT_DIM = HEAD_DIM // 4
ROPE_THETA = 500000.0
EPS = 1e-6
QK_SCALE = HEAD_DIM ** -0.5
LOG2E = math.log2(math.e)
ONES_ROWS = 16

LANES = 128
SUBLANES = 8
VMEM_BYTES_V7X = 64 * 2 ** 20
NEG_INF = float("-inf")

NT = (((1,), (1,)), ((), ()))
TN = (((0,), (0,)), ((), ()))


def _params(semantics, est_bytes):
    limit = min(max(int(est_bytes * 1.5), 16 * 2 ** 20), VMEM_BYTES_V7X - 8 * 2 ** 20)
    return pltpu.CompilerParams(dimension_semantics=semantics, vmem_limit_bytes=limit)


def _row_tile(m, pref):
    t = min(m, pref)
    while m % t or t % SUBLANES:
        t -= 1
    return t


def _rms(x, g):
    return x * lax.rsqrt(jnp.mean(x * x, axis=-1, keepdims=True) + EPS) * g


def _sigmoid(x):
    return 1.0 / (1.0 + jnp.exp(-x))


def _softplus(x):
    return jnp.maximum(x, 0.0) + jnp.log(1.0 + jnp.exp(-jnp.abs(x)))


def _bdot(a, b):
    return jnp.dot(a.astype(BF16), b.astype(BF16), preferred_element_type=F32)


def _bdot_nt(a, b):
    return lax.dot_general(a.astype(BF16), b.astype(BF16), NT, preferred_element_type=F32)


MXU_WIDTH = 2 * LANES


def _ffn_kernel(*refs, splits, final):
    if final:
        x_ref, g_ref, w1_ref, w3_ref, w2_ref, fg_ref, o_ref = refs
    else:
        x_ref, g_ref, w1_ref, w3_ref, w2_ref, o_ref = refs
    x = x_ref[...]
    h = _rms(x, g_ref[...]).astype(BF16)
    acc = None
    for lo, hi in splits:
        a = jnp.dot(h, w1_ref[:, lo:hi], preferred_element_type=F32)
        b = jnp.dot(h, w3_ref[:, lo:hi], preferred_element_type=F32)
        u = ((a * _sigmoid(a)) * b).astype(BF16)
        part = jnp.dot(u, w2_ref[lo:hi, :], preferred_element_type=F32)
        acc = part if acc is None else acc + part
    y = x + 0.5 * acc
    if final:
        y = _rms(y, fg_ref[...])
    o_ref[...] = y


def _ffn(x, g, w1, w3, w2, layer, final_gain=None):
    m, d = x.shape
    d_ff = w1.shape[2]
    tm = _row_tile(m, 512)
    tiles = d_ff // MXU_WIDTH
    cut = (tiles // 2) * MXU_WIDTH
    splits = ((0, cut), (cut, d_ff)) if cut else ((0, d_ff),)
    final = final_gain is not None
    resident = lambda shape: pl.BlockSpec((None,) + shape, lambda i: (layer, 0, 0),
                                          pipeline_mode=pl.Buffered(1))
    in_specs = [
        pl.BlockSpec((tm, d), lambda i: (i, 0)),
        pl.BlockSpec((1, d), lambda i: (0, 0)),
        resident((d, d_ff)), resident((d, d_ff)), resident((d_ff, d)),
    ]
    args = [x, g.reshape(1, d), w1, w3, w2]
    if final:
        in_specs.append(pl.BlockSpec((1, d), lambda i: (0, 0)))
        args.append(final_gain.reshape(1, d))
    widest = max(hi - lo for lo, hi in splits)
    est = 4 * tm * d * 4 + 3 * d * d_ff * 2 + 2 * tm * d * 4 + 3 * tm * widest * 4
    return pl.pallas_call(
        functools.partial(_ffn_kernel, splits=splits, final=final),
        grid=(m // tm,),
        in_specs=in_specs,
        out_specs=pl.BlockSpec((tm, d), lambda i: (i, 0)),
        out_shape=jax.ShapeDtypeStruct((m, d), F32),
        compiler_params=_params(("parallel",), est),
        name="ffn_final" if final else "ffn",
    )(*args)


def _norm_proj_kernel(x_ref, g_ref, w_ref, *out_refs, splits):
    h = _rms(x_ref[...], g_ref[...]).astype(BF16)
    for o_ref, (a, b) in zip(out_refs, splits):
        o_ref[...] = jnp.dot(h, w_ref[:, a:b], preferred_element_type=F32)


def _norm_proj(x, g, w, widths, name):
    m, d = x.shape
    n = w.shape[1]
    assert sum(widths) == n and all(c % LANES == 0 for c in widths)
    tm = _row_tile(m, 256)
    splits, a = [], 0
    for c in widths:
        splits.append((a, a + c))
        a += c
    est = 2 * tm * d * 4 + 2 * d * n * 2 + 3 * tm * n * 4
    return pl.pallas_call(
        functools.partial(_norm_proj_kernel, splits=tuple(splits)),
        grid=(m // tm,),
        in_specs=[
            pl.BlockSpec((tm, d), lambda i: (i, 0)),
            pl.BlockSpec((1, d), lambda i: (0, 0)),
            pl.BlockSpec((d, n), lambda i: (0, 0)),
        ],
        out_specs=[pl.BlockSpec((tm, c), lambda i: (i, 0)) for c in widths],
        out_shape=[jax.ShapeDtypeStruct((m, c), F32) for c in widths],
        compiler_params=_params(("parallel",), est),
        name=name,
    )(x, g.reshape(1, d), w)


def _rope_table_kernel(inv_ref, c_ref, s1_ref, s2_ref, *, period, base, tl):
    i = pl.program_id(0)
    row = lax.broadcasted_iota(jnp.int32, (tl, LANES), 0) + i * tl
    pos = base + lax.rem(row, period)
    ang = pos.astype(F32) * inv_ref[...]
    in_head = lax.broadcasted_iota(jnp.int32, (tl, LANES), 1) % HEAD_DIM
    half = ROT_DIM // 2
    c = jnp.cos(ang)
    s = jnp.sin(ang)
    c_ref[...] = jnp.where(in_head < ROT_DIM, c, 1.0)
    s1_ref[...] = jnp.where(in_head < half, -s, 0.0)
    s2_ref[...] = jnp.where((in_head >= half) & (in_head < ROT_DIM), s, 0.0)


def _rope_tables(n_rows, period, base):
    half = ROT_DIM // 2
    inv = ROPE_THETA ** (-jnp.arange(half, dtype=F32) * 2.0 / ROT_DIM)
    in_head = jnp.arange(LANES) % HEAD_DIM
    inv_lane = jnp.where(in_head < ROT_DIM, inv[in_head % half], 0.0).astype(F32).reshape(1, LANES)
    tl = _row_tile(n_rows, 1024)
    shp = jax.ShapeDtypeStruct((n_rows, LANES), F32)
    return pl.pallas_call(
        functools.partial(_rope_table_kernel, period=period, base=base, tl=tl),
        grid=(n_rows // tl,),
        in_specs=[pl.BlockSpec((1, LANES), lambda i: (0, 0))],
        out_specs=[pl.BlockSpec((tl, LANES), lambda i: (i, 0))] * 3,
        out_shape=[shp] * 3,
        compiler_params=_params(("parallel",), 16 * tl * LANES * 4),
        name="rope_tables",
    )(inv_lane)


def _apply_rope(t, c, s1, s2):
    w = t.shape[1]
    reps = w // LANES
    half = ROT_DIM // 2
    tile = lambda a: jnp.concatenate([a] * reps, axis=1) if reps > 1 else a
    up = pltpu.roll(t, w - half, 1)
    down = pltpu.roll(t, half, 1)
    return t * tile(c) + up * tile(s1) + down * tile(s2)


def _kv_proj_kernel(x_ref, g_ref, w_ref, c_ref, s1_ref, s2_ref, *out_refs, for_prompt):
    h = _rms(x_ref[...], g_ref[...]).astype(BF16)
    kv_w = N_KV_B * HEAD_DIM
    k = jnp.dot(h, w_ref[:, :kv_w], preferred_element_type=F32)
    v = jnp.dot(h, w_ref[:, kv_w:], preferred_element_type=F32)
    k = _apply_rope(k, c_ref[...], s1_ref[...], s2_ref[...])
    if for_prompt:
        kt_ref, vt_ref, kh_ref, vtb_ref, kbar_ref, v_scr = out_refs
        v_scr[...] = v
        v_t = v_scr[...].T
        kt_ref[0] = k.T
        vt_ref[0] = v_t
        for hk in range(N_KV_B):
            sl = slice(hk * HEAD_DIM, (hk + 1) * HEAD_DIM)
            k_h = k[:, sl]
            kh_ref[hk] = jnp.concatenate([k_h, jnp.zeros_like(k_h)], axis=1).astype(BF16)
            vtb_ref[0, hk, 0:HEAD_DIM, :] = v_t[sl, :].astype(BF16)
            vtb_ref[0, hk, HEAD_DIM:, :] = jnp.ones((ONES_ROWS, v_t.shape[1]), BF16)
        kbar_ref[0] = jnp.sum(k, axis=0, keepdims=True) * (1.0 / MOBA_BLOCK)
    else:
        out_refs[0][...] = k
        out_refs[1][...] = v


def _kv_proj(x, g, w, tables, seq_len, for_prompt):
    m, d = x.shape
    kv_w = N_KV_B * HEAD_DIM
    tm = MOBA_BLOCK if for_prompt else _row_tile(m, 256)
    n_tab = tables[0].shape[0] // tm
    row = lambda i: (i, 0)
    tab = lambda i: (i % n_tab, 0)
    if for_prompt:
        nb = seq_len // tm
        t_spec = pl.BlockSpec((1, kv_w, tm), lambda i: (i // nb, 0, i % nb))
        t_shape = jax.ShapeDtypeStruct((m // seq_len, kv_w, seq_len), F32)
        vt_rows = HEAD_DIM + ONES_ROWS
        out_specs = [t_spec, t_spec,
                     pl.BlockSpec((N_KV_B, tm, LANES), lambda i: (0, i, 0)),
                     pl.BlockSpec((1, N_KV_B, vt_rows, tm), lambda i: (i, 0, 0, 0)),
                     pl.BlockSpec((1, 1, kv_w), lambda i: (i, 0, 0))]
        out_shape = [t_shape, t_shape,
                     jax.ShapeDtypeStruct((N_KV_B, m, LANES), BF16),
                     jax.ShapeDtypeStruct((m // tm, N_KV_B, vt_rows, tm), BF16),
                     jax.ShapeDtypeStruct((m // tm, 1, kv_w), F32)]
    else:
        out_specs = [pl.BlockSpec((tm, kv_w), row), pl.BlockSpec((tm, kv_w), row)]
        out_shape = [jax.ShapeDtypeStruct((m, kv_w), F32)] * 2
    est = 2 * tm * d * 4 + 2 * d * 2 * kv_w * 2 + 12 * tm * kv_w * 4
    return pl.pallas_call(
        functools.partial(_kv_proj_kernel, for_prompt=for_prompt),
        grid=(m // tm,),
        in_specs=[
            pl.BlockSpec((tm, d), row),
            pl.BlockSpec((1, d), lambda i: (0, 0)),
            pl.BlockSpec((d, 2 * kv_w), lambda i: (0, 0)),
            pl.BlockSpec((tm, LANES), tab),
            pl.BlockSpec((tm, LANES), tab),
            pl.BlockSpec((tm, LANES), tab),
        ],
        out_specs=out_specs,
        out_shape=out_shape,
        scratch_shapes=[pltpu.VMEM((tm, kv_w), F32)] if for_prompt else [],
        compiler_params=_params(("parallel",), est),
        name="kv_proj",
    )(x, g.reshape(1, d), w, *tables)


def _q_proj_kernel(x_ref, g_ref, w_ref, c_ref, s1_ref, s2_ref, qh_ref, qm_ref):
    h = _rms(x_ref[...], g_ref[...]).astype(BF16)
    q = jnp.dot(h, w_ref[:, :MIX_W], preferred_element_type=F32)
    qm_ref[...] = jnp.dot(h, w_ref[:, MIX_W:], preferred_element_type=F32)
    q = _apply_rope(q, c_ref[...], s1_ref[...], s2_ref[...]) * (QK_SCALE * LOG2E)
    for hk in range(N_KV_B):
        qh_ref[hk] = q[:, hk * GROUP_W:(hk + 1) * GROUP_W]


def _q_proj(x, g, w, tables):
    m, d = x.shape
    n = w.shape[1]
    tm = _row_tile(m, 512)
    n_tab = tables[0].shape[0] // tm
    row = lambda i: (i, 0)
    tab = lambda i: (i % n_tab, 0)
    est = 2 * tm * d * 4 + 2 * d * n * 2 + 16 * tm * MIX_W * 4
    return pl.pallas_call(
        _q_proj_kernel,
        grid=(m // tm,),
        in_specs=[
            pl.BlockSpec((tm, d), row),
            pl.BlockSpec((1, d), lambda i: (0, 0)),
            pl.BlockSpec((d, n), lambda i: (0, 0)),
            pl.BlockSpec((tm, LANES), tab),
            pl.BlockSpec((tm, LANES), tab),
            pl.BlockSpec((tm, LANES), tab),
        ],
        out_specs=[
            pl.BlockSpec((N_KV_B, tm, GROUP_W), lambda i: (0, i, 0)),
            pl.BlockSpec((tm, MEM_W), row),
        ],
        out_shape=[
            jax.ShapeDtypeStruct((N_KV_B, m, GROUP_W), F32),
            jax.ShapeDtypeStruct((m, MEM_W), F32),
        ],
        compiler_params=_params(("parallel",), est),
        name="q_proj",
    )(x, g.reshape(1, d), w, *tables)


def _conv_kernel(u_ref, prev_ref, buf_ref, w_ref, y_ref, nb_ref, pad_scr, *, tl, n_t):
    i = pl.program_id(1)
    halo = CONV_W - 1
    lo = SUBLANES - halo
    pad_scr[SUBLANES:SUBLANES + tl, :] = u_ref[0]

    @pl.when(i == 0)
    def _():
        pad_scr[lo:SUBLANES, :] = buf_ref[0]

    @pl.when(i > 0)
    def _():
        p = prev_ref[0]
        pad_scr[lo:SUBLANES, :] = p[p.shape[0] - halo:, :]

    y = pad_scr[lo:lo + tl, :] * w_ref[0:1, :]
    for j in range(1, CONV_W):
        y = y + pad_scr[lo + j:lo + j + tl, :] * w_ref[j:j + 1, :]
    y_ref[0] = y * _sigmoid(y)

    @pl.when(i == n_t - 1)
    def _():
        nb_ref[0] = pad_scr[lo + tl:SUBLANES + tl, :]


def _conv(u, buf, w):
    b, l, c = u.shape
    tl = _row_tile(l, 512) if l % SUBLANES == 0 else l
    n_t = l // tl
    pr = min(SUBLANES, l)
    per = tl // pr
    est = 6 * tl * c * 4
    return pl.pallas_call(
        functools.partial(_conv_kernel, tl=tl, n_t=n_t),
        grid=(b, n_t),
        in_specs=[
            pl.BlockSpec((1, tl, c), lambda bi, i: (bi, i, 0)),
            pl.BlockSpec((1, pr, c), lambda bi, i: (bi, jnp.maximum(i * per - 1, 0), 0)),
            pl.BlockSpec((1, CONV_W - 1, c), lambda bi, i: (bi, 0, 0)),
            pl.BlockSpec((CONV_W, c), lambda bi, i: (0, 0)),
        ],
        out_specs=[
            pl.BlockSpec((1, tl, c), lambda bi, i: (bi, i, 0)),
            pl.BlockSpec((1, CONV_W - 1, c), lambda bi, i: (bi, 0, 0)),
        ],
        out_shape=[
            jax.ShapeDtypeStruct((b, l, c), F32),
            jax.ShapeDtypeStruct((b, CONV_W - 1, c), F32),
        ],
        scratch_shapes=[pltpu.VMEM((tl + SUBLANES, c), F32)],
        compiler_params=_params(("parallel", "arbitrary"), est),
        name="short_conv",
    )(u, u, buf, w)


IN_PROJ_TILE = 256


def _in_proj_conv_kernel(x_ref, g_ref, w_ref, buf_ref, cw_ref, qkv_ref, gate_ref, qm_ref, ab_ref,
                         nb_ref, prev_scr, *, tm, n_t):
    i = pl.program_id(1)
    halo = CONV_W - 1
    lo = SUBLANES - halo
    c3 = 3 * MIX_W
    h = _rms(x_ref[...], g_ref[...]).astype(BF16)

    @pl.when(i == 0)
    def _():
        prev_scr[...] = jnp.zeros(prev_scr.shape, F32)
        prev_scr[lo:SUBLANES, :] = buf_ref[0]

    u = jnp.dot(h, w_ref[:, :c3], preferred_element_type=F32)
    gate_ref[...] = jnp.dot(h, w_ref[:, c3:c3 + MIX_W], preferred_element_type=F32)
    qm_ref[...] = jnp.dot(h, w_ref[:, c3 + MIX_W:c3 + MIX_W + MEM_W], preferred_element_type=F32)
    ab_ref[...] = jnp.dot(h, w_ref[:, c3 + MIX_W + MEM_W:], preferred_element_type=F32)
    prev = prev_scr[...]
    row8 = lax.broadcasted_iota(jnp.int32, (SUBLANES, c3), 0)
    y = u * cw_ref[halo:CONV_W, :]
    y8 = u[0:SUBLANES] * cw_ref[halo:CONV_W, :]
    for s in range(1, CONV_W):
        tap = cw_ref[halo - s:CONV_W - s, :]
        shifted = pltpu.roll(u, s, 0)
        y = y + shifted * tap
        y8 = y8 + jnp.where(row8 < s, pltpu.roll(prev, s, 0), shifted[0:SUBLANES]) * tap
    qkv_ref[...] = y * _sigmoid(y)
    qkv_ref[0:SUBLANES, :] = y8 * _sigmoid(y8)
    last8 = u[tm - SUBLANES:, :]

    @pl.when(i == n_t - 1)
    def _():
        nb_ref[0] = last8[lo:, :]

    prev_scr[...] = last8


def _in_proj_conv(x, g, w, buf, conv_w, seq_len):
    m, d = x.shape
    n = w.shape[1]
    c3 = 3 * MIX_W
    tm = IN_PROJ_TILE
    n_t = seq_len // tm
    b = m // seq_len
    row = lambda bi, i: (bi * n_t + i, 0)
    const = lambda bi, i: (0, 0)
    widths = (c3, MIX_W, MEM_W, n - c3 - MIX_W - MEM_W)
    est = 2 * tm * d * 4 + 2 * d * n * 2 + 4 * tm * n * 4
    return pl.pallas_call(
        functools.partial(_in_proj_conv_kernel, tm=tm, n_t=n_t),
        grid=(b, n_t),
        in_specs=[
            pl.BlockSpec((tm, d), row),
            pl.BlockSpec((1, d), const),
            pl.BlockSpec((d, n), const),
            pl.BlockSpec((1, CONV_W - 1, c3), lambda bi, i: (bi, 0, 0)),
            pl.BlockSpec((CONV_W, c3), const),
        ],
        out_specs=[pl.BlockSpec((tm, c), row) for c in widths]
        + [pl.BlockSpec((1, CONV_W - 1, c3), lambda bi, i: (bi, 0, 0))],
        out_shape=[jax.ShapeDtypeStruct((m, c), F32) for c in widths]
        + [jax.ShapeDtypeStruct((b, CONV_W - 1, c3), F32)],
        scratch_shapes=[pltpu.VMEM((SUBLANES, c3), F32)],
        compiler_params=_params(("parallel", "arbitrary"), est),
        name="in_proj_conv",
    )(x, g.reshape(1, d), w, buf, conv_w)


HEADS_PER_SLAB = LANES * 2 // HEAD_DIM
SLAB_W = HEADS_PER_SLAB * HEAD_DIM
N_SLABS = N_HEADS // HEADS_PER_SLAB


def _split2(x):
    hi = x.astype(BF16)
    return hi, (x - hi.astype(F32)).astype(BF16)


def _split3(x):
    hi = x.astype(BF16)
    r = x - hi.astype(F32)
    mid = r.astype(BF16)
    return hi, mid, (r - mid.astype(F32)).astype(BF16)


def _dot_right01(x, sel):
    hi, mid, lo = _split3(x)
    d = lambda a: jnp.dot(a, sel, preferred_element_type=F32)
    return d(hi) + (d(mid) + d(lo))


def _dot_left01(sel, x):
    hi, mid, lo = _split3(x)
    d = lambda a: jnp.dot(sel, a, preferred_element_type=F32)
    return d(hi) + (d(mid) + d(lo))


def _dot3(a, b_hi, b_lo):
    a_hi, a_lo = _split2(a)
    d = lambda x, y: jnp.dot(x, y, preferred_element_type=F32)
    return d(a_hi, b_hi) + (d(a_hi, b_lo) + d(a_lo, b_hi))


def _block_diag(x, mask):
    return jnp.concatenate([x] * HEADS_PER_SLAB, axis=0) * mask


def _block_diag_pieces(x, mask):
    hi, lo = _split2(x)
    return _block_diag(hi, mask), _block_diag(lo, mask)


def _unit_lower_inverses(ms, eye_t, mask):
    c = ms[0].shape[0]
    xs = [eye_t - m for m in ms]
    ps = [_dot3(m, *_block_diag_pieces(m, mask)) for m in ms]
    power = 2
    while 2 * power < c:
        xps = [_dot3(jnp.concatenate([x, p], axis=0), *_block_diag_pieces(p, mask))
               for x, p in zip(xs, ps)]
        xs = [x + xp[:c] for x, xp in zip(xs, xps)]
        ps = [xp[c:] for xp in xps]
        power *= 2
    return [x + _dot3(x, *_block_diag_pieces(p, mask)) for x, p in zip(xs, ps)]


def _gdn_kernel(q_ref, k_ref, v_ref, gate_ref, ab_ref, alog_ref, dt_ref, gain_ref, s0_ref,
                o_ref, sfin_ref,
                s_scr, qn_scr, kn_scr, gcb_scr, bb_scr, u_scr, w_scr, a_scr, qg_scr, kd_scr, egl_scr,
                *, t_rows, n_t, l_valid):
    t = pl.program_id(1)
    c = DN_CHUNK
    d = HEAD_DIM
    n_chunks = t_rows // c

    r_s = lax.broadcasted_iota(jnp.int32, (SLAB_W, SLAB_W), 0)
    c_s = lax.broadcasted_iota(jnp.int32, (SLAB_W, SLAB_W), 1)
    same_head = (r_s // d) == (c_s // d)
    bd_f32 = jnp.where(same_head, 1.0, 0.0).astype(F32)
    bd_mask = bd_f32.astype(BF16)
    r_t = lax.broadcasted_iota(jnp.int32, (c, SLAB_W), 0)
    c_t = lax.broadcasted_iota(jnp.int32, (c, SLAB_W), 1) % d
    lower_t = r_t >= c_t
    strict_t = r_t > c_t
    eye_t = jnp.where(r_t == c_t, 1.0, 0.0).astype(F32)
    r_c = lax.broadcasted_iota(jnp.int32, (c, c), 0)
    c_c = lax.broadcasted_iota(jnp.int32, (c, c), 1)
    tril = jnp.where(r_c >= c_c, 1.0, 0.0).astype(BF16)
    er = lax.broadcasted_iota(jnp.int32, (LANES, MIX_W), 0)
    ec = lax.broadcasted_iota(jnp.int32, (LANES, MIX_W), 1) // d
    sel_g = jnp.where(er == ec, 1.0, 0.0).astype(BF16)
    sel_b = jnp.where(er == ec + N_HEADS, 1.0, 0.0).astype(BF16)

    @pl.when(t == 0)
    def _():
        for s in range(N_SLABS):
            rows = jnp.concatenate([s0_ref[0, s * HEADS_PER_SLAB + h] for h in range(HEADS_PER_SLAB)],
                                   axis=0)
            s_scr[s] = jnp.concatenate([rows] * HEADS_PER_SLAB, axis=1) * bd_f32

    ab = ab_ref[0]
    row = lax.broadcasted_iota(jnp.int32, ab.shape, 0) + t * t_rows
    valid = row < l_valid
    g = jnp.where(valid, -jnp.exp(alog_ref[...]) * _softplus(ab + dt_ref[...]), 0.0)
    beta = jnp.where(valid, _sigmoid(ab), 0.0)
    gc = jnp.concatenate([_dot_left01(tril, g[i * c:(i + 1) * c]) for i in range(n_chunks)], axis=0)
    gcb_scr[...] = _dot_right01(gc, sel_g)
    bb_scr[...] = _dot_right01(beta, sel_b)
    for s in range(N_SLABS):
        cs = slice(s * SLAB_W, (s + 1) * SLAB_W)
        for src, dst, scale in ((q_ref, qn_scr, QK_SCALE), (k_ref, kn_scr, 1.0)):
            x = src[0, :, cs]
            hi, lo = _split2(x * x)
            ssq = (jnp.dot(hi, bd_mask, preferred_element_type=F32)
                   + jnp.dot(lo, bd_mask, preferred_element_type=F32))
            dst[:, cs] = x * (lax.rsqrt(ssq + EPS) * scale)

    per_iter = math.gcd(n_chunks, 2)
    slabs = [slice(s * SLAB_W, (s + 1) * SLAB_W) for s in range(N_SLABS)]

    def factors(it, carry):
        items = []
        for u in range(per_iter):
            ci = it * per_iter + u
            rows = pl.ds(pl.multiple_of(ci * c, c), c)
            items += [(ci, rows, cs) for cs in slabs]
        pre = []
        for ci, rows, cs in items:
            gcb = gcb_scr[rows, cs]
            kn = kn_scr[rows, cs]
            qn = qn_scr[rows, cs]
            b_ = bb_scr[rows, cs]
            kb = kn * b_
            g_row = jnp.sum(gcb * eye_t, axis=0, keepdims=True)
            dec = jnp.exp(jnp.where(lower_t, gcb - g_row, NEG_INF))
            both = lax.dot_general(jnp.concatenate([kb, qn], axis=0).astype(BF16),
                                   _block_diag(kn.astype(BF16), bd_mask), NT,
                                   preferred_element_type=F32)
            pre.append((gcb, kn, qn, kb, dec, both))
        ms = []
        for (ci, rows, cs), (gcb, kn, qn, kb, dec, both) in zip(items, pre):
            ms.append(jnp.where(strict_t, both[:c] * dec, 0.0))
            a_scr[rows, cs] = (both[c:] * dec).astype(BF16)
        tinvs = _unit_lower_inverses(ms, eye_t, bd_mask)
        sols = []
        for (ci, rows, cs), (gcb, kn, qn, kb, dec, both), tinv in zip(items, pre, tinvs):
            eg = jnp.exp(gcb)
            v_hi, v_lo = _block_diag_pieces(v_ref[0, rows, cs] * bb_scr[rows, cs], bd_mask)
            k_hi, k_lo = _block_diag_pieces(kb * eg, bd_mask)
            sols.append(_dot3(tinv, jnp.concatenate([v_hi, k_hi], axis=1),
                              jnp.concatenate([v_lo, k_lo], axis=1)))
            glast = gcb[c - 1:c, :]
            qg_scr[rows, cs] = (qn * eg).astype(BF16)
            kd_scr[rows, cs] = (kn * jnp.exp(glast - gcb)).astype(BF16)
            egl_scr[pl.ds(ci, 1), cs] = jnp.exp(glast)
        for (ci, rows, cs), sol in zip(items, sols):
            u_scr[rows, cs] = sol[:, :SLAB_W]
            w_scr[rows, cs] = sol[:, SLAB_W:].astype(BF16)
        return carry

    lax.fori_loop(0, n_chunks // per_iter, factors, 0)

    gain = gain_ref[...]

    def advance(ci, rows):
        olds = [s_scr[s] for s in range(N_SLABS)]
        rs = [jnp.dot(jnp.concatenate([w_scr[rows, cs], qg_scr[rows, cs]], axis=0),
                      s_old.astype(BF16), preferred_element_type=F32)
              for cs, s_old in zip(slabs, olds)]
        v_news = [(u_scr[rows, cs] - r[:c]).astype(BF16) for cs, r in zip(slabs, rs)]
        upds = [lax.dot_general(kd_scr[rows, cs], v_new, TN, preferred_element_type=F32)
                for cs, v_new in zip(slabs, v_news)]
        os_ = [r[c:] + jnp.dot(a_scr[rows, cs], _block_diag(v_new, bd_mask),
                               preferred_element_type=F32)
               for cs, r, v_new in zip(slabs, rs, v_news)]
        for s, (cs, s_old, upd) in enumerate(zip(slabs, olds, upds)):
            s_scr[s] = (s_old * egl_scr[pl.ds(ci, 1), cs] + upd) * bd_f32
        return os_

    def emit(rows, os_):
        ssqs = []
        for o in os_:
            hi, lo = _split2(o * o)
            ssqs.append(jnp.dot(hi, bd_mask, preferred_element_type=F32)
                        + jnp.dot(lo, bd_mask, preferred_element_type=F32))
        for cs, o, ssq in zip(slabs, os_, ssqs):
            gt = gate_ref[0, rows, cs]
            o_ref[0, rows, cs] = (o * lax.rsqrt(ssq * (1.0 / d) + EPS) * gain[:, cs]
                                  * (gt * _sigmoid(gt)))

    def recur(it, carry):
        done = []
        for u in range(per_iter):
            ci = it * per_iter + u
            rows = pl.ds(pl.multiple_of(ci * c, c), c)
            done.append((rows, advance(ci, rows)))
        for rows, os_ in done:
            emit(rows, os_)
        return carry

    lax.fori_loop(0, n_chunks // per_iter, recur, 0)

    @pl.when(t == n_t - 1)
    def _():
        for s in range(N_SLABS):
            full = s_scr[s]
            for h in range(HEADS_PER_SLAB):
                sfin_ref[0, s * HEADS_PER_SLAB + h] = full[h * d:(h + 1) * d, h * d:(h + 1) * d]


def _gdn(qkv, gate, ab, a_log, dt_bias, out_gain, s0, l_valid):
    b, lp, _ = qkv.shape
    assert lp % DN_CHUNK == 0
    t_rows = DN_CHUNK * math.gcd(lp // DN_CHUNK, 8)
    n_t = lp // t_rows
    n_chunks = t_rows // DN_CHUNK
    pad = lambda a: jnp.pad(a.astype(F32), (0, LANES - N_HEADS)).reshape(1, LANES)
    blk = lambda j: pl.BlockSpec((1, t_rows, MIX_W), lambda bi, t, j=j: (bi, t, j))
    state = pl.BlockSpec((1, N_HEADS, HEAD_DIM, HEAD_DIM), lambda bi, t: (bi, 0, 0, 0))
    vec = pl.BlockSpec((1, LANES), lambda bi, t: (0, 0))
    wide_f32 = pltpu.VMEM((t_rows, MIX_W), F32)
    wide_bf16 = pltpu.VMEM((t_rows, MIX_W), BF16)
    est = 10 * t_rows * MIX_W * 4 + 5 * t_rows * MIX_W * 4 + 4 * t_rows * MIX_W * 2 + 64 * SLAB_W * SLAB_W * 4
    return pl.pallas_call(
        functools.partial(_gdn_kernel, t_rows=t_rows, n_t=n_t, l_valid=l_valid),
        grid=(b, n_t),
        in_specs=[
            blk(0), blk(1), blk(2),
            pl.BlockSpec((1, t_rows, MIX_W), lambda bi, t: (bi, t, 0)),
            pl.BlockSpec((1, t_rows, LANES), lambda bi, t: (bi, t, 0)),
            vec, vec,
            pl.BlockSpec((1, MIX_W), lambda bi, t: (0, 0)),
            state,
        ],
        out_specs=[pl.BlockSpec((1, t_rows, MIX_W), lambda bi, t: (bi, t, 0)), state],
        out_shape=[
            jax.ShapeDtypeStruct((b, lp, MIX_W), F32),
            jax.ShapeDtypeStruct((b, N_HEADS, HEAD_DIM, HEAD_DIM), F32),
        ],
        scratch_shapes=[
            pltpu.VMEM((N_SLABS, SLAB_W, SLAB_W), F32),
            wide_f32, wide_f32,
            wide_f32, wide_f32,
            wide_f32,
            wide_bf16, wide_bf16, wide_bf16, wide_bf16,
            pltpu.VMEM((max(n_chunks, SUBLANES), MIX_W), F32),
        ],
        compiler_params=_params(("parallel", "arbitrary"), est),
        name="gated_delta_rule",
    )(qkv, qkv, qkv, gate, ab, pad(a_log), pad(dt_bias),
      jnp.tile(out_gain.astype(F32), N_HEADS).reshape(1, MIX_W), s0)


def _mem_kv_kernel(x_ref, g_ref, w_ref, kt_ref, vt_ref, kv_scr):
    h = _rms(x_ref[...], g_ref[...]).astype(BF16)
    kv_scr[...] = jnp.dot(h, w_ref[...], preferred_element_type=F32)
    kt_ref[0] = kv_scr[:, :MEM_W].T
    vt_ref[0] = kv_scr[:, MEM_W:].T


def _mem_kv(mem, g, w):
    b, n_mem, d = mem.shape
    shp = jax.ShapeDtypeStruct((b, MEM_W, n_mem), F32)
    spec = pl.BlockSpec((1, MEM_W, n_mem), lambda i: (i, 0, 0))
    return pl.pallas_call(
        _mem_kv_kernel,
        grid=(b,),
        in_specs=[
            pl.BlockSpec((n_mem, d), lambda i: (i, 0)),
            pl.BlockSpec((1, d), lambda i: (0, 0)),
            pl.BlockSpec((d, 2 * MEM_W), lambda i: (0, 0)),
        ],
        out_specs=[spec, spec],
        out_shape=[shp, shp],
        scratch_shapes=[pltpu.VMEM((n_mem, 2 * MEM_W), F32)],
        compiler_params=_params(("parallel",), 2 * n_mem * d * 4 + 2 * d * 2 * MEM_W * 2 + 8 * n_mem * MEM_W * 4),
        name="mem_kv",
    )(mem.reshape(b * n_mem, d), g.reshape(1, d), w)


def _mem_attn_kernel(q_ref, kt_ref, vt_ref, o_ref):
    q = q_ref[0]
    kt = kt_ref[0]
    vt = vt_ref[0]
    outs = []
    for h in range(N_MEM_HEADS):
        sl = slice(h * HEAD_DIM, (h + 1) * HEAD_DIM)
        s = _bdot(q[:, sl], kt[sl, :]) * QK_SCALE
        p = jnp.exp(s - jnp.max(s, axis=-1, keepdims=True))
        outs.append(_bdot_nt(p, vt[sl, :]) / jnp.sum(p, axis=-1, keepdims=True))
    o_ref[0] = jnp.concatenate(outs, axis=1)


def _mem_attn(q, mk_t, mv_t):
    b, l, w = q.shape
    n_mem = mk_t.shape[2]
    tl = _row_tile(l, 1024)
    est = 4 * tl * w * 4 + 4 * n_mem * w * 4 + 8 * tl * n_mem * 4
    return pl.pallas_call(
        _mem_attn_kernel,
        grid=(b, l // tl),
        in_specs=[
            pl.BlockSpec((1, tl, w), lambda bi, i: (bi, i, 0)),
            pl.BlockSpec((1, w, n_mem), lambda bi, i: (bi, 0, 0)),
            pl.BlockSpec((1, w, n_mem), lambda bi, i: (bi, 0, 0)),
        ],
        out_specs=pl.BlockSpec((1, tl, w), lambda bi, i: (bi, i, 0)),
        out_shape=jax.ShapeDtypeStruct((b, l, w), F32),
        compiler_params=_params(("parallel", "parallel"), est),
        name="mem_attn",
    )(q, mk_t, mv_t)


def _out_proj_kernel(x_ref, o_ref, mo_ref, wo_ref, wm_ref, y_ref, *, grouped):
    y = x_ref[...] + _bdot(mo_ref[...], wm_ref[...])
    if grouped:
        for hk in range(N_KV_B):
            y = y + _bdot(o_ref[hk], wo_ref[hk])
    else:
        y = y + _bdot(o_ref[...], wo_ref[...])
    y_ref[...] = y


def _out_proj(x, o, mo, w_out, grouped):
    m, d = x.shape
    tm = _row_tile(m, 512)
    row = lambda i: (i, 0)
    if grouped:
        wo = w_out[:MIX_W].reshape(N_KV_B, GROUP_W, d)
        o_spec = pl.BlockSpec((N_KV_B, tm, GROUP_W), lambda i: (0, i, 0))
        wo_spec = pl.BlockSpec((N_KV_B, GROUP_W, d), lambda i: (0, 0, 0))
    else:
        wo = w_out[:MIX_W]
        o_spec = pl.BlockSpec((tm, MIX_W), row)
        wo_spec = pl.BlockSpec((MIX_W, d), lambda i: (0, 0))
    est = 6 * tm * d * 4 + 4 * tm * d * 4 + 4 * d * d * 2
    return pl.pallas_call(
        functools.partial(_out_proj_kernel, grouped=grouped),
        grid=(m // tm,),
        in_specs=[
            pl.BlockSpec((tm, d), row),
            o_spec,
            pl.BlockSpec((tm, MEM_W), row),
            wo_spec,
            pl.BlockSpec((MEM_W, d), lambda i: (0, 0)),
        ],
        out_specs=pl.BlockSpec((tm, d), row),
        out_shape=jax.ShapeDtypeStruct((m, d), F32),
        compiler_params=_params(("parallel",), est),
        name="out_proj",
    )(x, o, mo, wo, w_out[MIX_W:])


def _select_blocks(gate, n_past, k_sel):
    lane = lax.broadcasted_iota(jnp.int32, gate.shape, 1)
    lane_f = lane.astype(F32)
    gate = jnp.where(lane < n_past, gate, NEG_INF)
    sel = jnp.zeros(gate.shape, F32)
    for _ in range(k_sel):
        mx = jnp.max(gate, axis=1, keepdims=True)
        first = jnp.min(jnp.where(gate == mx, lane_f, float(gate.shape[1])), axis=1, keepdims=True)
        pick = (lane_f == first) & (mx > NEG_INF)
        sel = jnp.where(pick, 1.0, sel)
        gate = jnp.where(pick, NEG_INF, gate)
    return sel


def _select_blocks_t(gate, n_past, k_sel):
    blk = lax.broadcasted_iota(jnp.int32, gate.shape, 0)
    blk_f = blk.astype(F32)
    gate = jnp.where(blk < n_past, gate, NEG_INF)
    sel = jnp.zeros(gate.shape, F32)
    for _ in range(k_sel):
        mx = jnp.max(gate, axis=0, keepdims=True)
        first = jnp.min(jnp.where(gate == mx, blk_f, float(gate.shape[0])), axis=0, keepdims=True)
        pick = (blk_f == first) & (mx > NEG_INF)
        sel = jnp.where(pick, 1.0, sel)
        gate = jnp.where(pick, NEG_INF, gate)
    return sel


def _moba_prompt_kernel(q_ref, k_ref, vt_ref, kbar_ref, o_ref, q_scr, m_scr, acc_scr, sel_scr,
                        s_scr, p_scr, alpha_scr, *, k_sel):
    i = pl.program_id(2)
    blk = MOBA_BLOCK
    d = HEAD_DIM
    cw = LANES
    n_chunks = KV_GROUP * blk // cw
    n_blocks = sel_scr.shape[0]
    chunks = [slice(c * cw, (c + 1) * cw) for c in range(n_chunks)]
    q = q_ref[0]
    q3 = jnp.concatenate([q[:, g * d:(g + 1) * d] for g in range(KV_GROUP)], axis=0)
    q_scr[...] = jnp.concatenate([q3, jnp.zeros_like(q3)], axis=1).astype(BF16)

    def block_scores(k_blk, c):
        return lax.dot_general(k_blk, q_scr[c * cw:(c + 1) * cw, :], NT,
                               preferred_element_type=F32)

    k_own = k_ref[0, pl.ds(pl.multiple_of(i * blk, blk), blk), :]
    own_scores = [block_scores(k_own, c) for c in range(n_chunks)]
    k_first = k_ref[0, 0:blk, :]
    first_scores = [block_scores(k_first, c) for c in range(n_chunks)]
    gate_t = lax.dot_general(kbar_ref[0, 0], q3, NT, precision=HI, preferred_element_type=F32)
    for c, cs in enumerate(chunks):
        s_scr[0, :, cs] = first_scores[c]
    for c, cs in enumerate(chunks):
        kpos = lax.broadcasted_iota(jnp.int32, (blk, cw), 0)
        qpos = (lax.broadcasted_iota(jnp.int32, (blk, cw), 1) + c * cw) % blk
        s = jnp.where(kpos <= qpos, own_scores[c], NEG_INF)
        m0 = jnp.max(s, axis=0, keepdims=True)
        m_scr[:, cs] = m0
        p_scr[1, :, cs] = jnp.exp2(s - m0).astype(BF16)
    acc_scr[...] = jnp.zeros(acc_scr.shape, F32)
    alpha_scr[...] = jnp.ones(alpha_scr.shape, F32)
    sel_scr[...] = _select_blocks_t(gate_t, i, k_sel)

    def pv_update(j_prev, slot):
        vt_prev = vt_ref[j_prev, 0]
        pv = [jnp.dot(vt_prev, p_scr[slot, :, cs], preferred_element_type=F32) for cs in chunks]
        for cs, r in zip(chunks, pv):
            acc_scr[:, cs] = alpha_scr[:, cs] * acc_scr[:, cs] + r

    def step(j, cur):
        nxt = 1 - cur
        j_next = jnp.minimum(j + 1, n_blocks - 1)
        k_next = k_ref[0, pl.ds(pl.multiple_of(j_next * blk, blk), blk), :]
        for c, cs in enumerate(chunks):
            s_scr[nxt, :, cs] = block_scores(k_next, c)
        pv_update(jnp.where(j == 0, i, jnp.minimum(j - 1, n_blocks - 1)), nxt)
        chosen = (sel_scr[pl.ds(jnp.minimum(j, n_blocks - 1), 1), :] > 0.0) & (j < i)
        for c, cs in enumerate(chunks):
            s = s_scr[cur, :, cs]
            m_old = m_scr[:, cs]
            m_new = jnp.maximum(m_old, jnp.where(chosen[:, cs], jnp.max(s, axis=0, keepdims=True),
                                                 NEG_INF))
            p_scr[cur, :, cs] = jnp.exp2(s - jnp.where(chosen[:, cs], m_new, float("inf"))).astype(BF16)
            alpha_scr[:, cs] = jnp.exp2(m_old - m_new)
            m_scr[:, cs] = m_new

    long_trip, short_trip = 8, 2

    def trips(base, per_trip):
        def body(t, carry):
            for h in range(per_trip):
                step(base + per_trip * t + h, h % 2)
            return carry
        return body

    n_long = i // long_trip
    lax.fori_loop(0, n_long, trips(0, long_trip), 0)
    done = n_long * long_trip
    n_short = (i - done + short_trip - 1) // short_trip
    lax.fori_loop(0, n_short, trips(done, short_trip), 0)
    last = done + n_short * short_trip - 1
    pv_update(jnp.where(i == 0, i, jnp.minimum(last, n_blocks - 1)), 1)
    acc = acc_scr[...]
    o_t = (acc[:d] / acc[d:d + 1]).astype(BF16)
    eye = (lax.broadcasted_iota(jnp.int32, (blk, blk), 0)
           == lax.broadcasted_iota(jnp.int32, (blk, blk), 1)).astype(BF16)
    o_ref[0] = jnp.concatenate(
        [lax.dot_general(eye, o_t[:, g * blk:(g + 1) * blk], NT, preferred_element_type=F32)
         for g in range(KV_GROUP)], axis=1)


def _moba_prompt(qh, kh, vt, kbar, b, l):
    nb = l // MOBA_BLOCK
    rows = KV_GROUP * MOBA_BLOCK
    k_sel = min(MOBA_TOPK, nb)
    vt_rows = vt.shape[2]
    est = 4 * l * LANES * 2 + 4 * l * vt_rows * 2 + 8 * MOBA_BLOCK * GROUP_W * 4 + 24 * rows * LANES * 4
    return pl.pallas_call(
        functools.partial(_moba_prompt_kernel, k_sel=k_sel),
        grid=(N_KV_B, b, nb),
        in_specs=[
            pl.BlockSpec((1, MOBA_BLOCK, GROUP_W), lambda hk, bi, i: (hk, bi * nb + i, 0)),
            pl.BlockSpec((1, l, LANES), lambda hk, bi, i: (hk, bi, 0)),
            pl.BlockSpec((nb, 1, vt_rows, MOBA_BLOCK), lambda hk, bi, i: (bi, hk, 0, 0)),
            pl.BlockSpec((1, 1, nb, HEAD_DIM), lambda hk, bi, i: (hk, bi, 0, 0)),
        ],
        out_specs=pl.BlockSpec((1, MOBA_BLOCK, GROUP_W), lambda hk, bi, i: (hk, bi * nb + i, 0)),
        out_shape=jax.ShapeDtypeStruct((N_KV_B, b * l, GROUP_W), F32),
        scratch_shapes=[
            pltpu.VMEM((rows, LANES), BF16),
            pltpu.VMEM((1, rows), F32),
            pltpu.VMEM((vt_rows, rows), F32),
            pltpu.VMEM((nb, rows), F32),
            pltpu.VMEM((2, MOBA_BLOCK, rows), F32),
            pltpu.VMEM((2, MOBA_BLOCK, rows), BF16),
            pltpu.VMEM((1, rows), F32),
        ],
        compiler_params=_params(("parallel", "parallel", "arbitrary"), est),
        name="moba_prompt",
    )(qh, kh, vt, kbar)


def _moba_sample_kernel(pt_ref, q_ref, kn_ref, vn_ref, kt_hbm, vt_hbm, o_ref,
                        kbuf, vbuf, sem, s_scr, p_scr,
                        *, n_seq, n_pages, l_new, k_sel):
    b = pl.program_id(0)
    slot = lax.rem(b, 2)
    nb = n_pages // 2
    page = kbuf.shape[-1]
    unroll = math.gcd(nb, 4)

    def page_copy(hbm, buf, sl, which, pg, p):
        return pltpu.make_async_copy(hbm.at[pg], buf.at[sl, p], sem.at[sl, which])

    def fetch(seq, sl):
        def body(p, carry):
            pg = pt_ref[seq, p]
            page_copy(kt_hbm, kbuf, sl, 0, pg, p).start()
            page_copy(vt_hbm, vbuf, sl, 1, pg, p).start()
            return carry
        lax.fori_loop(0, n_pages, body, 0)

    def wait_pages(hbm, buf, which):
        def body(p, carry):
            page_copy(hbm, buf, slot, which, 0, p).wait()
            return carry
        lax.fori_loop(0, n_pages, body, 0)

    @pl.when(b == 0)
    def _():
        fetch(0, 0)

    @pl.when(b + 1 < n_seq)
    def _():
        fetch(b + 1, 1 - slot)

    q = q_ref[0]
    qb = q.astype(BF16)
    rows = q.shape[0]
    wait_pages(kt_hbm, kbuf, 0)

    blk_lane = lax.broadcasted_iota(jnp.int32, (q.shape[1], nb), 1)

    def mean_body(j, kbar_t):
        t = kbuf[slot, 2 * j] + kbuf[slot, 2 * j + 1]
        col = jnp.sum(t, axis=1, keepdims=True) * (1.0 / MOBA_BLOCK)
        return jnp.where(blk_lane == j, col, kbar_t)

    kbar_t = lax.fori_loop(0, nb, mean_body, jnp.zeros((q.shape[1], nb), F32), unroll=unroll)
    gate = jnp.dot(q, kbar_t, precision=HI, preferred_element_type=F32)
    sel = _select_blocks(gate, nb, k_sel)
    lane = lax.broadcasted_iota(jnp.int32, sel.shape, 1)

    def score_body(j, carry):
        chosen = jnp.max(jnp.where(lane == j, sel, 0.0), axis=1, keepdims=True)
        s = jnp.concatenate(
            [jnp.dot(qb, kbuf[slot, 2 * j + h].astype(BF16), preferred_element_type=F32)
             for h in range(2)], axis=1)
        s = jnp.where(chosen > 0.0, s, NEG_INF)
        s_scr[j] = s
        return jnp.maximum(carry, s)

    s_max = lax.fori_loop(0, nb, score_body, jnp.full((rows, 2 * page), NEG_INF, F32), unroll=unroll)
    s_new = _bdot_nt(qb, kn_ref[0])
    t_q = lax.broadcasted_iota(jnp.int32, s_new.shape, 0) % l_new
    t_k = lax.broadcasted_iota(jnp.int32, s_new.shape, 1)
    s_new = jnp.where(t_k <= t_q, s_new, NEG_INF)
    m = jnp.maximum(jnp.max(s_max, axis=1, keepdims=True), jnp.max(s_new, axis=1, keepdims=True))
    p_new = jnp.exp2(s_new - m)

    def prob_body(j, carry):
        p = jnp.exp2(s_scr[j] - m)
        p_scr[j] = p.astype(BF16)
        return carry + p

    p_sum = lax.fori_loop(0, nb, prob_body, jnp.zeros((rows, 2 * page), F32), unroll=unroll)
    denom = jnp.sum(p_sum, axis=1, keepdims=True) + jnp.sum(p_new, axis=1, keepdims=True)
    wait_pages(vt_hbm, vbuf, 1)

    def pv_body(j, acc):
        pj = p_scr[j]
        for h in range(2):
            acc = acc + lax.dot_general(pj[:, h * page:(h + 1) * page],
                                        vbuf[slot, 2 * j + h].astype(BF16), NT,
                                        preferred_element_type=F32)
        return acc

    acc = lax.fori_loop(0, nb, pv_body, _bdot(p_new, vn_ref[0]), unroll=unroll)
    o_ref[0] = acc / denom


def _moba_sample(q_bd, page_table, kt_pool, vt_pool, k_new, v_new, l_new):
    bs, rows, w = q_bd.shape
    n_pages = page_table.shape[1]
    page = kt_pool.shape[2]
    assert 2 * page == MOBA_BLOCK and kt_pool.shape[1] == w
    nb = n_pages // 2
    k_sel = min(MOBA_TOPK, nb + 1)
    lp = k_new.shape[1]
    per_b = lambda r: pl.BlockSpec((1, r, w), lambda b, pt: (b, 0, 0))
    pool_bytes = 2 * n_pages * w * page * 4
    est = 2 * pool_bytes + 8 * nb * rows * 2 * page * 4
    return pl.pallas_call(
        functools.partial(_moba_sample_kernel, n_seq=bs, n_pages=n_pages, l_new=l_new, k_sel=k_sel),
        grid_spec=pltpu.PrefetchScalarGridSpec(
            num_scalar_prefetch=1,
            grid=(bs,),
            in_specs=[per_b(rows), per_b(lp), per_b(lp),
                      pl.BlockSpec(memory_space=pl.ANY), pl.BlockSpec(memory_space=pl.ANY)],
            out_specs=per_b(rows),
            scratch_shapes=[
                pltpu.VMEM((2, n_pages, w, page), F32),
                pltpu.VMEM((2, n_pages, w, page), F32),
                pltpu.SemaphoreType.DMA((2, 2)),
                pltpu.VMEM((nb, rows, 2 * page), F32),
                pltpu.VMEM((nb, rows, 2 * page), BF16),
            ],
        ),
        out_shape=jax.ShapeDtypeStruct((bs, rows, w), F32),
        compiler_params=pltpu.CompilerParams(
            dimension_semantics=("arbitrary",),
            vmem_limit_bytes=min(int(est * 1.25), VMEM_BYTES_V7X - 8 * 2 ** 20)),
        name="moba_sample",
    )(page_table, q_bd, k_new, v_new, kt_pool, vt_pool)


def _pack_w_in_a(w):
    d = w.shape[0]
    qkv_gate = w[:, :4 * MIX_W]
    ab = w[:, 4 * MIX_W:4 * MIX_W + 2 * N_HEADS]
    q_mem = w[:, 4 * MIX_W + 2 * N_HEADS:]
    ab = jnp.pad(ab, ((0, 0), (0, LANES - 2 * N_HEADS)))
    return jnp.concatenate([qkv_gate, q_mem, ab], axis=1).astype(BF16)


def _pad_rows(a, rows):
    return jnp.pad(a, ((0, 0), (0, rows - a.shape[1]), (0, 0)))


def _trunk(p, x3, pos_base, mem_k, mem_v, conv0, s0, paged):
    b, l, d = x3.shape
    m = b * l
    x = x3.reshape(m, d)
    sample = paged is not None
    ffn = lambda x, tag, i, fg=None: _ffn(x, p[tag + "_norm"][i], p[tag + "_w1"], p[tag + "_w3"],
                                          p[tag + "_w2"], i, fg)
    lq = -(-l // SUBLANES) * SUBLANES

    def memory(q_mem, layer):
        q3 = _pad_rows(q_mem.reshape(b, l, MEM_W), lq)
        mo = _mem_attn(q3, mem_k[layer], mem_v[layer])
        return mo[:, :l].reshape(m, MEM_W)

    x = ffn(x, "ffn1", 0)
    if l % IN_PROJ_TILE == 0:
        qkv, gate, q_mem, ab, new_conv = _in_proj_conv(x, p["mix_norm"][0], p["w_in_a"], conv0,
                                                       p["dn_conv_w"], l)
        qkv = qkv.reshape(b, l, 3 * MIX_W)
    else:
        qkv, gate, q_mem, ab = _norm_proj(x, p["mix_norm"][0], p["w_in_a"],
                                          (3 * MIX_W, MIX_W, MEM_W, LANES), "in_proj_a")
        qkv, new_conv = _conv(qkv.reshape(b, l, 3 * MIX_W), conv0, p["dn_conv_w"])
    lp = -(-l // DN_CHUNK) * DN_CHUNK
    o, s_fin = _gdn(_pad_rows(qkv, lp), _pad_rows(gate.reshape(b, l, MIX_W), lp),
                    _pad_rows(ab.reshape(b, l, LANES), lp),
                    p["dn_a_log"], p["dn_dt_bias"], p["dn_out_norm"], s0, l)
    o = o[:, :l].reshape(m, MIX_W)
    x = _out_proj(x, o, memory(q_mem, 0), p["w_out"][0], grouped=False)
    x = ffn(x, "ffn2", 0)

    if sample:
        tables = _rope_tables(m, l, pos_base)
    else:
        tables = _rope_tables(l, l, pos_base)
    kv = _kv_proj(x, p["kv_norm"], p["w_kv"], tables, l, for_prompt=not sample)
    x = ffn(x, "ffn1", 1)
    qh, q_mem = _q_proj(x, p["mix_norm"][1], p["w_in_b"], tables)
    if sample:
        page_table, kt_pool, vt_pool = paged
        k_new, v_new = kv
        k_out = k_new.reshape(b, l, N_KV_B, HEAD_DIM)
        v_out = v_new.reshape(b, l, N_KV_B, HEAD_DIM)
        eye = jnp.eye(N_KV_B, dtype=F32)
        q5 = qh.reshape(N_KV_B, b, l, KV_GROUP, HEAD_DIM).transpose(1, 0, 3, 2, 4)
        q_bd = (q5[:, :, :, :, None, :] * eye[None, :, None, None, :, None]).reshape(
            b, N_HEADS * l, N_KV_B * HEAD_DIM)
        o_bd = _moba_sample(q_bd, page_table, kt_pool, vt_pool,
                            _pad_rows(k_new.reshape(b, l, -1), lq),
                            _pad_rows(v_new.reshape(b, l, -1), lq), l)
        o6 = o_bd.reshape(b, N_KV_B, KV_GROUP, l, N_KV_B, HEAD_DIM)
        o = (o6 * eye[None, :, None, None, :, None]).sum(axis=4)
        o = o.transpose(0, 3, 1, 2, 4).reshape(m, MIX_W)
        x = _out_proj(x, o, memory(q_mem, 1), p["w_out"][1], grouped=False)
    else:
        k_t, v_t, kh, vt, kbar = kv
        to_rows = lambda a: a.reshape(b, N_KV_B, HEAD_DIM, l).transpose(0, 3, 1, 2)
        k_out, v_out = to_rows(k_t), to_rows(v_t)
        nb = l // MOBA_BLOCK
        kbar = kbar.reshape(b, nb, N_KV_B, HEAD_DIM).transpose(2, 0, 1, 3)
        o4 = _moba_prompt(qh, kh, vt, kbar, b, l)
        x = _out_proj(x, o4, memory(q_mem, 1), p["w_out"][1], grouped=True)
    y = ffn(x, "ffn2", 1, p["final_norm"])
    return y.reshape(b, l, d), new_conv[None], s_fin[None], k_out, v_out


def kernel(x_prompt, x_sample, mem_prompt, cache_mem_k, cache_mem_v, state_dn_conv, state_dn_S,
           cache_kv_k, cache_kv_v, page_table,
           ffn1_norm, ffn1_w1, ffn1_w3, ffn1_w2, mix_norm, w_in_a, w_in_b, w_out,
           dn_conv_w, dn_a_log, dn_dt_bias, dn_out_norm, kv_norm, w_kv, mem_norm, w_mem_kv,
           ffn2_norm, ffn2_w1, ffn2_w3, ffn2_w2, final_norm):
    assert w_in_a.shape[0] == 1 and w_in_b.shape[0] == 1 and ffn1_w1.shape[0] == 2
    bf = lambda a: a.astype(BF16)
    p = dict(ffn1_norm=ffn1_norm, ffn1_w1=bf(ffn1_w1), ffn1_w3=bf(ffn1_w3), ffn1_w2=bf(ffn1_w2),
             ffn2_norm=ffn2_norm, ffn2_w1=bf(ffn2_w1), ffn2_w3=bf(ffn2_w3), ffn2_w2=bf(ffn2_w2),
             mix_norm=mix_norm, w_in_a=_pack_w_in_a(w_in_a[0]), w_in_b=bf(w_in_b[0]), w_out=bf(w_out),
             dn_conv_w=dn_conv_w[0], dn_a_log=dn_a_log[0], dn_dt_bias=dn_dt_bias[0],
             dn_out_norm=dn_out_norm[0], kv_norm=kv_norm, w_kv=bf(w_kv), final_norm=final_norm)

    bp, lp, _ = x_prompt.shape
    assert lp % MOBA_BLOCK == 0
    n_mem = mem_prompt.shape[1]
    n_layers = mem_norm.shape[0]
    mem_t = [_mem_kv(mem_prompt, mem_norm[layer], bf(w_mem_kv[layer])) for layer in range(n_layers)]
    mk = [t[0] for t in mem_t]
    mv = [t[1] for t in mem_t]
    conv0 = jnp.zeros((bp, CONV_W - 1, 3 * MIX_W), F32)
    s0 = jnp.zeros((bp, N_HEADS, HEAD_DIM, HEAD_DIM), F32)
    y_p, conv_p, s_p, k_p, v_p = _trunk(p, x_prompt, 0, mk, mv, conv0, s0, None)
    token_major = lambda ts: jnp.stack(ts).reshape(n_layers, bp, N_MEM_HEADS, HEAD_DIM, n_mem).transpose(
        0, 1, 4, 2, 3)
    mem_k_p = token_major(mk)
    mem_v_p = token_major(mv)

    bs = x_sample.shape[0]
    n_pool, page = cache_kv_k.shape[:2]
    past_len = page_table.shape[1] * page
    assert past_len % MOBA_BLOCK == 0 and x_sample.shape[1] <= MOBA_BLOCK
    token_minor = lambda c: jnp.transpose(c, (0, 1, 3, 4, 2)).reshape(c.shape[0], bs, MEM_W, n_mem)
    cmk = token_minor(cache_mem_k)
    cmv = token_minor(cache_mem_v)
    transposed_pages = lambda c: jnp.transpose(c, (0, 2, 3, 1)).reshape(n_pool, -1, page)
    paged = (page_table, transposed_pages(cache_kv_k), transposed_pages(cache_kv_v))
    y_s, conv_s, s_s, k_s, v_s = _trunk(p, x_sample, past_len, cmk, cmv, state_dn_conv[0],
                                        state_dn_S[0], paged)
    return (y_p, y_s, conv_p, s_p, k_p, v_p, mem_k_p, mem_v_p, conv_s, s_s, k_s, v_s)
```

```python
import functools
import math

import jax
import jax.numpy as jnp
from jax import lax
from jax.experimental import pallas as pl
from jax.experimental.pallas import tpu as pltpu

F32 = jnp.float32
BF16 = jnp.bfloat16
HI = lax.Precision.HIGHEST

HEAD_DIM = 64
N_HEADS = 12
MIX_W = N_HEADS * HEAD_DIM
N_MEM_HEADS = 4
MEM_W = N_MEM_HEADS * HEAD_DIM
N_KV_B = 4
KV_GROUP = N_HEADS // N_KV_B
GROUP_W = KV_GROUP * HEAD_DIM
CONV_W = 4
DN_CHUNK = 64
MOBA_BLOCK = 256
MOBA_TOPK = 3
ROT_DIM = HEAD_DIM // 4
ROPE_THETA = 500000.0
EPS = 1e-6
QK_SCALE = HEAD_DIM ** -0.5
LOG2E = math.log2(math.e)
ONES_ROWS = 16

LANES = 128
SUBLANES = 8
VMEM_BYTES_V7X = 64 * 2 ** 20
NEG_INF = float("-inf")

NT = (((1,), (1,)), ((), ()))
TN = (((0,), (0,)), ((), ()))


def _params(semantics, est_bytes):
    limit = min(max(int(est_bytes * 1.5), 16 * 2 ** 20), VMEM_BYTES_V7X - 8 * 2 ** 20)
    return pltpu.CompilerParams(dimension_semantics=semantics, vmem_limit_bytes=limit)


def _row_tile(m, pref):
    t = min(m, pref)
    while m % t or t % SUBLANES:
        t -= 1
    return t


def _rms(x, g):
    return x * lax.rsqrt(jnp.mean(x * x, axis=-1, keepdims=True) + EPS) * g


def _sigmoid(x):
    return 1.0 / (1.0 + jnp.exp(-x))


def _softplus(x):
    return jnp.maximum(x, 0.0) + jnp.log(1.0 + jnp.exp(-jnp.abs(x)))


def _bdot(a, b):
    return jnp.dot(a.astype(BF16), b.astype(BF16), preferred_element_type=F32)


def _bdot_nt(a, b):
    return lax.dot_general(a.astype(BF16), b.astype(BF16), NT, preferred_element_type=F32)


MXU_WIDTH = 2 * LANES


def _ffn_kernel(*refs, splits, final):
    if final:
        x_ref, g_ref, w1_ref, w3_ref, w2_ref, fg_ref, o_ref = refs
    else:
        x_ref, g_ref, w1_ref, w3_ref, w2_ref, o_ref = refs
    x = x_ref[...]
    h = _rms(x, g_ref[...]).astype(BF16)
    acc = None
    for lo, hi in splits:
        a = jnp.dot(h, w1_ref[:, lo:hi], preferred_element_type=F32)
        b = jnp.dot(h, w3_ref[:, lo:hi], preferred_element_type=F32)
        u = ((a * _sigmoid(a)) * b).astype(BF16)
        part = jnp.dot(u, w2_ref[lo:hi, :], preferred_element_type=F32)
        acc = part if acc is None else acc + part
    y = x + 0.5 * acc
    if final:
        y = _rms(y, fg_ref[...])
    o_ref[...] = y


def _ffn(x, g, w1, w3, w2, layer, final_gain=None):
    m, d = x.shape
    d_ff = w1.shape[2]
    tm = _row_tile(m, 512)
    tiles = d_ff // MXU_WIDTH
    cut = (tiles // 2) * MXU_WIDTH
    splits = ((0, cut), (cut, d_ff)) if cut else ((0, d_ff),)
    final = final_gain is not None
    resident = lambda shape: pl.BlockSpec((None,) + shape, lambda i: (layer, 0, 0),
                                          pipeline_mode=pl.Buffered(1))
    in_specs = [
        pl.BlockSpec((tm, d), lambda i: (i, 0)),
        pl.BlockSpec((1, d), lambda i: (0, 0)),
        resident((d, d_ff)), resident((d, d_ff)), resident((d_ff, d)),
    ]
    args = [x, g.reshape(1, d), w1, w3, w2]
    if final:
        in_specs.append(pl.BlockSpec((1, d), lambda i: (0, 0)))
        args.append(final_gain.reshape(1, d))
    widest = max(hi - lo for lo, hi in splits)
    est = 4 * tm * d * 4 + 3 * d * d_ff * 2 + 2 * tm * d * 4 + 3 * tm * widest * 4
    return pl.pallas_call(
        functools.partial(_ffn_kernel, splits=splits, final=final),
        grid=(m // tm,),
        in_specs=in_specs,
        out_specs=pl.BlockSpec((tm, d), lambda i: (i, 0)),
        out_shape=jax.ShapeDtypeStruct((m, d), F32),
        compiler_params=_params(("parallel",), est),
        name="ffn_final" if final else "ffn",
    )(*args)


def _norm_proj_kernel(x_ref, g_ref, w_ref, *out_refs, splits):
    h = _rms(x_ref[...], g_ref[...]).astype(BF16)
    for o_ref, (a, b) in zip(out_refs, splits):
        o_ref[...] = jnp.dot(h, w_ref[:, a:b], preferred_element_type=F32)


def _norm_proj(x, g, w, widths, name):
    m, d = x.shape
    n = w.shape[1]
    assert sum(widths) == n and all(c % LANES == 0 for c in widths)
    tm = _row_tile(m, 256)
    splits, a = [], 0
    for c in widths:
        splits.append((a, a + c))
        a += c
    est = 2 * tm * d * 4 + 2 * d * n * 2 + 3 * tm * n * 4
    return pl.pallas_call(
        functools.partial(_norm_proj_kernel, splits=tuple(splits)),
        grid=(m // tm,),
        in_specs=[
            pl.BlockSpec((tm, d), lambda i: (i, 0)),
            pl.BlockSpec((1, d), lambda i: (0, 0)),
            pl.BlockSpec((d, n), lambda i: (0, 0)),
        ],
        out_specs=[pl.BlockSpec((tm, c), lambda i: (i, 0)) for c in widths],
        out_shape=[jax.ShapeDtypeStruct((m, c), F32) for c in widths],
        compiler_params=_params(("parallel",), est),
        name=name,
    )(x, g.reshape(1, d), w)


def _rope_table_kernel(inv_ref, c_ref, s1_ref, s2_ref, *, period, base, tl):
    i = pl.program_id(0)
    row = lax.broadcasted_iota(jnp.int32, (tl, LANES), 0) + i * tl
    pos = base + lax.rem(row, period)
    ang = pos.astype(F32) * inv_ref[...]
    in_head = lax.broadcasted_iota(jnp.int32, (tl, LANES), 1) % HEAD_DIM
    half = ROT_DIM // 2
    c = jnp.cos(ang)
    s = jnp.sin(ang)
    c_ref[...] = jnp.where(in_head < ROT_DIM, c, 1.0)
    s1_ref[...] = jnp.where(in_head < half, -s, 0.0)
    s2_ref[...] = jnp.where((in_head >= half) & (in_head < ROT_DIM), s, 0.0)


def _rope_tables(n_rows, period, base):
    half = ROT_DIM // 2
    inv = ROPE_THETA ** (-jnp.arange(half, dtype=F32) * 2.0 / ROT_DIM)
    in_head = jnp.arange(LANES) % HEAD_DIM
    inv_lane = jnp.where(in_head < ROT_DIM, inv[in_head % half], 0.0).astype(F32).reshape(1, LANES)
    tl = _row_tile(n_rows, 1024)
    shp = jax.ShapeDtypeStruct((n_rows, LANES), F32)
    return pl.pallas_call(
        functools.partial(_rope_table_kernel, period=period, base=base, tl=tl),
        grid=(n_rows // tl,),
        in_specs=[pl.BlockSpec((1, LANES), lambda i: (0, 0))],
        out_specs=[pl.BlockSpec((tl, LANES), lambda i: (i, 0))] * 3,
        out_shape=[shp] * 3,
        compiler_params=_params(("parallel",), 16 * tl * LANES * 4),
        name="rope_tables",
    )(inv_lane)


def _apply_rope(t, c, s1, s2):
    w = t.shape[1]
    reps = w // LANES
    half = ROT_DIM // 2
    tile = lambda a: jnp.concatenate([a] * reps, axis=1) if reps > 1 else a
    up = pltpu.roll(t, w - half, 1)
    down = pltpu.roll(t, half, 1)
    return t * tile(c) + up * tile(s1) + down * tile(s2)


def _kv_proj_kernel(x_ref, g_ref, w_ref, c_ref, s1_ref, s2_ref, *out_refs, for_prompt):
    h = _rms(x_ref[...], g_ref[...]).astype(BF16)
    kv_w = N_KV_B * HEAD_DIM
    k = jnp.dot(h, w_ref[:, :kv_w], preferred_element_type=F32)
    v = jnp.dot(h, w_ref[:, kv_w:], preferred_element_type=F32)
    k = _apply_rope(k, c_ref[...], s1_ref[...], s2_ref[...])
    if for_prompt:
        kt_ref, vt_ref, kh_ref, vtb_ref, kbar_ref, v_scr = out_refs
        v_scr[...] = v
        v_t = v_scr[...].T
        kt_ref[0] = k.T
        vt_ref[0] = v_t
        for hk in range(N_KV_B):
            sl = slice(hk * HEAD_DIM, (hk + 1) * HEAD_DIM)
            k_h = k[:, sl]
            kh_ref[hk] = jnp.concatenate([k_h, jnp.zeros_like(k_h)], axis=1).astype(BF16)
            vtb_ref[0, hk, 0:HEAD_DIM, :] = v_t[sl, :].astype(BF16)
            vtb_ref[0, hk, HEAD_DIM:, :] = jnp.ones((ONES_ROWS, v_t.shape[1]), BF16)
        kbar_ref[0] = jnp.sum(k, axis=0, keepdims=True) * (1.0 / MOBA_BLOCK)
    else:
        out_refs[0][...] = k
        out_refs[1][...] = v


def _kv_proj(x, g, w, tables, seq_len, for_prompt):
    m, d = x.shape
    kv_w = N_KV_B * HEAD_DIM
    tm = MOBA_BLOCK if for_prompt else _row_tile(m, 256)
    n_tab = tables[0].shape[0] // tm
    row = lambda i: (i, 0)
    tab = lambda i: (i % n_tab, 0)
    if for_prompt:
        nb = seq_len // tm
        t_spec = pl.BlockSpec((1, kv_w, tm), lambda i: (i // nb, 0, i % nb))
        t_shape = jax.ShapeDtypeStruct((m // seq_len, kv_w, seq_len), F32)
        vt_rows = HEAD_DIM + ONES_ROWS
        out_specs = [t_spec, t_spec,
                     pl.BlockSpec((N_KV_B, tm, LANES), lambda i: (0, i, 0)),
                     pl.BlockSpec((1, N_KV_B, vt_rows, tm), lambda i: (i, 0, 0, 0)),
                     pl.BlockSpec((1, 1, kv_w), lambda i: (i, 0, 0))]
        out_shape = [t_shape, t_shape,
                     jax.ShapeDtypeStruct((N_KV_B, m, LANES), BF16),
                     jax.ShapeDtypeStruct((m // tm, N_KV_B, vt_rows, tm), BF16),
                     jax.ShapeDtypeStruct((m // tm, 1, kv_w), F32)]
    else:
        out_specs = [pl.BlockSpec((tm, kv_w), row), pl.BlockSpec((tm, kv_w), row)]
        out_shape = [jax.ShapeDtypeStruct((m, kv_w), F32)] * 2
    est = 2 * tm * d * 4 + 2 * d * 2 * kv_w * 2 + 12 * tm * kv_w * 4
    return pl.pallas_call(
        functools.partial(_kv_proj_kernel, for_prompt=for_prompt),
        grid=(m // tm,),
        in_specs=[
            pl.BlockSpec((tm, d), row),
            pl.BlockSpec((1, d), lambda i: (0, 0)),
            pl.BlockSpec((d, 2 * kv_w), lambda i: (0, 0)),
            pl.BlockSpec((tm, LANES), tab),
            pl.BlockSpec((tm, LANES), tab),
            pl.BlockSpec((tm, LANES), tab),
        ],
        out_specs=out_specs,
        out_shape=out_shape,
        scratch_shapes=[pltpu.VMEM((tm, kv_w), F32)] if for_prompt else [],
        compiler_params=_params(("parallel",), est),
        name="kv_proj",
    )(x, g.reshape(1, d), w, *tables)


def _q_proj_kernel(x_ref, g_ref, w_ref, c_ref, s1_ref, s2_ref, qh_ref, qm_ref):
    h = _rms(x_ref[...], g_ref[...]).astype(BF16)
    q = jnp.dot(h, w_ref[:, :MIX_W], preferred_element_type=F32)
    qm_ref[...] = jnp.dot(h, w_ref[:, MIX_W:], preferred_element_type=F32)
    q = _apply_rope(q, c_ref[...], s1_ref[...], s2_ref[...]) * (QK_SCALE * LOG2E)
    for hk in range(N_KV_B):
        qh_ref[hk] = q[:, hk * GROUP_W:(hk + 1) * GROUP_W]


def _q_proj(x, g, w, tables):
    m, d = x.shape
    n = w.shape[1]
    tm = _row_tile(m, 512)
    n_tab = tables[0].shape[0] // tm
    row = lambda i: (i, 0)
    tab = lambda i: (i % n_tab, 0)
    est = 2 * tm * d * 4 + 2 * d * n * 2 + 16 * tm * MIX_W * 4
    return pl.pallas_call(
        _q_proj_kernel,
        grid=(m // tm,),
        in_specs=[
            pl.BlockSpec((tm, d), row),
            pl.BlockSpec((1, d), lambda i: (0, 0)),
            pl.BlockSpec((d, n), lambda i: (0, 0)),
            pl.BlockSpec((tm, LANES), tab),
            pl.BlockSpec((tm, LANES), tab),
            pl.BlockSpec((tm, LANES), tab),
        ],
        out_specs=[
            pl.BlockSpec((N_KV_B, tm, GROUP_W), lambda i: (0, i, 0)),
            pl.BlockSpec((tm, MEM_W), row),
        ],
        out_shape=[
            jax.ShapeDtypeStruct((N_KV_B, m, GROUP_W), F32),
            jax.ShapeDtypeStruct((m, MEM_W), F32),
        ],
        compiler_params=_params(("parallel",), est),
        name="q_proj",
    )(x, g.reshape(1, d), w, *tables)


def _conv_kernel(u_ref, prev_ref, buf_ref, w_ref, y_ref, nb_ref, pad_scr, *, tl, n_t):
    i = pl.program_id(1)
    halo = CONV_W - 1
    lo = SUBLANES - halo
    pad_scr[SUBLANES:SUBLANES + tl, :] = u_ref[0]

    @pl.when(i == 0)
    def _():
        pad_scr[lo:SUBLANES, :] = buf_ref[0]

    @pl.when(i > 0)
    def _():
        p = prev_ref[0]
        pad_scr[lo:SUBLANES, :] = p[p.shape[0] - halo:, :]

    y = pad_scr[lo:lo + tl, :] * w_ref[0:1, :]
    for j in range(1, CONV_W):
        y = y + pad_scr[lo + j:lo + j + tl, :] * w_ref[j:j + 1, :]
    y_ref[0] = y * _sigmoid(y)

    @pl.when(i == n_t - 1)
    def _():
        nb_ref[0] = pad_scr[lo + tl:SUBLANES + tl, :]


def _conv(u, buf, w):
    b, l, c = u.shape
    tl = _row_tile(l, 512) if l % SUBLANES == 0 else l
    n_t = l // tl
    pr = min(SUBLANES, l)
    per = tl // pr
    est = 6 * tl * c * 4
    return pl.pallas_call(
        functools.partial(_conv_kernel, tl=tl, n_t=n_t),
        grid=(b, n_t),
        in_specs=[
            pl.BlockSpec((1, tl, c), lambda bi, i: (bi, i, 0)),
            pl.BlockSpec((1, pr, c), lambda bi, i: (bi, jnp.maximum(i * per - 1, 0), 0)),
            pl.BlockSpec((1, CONV_W - 1, c), lambda bi, i: (bi, 0, 0)),
            pl.BlockSpec((CONV_W, c), lambda bi, i: (0, 0)),
        ],
        out_specs=[
            pl.BlockSpec((1, tl, c), lambda bi, i: (bi, i, 0)),
            pl.BlockSpec((1, CONV_W - 1, c), lambda bi, i: (bi, 0, 0)),
        ],
        out_shape=[
            jax.ShapeDtypeStruct((b, l, c), F32),
            jax.ShapeDtypeStruct((b, CONV_W - 1, c), F32),
        ],
        scratch_shapes=[pltpu.VMEM((tl + SUBLANES, c), F32)],
        compiler_params=_params(("parallel", "arbitrary"), est),
        name="short_conv",
    )(u, u, buf, w)


IN_PROJ_TILE = 256


def _in_proj_conv_kernel(x_ref, g_ref, w_ref, buf_ref, cw_ref, qkv_ref, gate_ref, qm_ref, ab_ref,
                         nb_ref, prev_scr, *, tm, n_t):
    i = pl.program_id(1)
    halo = CONV_W - 1
    lo = SUBLANES - halo
    c3 = 3 * MIX_W
    h = _rms(x_ref[...], g_ref[...]).astype(BF16)

    @pl.when(i == 0)
    def _():
        prev_scr[...] = jnp.zeros(prev_scr.shape, F32)
        prev_scr[lo:SUBLANES, :] = buf_ref[0]

    u = jnp.dot(h, w_ref[:, :c3], preferred_element_type=F32)
    gate_ref[...] = jnp.dot(h, w_ref[:, c3:c3 + MIX_W], preferred_element_type=F32)
    qm_ref[...] = jnp.dot(h, w_ref[:, c3 + MIX_W:c3 + MIX_W + MEM_W], preferred_element_type=F32)
    ab_ref[...] = jnp.dot(h, w_ref[:, c3 + MIX_W + MEM_W:], preferred_element_type=F32)
    prev = prev_scr[...]
    row8 = lax.broadcasted_iota(jnp.int32, (SUBLANES, c3), 0)
    y = u * cw_ref[halo:CONV_W, :]
    y8 = u[0:SUBLANES] * cw_ref[halo:CONV_W, :]
    for s in range(1, CONV_W):
        tap = cw_ref[halo - s:CONV_W - s, :]
        shifted = pltpu.roll(u, s, 0)
        y = y + shifted * tap
        y8 = y8 + jnp.where(row8 < s, pltpu.roll(prev, s, 0), shifted[0:SUBLANES]) * tap
    qkv_ref[...] = y * _sigmoid(y)
    qkv_ref[0:SUBLANES, :] = y8 * _sigmoid(y8)
    last8 = u[tm - SUBLANES:, :]

    @pl.when(i == n_t - 1)
    def _():
        nb_ref[0] = last8[lo:, :]

    prev_scr[...] = last8


def _in_proj_conv(x, g, w, buf, conv_w, seq_len):
    m, d = x.shape
    n = w.shape[1]
    c3 = 3 * MIX_W
    tm = IN_PROJ_TILE
    n_t = seq_len // tm
    b = m // seq_len
    row = lambda bi, i: (bi * n_t + i, 0)
    const = lambda bi, i: (0, 0)
    widths = (c3, MIX_W, MEM_W, n - c3 - MIX_W - MEM_W)
    est = 2 * tm * d * 4 + 2 * d * n * 2 + 4 * tm * n * 4
    return pl.pallas_call(
        functools.partial(_in_proj_conv_kernel, tm=tm, n_t=n_t),
        grid=(b, n_t),
        in_specs=[
            pl.BlockSpec((tm, d), row),
            pl.BlockSpec((1, d), const),
            pl.BlockSpec((d, n), const),
            pl.BlockSpec((1, CONV_W - 1, c3), lambda bi, i: (bi, 0, 0)),
            pl.BlockSpec((CONV_W, c3), const),
        ],
        out_specs=[pl.BlockSpec((tm, c), row) for c in widths]
        + [pl.BlockSpec((1, CONV_W - 1, c3), lambda bi, i: (bi, 0, 0))],
        out_shape=[jax.ShapeDtypeStruct((m, c), F32) for c in widths]
        + [jax.ShapeDtypeStruct((b, CONV_W - 1, c3), F32)],
        scratch_shapes=[pltpu.VMEM((SUBLANES, c3), F32)],
        compiler_params=_params(("parallel", "arbitrary"), est),
        name="in_proj_conv",
    )(x, g.reshape(1, d), w, buf, conv_w)


HEADS_PER_SLAB = LANES * 2 // HEAD_DIM
SLAB_W = HEADS_PER_SLAB * HEAD_DIM
N_SLABS = N_HEADS // HEADS_PER_SLAB


def _split2(x):
    hi = x.astype(BF16)
    return hi, (x - hi.astype(F32)).astype(BF16)


def _split3(x):
    hi = x.astype(BF16)
    r = x - hi.astype(F32)
    mid = r.astype(BF16)
    return hi, mid, (r - mid.astype(F32)).astype(BF16)


def _dot_right01(x, sel):
    hi, mid, lo = _split3(x)
    d = lambda a: jnp.dot(a, sel, preferred_element_type=F32)
    return d(hi) + (d(mid) + d(lo))


def _dot_left01(sel, x):
    hi, mid, lo = _split3(x)
    d = lambda a: jnp.dot(sel, a, preferred_element_type=F32)
    return d(hi) + (d(mid) + d(lo))


def _dot3(a, b_hi, b_lo):
    a_hi, a_lo = _split2(a)
    d = lambda x, y: jnp.dot(x, y, preferred_element_type=F32)
    return d(a_hi, b_hi) + (d(a_hi, b_lo) + d(a_lo, b_hi))


def _block_diag(x, mask):
    return jnp.concatenate([x] * HEADS_PER_SLAB, axis=0) * mask


def _block_diag_pieces(x, mask):
    hi, lo = _split2(x)
    return _block_diag(hi, mask), _block_diag(lo, mask)


def _unit_lower_inverses(ms, eye_t, mask):
    c = ms[0].shape[0]
    xs = [eye_t - m for m in ms]
    ps = [_dot3(m, *_block_diag_pieces(m, mask)) for m in ms]
    power = 2
    while 2 * power < c:
        xps = [_dot3(jnp.concatenate([x, p], axis=0), *_block_diag_pieces(p, mask))
               for x, p in zip(xs, ps)]
        xs = [x + xp[:c] for x, xp in zip(xs, xps)]
        ps = [xp[c:] for xp in xps]
        power *= 2
    return [x + _dot3(x, *_block_diag_pieces(p, mask)) for x, p in zip(xs, ps)]


def _gdn_kernel(q_ref, k_ref, v_ref, gate_ref, ab_ref, alog_ref, dt_ref, gain_ref, s0_ref,
                o_ref, sfin_ref,
                s_scr, qn_scr, kn_scr, gcb_scr, bb_scr, u_scr, w_scr, a_scr, qg_scr, kd_scr, egl_scr,
                *, t_rows, n_t, l_valid):
    t = pl.program_id(1)
    c = DN_CHUNK
    d = HEAD_DIM
    n_chunks = t_rows // c

    r_s = lax.broadcasted_iota(jnp.int32, (SLAB_W, SLAB_W), 0)
    c_s = lax.broadcasted_iota(jnp.int32, (SLAB_W, SLAB_W), 1)
    same_head = (r_s // d) == (c_s // d)
    bd_f32 = jnp.where(same_head, 1.0, 0.0).astype(F32)
    bd_mask = bd_f32.astype(BF16)
    r_t = lax.broadcasted_iota(jnp.int32, (c, SLAB_W), 0)
    c_t = lax.broadcasted_iota(jnp.int32, (c, SLAB_W), 1) % d
    lower_t = r_t >= c_t
    strict_t = r_t > c_t
    eye_t = jnp.where(r_t == c_t, 1.0, 0.0).astype(F32)
    r_c = lax.broadcasted_iota(jnp.int32, (c, c), 0)
    c_c = lax.broadcasted_iota(jnp.int32, (c, c), 1)
    tril = jnp.where(r_c >= c_c, 1.0, 0.0).astype(BF16)
    er = lax.broadcasted_iota(jnp.int32, (LANES, MIX_W), 0)
    ec = lax.broadcasted_iota(jnp.int32, (LANES, MIX_W), 1) // d
    sel_g = jnp.where(er == ec, 1.0, 0.0).astype(BF16)
    sel_b = jnp.where(er == ec + N_HEADS, 1.0, 0.0).astype(BF16)

    @pl.when(t == 0)
    def _():
        for s in range(N_SLABS):
            rows = jnp.concatenate([s0_ref[0, s * HEADS_PER_SLAB + h] for h in range(HEADS_PER_SLAB)],
                                   axis=0)
            s_scr[s] = jnp.concatenate([rows] * HEADS_PER_SLAB, axis=1) * bd_f32

    ab = ab_ref[0]
    row = lax.broadcasted_iota(jnp.int32, ab.shape, 0) + t * t_rows
    valid = row < l_valid
    g = jnp.where(valid, -jnp.exp(alog_ref[...]) * _softplus(ab + dt_ref[...]), 0.0)
    beta = jnp.where(valid, _sigmoid(ab), 0.0)
    gc = jnp.concatenate([_dot_left01(tril, g[i * c:(i + 1) * c]) for i in range(n_chunks)], axis=0)
    gcb_scr[...] = _dot_right01(gc, sel_g)
    bb_scr[...] = _dot_right01(beta, sel_b)
    for s in range(N_SLABS):
        cs = slice(s * SLAB_W, (s + 1) * SLAB_W)
        for src, dst, scale in ((q_ref, qn_scr, QK_SCALE), (k_ref, kn_scr, 1.0)):
            x = src[0, :, cs]
            hi, lo = _split2(x * x)
            ssq = (jnp.dot(hi, bd_mask, preferred_element_type=F32)
                   + jnp.dot(lo, bd_mask, preferred_element_type=F32))
            dst[:, cs] = x * (lax.rsqrt(ssq + EPS) * scale)

    per_iter = math.gcd(n_chunks, 2)
    slabs = [slice(s * SLAB_W, (s + 1) * SLAB_W) for s in range(N_SLABS)]

    def factors(it, carry):
        items = []
        for u in range(per_iter):
            ci = it * per_iter + u
            rows = pl.ds(pl.multiple_of(ci * c, c), c)
            items += [(ci, rows, cs) for cs in slabs]
        pre = []
        for ci, rows, cs in items:
            gcb = gcb_scr[rows, cs]
            kn = kn_scr[rows, cs]
            qn = qn_scr[rows, cs]
            b_ = bb_scr[rows, cs]
            kb = kn * b_
            g_row = jnp.sum(gcb * eye_t, axis=0, keepdims=True)
            dec = jnp.exp(jnp.where(lower_t, gcb - g_row, NEG_INF))
            both = lax.dot_general(jnp.concatenate([kb, qn], axis=0).astype(BF16),
                                   _block_diag(kn.astype(BF16), bd_mask), NT,
                                   preferred_element_type=F32)
            pre.append((gcb, kn, qn, kb, dec, both))
        ms = []
        for (ci, rows, cs), (gcb, kn, qn, kb, dec, both) in zip(items, pre):
            ms.append(jnp.where(strict_t, both[:c] * dec, 0.0))
            a_scr[rows, cs] = (both[c:] * dec).astype(BF16)
        tinvs = _unit_lower_inverses(ms, eye_t, bd_mask)
        sols = []
        for (ci, rows, cs), (gcb, kn, qn, kb, dec, both), tinv in zip(items, pre, tinvs):
            eg = jnp.exp(gcb)
            v_hi, v_lo = _block_diag_pieces(v_ref[0, rows, cs] * bb_scr[rows, cs], bd_mask)
            k_hi, k_lo = _block_diag_pieces(kb * eg, bd_mask)
            sols.append(_dot3(tinv, jnp.concatenate([v_hi, k_hi], axis=1),
                              jnp.concatenate([v_lo, k_lo], axis=1)))
            glast = gcb[c - 1:c, :]
            qg_scr[rows, cs] = (qn * eg).astype(BF16)
            kd_scr[rows, cs] = (kn * jnp.exp(glast - gcb)).astype(BF16)
            egl_scr[pl.ds(ci, 1), cs] = jnp.exp(glast)
        for (ci, rows, cs), sol in zip(items, sols):
            u_scr[rows, cs] = sol[:, :SLAB_W]
            w_scr[rows, cs] = sol[:, SLAB_W:].astype(BF16)
        return carry

    lax.fori_loop(0, n_chunks // per_iter, factors, 0)

    gain = gain_ref[...]

    def advance(ci, rows):
        olds = [s_scr[s] for s in range(N_SLABS)]
        rs = [jnp.dot(jnp.concatenate([w_scr[rows, cs], qg_scr[rows, cs]], axis=0),
                      s_old.astype(BF16), preferred_element_type=F32)
              for cs, s_old in zip(slabs, olds)]
        v_news = [(u_scr[rows, cs] - r[:c]).astype(BF16) for cs, r in zip(slabs, rs)]
        upds = [lax.dot_general(kd_scr[rows, cs], v_new, TN, preferred_element_type=F32)
                for cs, v_new in zip(slabs, v_news)]
        os_ = [r[c:] + jnp.dot(a_scr[rows, cs], _block_diag(v_new, bd_mask),
                               preferred_element_type=F32)
               for cs, r, v_new in zip(slabs, rs, v_news)]
        for s, (cs, s_old, upd) in enumerate(zip(slabs, olds, upds)):
            s_scr[s] = (s_old * egl_scr[pl.ds(ci, 1), cs] + upd) * bd_f32
        return os_

    def emit(rows, os_):
        ssqs = []
        for o in os_:
            hi, lo = _split2(o * o)
            ssqs.append(jnp.dot(hi, bd_mask, preferred_element_type=F32)
                        + jnp.dot(lo, bd_mask, preferred_element_type=F32))
        for cs, o, ssq in zip(slabs, os_, ssqs):
            gt = gate_ref[0, rows, cs]
            o_ref[0, rows, cs] = (o * lax.rsqrt(ssq * (1.0 / d) + EPS) * gain[:, cs]
                                  * (gt * _sigmoid(gt)))

    def recur(it, carry):
        done = []
        for u in range(per_iter):
            ci = it * per_iter + u
            rows = pl.ds(pl.multiple_of(ci * c, c), c)
            done.append((rows, advance(ci, rows)))
        for rows, os_ in done:
            emit(rows, os_)
        return carry

    lax.fori_loop(0, n_chunks // per_iter, recur, 0)

    @pl.when(t == n_t - 1)
    def _():
        for s in range(N_SLABS):
            full = s_scr[s]
            for h in range(HEADS_PER_SLAB):
                sfin_ref[0, s * HEADS_PER_SLAB + h] = full[h * d:(h + 1) * d, h * d:(h + 1) * d]


def _gdn(qkv, gate, ab, a_log, dt_bias, out_gain, s0, l_valid):
    b, lp, _ = qkv.shape
    assert lp % DN_CHUNK == 0
    t_rows = DN_CHUNK * math.gcd(lp // DN_CHUNK, 8)
    n_t = lp // t_rows
    n_chunks = t_rows // DN_CHUNK
    pad = lambda a: jnp.pad(a.astype(F32), (0, LANES - N_HEADS)).reshape(1, LANES)
    blk = lambda j: pl.BlockSpec((1, t_rows, MIX_W), lambda bi, t, j=j: (bi, t, j))
    state = pl.BlockSpec((1, N_HEADS, HEAD_DIM, HEAD_DIM), lambda bi, t: (bi, 0, 0, 0))
    vec = pl.BlockSpec((1, LANES), lambda bi, t: (0, 0))
    wide_f32 = pltpu.VMEM((t_rows, MIX_W), F32)
    wide_bf16 = pltpu.VMEM((t_rows, MIX_W), BF16)
    est = 10 * t_rows * MIX_W * 4 + 5 * t_rows * MIX_W * 4 + 4 * t_rows * MIX_W * 2 + 64 * SLAB_W * SLAB_W * 4
    return pl.pallas_call(
        functools.partial(_gdn_kernel, t_rows=t_rows, n_t=n_t, l_valid=l_valid),
        grid=(b, n_t),
        in_specs=[
            blk(0), blk(1), blk(2),
            pl.BlockSpec((1, t_rows, MIX_W), lambda bi, t: (bi, t, 0)),
            pl.BlockSpec((1, t_rows, LANES), lambda bi, t: (bi, t, 0)),
            vec, vec,
            pl.BlockSpec((1, MIX_W), lambda bi, t: (0, 0)),
            state,
        ],
        out_specs=[pl.BlockSpec((1, t_rows, MIX_W), lambda bi, t: (bi, t, 0)), state],
        out_shape=[
            jax.ShapeDtypeStruct((b, lp, MIX_W), F32),
            jax.ShapeDtypeStruct((b, N_HEADS, HEAD_DIM, HEAD_DIM), F32),
        ],
        scratch_shapes=[
            pltpu.VMEM((N_SLABS, SLAB_W, SLAB_W), F32),
            wide_f32, wide_f32,
            wide_f32, wide_f32,
            wide_f32,
            wide_bf16, wide_bf16, wide_bf16, wide_bf16,
            pltpu.VMEM((max(n_chunks, SUBLANES), MIX_W), F32),
        ],
        compiler_params=_params(("parallel", "arbitrary"), est),
        name="gated_delta_rule",
    )(qkv, qkv, qkv, gate, ab, pad(a_log), pad(dt_bias),
      jnp.tile(out_gain.astype(F32), N_HEADS).reshape(1, MIX_W), s0)


def _mem_kv_kernel(x_ref, g_ref, w_ref, kt_ref, vt_ref, kv_scr):
    h = _rms(x_ref[...], g_ref[...]).astype(BF16)
    kv_scr[...] = jnp.dot(h, w_ref[...], preferred_element_type=F32)
    kt_ref[0] = kv_scr[:, :MEM_W].T
    vt_ref[0] = kv_scr[:, MEM_W:].T


def _mem_kv(mem, g, w):
    b, n_mem, d = mem.shape
    shp = jax.ShapeDtypeStruct((b, MEM_W, n_mem), F32)
    spec = pl.BlockSpec((1, MEM_W, n_mem), lambda i: (i, 0, 0))
    return pl.pallas_call(
        _mem_kv_kernel,
        grid=(b,),
        in_specs=[
            pl.BlockSpec((n_mem, d), lambda i: (i, 0)),
            pl.BlockSpec((1, d), lambda i: (0, 0)),
            pl.BlockSpec((d, 2 * MEM_W), lambda i: (0, 0)),
        ],
        out_specs=[spec, spec],
        out_shape=[shp, shp],
        scratch_shapes=[pltpu.VMEM((n_mem, 2 * MEM_W), F32)],
        compiler_params=_params(("parallel",), 2 * n_mem * d * 4 + 2 * d * 2 * MEM_W * 2 + 8 * n_mem * MEM_W * 4),
        name="mem_kv",
    )(mem.reshape(b * n_mem, d), g.reshape(1, d), w)


def _mem_attn_kernel(q_ref, kt_ref, vt_ref, o_ref):
    q = q_ref[0]
    kt = kt_ref[0]
    vt = vt_ref[0]
    outs = []
    for h in range(N_MEM_HEADS):
        sl = slice(h * HEAD_DIM, (h + 1) * HEAD_DIM)
        s = _bdot(q[:, sl], kt[sl, :]) * QK_SCALE
        p = jnp.exp(s - jnp.max(s, axis=-1, keepdims=True))
        outs.append(_bdot_nt(p, vt[sl, :]) / jnp.sum(p, axis=-1, keepdims=True))
    o_ref[0] = jnp.concatenate(outs, axis=1)


def _mem_attn(q, mk_t, mv_t):
    b, l, w = q.shape
    n_mem = mk_t.shape[2]
    tl = _row_tile(l, 2048)
    est = 4 * tl * w * 4 + 4 * n_mem * w * 4 + 8 * tl * n_mem * 4
    return pl.pallas_call(
        _mem_attn_kernel,
        grid=(b, l // tl),
        in_specs=[
            pl.BlockSpec((1, tl, w), lambda bi, i: (bi, i, 0)),
            pl.BlockSpec((1, w, n_mem), lambda bi, i: (bi, 0, 0)),
            pl.BlockSpec((1, w, n_mem), lambda bi, i: (bi, 0, 0)),
        ],
        out_specs=pl.BlockSpec((1, tl, w), lambda bi, i: (bi, i, 0)),
        out_shape=jax.ShapeDtypeStruct((b, l, w), F32),
        compiler_params=_params(("parallel", "parallel"), est),
        name="mem_attn",
    )(q, mk_t, mv_t)


def _out_proj_kernel(x_ref, o_ref, mo_ref, wo_ref, wm_ref, y_ref, *, grouped):
    y = x_ref[...] + _bdot(mo_ref[...], wm_ref[...])
    if grouped:
        for hk in range(N_KV_B):
            y = y + _bdot(o_ref[hk], wo_ref[hk])
    else:
        y = y + _bdot(o_ref[...], wo_ref[...])
    y_ref[...] = y


def _out_proj(x, o, mo, w_out, grouped):
    m, d = x.shape
    tm = _row_tile(m, 1024)
    row = lambda i: (i, 0)
    if grouped:
        wo = w_out[:MIX_W].reshape(N_KV_B, GROUP_W, d)
        o_spec = pl.BlockSpec((N_KV_B, tm, GROUP_W), lambda i: (0, i, 0))
        wo_spec = pl.BlockSpec((N_KV_B, GROUP_W, d), lambda i: (0, 0, 0))
    else:
        wo = w_out[:MIX_W]
        o_spec = pl.BlockSpec((tm, MIX_W), row)
        wo_spec = pl.BlockSpec((MIX_W, d), lambda i: (0, 0))
    est = 6 * tm * d * 4 + 4 * tm * d * 4 + 4 * d * d * 2
    return pl.pallas_call(
        functools.partial(_out_proj_kernel, grouped=grouped),
        grid=(m // tm,),
        in_specs=[
            pl.BlockSpec((tm, d), row),
            o_spec,
            pl.BlockSpec((tm, MEM_W), row),
            wo_spec,
            pl.BlockSpec((MEM_W, d), lambda i: (0, 0)),
        ],
        out_specs=pl.BlockSpec((tm, d), row),
        out_shape=jax.ShapeDtypeStruct((m, d), F32),
        compiler_params=_params(("parallel",), est),
        name="out_proj",
    )(x, o, mo, wo, w_out[MIX_W:])


def _select_blocks(gate, n_past, k_sel):
    lane = lax.broadcasted_iota(jnp.int32, gate.shape, 1)
    lane_f = lane.astype(F32)
    gate = jnp.where(lane < n_past, gate, NEG_INF)
    sel = jnp.zeros(gate.shape, F32)
    for _ in range(k_sel):
        mx = jnp.max(gate, axis=1, keepdims=True)
        first = jnp.min(jnp.where(gate == mx, lane_f, float(gate.shape[1])), axis=1, keepdims=True)
        pick = (lane_f == first) & (mx > NEG_INF)
        sel = jnp.where(pick, 1.0, sel)
        gate = jnp.where(pick, NEG_INF, gate)
    return sel


def _select_blocks_t(gate, n_past, k_sel):
    blk = lax.broadcasted_iota(jnp.int32, gate.shape, 0)
    blk_f = blk.astype(F32)
    gate = jnp.where(blk < n_past, gate, NEG_INF)
    sel = jnp.zeros(gate.shape, F32)
    for _ in range(k_sel):
        mx = jnp.max(gate, axis=0, keepdims=True)
        first = jnp.min(jnp.where(gate == mx, blk_f, float(gate.shape[0])), axis=0, keepdims=True)
        pick = (blk_f == first) & (mx > NEG_INF)
        sel = jnp.where(pick, 1.0, sel)
        gate = jnp.where(pick, NEG_INF, gate)
    return sel


def _moba_prompt_kernel(q_ref, k_ref, vt_ref, kbar_ref, o_ref, q_scr, m_scr, acc_scr, sel_scr,
                        s_scr, p_scr, alpha_scr, *, k_sel):
    i = pl.program_id(2)
    blk = MOBA_BLOCK
    d = HEAD_DIM
    cw = LANES
    n_chunks = KV_GROUP * blk // cw
    n_blocks = sel_scr.shape[0]
    chunks = [slice(c * cw, (c + 1) * cw) for c in range(n_chunks)]
    q = q_ref[0]
    q3 = jnp.concatenate([q[:, g * d:(g + 1) * d] for g in range(KV_GROUP)], axis=0)
    q_scr[...] = jnp.concatenate([q3, jnp.zeros_like(q3)], axis=1).astype(BF16)

    def block_scores(k_blk, c):
        return lax.dot_general(k_blk, q_scr[c * cw:(c + 1) * cw, :], NT,
                               preferred_element_type=F32)

    k_own = k_ref[0, pl.ds(pl.multiple_of(i * blk, blk), blk), :]
    own_scores = [block_scores(k_own, c) for c in range(n_chunks)]
    k_first = k_ref[0, 0:blk, :]
    first_scores = [block_scores(k_first, c) for c in range(n_chunks)]
    gate_t = lax.dot_general(kbar_ref[0, 0], q3, NT, precision=HI, preferred_element_type=F32)
    for c, cs in enumerate(chunks):
        s_scr[0, :, cs] = first_scores[c]
    for c, cs in enumerate(chunks):
        kpos = lax.broadcasted_iota(jnp.int32, (blk, cw), 0)
        qpos = (lax.broadcasted_iota(jnp.int32, (blk, cw), 1) + c * cw) % blk
        s = jnp.where(kpos <= qpos, own_scores[c], NEG_INF)
        m0 = jnp.max(s, axis=0, keepdims=True)
        m_scr[:, cs] = m0
        p_scr[1, :, cs] = jnp.exp2(s - m0).astype(BF16)
    acc_scr[...] = jnp.zeros(acc_scr.shape, F32)
    alpha_scr[...] = jnp.ones(alpha_scr.shape, F32)
    sel_scr[...] = _select_blocks_t(gate_t, i, k_sel)

    def pv_update(j_prev, slot):
        vt_prev = vt_ref[j_prev, 0]
        pv = [jnp.dot(vt_prev, p_scr[slot, :, cs], preferred_element_type=F32) for cs in chunks]
        for cs, r in zip(chunks, pv):
            acc_scr[:, cs] = alpha_scr[:, cs] * acc_scr[:, cs] + r

    def step(j, cur):
        nxt = 1 - cur
        j_next = jnp.minimum(j + 1, n_blocks - 1)
        k_next = k_ref[0, pl.ds(pl.multiple_of(j_next * blk, blk), blk), :]
        for c, cs in enumerate(chunks):
            s_scr[nxt, :, cs] = block_scores(k_next, c)
        pv_update(jnp.where(j == 0, i, jnp.minimum(j - 1, n_blocks - 1)), nxt)
        chosen = (sel_scr[pl.ds(jnp.minimum(j, n_blocks - 1), 1), :] > 0.0) & (j < i)
        for c, cs in enumerate(chunks):
            s = s_scr[cur, :, cs]
            m_old = m_scr[:, cs]
            m_new = jnp.maximum(m_old, jnp.where(chosen[:, cs], jnp.max(s, axis=0, keepdims=True),
                                                 NEG_INF))
            p_scr[cur, :, cs] = jnp.exp2(s - jnp.where(chosen[:, cs], m_new, float("inf"))).astype(BF16)
            alpha_scr[:, cs] = jnp.exp2(m_old - m_new)
            m_scr[:, cs] = m_new

    long_trip, short_trip = 8, 2

    def trips(base, per_trip):
        def body(t, carry):
            for h in range(per_trip):
                step(base + per_trip * t + h, h % 2)
            return carry
        return body

    n_long = i // long_trip
    lax.fori_loop(0, n_long, trips(0, long_trip), 0)
    done = n_long * long_trip
    n_short = (i - done + short_trip - 1) // short_trip
    lax.fori_loop(0, n_short, trips(done, short_trip), 0)
    last = done + n_short * short_trip - 1
    pv_update(jnp.where(i == 0, i, jnp.minimum(last, n_blocks - 1)), 1)
    acc = acc_scr[...]
    o_t = (acc[:d] / acc[d:d + 1]).astype(BF16)
    eye = (lax.broadcasted_iota(jnp.int32, (blk, blk), 0)
           == lax.broadcasted_iota(jnp.int32, (blk, blk), 1)).astype(BF16)
    o_ref[0] = jnp.concatenate(
        [lax.dot_general(eye, o_t[:, g * blk:(g + 1) * blk], NT, preferred_element_type=F32)
         for g in range(KV_GROUP)], axis=1)


def _moba_prompt(qh, kh, vt, kbar, b, l):
    nb = l // MOBA_BLOCK
    rows = KV_GROUP * MOBA_BLOCK
    k_sel = min(MOBA_TOPK, nb)
    vt_rows = vt.shape[2]
    est = 4 * l * LANES * 2 + 4 * l * vt_rows * 2 + 8 * MOBA_BLOCK * GROUP_W * 4 + 24 * rows * LANES * 4
    return pl.pallas_call(
        functools.partial(_moba_prompt_kernel, k_sel=k_sel),
        grid=(N_KV_B, b, nb),
        in_specs=[
            pl.BlockSpec((1, MOBA_BLOCK, GROUP_W), lambda hk, bi, i: (hk, bi * nb + i, 0)),
            pl.BlockSpec((1, l, LANES), lambda hk, bi, i: (hk, bi, 0)),
            pl.BlockSpec((nb, 1, vt_rows, MOBA_BLOCK), lambda hk, bi, i: (bi, hk, 0, 0)),
            pl.BlockSpec((1, 1, nb, HEAD_DIM), lambda hk, bi, i: (hk, bi, 0, 0)),
        ],
        out_specs=pl.BlockSpec((1, MOBA_BLOCK, GROUP_W), lambda hk, bi, i: (hk, bi * nb + i, 0)),
        out_shape=jax.ShapeDtypeStruct((N_KV_B, b * l, GROUP_W), F32),
        scratch_shapes=[
            pltpu.VMEM((rows, LANES), BF16),
            pltpu.VMEM((1, rows), F32),
            pltpu.VMEM((vt_rows, rows), F32),
            pltpu.VMEM((nb, rows), F32),
            pltpu.VMEM((2, MOBA_BLOCK, rows), F32),
            pltpu.VMEM((2, MOBA_BLOCK, rows), BF16),
            pltpu.VMEM((1, rows), F32),
        ],
        compiler_params=_params(("parallel", "parallel", "arbitrary"), est),
        name="moba_prompt",
    )(qh, kh, vt, kbar)


def _moba_sample_kernel(pt_ref, q_ref, kn_ref, vn_ref, kt_hbm, vt_hbm, o_ref,
                        kbuf, vbuf, sem, s_scr, p_scr,
                        *, n_seq, n_pages, l_new, k_sel):
    b = pl.program_id(0)
    slot = lax.rem(b, 2)
    nb = n_pages // 2
    page = kbuf.shape[-1]
    unroll = math.gcd(nb, 4)

    def page_copy(hbm, buf, sl, which, pg, p):
        return pltpu.make_async_copy(hbm.at[pg], buf.at[sl, p], sem.at[sl, which])

    def fetch(seq, sl):
        def body(p, carry):
            pg = pt_ref[seq, p]
            page_copy(kt_hbm, kbuf, sl, 0, pg, p).start()
            page_copy(vt_hbm, vbuf, sl, 1, pg, p).start()
            return carry
        lax.fori_loop(0, n_pages, body, 0)

    def wait_pages(hbm, buf, which):
        def body(p, carry):
            page_copy(hbm, buf, slot, which, 0, p).wait()
            return carry
        lax.fori_loop(0, n_pages, body, 0)

    @pl.when(b == 0)
    def _():
        fetch(0, 0)

    @pl.when(b + 1 < n_seq)
    def _():
        fetch(b + 1, 1 - slot)

    q = q_ref[0]
    qb = q.astype(BF16)
    rows = q.shape[0]
    wait_pages(kt_hbm, kbuf, 0)

    blk_lane = lax.broadcasted_iota(jnp.int32, (q.shape[1], nb), 1)

    def mean_body(j, kbar_t):
        t = kbuf[slot, 2 * j] + kbuf[slot, 2 * j + 1]
        col = jnp.sum(t, axis=1, keepdims=True) * (1.0 / MOBA_BLOCK)
        return jnp.where(blk_lane == j, col, kbar_t)

    kbar_t = lax.fori_loop(0, nb, mean_body, jnp.zeros((q.shape[1], nb), F32), unroll=unroll)
    gate = jnp.dot(q, kbar_t, precision=HI, preferred_element_type=F32)
    sel = _select_blocks(gate, nb, k_sel)
    lane = lax.broadcasted_iota(jnp.int32, sel.shape, 1)

    def score_body(j, carry):
        chosen = jnp.max(jnp.where(lane == j, sel, 0.0), axis=1, keepdims=True)
        s = jnp.concatenate(
            [jnp.dot(qb, kbuf[slot, 2 * j + h].astype(BF16), preferred_element_type=F32)
             for h in range(2)], axis=1)
        s = jnp.where(chosen > 0.0, s, NEG_INF)
        s_scr[j] = s
        return jnp.maximum(carry, s)

    s_max = lax.fori_loop(0, nb, score_body, jnp.full((rows, 2 * page), NEG_INF, F32), unroll=unroll)
    s_new = _bdot_nt(qb, kn_ref[0])
    t_q = lax.broadcasted_iota(jnp.int32, s_new.shape, 0) % l_new
    t_k = lax.broadcasted_iota(jnp.int32, s_new.shape, 1)
    s_new = jnp.where(t_k <= t_q, s_new, NEG_INF)
    m = jnp.maximum(jnp.max(s_max, axis=1, keepdims=True), jnp.max(s_new, axis=1, keepdims=True))
    p_new = jnp.exp2(s_new - m)

    def prob_body(j, carry):
        p = jnp.exp2(s_scr[j] - m)
        p_scr[j] = p.astype(BF16)
        return carry + p

    p_sum = lax.fori_loop(0, nb, prob_body, jnp.zeros((rows, 2 * page), F32), unroll=unroll)
    denom = jnp.sum(p_sum, axis=1, keepdims=True) + jnp.sum(p_new, axis=1, keepdims=True)
    wait_pages(vt_hbm, vbuf, 1)

    def pv_body(j, acc):
        pj = p_scr[j]
        for h in range(2):
            acc = acc + lax.dot_general(pj[:, h * page:(h + 1) * page],
                                        vbuf[slot, 2 * j + h].astype(BF16), NT,
                                        preferred_element_type=F32)
        return acc

    acc = lax.fori_loop(0, nb, pv_body, _bdot(p_new, vn_ref[0]), unroll=unroll)
    o_ref[0] = acc / denom


def _moba_sample(q_bd, page_table, kt_pool, vt_pool, k_new, v_new, l_new):
    bs, rows, w = q_bd.shape
    n_pages = page_table.shape[1]
    page = kt_pool.shape[2]
    assert 2 * page == MOBA_BLOCK and kt_pool.shape[1] == w
    nb = n_pages // 2
    k_sel = min(MOBA_TOPK, nb + 1)
    lp = k_new.shape[1]
    per_b = lambda r: pl.BlockSpec((1, r, w), lambda b, pt: (b, 0, 0))
    pool_bytes = 2 * n_pages * w * page * 4
    est = 2 * pool_bytes + 8 * nb * rows * 2 * page * 4
    return pl.pallas_call(
        functools.partial(_moba_sample_kernel, n_seq=bs, n_pages=n_pages, l_new=l_new, k_sel=k_sel),
        grid_spec=pltpu.PrefetchScalarGridSpec(
            num_scalar_prefetch=1,
            grid=(bs,),
            in_specs=[per_b(rows), per_b(lp), per_b(lp),
                      pl.BlockSpec(memory_space=pl.ANY), pl.BlockSpec(memory_space=pl.ANY)],
            out_specs=per_b(rows),
            scratch_shapes=[
                pltpu.VMEM((2, n_pages, w, page), F32),
                pltpu.VMEM((2, n_pages, w, page), F32),
                pltpu.SemaphoreType.DMA((2, 2)),
                pltpu.VMEM((nb, rows, 2 * page), F32),
                pltpu.VMEM((nb, rows, 2 * page), BF16),
            ],
        ),
        out_shape=jax.ShapeDtypeStruct((bs, rows, w), F32),
        compiler_params=pltpu.CompilerParams(
            dimension_semantics=("arbitrary",),
            vmem_limit_bytes=min(int(est * 1.25), VMEM_BYTES_V7X - 8 * 2 ** 20)),
        name="moba_sample",
    )(page_table, q_bd, k_new, v_new, kt_pool, vt_pool)


def _pack_w_in_a(w):
    d = w.shape[0]
    qkv_gate = w[:, :4 * MIX_W]
    ab = w[:, 4 * MIX_W:4 * MIX_W + 2 * N_HEADS]
    q_mem = w[:, 4 * MIX_W + 2 * N_HEADS:]
    ab = jnp.pad(ab, ((0, 0), (0, LANES - 2 * N_HEADS)))
    return jnp.concatenate([qkv_gate, q_mem, ab], axis=1).astype(BF16)


def _pad_rows(a, rows):
    return jnp.pad(a, ((0, 0), (0, rows - a.shape[1]), (0, 0)))


def _trunk(p, x3, pos_base, mem_k, mem_v, conv0, s0, paged):
    b, l, d = x3.shape
    m = b * l
    x = x3.reshape(m, d)
    sample = paged is not None
    ffn = lambda x, tag, i, fg=None: _ffn(x, p[tag + "_norm"][i], p[tag + "_w1"], p[tag + "_w3"],
                                          p[tag + "_w2"], i, fg)
    lq = -(-l // SUBLANES) * SUBLANES

    def memory(q_mem, layer):
        q3 = _pad_rows(q_mem.reshape(b, l, MEM_W), lq)
        mo = _mem_attn(q3, mem_k[layer], mem_v[layer])
        return mo[:, :l].reshape(m, MEM_W)

    x = ffn(x, "ffn1", 0)
    if l % IN_PROJ_TILE == 0:
        qkv, gate, q_mem, ab, new_conv = _in_proj_conv(x, p["mix_norm"][0], p["w_in_a"], conv0,
                                                       p["dn_conv_w"], l)
        qkv = qkv.reshape(b, l, 3 * MIX_W)
    else:
        qkv, gate, q_mem, ab = _norm_proj(x, p["mix_norm"][0], p["w_in_a"],
                                          (3 * MIX_W, MIX_W, MEM_W, LANES), "in_proj_a")
        qkv, new_conv = _conv(qkv.reshape(b, l, 3 * MIX_W), conv0, p["dn_conv_w"])
    lp = -(-l // DN_CHUNK) * DN_CHUNK
    o, s_fin = _gdn(_pad_rows(qkv, lp), _pad_rows(gate.reshape(b, l, MIX_W), lp),
                    _pad_rows(ab.reshape(b, l, LANES), lp),
                    p["dn_a_log"], p["dn_dt_bias"], p["dn_out_norm"], s0, l)
    o = o[:, :l].reshape(m, MIX_W)
    x = _out_proj(x, o, memory(q_mem, 0), p["w_out"][0], grouped=False)
    x = ffn(x, "ffn2", 0)

    if sample:
        tables = _rope_tables(m, l, pos_base)
    else:
        tables = _rope_tables(l, l, pos_base)
    kv = _kv_proj(x, p["kv_norm"], p["w_kv"], tables, l, for_prompt=not sample)
    x = ffn(x, "ffn1", 1)
    qh, q_mem = _q_proj(x, p["mix_norm"][1], p["w_in_b"], tables)
    if sample:
        page_table, kt_pool, vt_pool = paged
        k_new, v_new = kv
        k_out = k_new.reshape(b, l, N_KV_B, HEAD_DIM)
        v_out = v_new.reshape(b, l, N_KV_B, HEAD_DIM)
        eye = jnp.eye(N_KV_B, dtype=F32)
        q5 = qh.reshape(N_KV_B, b, l, KV_GROUP, HEAD_DIM).transpose(1, 0, 3, 2, 4)
        q_bd = (q5[:, :, :, :, None, :] * eye[None, :, None, None, :, None]).reshape(
            b, N_HEADS * l, N_KV_B * HEAD_DIM)
        o_bd = _moba_sample(q_bd, page_table, kt_pool, vt_pool,
                            _pad_rows(k_new.reshape(b, l, -1), lq),
                            _pad_rows(v_new.reshape(b, l, -1), lq), l)
        o6 = o_bd.reshape(b, N_KV_B, KV_GROUP, l, N_KV_B, HEAD_DIM)
        o = (o6 * eye[None, :, None, None, :, None]).sum(axis=4)
        o = o.transpose(0, 3, 1, 2, 4).reshape(m, MIX_W)
        x = _out_proj(x, o, memory(q_mem, 1), p["w_out"][1], grouped=False)
    else:
        k_t, v_t, kh, vt, kbar = kv
        to_rows = lambda a: a.reshape(b, N_KV_B, HEAD_DIM, l).transpose(0, 3, 1, 2)
        k_out, v_out = to_rows(k_t), to_rows(v_t)
        nb = l // MOBA_BLOCK
        kbar = kbar.reshape(b, nb, N_KV_B, HEAD_DIM).transpose(2, 0, 1, 3)
        o4 = _moba_prompt(qh, kh, vt, kbar, b, l)
        x = _out_proj(x, o4, memory(q_mem, 1), p["w_out"][1], grouped=True)
    y = ffn(x, "ffn2", 1, p["final_norm"])
    return y.reshape(b, l, d), new_conv[None], s_fin[None], k_out, v_out


def kernel(x_prompt, x_sample, mem_prompt, cache_mem_k, cache_mem_v, state_dn_conv, state_dn_S,
           cache_kv_k, cache_kv_v, page_table,
           ffn1_norm, ffn1_w1, ffn1_w3, ffn1_w2, mix_norm, w_in_a, w_in_b, w_out,
           dn_conv_w, dn_a_log, dn_dt_bias, dn_out_norm, kv_norm, w_kv, mem_norm, w_mem_kv,
           ffn2_norm, ffn2_w1, ffn2_w3, ffn2_w2, final_norm):
    assert w_in_a.shape[0] == 1 and w_in_b.shape[0] == 1 and ffn1_w1.shape[0] == 2
    bf = lambda a: a.astype(BF16)
    p = dict(ffn1_norm=ffn1_norm, ffn1_w1=bf(ffn1_w1), ffn1_w3=bf(ffn1_w3), ffn1_w2=bf(ffn1_w2),
             ffn2_norm=ffn2_norm, ffn2_w1=bf(ffn2_w1), ffn2_w3=bf(ffn2_w3), ffn2_w2=bf(ffn2_w2),
             mix_norm=mix_norm, w_in_a=_pack_w_in_a(w_in_a[0]), w_in_b=bf(w_in_b[0]), w_out=bf(w_out),
             dn_conv_w=dn_conv_w[0], dn_a_log=dn_a_log[0], dn_dt_bias=dn_dt_bias[0],
             dn_out_norm=dn_out_norm[0], kv_norm=kv_norm, w_kv=bf(w_kv), final_norm=final_norm)

    bp, lp, _ = x_prompt.shape
    assert lp % MOBA_BLOCK == 0
    n_mem = mem_prompt.shape[1]
    n_layers = mem_norm.shape[0]
    mem_t = [_mem_kv(mem_prompt, mem_norm[layer], bf(w_mem_kv[layer])) for layer in range(n_layers)]
    mk = [t[0] for t in mem_t]
    mv = [t[1] for t in mem_t]
    conv0 = jnp.zeros((bp, CONV_W - 1, 3 * MIX_W), F32)
    s0 = jnp.zeros((bp, N_HEADS, HEAD_DIM, HEAD_DIM), F32)
    y_p, conv_p, s_p, k_p, v_p = _trunk(p, x_prompt, 0, mk, mv, conv0, s0, None)
    token_major = lambda ts: jnp.stack(ts).reshape(n_layers, bp, N_MEM_HEADS, HEAD_DIM, n_mem).transpose(
        0, 1, 4, 2, 3)
    mem_k_p = token_major(mk)
    mem_v_p = token_major(mv)

    bs = x_sample.shape[0]
    n_pool, page = cache_kv_k.shape[:2]
    past_len = page_table.shape[1] * page
    assert past_len % MOBA_BLOCK == 0 and x_sample.shape[1] <= MOBA_BLOCK
    token_minor = lambda c: jnp.transpose(c, (0, 1, 3, 4, 2)).reshape(c.shape[0], bs, MEM_W, n_mem)
    cmk = token_minor(cache_mem_k)
    cmv = token_minor(cache_mem_v)
    transposed_pages = lambda c: jnp.transpose(c, (0, 2, 3, 1)).reshape(n_pool, -1, page)
    paged = (page_table, transposed_pages(cache_kv_k), transposed_pages(cache_kv_v))
    y_s, conv_s, s_s, k_s, v_s = _trunk(p, x_sample, past_len, cmk, cmv, state_dn_conv[0],
                                        state_dn_S[0], paged)
    return (y_p, y_s, conv_p, s_p, k_p, v_p, mem_k_p, mem_v_p, conv_s, s_s, k_s, v_s)
```
